```python
import math
import jax, jax.numpy as jnp
from jax import lax
import numpy as np

D_MODEL = 1024
BATCH = 8
SEQ = 8192
DEPTH = 2

GRID_W = 64
CTX_LEN = 256
N_EVEN = (DEPTH + 1) // 2
N_ODD = DEPTH // 2
EPS = 1e-6
CHUNK = 64
HA = 8
DKA = 64
DVA = 64
WA = HA * DVA
HB = 8
PB = 64
D_INNER = HB * PB
N_GROUPS = 2
D_STATE = 128
CONV_W = 5
XBC_DIM = D_INNER + 2 * N_GROUPS * D_STATE
REC_SPLITS = (HA * DKA, HA * DKA, HA * DKA, WA, WA, D_INNER, D_INNER, N_GROUPS * D_STATE, N_GROUPS * D_STATE, HB, HB)
IN_COLS = 3 * HA * DKA + 2 * WA + 2 * D_INNER + 2 * N_GROUPS * D_STATE + 2 * HB
H_C = 8
NOPE = 128
ROPE = 64
VH = 128
Q_LORA = 384
KV_LORA = 256
ROPE_THETA = 10000.0
Q_BLOCK = 128
ATTN_SCALE = 1.0 / math.sqrt(NOPE + ROPE)
N_EXPERTS = 16
EXPERT_FF = 1024
CAP_FACTOR = 2

kernel_name = 'hybrid_hgrn2_ssd_mla_ecmoe_diffusion'


def rmsnorm(x, gain):
    xf = x.astype(jnp.float32)
    y = xf * lax.rsqrt(jnp.mean(xf * xf, axis=-1, keepdims=True) + EPS)
    return (y * gain.astype(jnp.float32)).astype(x.dtype)


def modulate(x, gain, shift, scale):
    return rmsnorm(x, gain) * (1 + scale) + shift


def split_cols(p, sizes):
    out, off = [], 0
    for s in sizes:
        out.append(p[..., off:off + s])
        off += s
    return out


def gla_chunked(q, k, v, log_a, s0):
    B, T, H, K = q.shape
    V = v.shape[-1]
    n = T // CHUNK
    r = lambda a: a.reshape(B, n, CHUNK, H, a.shape[-1])
    q, k, v, log_a = r(q), r(k), r(v), r(log_a)
    b = jnp.cumsum(log_a, axis=2)
    b_ref = b[:, :, CHUNK // 2:CHUNK // 2 + 1]
    b_last = b[:, :, -1:]
    att = jnp.einsum('bnthk,bnshk->bnhts', q * jnp.exp(b - b_ref), k * jnp.exp(b_ref - b))
    mask = jnp.tril(jnp.ones((CHUNK, CHUNK), dtype=bool))
    att = jnp.where(mask, att, 0.0)
    o = jnp.einsum('bnhts,bnshv->bnthv', att, v)
    u = jnp.einsum('bnshk,bnshv->nbhkv', k * jnp.exp(b_last - b), v)
    dec = jnp.exp(jnp.moveaxis(b_last[:, :, 0], 1, 0))

    def step(s, xs):
        d, uu = xs
        return d[..., None] * s + uu, s

    s_fin, s_in = lax.scan(step, s0, (dec, u))
    o = o + jnp.einsum('bnthk,nbhkv->bnthv', q * jnp.exp(b), s_in)
    return o.reshape(B, T, H, V), s_fin


def ssd_chunked(x, dt, la, bm, cm, s0):
    Bsz, T, H, P = x.shape
    G, N = bm.shape[-2], bm.shape[-1]
    hpg = H // G
    n = T // CHUNK
    xdt = (x * dt[..., None]).reshape(Bsz, n, CHUNK, G, hpg, P)
    la = la.reshape(Bsz, n, CHUNK, G, hpg)
    bm = bm.reshape(Bsz, n, CHUNK, G, N)
    cm = cm.reshape(Bsz, n, CHUNK, G, N)
    cum = jnp.cumsum(la, axis=2)
    cum_h = jnp.moveaxis(cum, 2, -1)
    diff = cum_h[..., :, None] - cum_h[..., None, :]
    mask = jnp.tril(jnp.ones((CHUNK, CHUNK), dtype=bool))
    lmat = jnp.exp(jnp.where(mask, diff, -jnp.inf))
    cb = jnp.einsum('bntgd,bnsgd->bngts', cm, bm)
    y = jnp.einsum('bnghts,bnsghp->bntghp', cb[:, :, :, None] * lmat, xdt)
    cum_last = cum[:, :, -1]
    w_end = jnp.exp(cum_last[:, :, None] - cum)
    u = jnp.einsum('bnsgd,bnsgh,bnsghp->nbghdp', bm, w_end, xdt)
    dec = jnp.exp(jnp.moveaxis(cum_last, 1, 0))

    def step(s, xs):
        d, uu = xs
        return d[..., None, None] * s + uu, s

    s_fin, s_in = lax.scan(step, s0, (dec, u))
    y = y + jnp.einsum('bntgd,bntgh,nbghdp->bntghp', cm, jnp.exp(cum), s_in)
    return y.reshape(Bsz, T, H, P), s_fin


def bidir_two_stream(scan, ctx_f, ctx_b, lat_f, lat_b, s0):
    flip = lambda args: tuple(jnp.flip(a, axis=1) for a in args)
    oc_f, sc_f = scan(*ctx_f, s0)
    ol_f, _ = scan(*lat_f, sc_f)
    oc_b, sc_b = scan(*flip(ctx_b), s0)
    ol_b, _ = scan(*flip(lat_b), sc_b)
    return oc_f + jnp.flip(oc_b, axis=1), ol_f + jnp.flip(ol_b, axis=1)


def centred_dwconv_silu(u, w, b):
    C = u.shape[-1]
    y = lax.conv_general_dilated(u, w[:, None, :].astype(u.dtype), window_strides=(1,),
                                 padding=[(CONV_W // 2, CONV_W // 2)],
                                 dimension_numbers=('NWC', 'WIO', 'NWC'), feature_group_count=C)
    return jax.nn.silu(y + b)


def recurrent_mixer(h_ctx, h_lat, w_in, w_out, conv_w, conv_b, lb, dt_bias, a_log, d_skip,
                    hgrn_norm, mamba_norm, need_ctx):
    f32 = jnp.float32

    def prepare(h):
        B, T, _ = h.shape
        q, ff, fb, i, g, z, xs, bs, cs, dtf, dtb = split_cols(h @ w_in, REC_SPLITS)
        heads_k = lambda a: a.reshape(B, T, HA, DKA)
        q = heads_k(jax.nn.silu(q.astype(f32)))
        v = i.astype(f32).reshape(B, T, HA, DVA)

        def forget(fr, lbd):
            f = lbd + (1.0 - lbd) * jax.nn.sigmoid(fr.astype(f32))
            return heads_k(1.0 - f), heads_k(jnp.log(f))

        kf, laf = forget(ff, lb[0])
        kb, lab = forget(fb, lb[1])
        xbc = centred_dwconv_silu(jnp.concatenate([xs, bs, cs], axis=-1), conv_w, conv_b).astype(f32)
        xm, bm, cm = split_cols(xbc, (D_INNER, N_GROUPS * D_STATE, N_GROUPS * D_STATE))
        xm = xm.reshape(B, T, HB, PB)
        bm = bm.reshape(B, T, N_GROUPS, D_STATE)
        cm = cm.reshape(B, T, N_GROUPS, D_STATE)

        def step_size(dr, bias, alog):
            dt = jax.nn.softplus(dr.astype(f32) + bias)
            return dt, -dt * jnp.exp(alog.astype(f32))

        dt_f, la_f = step_size(dtf, dt_bias[0], a_log[0])
        dt_b, la_b = step_size(dtb, dt_bias[1], a_log[1])
        return ((q, kf, v, laf), (q, kb, v, lab),
                (xm, dt_f, la_f, bm, cm), (xm, dt_b, la_b, bm, cm), (g, z, xm))

    cp = prepare(h_ctx)
    lp = prepare(h_lat)
    B = h_lat.shape[0]
    s0_h = jnp.zeros((B, HA, DKA, DVA), f32)
    s0_m = jnp.zeros((B, N_GROUPS, HB // N_GROUPS, D_STATE, PB), f32)
    oh_c, oh_l = bidir_two_stream(gla_chunked, cp[0], cp[1], lp[0], lp[1], s0_h)
    om_c, om_l = bidir_two_stream(ssd_chunked, cp[2], cp[3], lp[2], lp[3], s0_m)

    def merge(oh, om, extras, h):
        g, z, xm = extras
        B, T, _ = h.shape
        oh = rmsnorm(oh, hgrn_norm.reshape(HA, DVA)).reshape(B, T, WA) * jax.nn.sigmoid(g.astype(f32))
        y = (om + d_skip.astype(f32)[:, None] * xm).reshape(B, T, D_INNER)
        y = rmsnorm(y * jax.nn.silu(z.astype(f32)), mamba_norm)
        return jnp.concatenate([oh, y], axis=-1).astype(h.dtype) @ w_out

    o_lat = merge(oh_l, om_l, lp[4], h_lat)
    o_ctx = merge(oh_c, om_c, cp[4], h_ctx) if need_ctx else None
    return o_ctx, o_lat


def axial_rope_tables(T):
    rows = T // GRID_W
    row = jnp.repeat(jnp.arange(rows, dtype=jnp.int32), GRID_W)
    col = jnp.tile(jnp.arange(GRID_W, dtype=jnp.int32), rows)
    nf = ROPE // 4
    inv_freq = ROPE_THETA ** (-jnp.arange(nf, dtype=jnp.float32) / nf)
    pos = jnp.stack([row, col], axis=-1).astype(jnp.float32)
    ang = pos[..., None] * inv_freq
    return jnp.cos(ang), jnp.sin(ang)


def rope2d(x, cos, sin):
    nf = ROPE // 4
    xr = x.reshape(x.shape[:-1] + (2, 2, nf))
    x1, x2 = xr[..., 0, :], xr[..., 1, :]
    cos, sin = cos.astype(x.dtype), sin.astype(x.dtype)
    return jnp.stack([x1 * cos - x2 * sin, x2 * cos + x1 * sin], axis=-2).reshape(x.shape)


def attend(qn, qr, kn, kr, v):
    s = jnp.einsum('bqhd,bkhd->bhqk', qn, kn) + jnp.einsum('bqhr,bkr->bhqk', qr, kr)
    p = jax.nn.softmax(s.astype(jnp.float32) * ATTN_SCALE, axis=-1).astype(v.dtype)
    return jnp.einsum('bhqk,bkhv->bqhv', p, v)


def mla_mixer(h_ctx, h_lat, w_dq, q_norm, w_uq, w_dkv, kv_norm, w_ukv, w_kr, w_o, need_ctx):
    def q_proj(h):
        B, T, _ = h.shape
        q = (rmsnorm(h @ w_dq, q_norm) @ w_uq).reshape(B, T, H_C, NOPE + ROPE)
        return q[..., :NOPE], q[..., NOPE:]

    def kv_proj(h):
        B, T, _ = h.shape
        kv = (rmsnorm(h @ w_dkv, kv_norm) @ w_ukv).reshape(B, T, H_C, NOPE + VH)
        return kv[..., :NOPE], h @ w_kr, kv[..., NOPE:]

    B, T, _ = h_lat.shape
    cos, sin = axial_rope_tables(T)
    kn_c, kr_c, v_c = kv_proj(h_ctx)
    kn_l, kr_l, v_l = kv_proj(h_lat)
    qn_l, qr_l = q_proj(h_lat)
    qr_l = rope2d(qr_l, cos[:, None], sin[:, None])
    kr_l = rope2d(kr_l, cos, sin)
    kn = jnp.concatenate([kn_c, kn_l], axis=1)
    kr = jnp.concatenate([kr_c, kr_l], axis=1)
    v = jnp.concatenate([v_c, v_l], axis=1)
    nb = T // Q_BLOCK

    def to_blocks(a):
        return jnp.moveaxis(a.reshape((B, nb, Q_BLOCK) + a.shape[2:]), 1, 0)

    o_l = lax.map(lambda qb: attend(qb[0], qb[1], kn, kr, v), (to_blocks(qn_l), to_blocks(qr_l)))
    o_l = jnp.moveaxis(o_l, 0, 1).reshape(B, T, H_C * VH) @ w_o
    o_c = None
    if need_ctx:
        qn_c, qr_c = q_proj(h_ctx)
        o_c = attend(qn_c, qr_c, kn_c, kr_c, v_c).reshape(B, h_ctx.shape[1], H_C * VH) @ w_o
    return o_c, o_l


def expert_choice_ffn(h, w_router, w_gate, w_up, w_down):
    B, T, _ = h.shape
    cap = CAP_FACTOR * T // N_EXPERTS
    aff = jax.nn.softmax((h @ w_router).astype(jnp.float32), axis=-1)
    g, idx = lax.top_k(jnp.swapaxes(aff, 1, 2), cap)
    bidx = jnp.arange(B)[:, None, None]
    xg = h[bidx, idx]
    hid = jax.nn.silu(jnp.einsum('becd,edf->becf', xg, w_gate)) * jnp.einsum('becd,edf->becf', xg, w_up)
    y = jnp.einsum('becf,efd->becd', hid, w_down) * g[..., None].astype(h.dtype)
    return jnp.zeros_like(h).at[bidx, idx].add(y)


def setup_inputs(seed: int = 0) -> dict:
    key = jax.random.key(seed)
    ks = jax.random.split(key, 32)
    f32 = jnp.float32
    nrm = lambda k, shape, fan_in: jax.random.normal(k, shape, f32) * fan_in ** -0.5
    gain = lambda k, shape: 1.0 + 0.02 * jax.random.normal(k, shape, f32)
    D = D_MODEL
    dt0 = jnp.exp(jax.random.uniform(ks[14], (N_EVEN, 2, HB), f32, math.log(1e-3), math.log(1e-1)))
    return {
        'x': jax.random.normal(ks[0], (BATCH, SEQ, D), f32),
        'c': jax.random.normal(ks[1], (BATCH, D), f32),
        'ctx': jax.random.normal(ks[2], (BATCH, CTX_LEN, D), f32),
        'c_ctx': jax.random.normal(ks[3], (D,), f32),
        'w_mod': 0.5 * nrm(ks[4], (DEPTH, D, 6 * D), D),
        'b_mod': 0.02 * jax.random.normal(ks[5], (DEPTH, 6 * D), f32),
        'norm_mix': gain(ks[6], (DEPTH, D)),
        'norm_ffn': gain(ks[7], (DEPTH, D)),
        'norm_out': gain(ks[8], (D,)),
        'w_in': nrm(ks[9], (N_EVEN, D, IN_COLS), D),
        'w_out_rec': nrm(ks[10], (N_EVEN, WA + D_INNER, D), WA + D_INNER),
        'conv_w': nrm(ks[11], (N_EVEN, CONV_W, XBC_DIM), CONV_W),
        'conv_b': 0.02 * jax.random.normal(ks[12], (N_EVEN, XBC_DIM), f32),
        'lb_gamma': 0.1 * jax.random.normal(ks[13], (N_EVEN + 1, 2, HA * DKA), f32),
        'dt_bias': dt0 + jnp.log(-jnp.expm1(-dt0)),
        'a_log': jnp.log(jax.random.uniform(ks[15], (N_EVEN, 2, HB), f32, 1.0, 16.0)),
        'd_skip': 1.0 + 0.1 * jax.random.normal(ks[16], (N_EVEN, HB), f32),
        'hgrn_norm': gain(ks[17], (N_EVEN, WA)),
        'mamba_norm': gain(ks[18], (N_EVEN, D_INNER)),
        'w_dq': nrm(ks[19], (N_ODD, D, Q_LORA), D),
        'q_norm': gain(ks[20], (N_ODD, Q_LORA)),
        'w_uq': nrm(ks[21], (N_ODD, Q_LORA, H_C * (NOPE + ROPE)), Q_LORA),
        'w_dkv': nrm(ks[22], (N_ODD, D, KV_LORA), D),
        'kv_norm': gain(ks[23], (N_ODD, KV_LORA)),
        'w_ukv': nrm(ks[24], (N_ODD, KV_LORA, H_C * (NOPE + VH)), KV_LORA),
        'w_kr': nrm(ks[25], (N_ODD, D, ROPE), D),
        'w_o': nrm(ks[26], (N_ODD, H_C * VH, D), H_C * VH),
        'w_router': nrm(ks[27], (DEPTH, D, N_EXPERTS), D),
        'w_gate': nrm(ks[28], (DEPTH, N_EXPERTS, D, EXPERT_FF), D),
        'w_up': nrm(ks[29], (DEPTH, N_EXPERTS, D, EXPERT_FF), D),
        'w_down': nrm(ks[30], (DEPTH, N_EXPERTS, EXPERT_FF, D), EXPERT_FF),
    }


def reference(x, c, ctx, c_ctx, w_mod, b_mod, norm_mix, norm_ffn, norm_out, w_in, w_out_rec,
              conv_w, conv_b, lb_gamma, dt_bias, a_log, d_skip, hgrn_norm, mamba_norm,
              w_dq, q_norm, w_uq, w_dkv, kv_norm, w_ukv, w_kr, w_o,
              w_router, w_gate, w_up, w_down):
    lb = jnp.cumsum(jax.nn.softmax(lb_gamma.astype(jnp.float32), axis=0), axis=0)
    s_lat = jax.nn.silu(c)
    s_ctx = jax.nn.silu(c_ctx)
    x_lat, x_ctx = x, ctx
    for l in range(DEPTH):
        need_ctx = l < DEPTH - 1
        m_lat = jnp.split((s_lat @ w_mod[l] + b_mod[l])[:, None, :], 6, axis=-1)
        m_ctx = jnp.split(s_ctx @ w_mod[l] + b_mod[l], 6, axis=-1)
        a_lat = modulate(x_lat, norm_mix[l], m_lat[0], m_lat[1])
        a_ctx = modulate(x_ctx, norm_mix[l], m_ctx[0], m_ctx[1])
        if l % 2 == 0:
            e = l // 2
            o_ctx, o_lat = recurrent_mixer(a_ctx, a_lat, w_in[e], w_out_rec[e], conv_w[e], conv_b[e],
                                           lb[e], dt_bias[e], a_log[e], d_skip[e], hgrn_norm[e],
                                           mamba_norm[e], need_ctx)
        else:
            j = l // 2
            o_ctx, o_lat = mla_mixer(a_ctx, a_lat, w_dq[j], q_norm[j], w_uq[j], w_dkv[j], kv_norm[j],
                                     w_ukv[j], w_kr[j], w_o[j], need_ctx)
        x_lat = x_lat + m_lat[2] * o_lat
        x_lat = x_lat + m_lat[5] * expert_choice_ffn(modulate(x_lat, norm_ffn[l], m_lat[3], m_lat[4]),
                                                     w_router[l], w_gate[l], w_up[l], w_down[l])
        if need_ctx:
            x_ctx = x_ctx + m_ctx[2] * o_ctx
            x_ctx = x_ctx + m_ctx[5] * expert_choice_ffn(modulate(x_ctx, norm_ffn[l], m_ctx[3], m_ctx[4]),
                                                         w_router[l], w_gate[l], w_up[l], w_down[l])
    return rmsnorm(x_lat, norm_out)
```

```python
import functools
import math

import jax
import jax.numpy as jnp
from jax import lax
from jax.experimental import pallas as pl
from jax.experimental.pallas import tpu as pltpu

F32 = jnp.float32
BF16 = jnp.bfloat16

D_MODEL = 1024
DEPTH = 2
GRID_W = 64
EPS = 1e-6
CHUNK = 64
HA, DKA, DVA = 8, 64, 64
WA = HA * DVA
HB, PB = 8, 64
D_INNER = HB * PB
N_GROUPS, D_STATE = 2, 128
CONV_W = 5
REC_SPLITS = (HA * DKA, HA * DKA, HA * DKA, WA, WA, D_INNER, D_INNER,
              N_GROUPS * D_STATE, N_GROUPS * D_STATE, HB, HB)
IN_COLS = sum(REC_SPLITS)
IN_COLS_PAD = 4224
H_C, NOPE, ROPE, VH = 8, 128, 64, 128
Q_LORA, KV_LORA = 384, 256
ROPE_THETA = 10000.0
ATTN_SCALE = 1.0 / math.sqrt(NOPE + ROPE)
QK_DIM = 256
N_EXPERTS = 16
EXPERT_FF = 1024
CAP_FACTOR = 2

VMEM_LIMIT = 56 * 1024 * 1024


def _cparams(sem):
    return pltpu.CompilerParams(dimension_semantics=sem, vmem_limit_bytes=VMEM_LIMIT)


def _silu(v):
    return v * jax.nn.sigmoid(v)


def _modnorm(x, gain, scale, shift):
    ms = jnp.mean(x * x, axis=-1, keepdims=True)
    return (x * lax.rsqrt(ms + EPS) * gain) * (1.0 + scale) + shift


def _mod_kernel(s_ref, w_ref, b_ref, o_ref):
    s = _silu(s_ref[...])
    o_ref[0] = jnp.dot(s, w_ref[0], preferred_element_type=F32,
                       precision=lax.Precision.HIGHEST) + b_ref[0]


def _modulation(cvec, w_mod, b_mod):
    n = w_mod.shape[-1]
    tn = 1536
    return pl.pallas_call(
        _mod_kernel,
        grid=(DEPTH, n // tn),
        in_specs=[pl.BlockSpec((16, D_MODEL), lambda l, j: (0, 0)),
                  pl.BlockSpec((1, D_MODEL, tn), lambda l, j: (l, 0, j)),
                  pl.BlockSpec((1, 1, tn), lambda l, j: (l, 0, j))],
        out_specs=pl.BlockSpec((1, 16, tn), lambda l, j: (l, 0, j)),
        out_shape=jax.ShapeDtypeStruct((DEPTH, 16, n), F32),
        compiler_params=_cparams(("parallel", "parallel")),
        name="modulation",
    )(cvec, w_mod, b_mod.reshape(DEPTH, 1, n))


def _modproj_kernel(x_ref, g_ref, sc_ref, sh_ref, w_ref, o_ref, a_scr):
    @pl.when(pl.program_id(2) == 0)
    def _():
        a_scr[...] = _modnorm(x_ref[0], g_ref[...], sc_ref[0], sh_ref[0]).astype(BF16)

    o_ref[0] = jnp.dot(a_scr[...], w_ref[...], preferred_element_type=F32).astype(o_ref.dtype)


def _modproj(x, gain, scale, shift, w, tm, tn, out_dtype=F32):
    b, t, d = x.shape
    n = w.shape[1]
    return pl.pallas_call(
        _modproj_kernel,
        grid=(b, t // tm, n // tn),
        in_specs=[pl.BlockSpec((1, tm, d), lambda i, j, k: (i, j, 0)),
                  pl.BlockSpec((1, d), lambda i, j, k: (0, 0)),
                  pl.BlockSpec((1, 1, d), lambda i, j, k: (i, 0, 0)),
                  pl.BlockSpec((1, 1, d), lambda i, j, k: (i, 0, 0)),
                  pl.BlockSpec((d, tn), lambda i, j, k: (0, k))],
        out_specs=pl.BlockSpec((1, tm, tn), lambda i, j, k: (i, j, k)),
        out_shape=jax.ShapeDtypeStruct((b, t, n), out_dtype),
        scratch_shapes=[pltpu.VMEM((tm, d), BF16)],
        compiler_params=_cparams(("parallel", "parallel", "arbitrary")),
        name="modproj",
    )(x, gain.reshape(1, d), scale, shift, w)


def _linres_kernel(x_ref, gate_ref, y_ref, w_ref, o_ref):
    acc = jnp.dot(y_ref[0].astype(BF16), w_ref[...], preferred_element_type=F32)
    o_ref[0] = x_ref[0] + gate_ref[0] * acc


def _linear_residual(x, gate, y, w, tm):
    b, t, d = x.shape
    k = y.shape[-1]
    return pl.pallas_call(
        _linres_kernel,
        grid=(b, t // tm),
        in_specs=[pl.BlockSpec((1, tm, d), lambda i, j: (i, j, 0)),
                  pl.BlockSpec((1, 1, d), lambda i, j: (i, 0, 0)),
                  pl.BlockSpec((1, tm, k), lambda i, j: (i, j, 0)),
                  pl.BlockSpec((k, d), lambda i, j: (0, 0))],
        out_specs=pl.BlockSpec((1, tm, d), lambda i, j: (i, j, 0)),
        out_shape=jax.ShapeDtypeStruct((b, t, d), F32),
        compiler_params=_cparams(("parallel", "parallel")),
        name="linear_residual",
    )(x, gate, y, w)


def _split_cols(p, sizes):
    out, off = [], 0
    for s in sizes:
        out.append(p[..., off:off + s])
        off += s
    return out


def _gla_chunked(q, k, v, log_a, s0):
    B, T, H, K = q.shape
    n = T // CHUNK
    r = lambda a: a.reshape(B, n, CHUNK, H, a.shape[-1])
    q, k, v, log_a = r(q), r(k), r(v), r(log_a)
    b = jnp.cumsum(log_a, axis=2)
    b_ref = b[:, :, CHUNK // 2:CHUNK // 2 + 1]
    b_last = b[:, :, -1:]
    att = jnp.einsum('bnthk,bnshk->bnhts', q * jnp.exp(b - b_ref), k * jnp.exp(b_ref - b))
    mask = jnp.tril(jnp.ones((CHUNK, CHUNK), dtype=bool))
    att = jnp.where(mask, att, 0.0)
    o = jnp.einsum('bnhts,bnshv->bnthv', att, v)
    u = jnp.einsum('bnshk,bnshv->nbhkv', k * jnp.exp(b_last - b), v)
    dec = jnp.exp(jnp.moveaxis(b_last[:, :, 0], 1, 0))

    def step(s, xs):
        d, uu = xs
        return d[..., None] * s + uu, s

    s_fin, s_in = lax.scan(step, s0, (dec, u))
    o = o + jnp.einsum('bnthk,nbhkv->bnthv', q * jnp.exp(b), s_in)
    return o.reshape(B, T, H, v.shape[-1]), s_fin


def _ssd_chunked(x, dt, la, bm, cm, s0):
    Bsz, T, H, P = x.shape
    G, N = bm.shape[-2], bm.shape[-1]
    hpg = H // G
    n = T // CHUNK
    xdt = (x * dt[..., None]).reshape(Bsz, n, CHUNK, G, hpg, P)
    la = la.reshape(Bsz, n, CHUNK, G, hpg)
    bm = bm.reshape(Bsz, n, CHUNK, G, N)
    cm = cm.reshape(Bsz, n, CHUNK, G, N)
    cum = jnp.cumsum(la, axis=2)
    cum_h = jnp.moveaxis(cum, 2, -1)
    diff = cum_h[..., :, None] - cum_h[..., None, :]
    mask = jnp.tril(jnp.ones((CHUNK, CHUNK), dtype=bool))
    lmat = jnp.exp(jnp.where(mask, diff, -jnp.inf))
    cb = jnp.einsum('bntgd,bnsgd->bngts', cm, bm)
    y = jnp.einsum('bnghts,bnsghp->bntghp', cb[:, :, :, None] * lmat, xdt)
    cum_last = cum[:, :, -1]
    w_end = jnp.exp(cum_last[:, :, None] - cum)
    u = jnp.einsum('bnsgd,bnsgh,bnsghp->nbghdp', bm, w_end, xdt)
    dec = jnp.exp(jnp.moveaxis(cum_last, 1, 0))

    def step(s, xs):
        d, uu = xs
        return d[..., None, None] * s + uu, s

    s_fin, s_in = lax.scan(step, s0, (dec, u))
    y = y + jnp.einsum('bntgd,bntgh,nbghdp->bntghp', cm, jnp.exp(cum), s_in)
    return y.reshape(Bsz, T, H, P), s_fin


def _bidir(scan, ctx_f, ctx_b, lat_f, lat_b, s0):
    flip = lambda args: tuple(jnp.flip(a, axis=1) for a in args)
    oc_f, sc_f = scan(*ctx_f, s0)
    ol_f, _ = scan(*lat_f, sc_f)
    oc_b, sc_b = scan(*flip(ctx_b), s0)
    ol_b, _ = scan(*flip(lat_b), sc_b)
    return oc_f + jnp.flip(oc_b, axis=1), ol_f + jnp.flip(ol_b, axis=1)


def _dwconv_silu(u, w, b):
    C = u.shape[-1]
    y = lax.conv_general_dilated(u, w[:, None, :], window_strides=(1,),
                                 padding=[(CONV_W // 2, CONV_W // 2)],
                                 dimension_numbers=('NWC', 'WIO', 'NWC'), feature_group_count=C)
    return jax.nn.silu(y + b)


def _prepare(p, conv_w, conv_b, lb, dt_bias, a_log):
    B, T, _ = p.shape
    q, ff, fb, i, g, z, xs, bs, cs, dtf, dtb = _split_cols(p[..., :IN_COLS], REC_SPLITS)
    heads_k = lambda a: a.reshape(B, T, HA, DKA)
    q = heads_k(jax.nn.silu(q))
    v = i.reshape(B, T, HA, DVA)

    def forget(fr, lbd):
        f = lbd + (1.0 - lbd) * jax.nn.sigmoid(fr)
        return heads_k(1.0 - f), heads_k(jnp.log(f))

    kf, laf = forget(ff, lb[0])
    kb, lab = forget(fb, lb[1])
    xbc = _dwconv_silu(jnp.concatenate([xs, bs, cs], axis=-1), conv_w, conv_b)
    xm, bm, cm = _split_cols(xbc, (D_INNER, N_GROUPS * D_STATE, N_GROUPS * D_STATE))
    xm = xm.reshape(B, T, HB, PB)
    bm = bm.reshape(B, T, N_GROUPS, D_STATE)
    cm = cm.reshape(B, T, N_GROUPS, D_STATE)

    def step_size(dr, bias, alog):
        dt = jax.nn.softplus(dr + bias)
        return dt, -dt * jnp.exp(alog)

    dt_f, la_f = step_size(dtf, dt_bias[0], a_log[0])
    dt_b, la_b = step_size(dtb, dt_bias[1], a_log[1])
    return ((q, kf, v, laf), (q, kb, v, lab),
            (xm, dt_f, la_f, bm, cm), (xm, dt_b, la_b, bm, cm), (g, z, xm))


def _rms(x, gain):
    return x * lax.rsqrt(jnp.mean(x * x, axis=-1, keepdims=True) + EPS) * gain


def _merge_pre(oh, om, extras, d_skip, hgrn_norm, mamba_norm):
    g, z, xm = extras
    B, T = g.shape[:2]
    oh = _rms(oh, hgrn_norm.reshape(HA, DVA)).reshape(B, T, WA) * jax.nn.sigmoid(g)
    y = (om + d_skip[:, None] * xm).reshape(B, T, D_INNER)
    y = _rms(y * jax.nn.silu(z), mamba_norm)
    return jnp.concatenate([oh, y], axis=-1)


def _rope_tables(t):
    rows = t // GRID_W
    row = jnp.repeat(jnp.arange(rows, dtype=jnp.int32), GRID_W)
    col = jnp.tile(jnp.arange(GRID_W, dtype=jnp.int32), rows)
    nf = ROPE // 4
    inv_freq = ROPE_THETA ** (-jnp.arange(nf, dtype=F32) / nf)
    pos = jnp.stack([row, col], axis=-1).astype(F32)
    ang = pos[..., None] * inv_freq
    cos, sin = jnp.cos(ang), jnp.sin(ang)
    cos64 = jnp.broadcast_to(cos[:, :, None, :], (t, 2, 2, nf)).reshape(t, ROPE)
    sin64 = jnp.broadcast_to(sin[:, :, None, :], (t, 2, 2, nf)).reshape(t, ROPE)
    zero = jnp.zeros((t, ROPE), F32)
    return jnp.concatenate([cos64, zero], -1), jnp.concatenate([sin64, zero], -1)


def _swap_cols(w):
    nf = ROPE // 4
    wr = w.reshape(w.shape[:-1] + (2, 2, nf))
    return jnp.stack([-wr[..., 1, :], wr[..., 0, :]], axis=-2).reshape(w.shape)


def _rope_tile(tile, cos_t, sin_t):
    return tile * cos_t + pltpu.roll(tile, ROPE, axis=1) * sin_t


def _mla_down_kernel(x_ref, g_ref, sc_ref, sh_ref, w_ref, qn_ref, kvn_ref, cos_ref, sin_ref,
                     cq_ref, ckv_ref, kr_ref):
    a = _modnorm(x_ref[0], g_ref[...], sc_ref[0], sh_ref[0]).astype(BF16)
    c = jnp.dot(a, w_ref[...], preferred_element_type=F32)
    cq = c[:, :Q_LORA]
    ckv = c[:, Q_LORA:Q_LORA + KV_LORA]
    cq_ref[0] = _rms(cq, qn_ref[...]).astype(BF16)
    ckv_ref[0] = _rms(ckv, kvn_ref[...]).astype(BF16)
    kr_ref[0] = _rope_tile(c[:, Q_LORA + KV_LORA:], cos_ref[...], sin_ref[...]).astype(BF16)


def _mla_down(x, gain, scale, shift, w_cat, q_norm, kv_norm, cos_t, sin_t, tm):
    b, t, d = x.shape
    n = w_cat.shape[1]
    return pl.pallas_call(
        _mla_down_kernel,
        grid=(b, t // tm),
        in_specs=[pl.BlockSpec((1, tm, d), lambda i, j: (i, j, 0)),
                  pl.BlockSpec((1, d), lambda i, j: (0, 0)),
                  pl.BlockSpec((1, 1, d), lambda i, j: (i, 0, 0)),
                  pl.BlockSpec((1, 1, d), lambda i, j: (i, 0, 0)),
                  pl.BlockSpec((d, n), lambda i, j: (0, 0)),
                  pl.BlockSpec((1, Q_LORA), lambda i, j: (0, 0)),
                  pl.BlockSpec((1, KV_LORA), lambda i, j: (0, 0)),
                  pl.BlockSpec((tm, 128), lambda i, j: (j, 0)),
                  pl.BlockSpec((tm, 128), lambda i, j: (j, 0))],
        out_specs=[pl.BlockSpec((1, tm, Q_LORA), lambda i, j: (i, j, 0)),
                   pl.BlockSpec((1, tm, KV_LORA), lambda i, j: (i, j, 0)),
                   pl.BlockSpec((1, tm, 128), lambda i, j: (i, j, 0))],
        out_shape=[jax.ShapeDtypeStruct((b, t, Q_LORA), BF16),
                   jax.ShapeDtypeStruct((b, t, KV_LORA), BF16),
                   jax.ShapeDtypeStruct((b, t, 128), BF16)],
        compiler_params=_cparams(("parallel", "parallel")),
        name="mla_down",
    )(x, gain.reshape(1, d), scale, shift, w_cat, q_norm.reshape(1, -1), kv_norm.reshape(1, -1),
      cos_t, sin_t)


def _q_up_kernel(cq_ref, w_ref, cos_ref, sin_ref, q_ref):
    q = jnp.dot(cq_ref[0], w_ref[0], preferred_element_type=F32)
    qn = q[:, :NOPE] * ATTN_SCALE
    qr = _rope_tile(q[:, NOPE:], cos_ref[...], sin_ref[...]) * ATTN_SCALE
    q_ref[0, 0] = jnp.concatenate([qn, qr], axis=1).astype(BF16)


def _q_up(cq, w_uq_h, cos_t, sin_t, tm):
    b, t, r = cq.shape
    return pl.pallas_call(
        _q_up_kernel,
        grid=(b, t // tm, H_C),
        in_specs=[pl.BlockSpec((1, tm, r), lambda i, j, h: (i, j, 0)),
                  pl.BlockSpec((1, r, QK_DIM), lambda i, j, h: (h, 0, 0)),
                  pl.BlockSpec((tm, 128), lambda i, j, h: (j, 0)),
                  pl.BlockSpec((tm, 128), lambda i, j, h: (j, 0))],
        out_specs=pl.BlockSpec((1, 1, tm, QK_DIM), lambda i, j, h: (i, h, j, 0)),
        out_shape=jax.ShapeDtypeStruct((b, H_C, t, QK_DIM), BF16),
        compiler_params=_cparams(("parallel", "parallel", "arbitrary")),
        name="q_up",
    )(cq, w_uq_h, cos_t, sin_t)


def _kv_up_kernel(ckv_ref, kr_ref, w_ref, k_ref, v_ref):
    kv = jnp.dot(ckv_ref[0], w_ref[0], preferred_element_type=F32)
    k_ref[0, 0] = jnp.concatenate([kv[:, :NOPE].astype(BF16), kr_ref[0]], axis=1)
    v_ref[0, 0] = kv[:, NOPE:].astype(BF16)


def _kv_up(ckv, kr, w_ukv_h, tm):
    b, s, r = ckv.shape
    return pl.pallas_call(
        _kv_up_kernel,
        grid=(b, s // tm, H_C),
        in_specs=[pl.BlockSpec((1, tm, r), lambda i, j, h: (i, j, 0)),
                  pl.BlockSpec((1, tm, 128), lambda i, j, h: (i, j, 0)),
                  pl.BlockSpec((1, r, NOPE + VH), lambda i, j, h: (h, 0, 0))],
        out_specs=[pl.BlockSpec((1, 1, tm, QK_DIM), lambda i, j, h: (i, h, j, 0)),
                   pl.BlockSpec((1, 1, tm, VH), lambda i, j, h: (i, h, j, 0))],
        out_shape=[jax.ShapeDtypeStruct((b, H_C, s, QK_DIM), BF16),
                   jax.ShapeDtypeStruct((b, H_C, s, VH), BF16)],
        compiler_params=_cparams(("parallel", "parallel", "arbitrary")),
        name="kv_up",
    )(ckv, kr, w_ukv_h)


def _attn_kernel(q_ref, k_ref, v_ref, o_ref, m_scr, l_scr, acc_scr, *, tk, nk):
    q = q_ref[0, 0]
    m_scr[...] = jnp.full(m_scr.shape, -jnp.inf, F32)
    l_scr[...] = jnp.zeros(l_scr.shape, F32)
    acc_scr[...] = jnp.zeros(acc_scr.shape, F32)

    def body(j, carry):
        off = pl.multiple_of(j * tk, tk)
        ks = k_ref[0, 0, pl.ds(off, tk), :]
        vs = v_ref[0, 0, pl.ds(off, tk), :]
        s = lax.dot_general(q, ks, (((1,), (1,)), ((), ())), preferred_element_type=F32)
        m_old = m_scr[...]
        m_new = jnp.maximum(m_old, jnp.max(s, axis=-1, keepdims=True))
        alpha = jnp.exp(m_old - m_new)
        p = jnp.exp(s - m_new)
        l_scr[...] = alpha * l_scr[...] + jnp.sum(p, axis=-1, keepdims=True)
        acc_scr[...] = alpha * acc_scr[...] + jnp.dot(p.astype(BF16), vs,
                                                      preferred_element_type=F32)
        m_scr[...] = m_new
        return carry

    lax.fori_loop(0, nk, body, 0)
    o_ref[0] = (acc_scr[...] / l_scr[...]).astype(o_ref.dtype)


def _attention(q, k, v, tq, tk):
    b, h, t, _ = q.shape
    s = k.shape[2]
    assert t % tq == 0 and s % tk == 0
    kern = functools.partial(_attn_kernel, tk=tk, nk=s // tk)
    return pl.pallas_call(
        kern,
        grid=(b, h, t // tq),
        in_specs=[pl.BlockSpec((1, 1, tq, QK_DIM), lambda i, j, n: (i, j, n, 0)),
                  pl.BlockSpec((1, 1, s, QK_DIM), lambda i, j, n: (i, j, 0, 0)),
                  pl.BlockSpec((1, 1, s, VH), lambda i, j, n: (i, j, 0, 0))],
        out_specs=pl.BlockSpec((1, tq, VH), lambda i, j, n: (i, n, j)),
        out_shape=jax.ShapeDtypeStruct((b, t, h * VH), BF16),
        scratch_shapes=[pltpu.VMEM((tq, 1), F32), pltpu.VMEM((tq, 1), F32),
                        pltpu.VMEM((tq, VH), F32)],
        compiler_params=_cparams(("parallel", "parallel", "arbitrary")),
        name="attention",
    )(q, k, v)


def _router_kernel(x_ref, g_ref, sc_ref, sh_ref, w_ref, h_ref, lg_ref):
    hn = _modnorm(x_ref[0], g_ref[...], sc_ref[0], sh_ref[0])
    h_ref[0] = hn.astype(BF16)
    lg_ref[0] = jnp.dot(hn, w_ref[...], preferred_element_type=F32,
                        precision=lax.Precision.HIGHEST)


def _router(x, gain, scale, shift, w_router, tm):
    b, t, d = x.shape
    e = w_router.shape[1]
    return pl.pallas_call(
        _router_kernel,
        grid=(b, t // tm),
        in_specs=[pl.BlockSpec((1, tm, d), lambda i, j: (i, j, 0)),
                  pl.BlockSpec((1, d), lambda i, j: (0, 0)),
                  pl.BlockSpec((1, 1, d), lambda i, j: (i, 0, 0)),
                  pl.BlockSpec((1, 1, d), lambda i, j: (i, 0, 0)),
                  pl.BlockSpec((d, e), lambda i, j: (0, 0))],
        out_specs=[pl.BlockSpec((1, tm, d), lambda i, j: (i, j, 0)),
                   pl.BlockSpec((1, tm, e), lambda i, j: (i, j, 0))],
        out_shape=[jax.ShapeDtypeStruct((b, t, d), BF16),
                   jax.ShapeDtypeStruct((b, t, e), F32)],
        compiler_params=_cparams(("parallel", "parallel")),
        name="router",
    )(x, gain.reshape(1, d), scale, shift, w_router)


def _ffn_kernel(x_ref, wg_ref, wu_ref, wd_ref, g_ref, o_ref, *, fc):
    x = x_ref[0, 0]
    nf = wg_ref.shape[-1]
    for c in range(nf // fc):
        sl = slice(c * fc, (c + 1) * fc)
        hg = jnp.dot(x, wg_ref[0, :, sl], preferred_element_type=F32)
        hu = jnp.dot(x, wu_ref[0, :, sl], preferred_element_type=F32)
        hid = (_silu(hg) * hu).astype(BF16)
        part = jnp.dot(hid, wd_ref[0, sl, :], preferred_element_type=F32)
        if c == 0:
            o_ref[0, 0] = part
        else:
            o_ref[0, 0] += part
    o_ref[0, 0] = o_ref[0, 0] * g_ref[0, 0]


def _expert_ffn(xg, gates, w_gate, w_up, w_down, tm, fc):
    b, e, cap, d = xg.shape
    f = w_gate.shape[-1]
    tm = min(tm, cap)
    kern = functools.partial(_ffn_kernel, fc=fc)
    return pl.pallas_call(
        kern,
        grid=(b, e, cap // tm),
        in_specs=[pl.BlockSpec((1, 1, tm, d), lambda i, j, m: (i, j, m, 0)),
                  pl.BlockSpec((1, d, f), lambda i, j, m: (j, 0, 0)),
                  pl.BlockSpec((1, d, f), lambda i, j, m: (j, 0, 0)),
                  pl.BlockSpec((1, f, d), lambda i, j, m: (j, 0, 0)),
                  pl.BlockSpec((1, 1, tm, 1), lambda i, j, m: (i, j, m, 0))],
        out_specs=pl.BlockSpec((1, 1, tm, d), lambda i, j, m: (i, j, m, 0)),
        out_shape=jax.ShapeDtypeStruct((b, e, cap, d), F32),
        compiler_params=_cparams(("parallel", "parallel", "arbitrary")),
        name="expert_ffn",
    )(xg, w_gate, w_up, w_down, gates)


def _moe(x, gain, scale, shift, gate_out, w_router, w_gate, w_up, w_down):
    b, t, d = x.shape
    cap = CAP_FACTOR * t // N_EXPERTS
    hn, logits = _router(x, gain, scale, shift, w_router, min(256, t))
    aff = jax.nn.softmax(logits, axis=-1)
    g, idx = lax.top_k(jnp.swapaxes(aff, 1, 2), cap)
    bidx = jnp.arange(b)[:, None, None]
    xg = hn[bidx, idx]
    y = _expert_ffn(xg, g[..., None], w_gate, w_up, w_down, 256, 512)
    upd = jnp.zeros((b, t, d), F32).at[bidx, idx].add(y)
    return x + gate_out * upd


def _final_kernel(x_ref, g_ref, o_ref):
    o_ref[0] = _rms(x_ref[0], g_ref[...])


def _final_norm(x, gain, tm):
    b, t, d = x.shape
    return pl.pallas_call(
        _final_kernel,
        grid=(b, t // tm),
        in_specs=[pl.BlockSpec((1, tm, d), lambda i, j: (i, j, 0)),
                  pl.BlockSpec((1, d), lambda i, j: (0, 0))],
        out_specs=pl.BlockSpec((1, tm, d), lambda i, j: (i, j, 0)),
        out_shape=jax.ShapeDtypeStruct((b, t, d), F32),
        compiler_params=_cparams(("parallel", "parallel")),
        name="final_norm",
    )(x, gain.reshape(1, d))


def kernel(x, c, ctx, c_ctx, w_mod, b_mod, norm_mix, norm_ffn, norm_out, w_in, w_out_rec,
           conv_w, conv_b, lb_gamma, dt_bias, a_log, d_skip, hgrn_norm, mamba_norm,
           w_dq, q_norm, w_uq, w_dkv, kv_norm, w_ukv, w_kr, w_o,
           w_router, w_gate, w_up, w_down):
    B, T, D = x.shape
    TC = ctx.shape[1]
    lb = jnp.cumsum(jax.nn.softmax(lb_gamma.astype(F32), axis=0), axis=0)

    cvec = jnp.zeros((16, D), F32).at[:B].set(c).at[B].set(c_ctx)
    mods = _modulation(cvec, w_mod, b_mod)

    x_lat, x_ctx = x, ctx
    for l in range(DEPTH):
        need_ctx = l < DEPTH - 1
        m_lat = [mods[l, :B, i * D:(i + 1) * D][:, None, :] for i in range(6)]
        m_ctx = [jnp.broadcast_to(mods[l, B, i * D:(i + 1) * D][None, None, :], (B, 1, D))
                 for i in range(6)]
        if l % 2 == 0:
            e = l // 2
            w_in_p = jnp.pad(w_in[e], ((0, 0), (0, IN_COLS_PAD - IN_COLS))).astype(BF16)
            w_out_b = w_out_rec[e].astype(BF16)
            p_lat = _modproj(x_lat, norm_mix[l], m_lat[1], m_lat[0], w_in_p, 256, 1408)
            p_ctx = _modproj(x_ctx, norm_mix[l], m_ctx[1], m_ctx[0], w_in_p, 256, 1408)
            cp = _prepare(p_ctx, conv_w[e], conv_b[e], lb[e], dt_bias[e], a_log[e])
            lp = _prepare(p_lat, conv_w[e], conv_b[e], lb[e], dt_bias[e], a_log[e])
            s0_h = jnp.zeros((B, HA, DKA, DVA), F32)
            s0_m = jnp.zeros((B, N_GROUPS, HB // N_GROUPS, D_STATE, PB), F32)
            oh_c, oh_l = _bidir(_gla_chunked, cp[0], cp[1], lp[0], lp[1], s0_h)
            om_c, om_l = _bidir(_ssd_chunked, cp[2], cp[3], lp[2], lp[3], s0_m)
            cat_l = _merge_pre(oh_l, om_l, lp[4], d_skip[e], hgrn_norm[e], mamba_norm[e])
            x_lat = _linear_residual(x_lat, m_lat[2], cat_l, w_out_b, 512)
            if need_ctx:
                cat_c = _merge_pre(oh_c, om_c, cp[4], d_skip[e], hgrn_norm[e], mamba_norm[e])
                x_ctx = _linear_residual(x_ctx, m_ctx[2], cat_c, w_out_b, 256)
        else:
            j = l // 2
            cos_l, sin_l = _rope_tables(T)
            cos_c = jnp.concatenate([jnp.ones((TC, ROPE), F32), jnp.zeros((TC, ROPE), F32)], -1)
            sin_c = jnp.zeros((TC, 128), F32)
            w_cat = jnp.concatenate([w_dq[j], w_dkv[j], w_kr[j], _swap_cols(w_kr[j])],
                                    axis=1).astype(BF16)
            wq = w_uq[j].reshape(Q_LORA, H_C, NOPE + ROPE)
            wq_h = jnp.concatenate([wq, _swap_cols(wq[..., NOPE:])], axis=-1)
            wq_h = jnp.transpose(wq_h, (1, 0, 2)).astype(BF16)
            wkv_h = jnp.transpose(w_ukv[j].reshape(KV_LORA, H_C, NOPE + VH), (1, 0, 2)).astype(BF16)
            cq_l, ckv_l, kr_l = _mla_down(x_lat, norm_mix[l], m_lat[1], m_lat[0], w_cat,
                                          q_norm[j], kv_norm[j], cos_l, sin_l, 256)
            _, ckv_c, kr_c = _mla_down(x_ctx, norm_mix[l], m_ctx[1], m_ctx[0], w_cat,
                                       q_norm[j], kv_norm[j], cos_c, sin_c, 256)
            ckv = jnp.concatenate([ckv_c, ckv_l], axis=1)
            kr = jnp.concatenate([kr_c, kr_l], axis=1)
            qh = _q_up(cq_l, wq_h, cos_l, sin_l, 256)
            kh, vh = _kv_up(ckv, kr, wkv_h, 256)
            o = _attention(qh, kh, vh, 256, 768)
            x_lat = _linear_residual(x_lat, m_lat[2], o, w_o[j].astype(BF16), 512)
        x_lat = _moe(x_lat, norm_ffn[l], m_lat[4], m_lat[3], m_lat[5], w_router[l],
                     w_gate[l].astype(BF16), w_up[l].astype(BF16), w_down[l].astype(BF16))
        if need_ctx:
            x_ctx = _moe(x_ctx, norm_ffn[l], m_ctx[4], m_ctx[3], m_ctx[5], w_router[l],
                         w_gate[l].astype(BF16), w_up[l].astype(BF16), w_down[l].astype(BF16))
    return _final_norm(x_lat, norm_out, 512)
```

```python
import functools
import math

import jax
import jax.numpy as jnp
from jax import lax
from jax.experimental import pallas as pl
from jax.experimental.pallas import tpu as pltpu

F32 = jnp.float32
BF16 = jnp.bfloat16

D_MODEL = 1024
DEPTH = 2
GRID_W = 64
EPS = 1e-6
CHUNK = 64
HA, DKA, DVA = 8, 64, 64
WA = HA * DVA
HB, PB = 8, 64
D_INNER = HB * PB
N_GROUPS, D_STATE = 2, 128
CONV_W = 5
REC_SPLITS = (HA * DKA, HA * DKA, HA * DKA, WA, WA, D_INNER, D_INNER,
              N_GROUPS * D_STATE, N_GROUPS * D_STATE, HB, HB)
IN_COLS = sum(REC_SPLITS)
IN_COLS_PAD = 4224
H_C, NOPE, ROPE, VH = 8, 128, 64, 128
Q_LORA, KV_LORA = 384, 256
ROPE_THETA = 10000.0
ATTN_SCALE = 1.0 / math.sqrt(NOPE + ROPE)
QK_SCALE = ATTN_SCALE * math.log2(math.e)
QK_DIM = 256
N_EXPERTS = 16
EXPERT_FF = 1024
CAP_FACTOR = 2

VMEM_LIMIT = 56 * 1024 * 1024


def _cparams(sem):
    return pltpu.CompilerParams(dimension_semantics=sem, vmem_limit_bytes=VMEM_LIMIT)


def _silu(v):
    return v * jax.nn.sigmoid(v)


def _modnorm(x, gain, scale, shift):
    ms = jnp.mean(x * x, axis=-1, keepdims=True)
    return (x * lax.rsqrt(ms + EPS) * gain) * (1.0 + scale) + shift


def _mod_kernel(s_ref, w_ref, b_ref, o_ref):
    s = _silu(s_ref[...])
    o_ref[0] = jnp.dot(s, w_ref[0], preferred_element_type=F32,
                       precision=lax.Precision.HIGHEST) + b_ref[0]


def _modulation(cvec, w_mod, b_mod):
    n = w_mod.shape[-1]
    tn = 1536
    return pl.pallas_call(
        _mod_kernel,
        grid=(DEPTH, n // tn),
        in_specs=[pl.BlockSpec((16, D_MODEL), lambda l, j: (0, 0)),
                  pl.BlockSpec((1, D_MODEL, tn), lambda l, j: (l, 0, j)),
                  pl.BlockSpec((1, 1, tn), lambda l, j: (l, 0, j))],
        out_specs=pl.BlockSpec((1, 16, tn), lambda l, j: (l, 0, j)),
        out_shape=jax.ShapeDtypeStruct((DEPTH, 16, n), F32),
        compiler_params=_cparams(("parallel", "parallel")),
        name="modulation",
    )(cvec, w_mod, b_mod.reshape(DEPTH, 1, n))


def _modproj_kernel(x_ref, g_ref, sc_ref, sh_ref, w_ref, o_ref):
    a = _modnorm(x_ref[0], g_ref[...], sc_ref[0], sh_ref[0]).astype(BF16)
    o_ref[0] = jnp.dot(a, w_ref[...], preferred_element_type=F32).astype(o_ref.dtype)


def _modproj(x, gain, scale, shift, w, tm, out_dtype=F32):
    b, t, d = x.shape
    n = w.shape[1]
    return pl.pallas_call(
        _modproj_kernel,
        grid=(b, t // tm),
        in_specs=[pl.BlockSpec((1, tm, d), lambda i, j: (i, j, 0)),
                  pl.BlockSpec((1, d), lambda i, j: (0, 0)),
                  pl.BlockSpec((1, 1, d), lambda i, j: (i, 0, 0)),
                  pl.BlockSpec((1, 1, d), lambda i, j: (i, 0, 0)),
                  pl.BlockSpec((d, n), lambda i, j: (0, 0))],
        out_specs=pl.BlockSpec((1, tm, n), lambda i, j: (i, j, 0)),
        out_shape=jax.ShapeDtypeStruct((b, t, n), out_dtype),
        compiler_params=_cparams(("parallel", "parallel")),
        name="modproj",
    )(x, gain.reshape(1, d), scale, shift, w)


def _linres_kernel(x_ref, gate_ref, y_ref, w_ref, o_ref):
    acc = jnp.dot(y_ref[0].astype(BF16), w_ref[...], preferred_element_type=F32)
    o_ref[0] = x_ref[0] + gate_ref[0] * acc


def _linear_residual(x, gate, y, w, tm):
    b, t, d = x.shape
    k = y.shape[-1]
    return pl.pallas_call(
        _linres_kernel,
        grid=(b, t // tm),
        in_specs=[pl.BlockSpec((1, tm, d), lambda i, j: (i, j, 0)),
                  pl.BlockSpec((1, 1, d), lambda i, j: (i, 0, 0)),
                  pl.BlockSpec((1, tm, k), lambda i, j: (i, j, 0)),
                  pl.BlockSpec((k, d), lambda i, j: (0, 0))],
        out_specs=pl.BlockSpec((1, tm, d), lambda i, j: (i, j, 0)),
        out_shape=jax.ShapeDtypeStruct((b, t, d), F32),
        compiler_params=_cparams(("parallel", "parallel")),
        name="linear_residual",
    )(x, gate, y, w)


def _split_cols(p, sizes):
    out, off = [], 0
    for s in sizes:
        out.append(p[..., off:off + s])
        off += s
    return out


def _gla_chunked(q, k, v, log_a, s0):
    B, T, H, K = q.shape
    n = T // CHUNK
    r = lambda a: a.reshape(B, n, CHUNK, H, a.shape[-1])
    q, k, v, log_a = r(q), r(k), r(v), r(log_a)
    b = jnp.cumsum(log_a, axis=2)
    b_ref = b[:, :, CHUNK // 2:CHUNK // 2 + 1]
    b_last = b[:, :, -1:]
    att = jnp.einsum('bnthk,bnshk->bnhts', q * jnp.exp(b - b_ref), k * jnp.exp(b_ref - b))
    mask = jnp.tril(jnp.ones((CHUNK, CHUNK), dtype=bool))
    att = jnp.where(mask, att, 0.0)
    o = jnp.einsum('bnhts,bnshv->bnthv', att, v)
    u = jnp.einsum('bnshk,bnshv->nbhkv', k * jnp.exp(b_last - b), v)
    dec = jnp.exp(jnp.moveaxis(b_last[:, :, 0], 1, 0))

    def step(s, xs):
        d, uu = xs
        return d[..., None] * s + uu, s

    s_fin, s_in = lax.scan(step, s0, (dec, u))
    o = o + jnp.einsum('bnthk,nbhkv->bnthv', q * jnp.exp(b), s_in)
    return o.reshape(B, T, H, v.shape[-1]), s_fin


def _ssd_chunked(x, dt, la, bm, cm, s0):
    Bsz, T, H, P = x.shape
    G, N = bm.shape[-2], bm.shape[-1]
    hpg = H // G
    n = T // CHUNK
    xdt = (x * dt[..., None]).reshape(Bsz, n, CHUNK, G, hpg, P)
    la = la.reshape(Bsz, n, CHUNK, G, hpg)
    bm = bm.reshape(Bsz, n, CHUNK, G, N)
    cm = cm.reshape(Bsz, n, CHUNK, G, N)
    cum = jnp.cumsum(la, axis=2)
    cum_h = jnp.moveaxis(cum, 2, -1)
    diff = cum_h[..., :, None] - cum_h[..., None, :]
    mask = jnp.tril(jnp.ones((CHUNK, CHUNK), dtype=bool))
    lmat = jnp.exp(jnp.where(mask, diff, -jnp.inf))
    cb = jnp.einsum('bntgd,bnsgd->bngts', cm, bm)
    y = jnp.einsum('bnghts,bnsghp->bntghp', cb[:, :, :, None] * lmat, xdt)
    cum_last = cum[:, :, -1]
    w_end = jnp.exp(cum_last[:, :, None] - cum)
    u = jnp.einsum('bnsgd,bnsgh,bnsghp->nbghdp', bm, w_end, xdt)
    dec = jnp.exp(jnp.moveaxis(cum_last, 1, 0))

    def step(s, xs):
        d, uu = xs
        return d[..., None, None] * s + uu, s

    s_fin, s_in = lax.scan(step, s0, (dec, u))
    y = y + jnp.einsum('bntgd,bntgh,nbghdp->bntghp', cm, jnp.exp(cum), s_in)
    return y.reshape(Bsz, T, H, P), s_fin


def _bidir(scan, ctx_f, ctx_b, lat_f, lat_b, s0):
    flip = lambda args: tuple(jnp.flip(a, axis=1) for a in args)
    oc_f, sc_f = scan(*ctx_f, s0)
    ol_f, _ = scan(*lat_f, sc_f)
    oc_b, sc_b = scan(*flip(ctx_b), s0)
    ol_b, _ = scan(*flip(lat_b), sc_b)
    return oc_f + jnp.flip(oc_b, axis=1), ol_f + jnp.flip(ol_b, axis=1)


def _dwconv_silu(u, w, b):
    C = u.shape[-1]
    y = lax.conv_general_dilated(u, w[:, None, :], window_strides=(1,),
                                 padding=[(CONV_W // 2, CONV_W // 2)],
                                 dimension_numbers=('NWC', 'WIO', 'NWC'), feature_group_count=C)
    return jax.nn.silu(y + b)


def _prepare(p, conv_w, conv_b, lb, dt_bias, a_log):
    B, T, _ = p.shape
    q, ff, fb, i, g, z, xs, bs, cs, dtf, dtb = _split_cols(p[..., :IN_COLS], REC_SPLITS)
    heads_k = lambda a: a.reshape(B, T, HA, DKA)
    q = heads_k(jax.nn.silu(q))
    v = i.reshape(B, T, HA, DVA)

    def forget(fr, lbd):
        f = lbd + (1.0 - lbd) * jax.nn.sigmoid(fr)
        return heads_k(1.0 - f), heads_k(jnp.log(f))

    kf, laf = forget(ff, lb[0])
    kb, lab = forget(fb, lb[1])
    xbc = _dwconv_silu(jnp.concatenate([xs, bs, cs], axis=-1), conv_w, conv_b)
    xm, bm, cm = _split_cols(xbc, (D_INNER, N_GROUPS * D_STATE, N_GROUPS * D_STATE))
    xm = xm.reshape(B, T, HB, PB)
    bm = bm.reshape(B, T, N_GROUPS, D_STATE)
    cm = cm.reshape(B, T, N_GROUPS, D_STATE)

    def step_size(dr, bias, alog):
        dt = jax.nn.softplus(dr + bias)
        return dt, -dt * jnp.exp(alog)

    dt_f, la_f = step_size(dtf, dt_bias[0], a_log[0])
    dt_b, la_b = step_size(dtb, dt_bias[1], a_log[1])
    return ((q, kf, v, laf), (q, kb, v, lab),
            (xm, dt_f, la_f, bm, cm), (xm, dt_b, la_b, bm, cm), (g, z, xm))


def _rms(x, gain):
    return x * lax.rsqrt(jnp.mean(x * x, axis=-1, keepdims=True) + EPS) * gain


def _merge_pre(oh, om, extras, d_skip, hgrn_norm, mamba_norm):
    g, z, xm = extras
    B, T = g.shape[:2]
    oh = _rms(oh, hgrn_norm.reshape(HA, DVA)).reshape(B, T, WA) * jax.nn.sigmoid(g)
    y = (om + d_skip[:, None] * xm).reshape(B, T, D_INNER)
    y = _rms(y * jax.nn.silu(z), mamba_norm)
    return jnp.concatenate([oh, y], axis=-1)


def _rope_tables(t):
    rows = t // GRID_W
    row = jnp.repeat(jnp.arange(rows, dtype=jnp.int32), GRID_W)
    col = jnp.tile(jnp.arange(GRID_W, dtype=jnp.int32), rows)
    nf = ROPE // 4
    inv_freq = ROPE_THETA ** (-jnp.arange(nf, dtype=F32) / nf)
    pos = jnp.stack([row, col], axis=-1).astype(F32)
    ang = pos[..., None] * inv_freq
    cos, sin = jnp.cos(ang), jnp.sin(ang)
    cos64 = jnp.broadcast_to(cos[:, :, None, :], (t, 2, 2, nf)).reshape(t, ROPE)
    sin64 = jnp.broadcast_to(sin[:, :, None, :], (t, 2, 2, nf)).reshape(t, ROPE)
    zero = jnp.zeros((t, ROPE), F32)
    return jnp.concatenate([cos64, zero], -1), jnp.concatenate([sin64, zero], -1)


def _swap_cols(w):
    nf = ROPE // 4
    wr = w.reshape(w.shape[:-1] + (2, 2, nf))
    return jnp.stack([-wr[..., 1, :], wr[..., 0, :]], axis=-2).reshape(w.shape)


def _rope_tile(tile, cos_t, sin_t):
    return tile * cos_t + pltpu.roll(tile, ROPE, axis=1) * sin_t


def _mla_down_kernel(x_ref, g_ref, sc_ref, sh_ref, w_ref, qn_ref, kvn_ref, cos_ref, sin_ref,
                     cq_ref, ckv_ref, kr_ref):
    a = _modnorm(x_ref[0], g_ref[...], sc_ref[0], sh_ref[0]).astype(BF16)
    c = jnp.dot(a, w_ref[...], preferred_element_type=F32)
    cq = c[:, :Q_LORA]
    ckv = c[:, Q_LORA:Q_LORA + KV_LORA]
    cq_ref[0] = _rms(cq, qn_ref[...]).astype(BF16)
    ckv_ref[0] = _rms(ckv, kvn_ref[...]).astype(BF16)
    kr_ref[0] = _rope_tile(c[:, Q_LORA + KV_LORA:], cos_ref[...], sin_ref[...]).astype(BF16)


def _mla_down(x, gain, scale, shift, w_cat, q_norm, kv_norm, cos_t, sin_t, tm):
    b, t, d = x.shape
    n = w_cat.shape[1]
    return pl.pallas_call(
        _mla_down_kernel,
        grid=(b, t // tm),
        in_specs=[pl.BlockSpec((1, tm, d), lambda i, j: (i, j, 0)),
                  pl.BlockSpec((1, d), lambda i, j: (0, 0)),
                  pl.BlockSpec((1, 1, d), lambda i, j: (i, 0, 0)),
                  pl.BlockSpec((1, 1, d), lambda i, j: (i, 0, 0)),
                  pl.BlockSpec((d, n), lambda i, j: (0, 0)),
                  pl.BlockSpec((1, Q_LORA), lambda i, j: (0, 0)),
                  pl.BlockSpec((1, KV_LORA), lambda i, j: (0, 0)),
                  pl.BlockSpec((tm, 128), lambda i, j: (j, 0)),
                  pl.BlockSpec((tm, 128), lambda i, j: (j, 0))],
        out_specs=[pl.BlockSpec((1, tm, Q_LORA), lambda i, j: (i, j, 0)),
                   pl.BlockSpec((1, tm, KV_LORA), lambda i, j: (i, j, 0)),
                   pl.BlockSpec((1, tm, 128), lambda i, j: (i, j, 0))],
        out_shape=[jax.ShapeDtypeStruct((b, t, Q_LORA), BF16),
                   jax.ShapeDtypeStruct((b, t, KV_LORA), BF16),
                   jax.ShapeDtypeStruct((b, t, 128), BF16)],
        compiler_params=_cparams(("parallel", "parallel")),
        name="mla_down",
    )(x, gain.reshape(1, d), scale, shift, w_cat, q_norm.reshape(1, -1), kv_norm.reshape(1, -1),
      cos_t, sin_t)


def _q_up_kernel(cq_ref, w_ref, cos_ref, sin_ref, q_ref):
    cq = cq_ref[0]
    for h in range(H_C):
        q = jnp.dot(cq, w_ref[h], preferred_element_type=F32)
        qn = q[:, :NOPE] * QK_SCALE
        qr = _rope_tile(q[:, NOPE:], cos_ref[...], sin_ref[...]) * QK_SCALE
        q_ref[0, h] = jnp.concatenate([qn, qr], axis=1).astype(BF16)


def _q_up(cq, w_uq_h, cos_t, sin_t, tm):
    b, t, r = cq.shape
    return pl.pallas_call(
        _q_up_kernel,
        grid=(b, t // tm),
        in_specs=[pl.BlockSpec((1, tm, r), lambda i, j: (i, j, 0)),
                  pl.BlockSpec((H_C, r, QK_DIM), lambda i, j: (0, 0, 0)),
                  pl.BlockSpec((tm, 128), lambda i, j: (j, 0)),
                  pl.BlockSpec((tm, 128), lambda i, j: (j, 0))],
        out_specs=pl.BlockSpec((1, H_C, tm, QK_DIM), lambda i, j: (i, 0, j, 0)),
        out_shape=jax.ShapeDtypeStruct((b, H_C, t, QK_DIM), BF16),
        compiler_params=_cparams(("parallel", "parallel")),
        name="q_up",
    )(cq, w_uq_h, cos_t, sin_t)


def _kv_up_kernel(ckv_ref, kr_ref, w_ref, k_ref, v_ref):
    ckv = ckv_ref[0]
    kr = kr_ref[0]
    for h in range(H_C):
        kv = jnp.dot(ckv, w_ref[h], preferred_element_type=F32)
        k_ref[0, h] = jnp.concatenate([kv[:, :NOPE].astype(BF16), kr], axis=1)
        v_ref[0, h] = kv[:, NOPE:].astype(BF16)


def _kv_up(ckv, kr, w_ukv_h, tm):
    b, s, r = ckv.shape
    return pl.pallas_call(
        _kv_up_kernel,
        grid=(b, s // tm),
        in_specs=[pl.BlockSpec((1, tm, r), lambda i, j: (i, j, 0)),
                  pl.BlockSpec((1, tm, 128), lambda i, j: (i, j, 0)),
                  pl.BlockSpec((H_C, r, NOPE + VH), lambda i, j: (0, 0, 0))],
        out_specs=[pl.BlockSpec((1, H_C, tm, QK_DIM), lambda i, j: (i, 0, j, 0)),
                   pl.BlockSpec((1, H_C, tm, VH), lambda i, j: (i, 0, j, 0))],
        out_shape=[jax.ShapeDtypeStruct((b, H_C, s, QK_DIM), BF16),
                   jax.ShapeDtypeStruct((b, H_C, s, VH), BF16)],
        compiler_params=_cparams(("parallel", "parallel")),
        name="kv_up",
    )(ckv, kr, w_ukv_h)


def _attn_kernel(q_ref, k_ref, v_ref, o_ref, s_scr, *, tk, nk):
    q = q_ref[0, 0]
    tq = q.shape[0]

    def pass_a(j, mrun):
        off = pl.multiple_of(j * tk, tk)
        ks = k_ref[0, 0, pl.ds(off, tk), :]
        s = lax.dot_general(q, ks, (((1,), (1,)), ((), ())), preferred_element_type=F32)
        s_scr[:, pl.ds(off, tk)] = s
        for c in range(tk // 128):
            mrun = jnp.maximum(mrun, s[:, c * 128:(c + 1) * 128])
        return mrun

    mrun = lax.fori_loop(0, nk, pass_a, jnp.full((tq, 128), -jnp.inf, F32), unroll=True)
    m = jnp.max(mrun, axis=-1, keepdims=True)

    def pass_b(j, carry):
        lrun, acc = carry
        off = pl.multiple_of(j * tk, tk)
        vs = v_ref[0, 0, pl.ds(off, tk), :]
        p = jnp.exp2(s_scr[:, pl.ds(off, tk)] - m)
        for c in range(tk // 128):
            lrun = lrun + p[:, c * 128:(c + 1) * 128]
        acc = acc + jnp.dot(p.astype(BF16), vs, preferred_element_type=F32)
        return lrun, acc

    lrun, acc = lax.fori_loop(0, nk, pass_b,
                              (jnp.zeros((tq, 128), F32), jnp.zeros((tq, VH), F32)), unroll=True)
    l = jnp.sum(lrun, axis=-1, keepdims=True)
    o_ref[0] = (acc / l).astype(o_ref.dtype)


def _attention(q, k, v, tq, tk):
    b, h, t, _ = q.shape
    s = k.shape[2]
    assert t % tq == 0 and s % tk == 0
    kern = functools.partial(_attn_kernel, tk=tk, nk=s // tk)
    return pl.pallas_call(
        kern,
        grid=(b, h, t // tq),
        in_specs=[pl.BlockSpec((1, 1, tq, QK_DIM), lambda i, j, n: (i, j, n, 0)),
                  pl.BlockSpec((1, 1, s, QK_DIM), lambda i, j, n: (i, j, 0, 0)),
                  pl.BlockSpec((1, 1, s, VH), lambda i, j, n: (i, j, 0, 0))],
        out_specs=pl.BlockSpec((1, tq, VH), lambda i, j, n: (i, n, j)),
        out_shape=jax.ShapeDtypeStruct((b, t, h * VH), BF16),
        scratch_shapes=[pltpu.VMEM((tq, s), F32)],
        compiler_params=_cparams(("parallel", "parallel", "arbitrary")),
        name="attention",
    )(q, k, v)


def _router_kernel(x_ref, g_ref, sc_ref, sh_ref, w_ref, h_ref, lg_ref):
    hn = _modnorm(x_ref[0], g_ref[...], sc_ref[0], sh_ref[0])
    h_ref[0] = hn.astype(BF16)
    lg_ref[0] = jnp.dot(hn, w_ref[...], preferred_element_type=F32,
                        precision=lax.Precision.HIGHEST)


def _router(x, gain, scale, shift, w_router, tm):
    b, t, d = x.shape
    e = w_router.shape[1]
    return pl.pallas_call(
        _router_kernel,
        grid=(b, t // tm),
        in_specs=[pl.BlockSpec((1, tm, d), lambda i, j: (i, j, 0)),
                  pl.BlockSpec((1, d), lambda i, j: (0, 0)),
                  pl.BlockSpec((1, 1, d), lambda i, j: (i, 0, 0)),
                  pl.BlockSpec((1, 1, d), lambda i, j: (i, 0, 0)),
                  pl.BlockSpec((d, e), lambda i, j: (0, 0))],
        out_specs=[pl.BlockSpec((1, tm, d), lambda i, j: (i, j, 0)),
                   pl.BlockSpec((1, tm, e), lambda i, j: (i, j, 0))],
        out_shape=[jax.ShapeDtypeStruct((b, t, d), BF16),
                   jax.ShapeDtypeStruct((b, t, e), F32)],
        compiler_params=_cparams(("parallel", "parallel")),
        name="router",
    )(x, gain.reshape(1, d), scale, shift, w_router)


def _ffn_kernel(x_ref, wg_ref, wu_ref, wd_ref, g_ref, o_ref, *, fc):
    x = x_ref[0, 0]
    nf = wg_ref.shape[-1]
    for c in range(nf // fc):
        sl = slice(c * fc, (c + 1) * fc)
        hg = jnp.dot(x, wg_ref[0, :, sl], preferred_element_type=F32)
        hu = jnp.dot(x, wu_ref[0, :, sl], preferred_element_type=F32)
        hid = (_silu(hg) * hu).astype(BF16)
        part = jnp.dot(hid, wd_ref[0, sl, :], preferred_element_type=F32)
        if c == 0:
            o_ref[0, 0] = part
        else:
            o_ref[0, 0] += part
    o_ref[0, 0] = o_ref[0, 0] * g_ref[0, 0]


def _expert_ffn(xg, gates, w_gate, w_up, w_down, tm, fc):
    b, e, cap, d = xg.shape
    f = w_gate.shape[-1]
    tm = min(tm, cap)
    kern = functools.partial(_ffn_kernel, fc=fc)
    return pl.pallas_call(
        kern,
        grid=(b, e, cap // tm),
        in_specs=[pl.BlockSpec((1, 1, tm, d), lambda i, j, m: (i, j, m, 0)),
                  pl.BlockSpec((1, d, f), lambda i, j, m: (j, 0, 0)),
                  pl.BlockSpec((1, d, f), lambda i, j, m: (j, 0, 0)),
                  pl.BlockSpec((1, f, d), lambda i, j, m: (j, 0, 0)),
                  pl.BlockSpec((1, 1, tm, 1), lambda i, j, m: (i, j, m, 0))],
        out_specs=pl.BlockSpec((1, 1, tm, d), lambda i, j, m: (i, j, m, 0)),
        out_shape=jax.ShapeDtypeStruct((b, e, cap, d), F32),
        compiler_params=_cparams(("parallel", "parallel", "arbitrary")),
        name="expert_ffn",
    )(xg, w_gate, w_up, w_down, gates)


def _moe(x, gain, scale, shift, gate_out, w_router, w_gate, w_up, w_down):
    b, t, d = x.shape
    cap = CAP_FACTOR * t // N_EXPERTS
    hn, logits = _router(x, gain, scale, shift, w_router, min(256, t))
    aff = jax.nn.softmax(logits, axis=-1)
    g, idx = lax.top_k(jnp.swapaxes(aff, 1, 2), cap)
    bidx = jnp.arange(b)[:, None, None]
    xg = hn[bidx, idx]
    y = _expert_ffn(xg, g[..., None], w_gate, w_up, w_down, 256, 512)
    upd = jnp.zeros((b, t, d), F32).at[bidx, idx].add(y)
    return x + gate_out * upd


def _final_kernel(x_ref, g_ref, o_ref):
    o_ref[0] = _rms(x_ref[0], g_ref[...])


def _final_norm(x, gain, tm):
    b, t, d = x.shape
    return pl.pallas_call(
        _final_kernel,
        grid=(b, t // tm),
        in_specs=[pl.BlockSpec((1, tm, d), lambda i, j: (i, j, 0)),
                  pl.BlockSpec((1, d), lambda i, j: (0, 0))],
        out_specs=pl.BlockSpec((1, tm, d), lambda i, j: (i, j, 0)),
        out_shape=jax.ShapeDtypeStruct((b, t, d), F32),
        compiler_params=_cparams(("parallel", "parallel")),
        name="final_norm",
    )(x, gain.reshape(1, d))


def kernel(x, c, ctx, c_ctx, w_mod, b_mod, norm_mix, norm_ffn, norm_out, w_in, w_out_rec,
           conv_w, conv_b, lb_gamma, dt_bias, a_log, d_skip, hgrn_norm, mamba_norm,
           w_dq, q_norm, w_uq, w_dkv, kv_norm, w_ukv, w_kr, w_o,
           w_router, w_gate, w_up, w_down):
    B, T, D = x.shape
    TC = ctx.shape[1]
    lb = jnp.cumsum(jax.nn.softmax(lb_gamma.astype(F32), axis=0), axis=0)

    cvec = jnp.zeros((16, D), F32).at[:B].set(c).at[B].set(c_ctx)
    mods = _modulation(cvec, w_mod, b_mod)

    x_lat, x_ctx = x, ctx
    for l in range(DEPTH):
        need_ctx = l < DEPTH - 1
        m_lat = [mods[l, :B, i * D:(i + 1) * D][:, None, :] for i in range(6)]
        m_ctx = [jnp.broadcast_to(mods[l, B, i * D:(i + 1) * D][None, None, :], (B, 1, D))
                 for i in range(6)]
        if l % 2 == 0:
            e = l // 2
            w_in_p = jnp.pad(w_in[e], ((0, 0), (0, IN_COLS_PAD - IN_COLS))).astype(BF16)
            w_out_b = w_out_rec[e].astype(BF16)
            p_lat = _modproj(x_lat, norm_mix[l], m_lat[1], m_lat[0], w_in_p, 512)
            p_ctx = _modproj(x_ctx, norm_mix[l], m_ctx[1], m_ctx[0], w_in_p, 256)
            cp = _prepare(p_ctx, conv_w[e], conv_b[e], lb[e], dt_bias[e], a_log[e])
            lp = _prepare(p_lat, conv_w[e], conv_b[e], lb[e], dt_bias[e], a_log[e])
            s0_h = jnp.zeros((B, HA, DKA, DVA), F32)
            s0_m = jnp.zeros((B, N_GROUPS, HB // N_GROUPS, D_STATE, PB), F32)
            oh_c, oh_l = _bidir(_gla_chunked, cp[0], cp[1], lp[0], lp[1], s0_h)
            om_c, om_l = _bidir(_ssd_chunked, cp[2], cp[3], lp[2], lp[3], s0_m)
            cat_l = _merge_pre(oh_l, om_l, lp[4], d_skip[e], hgrn_norm[e], mamba_norm[e])
            x_lat = _linear_residual(x_lat, m_lat[2], cat_l, w_out_b, 512)
            if need_ctx:
                cat_c = _merge_pre(oh_c, om_c, cp[4], d_skip[e], hgrn_norm[e], mamba_norm[e])
                x_ctx = _linear_residual(x_ctx, m_ctx[2], cat_c, w_out_b, 256)
        else:
            j = l // 2
            cos_l, sin_l = _rope_tables(T)
            cos_c = jnp.concatenate([jnp.ones((TC, ROPE), F32), jnp.zeros((TC, ROPE), F32)], -1)
            sin_c = jnp.zeros((TC, 128), F32)
            w_cat = jnp.concatenate([w_dq[j], w_dkv[j], w_kr[j], _swap_cols(w_kr[j])],
                                    axis=1).astype(BF16)
            wq = w_uq[j].reshape(Q_LORA, H_C, NOPE + ROPE)
            wq_h = jnp.concatenate([wq, _swap_cols(wq[..., NOPE:])], axis=-1)
            wq_h = jnp.transpose(wq_h, (1, 0, 2)).astype(BF16)
            wkv_h = jnp.transpose(w_ukv[j].reshape(KV_LORA, H_C, NOPE + VH), (1, 0, 2)).astype(BF16)
            cq_l, ckv_l, kr_l = _mla_down(x_lat, norm_mix[l], m_lat[1], m_lat[0], w_cat,
                                          q_norm[j], kv_norm[j], cos_l, sin_l, 256)
            _, ckv_c, kr_c = _mla_down(x_ctx, norm_mix[l], m_ctx[1], m_ctx[0], w_cat,
                                       q_norm[j], kv_norm[j], cos_c, sin_c, 256)
            ckv = jnp.concatenate([ckv_c, ckv_l], axis=1)
            kr = jnp.concatenate([kr_c, kr_l], axis=1)
            qh = _q_up(cq_l, wq_h, cos_l, sin_l, 512)
            kh, vh = _kv_up(ckv, kr, wkv_h, 768)
            o = _attention(qh, kh, vh, 256, 768)
            x_lat = _linear_residual(x_lat, m_lat[2], o, w_o[j].astype(BF16), 512)
        x_lat = _moe(x_lat, norm_ffn[l], m_lat[4], m_lat[3], m_lat[5], w_router[l],
                     w_gate[l].astype(BF16), w_up[l].astype(BF16), w_down[l].astype(BF16))
        if need_ctx:
            x_ctx = _moe(x_ctx, norm_ffn[l], m_ctx[4], m_ctx[3], m_ctx[5], w_router[l],
                         w_gate[l].astype(BF16), w_up[l].astype(BF16), w_down[l].astype(BF16))
    return _final_norm(x_lat, norm_out, 512)
```

```python
import functools
import math

import jax
import jax.numpy as jnp
from jax import lax
from jax.experimental import pallas as pl
from jax.experimental.pallas import tpu as pltpu

F32 = jnp.float32
BF16 = jnp.bfloat16

D_MODEL = 1024
DEPTH = 2
GRID_W = 64
EPS = 1e-6
CHUNK = 64
HA, DKA, DVA = 8, 64, 64
WA = HA * DVA
HB, PB = 8, 64
D_INNER = HB * PB
N_GROUPS, D_STATE = 2, 128
CONV_W = 5
REC_SPLITS = (HA * DKA, HA * DKA, HA * DKA, WA, WA, D_INNER, D_INNER,
              N_GROUPS * D_STATE, N_GROUPS * D_STATE, HB, HB)
IN_COLS = sum(REC_SPLITS)
IN_COLS_PAD = 4224
H_C, NOPE, ROPE, VH = 8, 128, 64, 128
Q_LORA, KV_LORA = 384, 256
ROPE_THETA = 10000.0
ATTN_SCALE = 1.0 / math.sqrt(NOPE + ROPE)
QK_SCALE = ATTN_SCALE * math.log2(math.e)
QK_DIM = 256
N_EXPERTS = 16
EXPERT_FF = 1024
CAP_FACTOR = 2

VMEM_LIMIT = 56 * 1024 * 1024


def _cparams(sem):
    return pltpu.CompilerParams(dimension_semantics=sem, vmem_limit_bytes=VMEM_LIMIT)


def _silu(v):
    return v * jax.nn.sigmoid(v)


def _modnorm(x, gain, scale, shift):
    ms = jnp.mean(x * x, axis=-1, keepdims=True)
    return (x * lax.rsqrt(ms + EPS) * gain) * (1.0 + scale) + shift


def _mod_kernel(s_ref, w_ref, b_ref, o_ref):
    s = _silu(s_ref[...])
    o_ref[0] = jnp.dot(s, w_ref[0], preferred_element_type=F32,
                       precision=lax.Precision.HIGHEST) + b_ref[0]


def _modulation(cvec, w_mod, b_mod):
    n = w_mod.shape[-1]
    tn = 1536
    return pl.pallas_call(
        _mod_kernel,
        grid=(DEPTH, n // tn),
        in_specs=[pl.BlockSpec((16, D_MODEL), lambda l, j: (0, 0)),
                  pl.BlockSpec((1, D_MODEL, tn), lambda l, j: (l, 0, j)),
                  pl.BlockSpec((1, 1, tn), lambda l, j: (l, 0, j))],
        out_specs=pl.BlockSpec((1, 16, tn), lambda l, j: (l, 0, j)),
        out_shape=jax.ShapeDtypeStruct((DEPTH, 16, n), F32),
        compiler_params=_cparams(("parallel", "parallel")),
        name="modulation",
    )(cvec, w_mod, b_mod.reshape(DEPTH, 1, n))


def _modproj_kernel(x_ref, g_ref, sc_ref, sh_ref, w_ref, o_ref):
    a = _modnorm(x_ref[0], g_ref[...], sc_ref[0, 0], sh_ref[0, 0]).astype(BF16)
    o_ref[0] = jnp.dot(a, w_ref[...], preferred_element_type=F32).astype(o_ref.dtype)


def _modproj(x, gain, scale2, shift2, w, tm, nct_tiles, out_dtype=F32):
    b, t, d = x.shape
    n = w.shape[1]
    sel = lambda i, j: (i, jnp.where(j < nct_tiles, 0, 1), 0, 0)
    return pl.pallas_call(
        _modproj_kernel,
        grid=(b, t // tm),
        in_specs=[pl.BlockSpec((1, tm, d), lambda i, j: (i, j, 0)),
                  pl.BlockSpec((1, d), lambda i, j: (0, 0)),
                  pl.BlockSpec((1, 1, 1, d), sel),
                  pl.BlockSpec((1, 1, 1, d), sel),
                  pl.BlockSpec((d, n), lambda i, j: (0, 0))],
        out_specs=pl.BlockSpec((1, tm, n), lambda i, j: (i, j, 0)),
        out_shape=jax.ShapeDtypeStruct((b, t, n), out_dtype),
        compiler_params=_cparams(("parallel", "parallel")),
        name="modproj",
    )(x, gain.reshape(1, d), scale2.reshape(b, 2, 1, d), shift2.reshape(b, 2, 1, d), w)


def _linres_kernel(x_ref, gate_ref, y_ref, w_ref, o_ref):
    acc = jnp.dot(y_ref[0].astype(BF16), w_ref[...], preferred_element_type=F32)
    o_ref[0] = x_ref[0] + gate_ref[0] * acc


def _linear_residual(x, gate, y, w, tm):
    b, t, d = x.shape
    k = y.shape[-1]
    return pl.pallas_call(
        _linres_kernel,
        grid=(b, t // tm),
        in_specs=[pl.BlockSpec((1, tm, d), lambda i, j: (i, j, 0)),
                  pl.BlockSpec((1, 1, d), lambda i, j: (i, 0, 0)),
                  pl.BlockSpec((1, tm, k), lambda i, j: (i, j, 0)),
                  pl.BlockSpec((k, d), lambda i, j: (0, 0))],
        out_specs=pl.BlockSpec((1, tm, d), lambda i, j: (i, j, 0)),
        out_shape=jax.ShapeDtypeStruct((b, t, d), F32),
        compiler_params=_cparams(("parallel", "parallel")),
        name="linear_residual",
    )(x, gate, y, w)


L = CHUNK
COL_Q, COL_FF, COL_FB, COL_I, COL_G, COL_Z, COL_X, COL_DT = 0, 512, 1024, 1536, 2048, 2560, 3072, 4096


def _rms(x, gain):
    return x * lax.rsqrt(jnp.mean(x * x, axis=-1, keepdims=True) + EPS) * gain


def _split3(x):
    a = x.astype(BF16)
    r = x - a.astype(F32)
    b = r.astype(BF16)
    c = (r - b.astype(F32)).astype(BF16)
    return a, b, c


def _cumsum_exact(tri3, x):
    a, b, c = _split3(x)
    return jnp.dot(tri3, jnp.concatenate([a, b, c], axis=0), preferred_element_type=F32)


def _expand_exact(x, e3):
    a, b, c = _split3(x)
    return jnp.dot(jnp.concatenate([a, b, c], axis=1), e3, preferred_element_type=F32)


def _dot_nt(a, b):
    return lax.dot_general(a, b, (((1,), (1,)), ((), ())), preferred_element_type=F32)


def _dot_tn(a, b):
    return lax.dot_general(a, b, (((0,), (0,)), ((), ())), preferred_element_type=F32)


def _scan_dir(rev, q_ref, f_ref, v_ref, x_ref, xp_ref, xn_ref, dt_ref, pv, nv,
              lb_row, cw_ref, cbias_ref, dtb_ref, aexp_ref, dskip_ref,
              oh_ref, om_ref, sg_ref, sm_ref, u_scr):
    d = 1 if rev else 0
    ti = lax.broadcasted_iota(jnp.int32, (L, L), 0)
    si = lax.broadcasted_iota(jnp.int32, (L, L), 1)
    keep = (si >= ti) if rev else (si <= ti)
    tri = keep.astype(BF16)
    tri3 = jnp.concatenate([tri, tri, tri], axis=1)
    ref_row = (L - 1 - L // 2) if rev else L // 2
    last_row = 0 if rev else L - 1
    lane = lax.broadcasted_iota(jnp.int32, (1, 128), 1)
    lo = lane < 64
    r128 = lax.broadcasted_iota(jnp.int32, (128, 128), 0)
    c128 = lax.broadcasted_iota(jnp.int32, (128, 128), 1)
    blockdiag = (r128 < 64) == (c128 < 64)

    q = q_ref[0]
    qs = _silu(q)
    f = lb_row + (1.0 - lb_row) * jax.nn.sigmoid(f_ref[0])
    k = 1.0 - f
    la = jnp.log(f)
    b = _cumsum_exact(tri3, la)
    bref = b[ref_row:ref_row + 1, :]
    blast = b[last_row:last_row + 1, :]
    qd = (qs * jnp.exp(b - bref)).astype(BF16)
    kd = (k * jnp.exp(bref - b)).astype(BF16)
    kl = (k * jnp.exp(blast - b)).astype(BF16)
    qb = (qs * jnp.exp(b)).astype(BF16)
    dec = jnp.exp(blast)
    vb = v_ref[0].astype(BF16)
    zero_b = jnp.zeros((), BF16)
    for p in range(HA // 2):
        sl = slice(128 * p, 128 * (p + 1))
        qd_p, kd_p, kl_p, qb_p, v_p = qd[:, sl], kd[:, sl], kl[:, sl], qb[:, sl], vb[:, sl]
        st = sg_ref[d, p]
        o_int = _dot_nt(qb_p, st.astype(BF16))
        outs = []
        for hh in range(2):
            msk = lo if hh == 0 else jnp.logical_not(lo)
            att = _dot_nt(jnp.where(msk, qd_p, zero_b), kd_p)
            att = jnp.where(keep, att, 0.0).astype(BF16)
            outs.append(jnp.dot(att, v_p, preferred_element_type=F32))
        oh_ref[0, :, sl] = jnp.where(lo, outs[0], outs[1]) + o_int
        upd = _dot_tn(v_p, kl_p)
        sg_ref[d, p] = st * dec[:, sl] + jnp.where(blockdiag, upd, 0.0)

    u_scr[0:8, :] = xp_ref[0] * pv
    u_scr[8:8 + L, :] = x_ref[0]
    u_scr[8 + L:16 + L, :] = xn_ref[0] * nv
    y = cbias_ref[...] + cw_ref[0:1, :] * u_scr[6:6 + L, :]
    for j in range(1, 5):
        y = y + cw_ref[j:j + 1, :] * u_scr[6 + j:6 + j + L, :]
    xbc = _silu(y)
    xm = xbc[:, :512]
    bm = xbc[:, 512:768].astype(BF16)
    cm = xbc[:, 768:1024].astype(BF16)
    draw = dt_ref[0] + dtb_ref[...]
    dt = jnp.maximum(draw, 0.0) + jnp.log(1.0 + jnp.exp(-jnp.abs(draw)))
    la_m = -dt * aexp_ref[...]
    cum = _cumsum_exact(tri3, la_m)
    er = lax.broadcasted_iota(jnp.int32, (128, 512), 0)
    ec = lax.broadcasted_iota(jnp.int32, (128, 512), 1)
    e1 = (er == (ec // 64) + 8 * d).astype(BF16)
    e1 = jnp.concatenate([e1, e1, e1], axis=0)
    er2 = lax.broadcasted_iota(jnp.int32, (128, 1024), 0)
    ec2 = lax.broadcasted_iota(jnp.int32, (128, 1024), 1)
    e2 = (er2 == (ec2 // 128) + 8 * d).astype(BF16)
    e2 = jnp.concatenate([e2, e2, e2], axis=0)
    dt_e = _expand_exact(dt, e1)
    cum_e = _expand_exact(cum, e1)
    cum_e2 = _expand_exact(cum, e2)
    cum_t = cum.T
    xdt = xm * dt_e
    clast = cum_e[last_row:last_row + 1, :]
    wx = (jnp.exp(clast - cum_e) * xdt).astype(BF16)
    dec_m = jnp.exp(clast)
    ecum = jnp.exp(cum_e)
    xdt_b = xdt.astype(BF16)
    for g in range(2):
        bm_g = bm[:, 128 * g:128 * (g + 1)]
        cm_g = cm[:, 128 * g:128 * (g + 1)]
        gl = slice(256 * g, 256 * (g + 1))
        cb_g = _dot_nt(cm_g, bm_g)
        sm = sm_ref[d, g]
        y_int = jnp.dot(cm_g, sm.astype(BF16), preferred_element_type=F32) * ecum[:, gl]
        for pp in range(2):
            sl = slice(256 * g + 128 * pp, 256 * g + 128 * (pp + 1))
            x_p = xdt_b[:, sl]
            outs = []
            for hh in range(2):
                h = 4 * g + 2 * pp + hh
                dmat = cum_e2[:, 128 * h:128 * h + L] - cum_t[h + 8 * d:h + 8 * d + 1, :]
                lm = jnp.exp(jnp.where(keep, dmat, -jnp.inf))
                outs.append(jnp.dot((cb_g * lm).astype(BF16), x_p, preferred_element_type=F32))
            yp = jnp.where(lo, outs[0], outs[1]) + y_int[:, 128 * pp:128 * (pp + 1)]
            if not rev:
                yp = yp + dskip_ref[:, sl] * xm[:, sl]
            om_ref[0, :, sl] = yp
        sm_ref[d, g] = sm * dec_m[:, gl] + _dot_tn(bm_g, wx[:, gl])


def _scan_kernel(qf, ff, vf, xf, xpf, xnf, dtf, qb, fb, vb_, xb, xpb, xnb, dtb,
                 lb_ref, cw_ref, cbias_ref, dtbias_ref, aexp_ref, dskip_ref,
                 ohf, omf, ohb, omb, sg_ref, sm_ref, u_scr, *, nct, nc):
    c = pl.program_id(1)

    @pl.when(c == 0)
    def _():
        sg_ref[...] = jnp.zeros(sg_ref.shape, F32)
        sm_ref[...] = jnp.zeros(sm_ref.shape, F32)

    cb = jnp.where(c < nct, nct - 1 - c, nc + nct - 1 - c)

    def edge_flags(ch):
        pv = jnp.where((ch == 0) | (ch == nct), 0.0, 1.0)
        nv = jnp.where((ch == nct - 1) | (ch == nc - 1), 0.0, 1.0)
        return pv, nv

    pvf, nvf = edge_flags(c)
    pvb, nvb = edge_flags(cb)
    _scan_dir(False, qf, ff, vf, xf, xpf, xnf, dtf, pvf, nvf, lb_ref[0:1, :], cw_ref, cbias_ref,
              dtbias_ref, aexp_ref, dskip_ref, ohf, omf, sg_ref, sm_ref, u_scr)
    _scan_dir(True, qb, fb, vb_, xb, xpb, xnb, dtb, pvb, nvb, lb_ref[1:2, :], cw_ref, cbias_ref,
              dtbias_ref, aexp_ref, dskip_ref, ohb, omb, sg_ref, sm_ref, u_scr)


def _scan_mixers(p, nct, lb, conv_w, conv_b, dt_bias, a_log, d_skip):
    bsz, s, _ = p.shape
    nc = s // L
    nb8 = s // 8

    def bmap(c):
        return jnp.where(c < nct, nct - 1 - c, nc + nct - 1 - c)

    def specs(cmap, fcol):
        colblk = lambda col, w: col // w
        return [
            pl.BlockSpec((1, L, 512), lambda i, c: (i, cmap(c), colblk(COL_Q, 512))),
            pl.BlockSpec((1, L, 512), lambda i, c: (i, cmap(c), colblk(fcol, 512))),
            pl.BlockSpec((1, L, 512), lambda i, c: (i, cmap(c), colblk(COL_I, 512))),
            pl.BlockSpec((1, L, 1024), lambda i, c: (i, cmap(c), colblk(COL_X, 1024))),
            pl.BlockSpec((1, 8, 1024), lambda i, c: (i, jnp.maximum(cmap(c) * 8 - 1, 0), colblk(COL_X, 1024))),
            pl.BlockSpec((1, 8, 1024), lambda i, c: (i, jnp.minimum(cmap(c) * 8 + 8, nb8 - 1), colblk(COL_X, 1024))),
            pl.BlockSpec((1, L, 128), lambda i, c: (i, cmap(c), colblk(COL_DT, 128))),
        ]

    ident = lambda c: c
    const2 = lambda i, c: (0, 0)
    dtb = jnp.zeros((1, 128), F32).at[0, :16].set(dt_bias.reshape(16))
    aexp = jnp.zeros((1, 128), F32).at[0, :16].set(jnp.exp(a_log.reshape(16)))
    dsk = jnp.repeat(d_skip, 64).reshape(1, 512)
    kern = functools.partial(_scan_kernel, nct=nct, nc=nc)
    out_sd = jax.ShapeDtypeStruct((bsz, s, 512), F32)
    return pl.pallas_call(
        kern,
        grid=(bsz, nc),
        in_specs=specs(ident, COL_FF) + specs(bmap, COL_FB) + [
            pl.BlockSpec((2, 512), const2), pl.BlockSpec((5, 1024), const2),
            pl.BlockSpec((1, 1024), const2), pl.BlockSpec((1, 128), const2),
            pl.BlockSpec((1, 128), const2), pl.BlockSpec((1, 512), const2)],
        out_specs=[pl.BlockSpec((1, L, 512), lambda i, c: (i, c, 0)),
                   pl.BlockSpec((1, L, 512), lambda i, c: (i, c, 0)),
                   pl.BlockSpec((1, L, 512), lambda i, c: (i, bmap(c), 0)),
                   pl.BlockSpec((1, L, 512), lambda i, c: (i, bmap(c), 0))],
        out_shape=[out_sd, out_sd, out_sd, out_sd],
        scratch_shapes=[pltpu.VMEM((2, HA // 2, 128, 128), F32),
                        pltpu.VMEM((2, 2, 128, 256), F32),
                        pltpu.VMEM((L + 16, 1024), F32)],
        compiler_params=_cparams(("parallel", "arbitrary")),
        name="scan_mixers",
    )(*([p] * 14), lb, conv_w, conv_b.reshape(1, 1024), dtb, aexp, dsk)


def _merge_kernel(ohf, ohb, omf, omb, g_ref, z_ref, x_ref, gate_ref, hn_ref, mn_ref, w_ref, o_ref):
    oh = ohf[0] + ohb[0]
    sq = oh * oh
    hi = sq.astype(BF16)
    lo = (sq - hi.astype(F32)).astype(BF16)
    r = lax.broadcasted_iota(jnp.int32, (512, 512), 0) // 64
    c = lax.broadcasted_iota(jnp.int32, (512, 512), 1) // 64
    avg = jnp.where(r == c, 1.0 / 64, 0.0).astype(BF16)
    ms = jnp.dot(hi, avg, preferred_element_type=F32) + jnp.dot(lo, avg, preferred_element_type=F32)
    oh = oh * lax.rsqrt(ms + EPS) * hn_ref[...] * jax.nn.sigmoid(g_ref[0])
    y = (omf[0] + omb[0]) * _silu(z_ref[0])
    y = y * lax.rsqrt(jnp.mean(y * y, axis=-1, keepdims=True) + EPS) * mn_ref[...]
    cat = jnp.concatenate([oh, y], axis=1).astype(BF16)
    o_ref[0] = x_ref[0] + gate_ref[0, 0] * jnp.dot(cat, w_ref[...], preferred_element_type=F32)


def _merge_mixers(x, gate2, ohf, ohb, omf, omb, p, hgrn_norm, mamba_norm, w_out, nct_tiles, tm):
    b, s, d = x.shape
    sel = lambda i, j: (i, jnp.where(j < nct_tiles, 0, 1), 0, 0)
    row = lambda i, j: (i, j, 0)
    const2 = lambda i, j: (0, 0)
    return pl.pallas_call(
        _merge_kernel,
        grid=(b, s // tm),
        in_specs=[pl.BlockSpec((1, tm, 512), row)] * 4 + [
            pl.BlockSpec((1, tm, 512), lambda i, j: (i, j, COL_G // 512)),
            pl.BlockSpec((1, tm, 512), lambda i, j: (i, j, COL_Z // 512)),
            pl.BlockSpec((1, tm, d), row),
            pl.BlockSpec((1, 1, 1, d), sel),
            pl.BlockSpec((1, 512), const2), pl.BlockSpec((1, 512), const2),
            pl.BlockSpec((2 * 512, d), const2)],
        out_specs=pl.BlockSpec((1, tm, d), row),
        out_shape=jax.ShapeDtypeStruct((b, s, d), F32),
        compiler_params=_cparams(("parallel", "parallel")),
        name="merge_mixers",
    )(ohf, ohb, omf, omb, p, p, x, gate2.reshape(b, 2, 1, d), hgrn_norm.reshape(1, 512),
      mamba_norm.reshape(1, 512), w_out)


def _rope_tables(t):
    rows = t // GRID_W
    row = jnp.repeat(jnp.arange(rows, dtype=jnp.int32), GRID_W)
    col = jnp.tile(jnp.arange(GRID_W, dtype=jnp.int32), rows)
    nf = ROPE // 4
    inv_freq = ROPE_THETA ** (-jnp.arange(nf, dtype=F32) / nf)
    pos = jnp.stack([row, col], axis=-1).astype(F32)
    ang = pos[..., None] * inv_freq
    cos, sin = jnp.cos(ang), jnp.sin(ang)
    cos64 = jnp.broadcast_to(cos[:, :, None, :], (t, 2, 2, nf)).reshape(t, ROPE)
    sin64 = jnp.broadcast_to(sin[:, :, None, :], (t, 2, 2, nf)).reshape(t, ROPE)
    zero = jnp.zeros((t, ROPE), F32)
    return jnp.concatenate([cos64, zero], -1), jnp.concatenate([sin64, zero], -1)


def _swap_cols(w):
    nf = ROPE // 4
    wr = w.reshape(w.shape[:-1] + (2, 2, nf))
    return jnp.stack([-wr[..., 1, :], wr[..., 0, :]], axis=-2).reshape(w.shape)


def _rope_tile(tile, cos_t, sin_t):
    return tile * cos_t + pltpu.roll(tile, ROPE, axis=1) * sin_t


def _mla_down_kernel(x_ref, g_ref, sc_ref, sh_ref, w_ref, qn_ref, kvn_ref, cos_ref, sin_ref,
                     cq_ref, ckv_ref, kr_ref):
    a = _modnorm(x_ref[0], g_ref[...], sc_ref[0], sh_ref[0]).astype(BF16)
    c = jnp.dot(a, w_ref[...], preferred_element_type=F32)
    cq = c[:, :Q_LORA]
    ckv = c[:, Q_LORA:Q_LORA + KV_LORA]
    cq_ref[0] = _rms(cq, qn_ref[...]).astype(BF16)
    ckv_ref[0] = _rms(ckv, kvn_ref[...]).astype(BF16)
    kr_ref[0] = _rope_tile(c[:, Q_LORA + KV_LORA:], cos_ref[...], sin_ref[...]).astype(BF16)


def _mla_down(x, gain, scale, shift, w_cat, q_norm, kv_norm, cos_t, sin_t, tm):
    b, t, d = x.shape
    n = w_cat.shape[1]
    return pl.pallas_call(
        _mla_down_kernel,
        grid=(b, t // tm),
        in_specs=[pl.BlockSpec((1, tm, d), lambda i, j: (i, j, 0)),
                  pl.BlockSpec((1, d), lambda i, j: (0, 0)),
                  pl.BlockSpec((1, 1, d), lambda i, j: (i, 0, 0)),
                  pl.BlockSpec((1, 1, d), lambda i, j: (i, 0, 0)),
                  pl.BlockSpec((d, n), lambda i, j: (0, 0)),
                  pl.BlockSpec((1, Q_LORA), lambda i, j: (0, 0)),
                  pl.BlockSpec((1, KV_LORA), lambda i, j: (0, 0)),
                  pl.BlockSpec((tm, 128), lambda i, j: (j, 0)),
                  pl.BlockSpec((tm, 128), lambda i, j: (j, 0))],
        out_specs=[pl.BlockSpec((1, tm, Q_LORA), lambda i, j: (i, j, 0)),
                   pl.BlockSpec((1, tm, KV_LORA), lambda i, j: (i, j, 0)),
                   pl.BlockSpec((1, tm, 128), lambda i, j: (i, j, 0))],
        out_shape=[jax.ShapeDtypeStruct((b, t, Q_LORA), BF16),
                   jax.ShapeDtypeStruct((b, t, KV_LORA), BF16),
                   jax.ShapeDtypeStruct((b, t, 128), BF16)],
        compiler_params=_cparams(("parallel", "parallel")),
        name="mla_down",
    )(x, gain.reshape(1, d), scale, shift, w_cat, q_norm.reshape(1, -1), kv_norm.reshape(1, -1),
      cos_t, sin_t)


def _q_up_kernel(cq_ref, w_ref, cos_ref, sin_ref, q_ref):
    cq = cq_ref[0]
    for h in range(H_C):
        q = jnp.dot(cq, w_ref[h], preferred_element_type=F32)
        qn = q[:, :NOPE] * QK_SCALE
        qr = _rope_tile(q[:, NOPE:], cos_ref[...], sin_ref[...]) * QK_SCALE
        q_ref[0, h] = jnp.concatenate([qn, qr], axis=1).astype(BF16)


def _q_up(cq, w_uq_h, cos_t, sin_t, tm):
    b, t, r = cq.shape
    return pl.pallas_call(
        _q_up_kernel,
        grid=(b, t // tm),
        in_specs=[pl.BlockSpec((1, tm, r), lambda i, j: (i, j, 0)),
                  pl.BlockSpec((H_C, r, QK_DIM), lambda i, j: (0, 0, 0)),
                  pl.BlockSpec((tm, 128), lambda i, j: (j, 0)),
                  pl.BlockSpec((tm, 128), lambda i, j: (j, 0))],
        out_specs=pl.BlockSpec((1, H_C, tm, QK_DIM), lambda i, j: (i, 0, j, 0)),
        out_shape=jax.ShapeDtypeStruct((b, H_C, t, QK_DIM), BF16),
        compiler_params=_cparams(("parallel", "parallel")),
        name="q_up",
    )(cq, w_uq_h, cos_t, sin_t)


def _kv_up_kernel(ckv_ref, kr_ref, w_ref, k_ref, v_ref):
    ckv = ckv_ref[0]
    kr = kr_ref[0]
    for h in range(H_C):
        kv = jnp.dot(ckv, w_ref[h], preferred_element_type=F32)
        k_ref[0, h] = jnp.concatenate([kv[:, :NOPE].astype(BF16), kr], axis=1)
        v_ref[0, h] = kv[:, NOPE:].astype(BF16)


def _kv_up(ckv, kr, w_ukv_h, tm):
    b, s, r = ckv.shape
    return pl.pallas_call(
        _kv_up_kernel,
        grid=(b, s // tm),
        in_specs=[pl.BlockSpec((1, tm, r), lambda i, j: (i, j, 0)),
                  pl.BlockSpec((1, tm, 128), lambda i, j: (i, j, 0)),
                  pl.BlockSpec((H_C, r, NOPE + VH), lambda i, j: (0, 0, 0))],
        out_specs=[pl.BlockSpec((1, H_C, tm, QK_DIM), lambda i, j: (i, 0, j, 0)),
                   pl.BlockSpec((1, H_C, tm, VH), lambda i, j: (i, 0, j, 0))],
        out_shape=[jax.ShapeDtypeStruct((b, H_C, s, QK_DIM), BF16),
                   jax.ShapeDtypeStruct((b, H_C, s, VH), BF16)],
        compiler_params=_cparams(("parallel", "parallel")),
        name="kv_up",
    )(ckv, kr, w_ukv_h)


def _attn_kernel(q_ref, k_ref, v_ref, o_ref, s_scr, *, tk, nk):
    q = q_ref[0, 0]
    tq = q.shape[0]

    def pass_a(j, mrun):
        off = pl.multiple_of(j * tk, tk)
        ks = k_ref[0, 0, pl.ds(off, tk), :]
        s = lax.dot_general(q, ks, (((1,), (1,)), ((), ())), preferred_element_type=F32)
        s_scr[:, pl.ds(off, tk)] = s
        for c in range(tk // 128):
            mrun = jnp.maximum(mrun, s[:, c * 128:(c + 1) * 128])
        return mrun

    mrun = lax.fori_loop(0, nk, pass_a, jnp.full((tq, 128), -jnp.inf, F32), unroll=True)
    m = jnp.max(mrun, axis=-1, keepdims=True)

    def pass_b(j, carry):
        lrun, acc = carry
        off = pl.multiple_of(j * tk, tk)
        vs = v_ref[0, 0, pl.ds(off, tk), :]
        p = jnp.exp2(s_scr[:, pl.ds(off, tk)] - m)
        for c in range(tk // 128):
            lrun = lrun + p[:, c * 128:(c + 1) * 128]
        acc = acc + jnp.dot(p.astype(BF16), vs, preferred_element_type=F32)
        return lrun, acc

    lrun, acc = lax.fori_loop(0, nk, pass_b,
                              (jnp.zeros((tq, 128), F32), jnp.zeros((tq, VH), F32)), unroll=True)
    l = jnp.sum(lrun, axis=-1, keepdims=True)
    o_ref[0] = (acc / l).astype(o_ref.dtype)


def _attention(q, k, v, tq, tk):
    b, h, t, _ = q.shape
    s = k.shape[2]
    assert t % tq == 0 and s % tk == 0
    kern = functools.partial(_attn_kernel, tk=tk, nk=s // tk)
    return pl.pallas_call(
        kern,
        grid=(b, h, t // tq),
        in_specs=[pl.BlockSpec((1, 1, tq, QK_DIM), lambda i, j, n: (i, j, n, 0)),
                  pl.BlockSpec((1, 1, s, QK_DIM), lambda i, j, n: (i, j, 0, 0)),
                  pl.BlockSpec((1, 1, s, VH), lambda i, j, n: (i, j, 0, 0))],
        out_specs=pl.BlockSpec((1, tq, VH), lambda i, j, n: (i, n, j)),
        out_shape=jax.ShapeDtypeStruct((b, t, h * VH), BF16),
        scratch_shapes=[pltpu.VMEM((tq, s), F32)],
        compiler_params=_cparams(("parallel", "parallel", "arbitrary")),
        name="attention",
    )(q, k, v)


def _router_kernel(x_ref, g_ref, sc_ref, sh_ref, w_ref, h_ref, lg_ref):
    hn = _modnorm(x_ref[0], g_ref[...], sc_ref[0], sh_ref[0])
    h_ref[0] = hn.astype(BF16)
    lg_ref[0] = jnp.dot(hn, w_ref[...], preferred_element_type=F32,
                        precision=lax.Precision.HIGHEST)


def _router(x, gain, scale, shift, w_router, tm):
    b, t, d = x.shape
    e = w_router.shape[1]
    return pl.pallas_call(
        _router_kernel,
        grid=(b, t // tm),
        in_specs=[pl.BlockSpec((1, tm, d), lambda i, j: (i, j, 0)),
                  pl.BlockSpec((1, d), lambda i, j: (0, 0)),
                  pl.BlockSpec((1, 1, d), lambda i, j: (i, 0, 0)),
                  pl.BlockSpec((1, 1, d), lambda i, j: (i, 0, 0)),
                  pl.BlockSpec((d, e), lambda i, j: (0, 0))],
        out_specs=[pl.BlockSpec((1, tm, d), lambda i, j: (i, j, 0)),
                   pl.BlockSpec((1, tm, e), lambda i, j: (i, j, 0))],
        out_shape=[jax.ShapeDtypeStruct((b, t, d), BF16),
                   jax.ShapeDtypeStruct((b, t, e), F32)],
        compiler_params=_cparams(("parallel", "parallel")),
        name="router",
    )(x, gain.reshape(1, d), scale, shift, w_router)


def _ffn_kernel(x_ref, wg_ref, wu_ref, wd_ref, g_ref, o_ref, *, fc):
    x = x_ref[0, 0]
    nf = wg_ref.shape[-1]
    for c in range(nf // fc):
        sl = slice(c * fc, (c + 1) * fc)
        hg = jnp.dot(x, wg_ref[0, :, sl], preferred_element_type=F32)
        hu = jnp.dot(x, wu_ref[0, :, sl], preferred_element_type=F32)
        hid = (_silu(hg) * hu).astype(BF16)
        part = jnp.dot(hid, wd_ref[0, sl, :], preferred_element_type=F32)
        if c == 0:
            o_ref[0, 0] = part
        else:
            o_ref[0, 0] += part
    o_ref[0, 0] = o_ref[0, 0] * g_ref[0, 0]


def _expert_ffn(xg, gates, w_gate, w_up, w_down, tm, fc):
    b, e, cap, d = xg.shape
    f = w_gate.shape[-1]
    tm = min(tm, cap)
    kern = functools.partial(_ffn_kernel, fc=fc)
    return pl.pallas_call(
        kern,
        grid=(b, e, cap // tm),
        in_specs=[pl.BlockSpec((1, 1, tm, d), lambda i, j, m: (i, j, m, 0)),
                  pl.BlockSpec((1, d, f), lambda i, j, m: (j, 0, 0)),
                  pl.BlockSpec((1, d, f), lambda i, j, m: (j, 0, 0)),
                  pl.BlockSpec((1, f, d), lambda i, j, m: (j, 0, 0)),
                  pl.BlockSpec((1, 1, tm, 1), lambda i, j, m: (i, j, m, 0))],
        out_specs=pl.BlockSpec((1, 1, tm, d), lambda i, j, m: (i, j, m, 0)),
        out_shape=jax.ShapeDtypeStruct((b, e, cap, d), F32),
        compiler_params=_cparams(("parallel", "parallel", "arbitrary")),
        name="expert_ffn",
    )(xg, w_gate, w_up, w_down, gates)


def _moe(x, gain, scale, shift, gate_out, w_router, w_gate, w_up, w_down):
    b, t, d = x.shape
    cap = CAP_FACTOR * t // N_EXPERTS
    hn, logits = _router(x, gain, scale, shift, w_router, min(256, t))
    aff = jax.nn.softmax(logits, axis=-1)
    g, idx = lax.top_k(jnp.swapaxes(aff, 1, 2), cap)
    bidx = jnp.arange(b)[:, None, None]
    xg = hn[bidx, idx]
    y = _expert_ffn(xg, g[..., None], w_gate, w_up, w_down, 256, 512)
    upd = jnp.zeros((b, t, d), F32).at[bidx, idx].add(y)
    return x + gate_out * upd


def _final_kernel(x_ref, g_ref, o_ref):
    o_ref[0] = _rms(x_ref[0], g_ref[...])


def _final_norm(x, gain, tm):
    b, t, d = x.shape
    return pl.pallas_call(
        _final_kernel,
        grid=(b, t // tm),
        in_specs=[pl.BlockSpec((1, tm, d), lambda i, j: (i, j, 0)),
                  pl.BlockSpec((1, d), lambda i, j: (0, 0))],
        out_specs=pl.BlockSpec((1, tm, d), lambda i, j: (i, j, 0)),
        out_shape=jax.ShapeDtypeStruct((b, t, d), F32),
        compiler_params=_cparams(("parallel", "parallel")),
        name="final_norm",
    )(x, gain.reshape(1, d))


def kernel(x, c, ctx, c_ctx, w_mod, b_mod, norm_mix, norm_ffn, norm_out, w_in, w_out_rec,
           conv_w, conv_b, lb_gamma, dt_bias, a_log, d_skip, hgrn_norm, mamba_norm,
           w_dq, q_norm, w_uq, w_dkv, kv_norm, w_ukv, w_kr, w_o,
           w_router, w_gate, w_up, w_down):
    B, T, D = x.shape
    TC = ctx.shape[1]
    lb = jnp.cumsum(jax.nn.softmax(lb_gamma.astype(F32), axis=0), axis=0)

    cvec = jnp.zeros((16, D), F32).at[:B].set(c).at[B].set(c_ctx)
    mods = _modulation(cvec, w_mod, b_mod)

    x_lat, x_ctx = x, ctx
    for l in range(DEPTH):
        need_ctx = l < DEPTH - 1
        m_lat = [mods[l, :B, i * D:(i + 1) * D][:, None, :] for i in range(6)]
        m_ctx = [jnp.broadcast_to(mods[l, B, i * D:(i + 1) * D][None, None, :], (B, 1, D))
                 for i in range(6)]
        if l % 2 == 0:
            e = l // 2
            w_in_p = jnp.pad(w_in[e], ((0, 0), (0, IN_COLS_PAD - IN_COLS))).astype(BF16)
            xs = jnp.concatenate([x_ctx, x_lat], axis=1)
            pair = lambda i: jnp.concatenate([m_ctx[i], m_lat[i]], axis=1)
            p = _modproj(xs, norm_mix[l], pair(1), pair(0), w_in_p, 256, TC // 256)
            ohf, omf, ohb, omb = _scan_mixers(p, TC // CHUNK, lb[e], conv_w[e], conv_b[e],
                                              dt_bias[e], a_log[e], d_skip[e])
            xs = _merge_mixers(xs, pair(2), ohf, ohb, omf, omb, p, hgrn_norm[e], mamba_norm[e],
                               w_out_rec[e].astype(BF16), TC // 256, 256)
            x_ctx, x_lat = xs[:, :TC], xs[:, TC:]
        else:
            j = l // 2
            cos_l, sin_l = _rope_tables(T)
            cos_c = jnp.concatenate([jnp.ones((TC, ROPE), F32), jnp.zeros((TC, ROPE), F32)], -1)
            sin_c = jnp.zeros((TC, 128), F32)
            w_cat = jnp.concatenate([w_dq[j], w_dkv[j], w_kr[j], _swap_cols(w_kr[j])],
                                    axis=1).astype(BF16)
            wq = w_uq[j].reshape(Q_LORA, H_C, NOPE + ROPE)
            wq_h = jnp.concatenate([wq, _swap_cols(wq[..., NOPE:])], axis=-1)
            wq_h = jnp.transpose(wq_h, (1, 0, 2)).astype(BF16)
            wkv_h = jnp.transpose(w_ukv[j].reshape(KV_LORA, H_C, NOPE + VH), (1, 0, 2)).astype(BF16)
            cq_l, ckv_l, kr_l = _mla_down(x_lat, norm_mix[l], m_lat[1], m_lat[0], w_cat,
                                          q_norm[j], kv_norm[j], cos_l, sin_l, 256)
            _, ckv_c, kr_c = _mla_down(x_ctx, norm_mix[l], m_ctx[1], m_ctx[0], w_cat,
                                       q_norm[j], kv_norm[j], cos_c, sin_c, 256)
            ckv = jnp.concatenate([ckv_c, ckv_l], axis=1)
            kr = jnp.concatenate([kr_c, kr_l], axis=1)
            qh = _q_up(cq_l, wq_h, cos_l, sin_l, 512)
            kh, vh = _kv_up(ckv, kr, wkv_h, 768)
            o = _attention(qh, kh, vh, 256, 768)
            x_lat = _linear_residual(x_lat, m_lat[2], o, w_o[j].astype(BF16), 512)
        x_lat = _moe(x_lat, norm_ffn[l], m_lat[4], m_lat[3], m_lat[5], w_router[l],
                     w_gate[l].astype(BF16), w_up[l].astype(BF16), w_down[l].astype(BF16))
        if need_ctx:
            x_ctx = _moe(x_ctx, norm_ffn[l], m_ctx[4], m_ctx[3], m_ctx[5], w_router[l],
                         w_gate[l].astype(BF16), w_up[l].astype(BF16), w_down[l].astype(BF16))
    return _final_norm(x_lat, norm_out, 512)
```

```python
import functools
import math

import jax
import jax.numpy as jnp
from jax import lax
from jax.experimental import pallas as pl
from jax.experimental.pallas import tpu as pltpu

F32 = jnp.float32
BF16 = jnp.bfloat16

D_MODEL = 1024
DEPTH = 2
GRID_W = 64
EPS = 1e-6
CHUNK = 64
HA, DKA, DVA = 8, 64, 64
WA = HA * DVA
HB, PB = 8, 64
D_INNER = HB * PB
N_GROUPS, D_STATE = 2, 128
CONV_W = 5
REC_SPLITS = (HA * DKA, HA * DKA, HA * DKA, WA, WA, D_INNER, D_INNER,
              N_GROUPS * D_STATE, N_GROUPS * D_STATE, HB, HB)
IN_COLS = sum(REC_SPLITS)
IN_COLS_PAD = 4224
H_C, NOPE, ROPE, VH = 8, 128, 64, 128
Q_LORA, KV_LORA = 384, 256
ROPE_THETA = 10000.0
ATTN_SCALE = 1.0 / math.sqrt(NOPE + ROPE)
QK_SCALE = ATTN_SCALE * math.log2(math.e)
QK_DIM = 256
N_EXPERTS = 16
EXPERT_FF = 1024
CAP_FACTOR = 2

VMEM_LIMIT = 56 * 1024 * 1024


def _cparams(sem):
    return pltpu.CompilerParams(dimension_semantics=sem, vmem_limit_bytes=VMEM_LIMIT)


def _silu(v):
    return v * jax.nn.sigmoid(v)


def _modnorm(x, gain, scale, shift):
    ms = jnp.mean(x * x, axis=-1, keepdims=True)
    return (x * lax.rsqrt(ms + EPS) * gain) * (1.0 + scale) + shift


def _mod_kernel(s_ref, w_ref, b_ref, o_ref):
    s = _silu(s_ref[...])
    o_ref[0] = jnp.dot(s, w_ref[0], preferred_element_type=F32,
                       precision=lax.Precision.HIGHEST) + b_ref[0]


def _modulation(cvec, w_mod, b_mod):
    n = w_mod.shape[-1]
    tn = 1536
    return pl.pallas_call(
        _mod_kernel,
        grid=(DEPTH, n // tn),
        in_specs=[pl.BlockSpec((16, D_MODEL), lambda l, j: (0, 0)),
                  pl.BlockSpec((1, D_MODEL, tn), lambda l, j: (l, 0, j)),
                  pl.BlockSpec((1, 1, tn), lambda l, j: (l, 0, j))],
        out_specs=pl.BlockSpec((1, 16, tn), lambda l, j: (l, 0, j)),
        out_shape=jax.ShapeDtypeStruct((DEPTH, 16, n), F32),
        compiler_params=_cparams(("parallel", "parallel")),
        name="modulation",
    )(cvec, w_mod, b_mod.reshape(DEPTH, 1, n))


def _modproj_kernel(x_ref, g_ref, sc_ref, sh_ref, w_ref, o_ref):
    a = _modnorm(x_ref[0], g_ref[...], sc_ref[0, 0], sh_ref[0, 0]).astype(BF16)
    o_ref[0] = jnp.dot(a, w_ref[...], preferred_element_type=F32).astype(o_ref.dtype)


def _modproj(x, gain, scale2, shift2, w, tm, nct_tiles, out_dtype=F32):
    b, t, d = x.shape
    n = w.shape[1]
    sel = lambda i, j: (i, jnp.where(j < nct_tiles, 0, 1), 0, 0)
    return pl.pallas_call(
        _modproj_kernel,
        grid=(b, t // tm),
        in_specs=[pl.BlockSpec((1, tm, d), lambda i, j: (i, j, 0)),
                  pl.BlockSpec((1, d), lambda i, j: (0, 0)),
                  pl.BlockSpec((1, 1, 1, d), sel),
                  pl.BlockSpec((1, 1, 1, d), sel),
                  pl.BlockSpec((d, n), lambda i, j: (0, 0))],
        out_specs=pl.BlockSpec((1, tm, n), lambda i, j: (i, j, 0)),
        out_shape=jax.ShapeDtypeStruct((b, t, n), out_dtype),
        compiler_params=_cparams(("parallel", "parallel")),
        name="modproj",
    )(x, gain.reshape(1, d), scale2.reshape(b, 2, 1, d), shift2.reshape(b, 2, 1, d), w)


def _linres_kernel(x_ref, gate_ref, y_ref, w_ref, o_ref):
    acc = jnp.dot(y_ref[0].astype(BF16), w_ref[...], preferred_element_type=F32)
    o_ref[0] = x_ref[0] + gate_ref[0] * acc


def _linear_residual(x, gate, y, w, tm):
    b, t, d = x.shape
    k = y.shape[-1]
    return pl.pallas_call(
        _linres_kernel,
        grid=(b, t // tm),
        in_specs=[pl.BlockSpec((1, tm, d), lambda i, j: (i, j, 0)),
                  pl.BlockSpec((1, 1, d), lambda i, j: (i, 0, 0)),
                  pl.BlockSpec((1, tm, k), lambda i, j: (i, j, 0)),
                  pl.BlockSpec((k, d), lambda i, j: (0, 0))],
        out_specs=pl.BlockSpec((1, tm, d), lambda i, j: (i, j, 0)),
        out_shape=jax.ShapeDtypeStruct((b, t, d), F32),
        compiler_params=_cparams(("parallel", "parallel")),
        name="linear_residual",
    )(x, gate, y, w)


L = CHUNK
COL_Q, COL_FF, COL_FB, COL_I, COL_G, COL_Z, COL_X, COL_DT = 0, 512, 1024, 1536, 2048, 2560, 3072, 4096


def _rms(x, gain):
    return x * lax.rsqrt(jnp.mean(x * x, axis=-1, keepdims=True) + EPS) * gain


def _split3(x):
    a = x.astype(BF16)
    r = x - a.astype(F32)
    b = r.astype(BF16)
    c = (r - b.astype(F32)).astype(BF16)
    return a, b, c


def _cumsum_exact(tri3, x):
    a, b, c = _split3(x)
    return jnp.dot(tri3, jnp.concatenate([a, b, c], axis=0), preferred_element_type=F32)


def _expand_exact(x, e3):
    a, b, c = _split3(x)
    return jnp.dot(jnp.concatenate([a, b, c], axis=1), e3, preferred_element_type=F32)


def _dot_nt(a, b):
    return lax.dot_general(a, b, (((1,), (1,)), ((), ())), preferred_element_type=F32)


def _dot_tn(a, b):
    return lax.dot_general(a, b, (((0,), (0,)), ((), ())), preferred_element_type=F32)


def _scan_dir(rev, q_ref, f_ref, v_ref, x_ref, xp_ref, xn_ref, dt_ref, pv, nv,
              lb_row, cw_ref, cbias_ref, dtb_ref, aexp_ref, dskip_ref,
              oh_ref, om_ref, sg_ref, sm_ref, u_scr):
    d = 1 if rev else 0
    ti = lax.broadcasted_iota(jnp.int32, (L, L), 0)
    si = lax.broadcasted_iota(jnp.int32, (L, L), 1)
    keep = (si >= ti) if rev else (si <= ti)
    tri = keep.astype(BF16)
    tri3 = jnp.concatenate([tri, tri, tri], axis=1)
    t2 = lax.broadcasted_iota(jnp.int32, (L, 128), 0)
    s2 = lax.broadcasted_iota(jnp.int32, (L, 128), 1) % L
    keep2 = (s2 >= t2) if rev else (s2 <= t2)
    ref_row = (L - 1 - L // 2) if rev else L // 2
    last_row = 0 if rev else L - 1
    lane = lax.broadcasted_iota(jnp.int32, (1, 128), 1)
    lo = lane < 64
    r128 = lax.broadcasted_iota(jnp.int32, (128, 128), 0)
    c128 = lax.broadcasted_iota(jnp.int32, (128, 128), 1)
    blockdiag = (r128 < 64) == (c128 < 64)
    zero_b = jnp.zeros((), BF16)

    def pair_rows(a):
        return jnp.concatenate([jnp.where(lo, a, zero_b), jnp.where(lo, zero_b, a)], axis=0)

    q = q_ref[0]
    qs = _silu(q)
    f = lb_row + (1.0 - lb_row) * jax.nn.sigmoid(f_ref[0])
    k = 1.0 - f
    la = jnp.log(f)
    b = _cumsum_exact(tri3, la)
    bref = b[ref_row:ref_row + 1, :]
    blast = b[last_row:last_row + 1, :]
    qd = (qs * jnp.exp(b - bref)).astype(BF16)
    kd = (k * jnp.exp(bref - b)).astype(BF16)
    kl = (k * jnp.exp(blast - b)).astype(BF16)
    qb = (qs * jnp.exp(b)).astype(BF16)
    dec_col = jnp.broadcast_to(jnp.exp(blast), (8, 512)).T
    vb = v_ref[0].astype(BF16)
    for p in range(HA // 2):
        sl = slice(128 * p, 128 * (p + 1))
        qd_p, kd_p, kl_p, qb_p, v_p = qd[:, sl], kd[:, sl], kl[:, sl], qb[:, sl], vb[:, sl]
        st = sg_ref[d, p]
        att = _dot_nt(qd_p, pair_rows(kd_p))
        att = jnp.where(keep2, att, 0.0).astype(BF16)
        lhs = jnp.concatenate([att, qb_p], axis=1)
        rhs = jnp.concatenate([pair_rows(v_p), st.astype(BF16)], axis=0)
        oh_ref[0, :, sl] = jnp.dot(lhs, rhs, preferred_element_type=F32)
        upd = _dot_tn(kl_p, v_p)
        sg_ref[d, p] = st * dec_col[sl, 0:1] + jnp.where(blockdiag, upd, 0.0)

    u_scr[0:8, :] = xp_ref[0] * pv
    u_scr[8:8 + L, :] = x_ref[0]
    u_scr[8 + L:16 + L, :] = xn_ref[0] * nv
    y = cbias_ref[...] + cw_ref[0:1, :] * u_scr[6:6 + L, :]
    for j in range(1, 5):
        y = y + cw_ref[j:j + 1, :] * u_scr[6 + j:6 + j + L, :]
    xbc = _silu(y)
    xm = xbc[:, :512]
    bm = xbc[:, 512:768].astype(BF16)
    cm = xbc[:, 768:1024].astype(BF16)
    draw = dt_ref[0] + dtb_ref[...]
    dt = jnp.maximum(draw, 0.0) + jnp.log(1.0 + jnp.exp(-jnp.abs(draw)))
    la_m = -dt * aexp_ref[...]
    cum = _cumsum_exact(tri3, la_m)
    er = lax.broadcasted_iota(jnp.int32, (128, 512), 0)
    ec = lax.broadcasted_iota(jnp.int32, (128, 512), 1)
    e1 = (er == (ec // 64) + 8 * d).astype(BF16)
    e1 = jnp.concatenate([e1, e1, e1], axis=0)
    both = _expand_exact(jnp.concatenate([dt, cum], axis=0), e1)
    dt_e, cum_e = both[:L], both[L:]
    cum_t = cum.T
    xdt = xm * dt_e
    clast = cum_e[last_row:last_row + 1, :]
    wx = (jnp.exp(clast - cum_e) * xdt).astype(BF16)
    dec_m = jnp.exp(clast)
    ecum = jnp.exp(cum_e)
    xdt_b = xdt.astype(BF16)
    for g in range(2):
        bm_g = bm[:, 128 * g:128 * (g + 1)]
        cm_g = cm[:, 128 * g:128 * (g + 1)]
        gl = slice(256 * g, 256 * (g + 1))
        cb2 = _dot_nt(cm_g, jnp.concatenate([bm_g, bm_g], axis=0))
        sm = sm_ref[d, g]
        y_int = jnp.dot(cm_g, sm.astype(BF16), preferred_element_type=F32) * ecum[:, gl]
        for pp in range(2):
            h0 = 4 * g + 2 * pp + 8 * d
            sl = slice(256 * g + 128 * pp, 256 * g + 128 * (pp + 1))
            crow = jnp.concatenate([cum_t[h0:h0 + 1, :], cum_t[h0 + 1:h0 + 2, :]], axis=1)
            lm = jnp.exp(jnp.where(keep2, cum_e[:, sl] - crow, -jnp.inf))
            yp = jnp.dot((cb2 * lm).astype(BF16), pair_rows(xdt_b[:, sl]), preferred_element_type=F32)
            yp = yp + y_int[:, 128 * pp:128 * (pp + 1)]
            if not rev:
                yp = yp + dskip_ref[:, sl] * xm[:, sl]
            om_ref[0, :, sl] = yp
        sm_ref[d, g] = sm * dec_m[:, gl] + _dot_tn(bm_g, wx[:, gl])


def _scan_kernel(qf, ff, vf, xf, xpf, xnf, dtf, qb, fb, vb_, xb, xpb, xnb, dtb,
                 lb_ref, cw_ref, cbias_ref, dtbias_ref, aexp_ref, dskip_ref,
                 ohf, omf, ohb, omb, sg_ref, sm_ref, u_scr, *, nct, nc):
    c = pl.program_id(1)

    @pl.when(c == 0)
    def _():
        sg_ref[...] = jnp.zeros(sg_ref.shape, F32)
        sm_ref[...] = jnp.zeros(sm_ref.shape, F32)

    cb = jnp.where(c < nct, nct - 1 - c, nc + nct - 1 - c)

    def edge_flags(ch):
        pv = jnp.where((ch == 0) | (ch == nct), 0.0, 1.0)
        nv = jnp.where((ch == nct - 1) | (ch == nc - 1), 0.0, 1.0)
        return pv, nv

    pvf, nvf = edge_flags(c)
    pvb, nvb = edge_flags(cb)
    _scan_dir(False, qf, ff, vf, xf, xpf, xnf, dtf, pvf, nvf, lb_ref[0:1, :], cw_ref, cbias_ref,
              dtbias_ref, aexp_ref, dskip_ref, ohf, omf, sg_ref, sm_ref, u_scr)
    _scan_dir(True, qb, fb, vb_, xb, xpb, xnb, dtb, pvb, nvb, lb_ref[1:2, :], cw_ref, cbias_ref,
              dtbias_ref, aexp_ref, dskip_ref, ohb, omb, sg_ref, sm_ref, u_scr)


def _scan_mixers(p, nct, lb, conv_w, conv_b, dt_bias, a_log, d_skip):
    bsz, s, _ = p.shape
    nc = s // L
    nb8 = s // 8

    def bmap(c):
        return jnp.where(c < nct, nct - 1 - c, nc + nct - 1 - c)

    def specs(cmap, fcol):
        colblk = lambda col, w: col // w
        return [
            pl.BlockSpec((1, L, 512), lambda i, c: (i, cmap(c), colblk(COL_Q, 512))),
            pl.BlockSpec((1, L, 512), lambda i, c: (i, cmap(c), colblk(fcol, 512))),
            pl.BlockSpec((1, L, 512), lambda i, c: (i, cmap(c), colblk(COL_I, 512))),
            pl.BlockSpec((1, L, 1024), lambda i, c: (i, cmap(c), colblk(COL_X, 1024))),
            pl.BlockSpec((1, 8, 1024), lambda i, c: (i, jnp.maximum(cmap(c) * 8 - 1, 0), colblk(COL_X, 1024))),
            pl.BlockSpec((1, 8, 1024), lambda i, c: (i, jnp.minimum(cmap(c) * 8 + 8, nb8 - 1), colblk(COL_X, 1024))),
            pl.BlockSpec((1, L, 128), lambda i, c: (i, cmap(c), colblk(COL_DT, 128))),
        ]

    ident = lambda c: c
    const2 = lambda i, c: (0, 0)
    dtb = jnp.zeros((1, 128), F32).at[0, :16].set(dt_bias.reshape(16))
    aexp = jnp.zeros((1, 128), F32).at[0, :16].set(jnp.exp(a_log.reshape(16)))
    dsk = jnp.repeat(d_skip, 64).reshape(1, 512)
    kern = functools.partial(_scan_kernel, nct=nct, nc=nc)
    out_sd = jax.ShapeDtypeStruct((bsz, s, 512), F32)
    return pl.pallas_call(
        kern,
        grid=(bsz, nc),
        in_specs=specs(ident, COL_FF) + specs(bmap, COL_FB) + [
            pl.BlockSpec((2, 512), const2), pl.BlockSpec((5, 1024), const2),
            pl.BlockSpec((1, 1024), const2), pl.BlockSpec((1, 128), const2),
            pl.BlockSpec((1, 128), const2), pl.BlockSpec((1, 512), const2)],
        out_specs=[pl.BlockSpec((1, L, 512), lambda i, c: (i, c, 0)),
                   pl.BlockSpec((1, L, 512), lambda i, c: (i, c, 0)),
                   pl.BlockSpec((1, L, 512), lambda i, c: (i, bmap(c), 0)),
                   pl.BlockSpec((1, L, 512), lambda i, c: (i, bmap(c), 0))],
        out_shape=[out_sd, out_sd, out_sd, out_sd],
        scratch_shapes=[pltpu.VMEM((2, HA // 2, 128, 128), F32),
                        pltpu.VMEM((2, 2, 128, 256), F32),
                        pltpu.VMEM((L + 16, 1024), F32)],
        compiler_params=_cparams(("parallel", "arbitrary")),
        name="scan_mixers",
    )(*([p] * 14), lb, conv_w, conv_b.reshape(1, 1024), dtb, aexp, dsk)


def _merge_kernel(ohf, ohb, omf, omb, g_ref, z_ref, x_ref, gate_ref, hn_ref, mn_ref, w_ref, o_ref):
    oh = ohf[0] + ohb[0]
    sq = oh * oh
    hi = sq.astype(BF16)
    lo = (sq - hi.astype(F32)).astype(BF16)
    r = lax.broadcasted_iota(jnp.int32, (512, 512), 0) // 64
    c = lax.broadcasted_iota(jnp.int32, (512, 512), 1) // 64
    avg = jnp.where(r == c, 1.0 / 64, 0.0).astype(BF16)
    ms = jnp.dot(hi, avg, preferred_element_type=F32) + jnp.dot(lo, avg, preferred_element_type=F32)
    oh = oh * lax.rsqrt(ms + EPS) * hn_ref[...] * jax.nn.sigmoid(g_ref[0])
    y = (omf[0] + omb[0]) * _silu(z_ref[0])
    y = y * lax.rsqrt(jnp.mean(y * y, axis=-1, keepdims=True) + EPS) * mn_ref[...]
    cat = jnp.concatenate([oh, y], axis=1).astype(BF16)
    o_ref[0] = x_ref[0] + gate_ref[0, 0] * jnp.dot(cat, w_ref[...], preferred_element_type=F32)


def _merge_mixers(x, gate2, ohf, ohb, omf, omb, p, hgrn_norm, mamba_norm, w_out, nct_tiles, tm):
    b, s, d = x.shape
    sel = lambda i, j: (i, jnp.where(j < nct_tiles, 0, 1), 0, 0)
    row = lambda i, j: (i, j, 0)
    const2 = lambda i, j: (0, 0)
    return pl.pallas_call(
        _merge_kernel,
        grid=(b, s // tm),
        in_specs=[pl.BlockSpec((1, tm, 512), row)] * 4 + [
            pl.BlockSpec((1, tm, 512), lambda i, j: (i, j, COL_G // 512)),
            pl.BlockSpec((1, tm, 512), lambda i, j: (i, j, COL_Z // 512)),
            pl.BlockSpec((1, tm, d), row),
            pl.BlockSpec((1, 1, 1, d), sel),
            pl.BlockSpec((1, 512), const2), pl.BlockSpec((1, 512), const2),
            pl.BlockSpec((2 * 512, d), const2)],
        out_specs=pl.BlockSpec((1, tm, d), row),
        out_shape=jax.ShapeDtypeStruct((b, s, d), F32),
        compiler_params=_cparams(("parallel", "parallel")),
        name="merge_mixers",
    )(ohf, ohb, omf, omb, p, p, x, gate2.reshape(b, 2, 1, d), hgrn_norm.reshape(1, 512),
      mamba_norm.reshape(1, 512), w_out)


def _rope_tables(t):
    rows = t // GRID_W
    row = jnp.repeat(jnp.arange(rows, dtype=jnp.int32), GRID_W)
    col = jnp.tile(jnp.arange(GRID_W, dtype=jnp.int32), rows)
    nf = ROPE // 4
    inv_freq = ROPE_THETA ** (-jnp.arange(nf, dtype=F32) / nf)
    pos = jnp.stack([row, col], axis=-1).astype(F32)
    ang = pos[..., None] * inv_freq
    cos, sin = jnp.cos(ang), jnp.sin(ang)
    cos64 = jnp.broadcast_to(cos[:, :, None, :], (t, 2, 2, nf)).reshape(t, ROPE)
    sin64 = jnp.broadcast_to(sin[:, :, None, :], (t, 2, 2, nf)).reshape(t, ROPE)
    zero = jnp.zeros((t, ROPE), F32)
    return jnp.concatenate([cos64, zero], -1), jnp.concatenate([sin64, zero], -1)


def _swap_cols(w):
    nf = ROPE // 4
    wr = w.reshape(w.shape[:-1] + (2, 2, nf))
    return jnp.stack([-wr[..., 1, :], wr[..., 0, :]], axis=-2).reshape(w.shape)


def _rope_tile(tile, cos_t, sin_t):
    return tile * cos_t + pltpu.roll(tile, ROPE, axis=1) * sin_t


def _mla_down_kernel(x_ref, g_ref, sc_ref, sh_ref, w_ref, qn_ref, kvn_ref, cos_ref, sin_ref,
                     cq_ref, ckv_ref, kr_ref):
    a = _modnorm(x_ref[0], g_ref[...], sc_ref[0], sh_ref[0]).astype(BF16)
    c = jnp.dot(a, w_ref[...], preferred_element_type=F32)
    cq = c[:, :Q_LORA]
    ckv = c[:, Q_LORA:Q_LORA + KV_LORA]
    cq_ref[0] = _rms(cq, qn_ref[...]).astype(BF16)
    ckv_ref[0] = _rms(ckv, kvn_ref[...]).astype(BF16)
    kr_ref[0] = _rope_tile(c[:, Q_LORA + KV_LORA:], cos_ref[...], sin_ref[...]).astype(BF16)


def _mla_down(x, gain, scale, shift, w_cat, q_norm, kv_norm, cos_t, sin_t, tm):
    b, t, d = x.shape
    n = w_cat.shape[1]
    return pl.pallas_call(
        _mla_down_kernel,
        grid=(b, t // tm),
        in_specs=[pl.BlockSpec((1, tm, d), lambda i, j: (i, j, 0)),
                  pl.BlockSpec((1, d), lambda i, j: (0, 0)),
                  pl.BlockSpec((1, 1, d), lambda i, j: (i, 0, 0)),
                  pl.BlockSpec((1, 1, d), lambda i, j: (i, 0, 0)),
                  pl.BlockSpec((d, n), lambda i, j: (0, 0)),
                  pl.BlockSpec((1, Q_LORA), lambda i, j: (0, 0)),
                  pl.BlockSpec((1, KV_LORA), lambda i, j: (0, 0)),
                  pl.BlockSpec((tm, 128), lambda i, j: (j, 0)),
                  pl.BlockSpec((tm, 128), lambda i, j: (j, 0))],
        out_specs=[pl.BlockSpec((1, tm, Q_LORA), lambda i, j: (i, j, 0)),
                   pl.BlockSpec((1, tm, KV_LORA), lambda i, j: (i, j, 0)),
                   pl.BlockSpec((1, tm, 128), lambda i, j: (i, j, 0))],
        out_shape=[jax.ShapeDtypeStruct((b, t, Q_LORA), BF16),
                   jax.ShapeDtypeStruct((b, t, KV_LORA), BF16),
                   jax.ShapeDtypeStruct((b, t, 128), BF16)],
        compiler_params=_cparams(("parallel", "parallel")),
        name="mla_down",
    )(x, gain.reshape(1, d), scale, shift, w_cat, q_norm.reshape(1, -1), kv_norm.reshape(1, -1),
      cos_t, sin_t)


def _q_up_kernel(cq_ref, w_ref, cos_ref, sin_ref, q_ref):
    cq = cq_ref[0]
    for h in range(H_C):
        q = jnp.dot(cq, w_ref[h], preferred_element_type=F32)
        qn = q[:, :NOPE] * QK_SCALE
        qr = _rope_tile(q[:, NOPE:], cos_ref[...], sin_ref[...]) * QK_SCALE
        q_ref[0, h] = jnp.concatenate([qn, qr], axis=1).astype(BF16)


def _q_up(cq, w_uq_h, cos_t, sin_t, tm):
    b, t, r = cq.shape
    return pl.pallas_call(
        _q_up_kernel,
        grid=(b, t // tm),
        in_specs=[pl.BlockSpec((1, tm, r), lambda i, j: (i, j, 0)),
                  pl.BlockSpec((H_C, r, QK_DIM), lambda i, j: (0, 0, 0)),
                  pl.BlockSpec((tm, 128), lambda i, j: (j, 0)),
                  pl.BlockSpec((tm, 128), lambda i, j: (j, 0))],
        out_specs=pl.BlockSpec((1, H_C, tm, QK_DIM), lambda i, j: (i, 0, j, 0)),
        out_shape=jax.ShapeDtypeStruct((b, H_C, t, QK_DIM), BF16),
        compiler_params=_cparams(("parallel", "parallel")),
        name="q_up",
    )(cq, w_uq_h, cos_t, sin_t)


def _kv_up_kernel(ckv_ref, kr_ref, w_ref, k_ref, v_ref):
    ckv = ckv_ref[0]
    kr = kr_ref[0]
    for h in range(H_C):
        kv = jnp.dot(ckv, w_ref[h], preferred_element_type=F32)
        k_ref[0, h] = jnp.concatenate([kv[:, :NOPE].astype(BF16), kr], axis=1)
        v_ref[0, h] = kv[:, NOPE:].astype(BF16)


def _kv_up(ckv, kr, w_ukv_h, tm):
    b, s, r = ckv.shape
    return pl.pallas_call(
        _kv_up_kernel,
        grid=(b, s // tm),
        in_specs=[pl.BlockSpec((1, tm, r), lambda i, j: (i, j, 0)),
                  pl.BlockSpec((1, tm, 128), lambda i, j: (i, j, 0)),
                  pl.BlockSpec((H_C, r, NOPE + VH), lambda i, j: (0, 0, 0))],
        out_specs=[pl.BlockSpec((1, H_C, tm, QK_DIM), lambda i, j: (i, 0, j, 0)),
                   pl.BlockSpec((1, H_C, tm, VH), lambda i, j: (i, 0, j, 0))],
        out_shape=[jax.ShapeDtypeStruct((b, H_C, s, QK_DIM), BF16),
                   jax.ShapeDtypeStruct((b, H_C, s, VH), BF16)],
        compiler_params=_cparams(("parallel", "parallel")),
        name="kv_up",
    )(ckv, kr, w_ukv_h)


def _attn_kernel(qa_ref, qb_ref, q0_ref, k_ref, v_ref, o_ref, s0_scr, s1_scr, m_scr, *, tk, nk, tq):
    n = pl.program_id(2)

    def scores(q, j):
        ks = k_ref[0, 0, j * tk:(j + 1) * tk, :]
        return lax.dot_general(q, ks, (((1,), (1,)), ((), ())), preferred_element_type=F32)

    def fold_max(mrun, s):
        for c in range(tk // 128):
            mrun = jnp.maximum(mrun, s[:, c * 128:(c + 1) * 128])
        return mrun

    @pl.when(n == 0)
    def _():
        q0 = q0_ref[0, 0]
        mrun = jnp.full((tq, 128), -jnp.inf, F32)
        for j in range(nk):
            s = scores(q0, j)
            s0_scr[:, j * tk:(j + 1) * tk] = s
            mrun = fold_max(mrun, s)
        m_scr[...] = jnp.max(mrun, axis=-1, keepdims=True)

    def fused(qn, s_read, s_write, m):
        mrun = jnp.full((tq, 128), -jnp.inf, F32)
        lrun = jnp.zeros((tq, 128), F32)
        acc = jnp.zeros((tq, VH), F32)
        for j in range(nk):
            sn = scores(qn, j)
            s_write[:, j * tk:(j + 1) * tk] = sn
            mrun = fold_max(mrun, sn)
            p = jnp.exp2(s_read[:, j * tk:(j + 1) * tk] - m)
            for c in range(tk // 128):
                lrun = lrun + p[:, c * 128:(c + 1) * 128]
            acc = acc + jnp.dot(p.astype(BF16), v_ref[0, 0, j * tk:(j + 1) * tk, :],
                                preferred_element_type=F32)
        out = acc / jnp.sum(lrun, axis=-1, keepdims=True)
        return out, jnp.max(mrun, axis=-1, keepdims=True)

    out_a, m1 = fused(qa_ref[0, 0], s0_scr, s1_scr, m_scr[...])
    o_ref[0, 0:tq, :] = out_a.astype(o_ref.dtype)
    out_b, m2 = fused(qb_ref[0, 0], s1_scr, s0_scr, m1)
    o_ref[0, tq:2 * tq, :] = out_b.astype(o_ref.dtype)
    m_scr[...] = m2


def _attention(q, k, v, tq, tk):
    b, h, t, _ = q.shape
    s = k.shape[2]
    nq = t // tq
    assert t % (2 * tq) == 0 and s % tk == 0
    kern = functools.partial(_attn_kernel, tk=tk, nk=s // tk, tq=tq)
    return pl.pallas_call(
        kern,
        grid=(b, h, nq // 2),
        in_specs=[pl.BlockSpec((1, 1, tq, QK_DIM), lambda i, j, n: (i, j, 2 * n + 1, 0)),
                  pl.BlockSpec((1, 1, tq, QK_DIM),
                               lambda i, j, n: (i, j, jnp.minimum(2 * n + 2, nq - 1), 0)),
                  pl.BlockSpec((1, 1, tq, QK_DIM), lambda i, j, n: (i, j, 0, 0)),
                  pl.BlockSpec((1, 1, s, QK_DIM), lambda i, j, n: (i, j, 0, 0)),
                  pl.BlockSpec((1, 1, s, VH), lambda i, j, n: (i, j, 0, 0))],
        out_specs=pl.BlockSpec((1, 2 * tq, VH), lambda i, j, n: (i, n, j)),
        out_shape=jax.ShapeDtypeStruct((b, t, h * VH), BF16),
        scratch_shapes=[pltpu.VMEM((tq, s), F32), pltpu.VMEM((tq, s), F32),
                        pltpu.VMEM((tq, 1), F32)],
        compiler_params=_cparams(("parallel", "parallel", "arbitrary")),
        name="attention",
    )(q, q, q, k, v)


def _router_kernel(x_ref, g_ref, sc_ref, sh_ref, w_ref, h_ref, lg_ref):
    hn = _modnorm(x_ref[0], g_ref[...], sc_ref[0], sh_ref[0])
    h_ref[0] = hn.astype(BF16)
    lg_ref[0] = jnp.dot(hn, w_ref[...], preferred_element_type=F32,
                        precision=lax.Precision.HIGHEST)


def _router(x, gain, scale, shift, w_router, tm):
    b, t, d = x.shape
    e = w_router.shape[1]
    return pl.pallas_call(
        _router_kernel,
        grid=(b, t // tm),
        in_specs=[pl.BlockSpec((1, tm, d), lambda i, j: (i, j, 0)),
                  pl.BlockSpec((1, d), lambda i, j: (0, 0)),
                  pl.BlockSpec((1, 1, d), lambda i, j: (i, 0, 0)),
                  pl.BlockSpec((1, 1, d), lambda i, j: (i, 0, 0)),
                  pl.BlockSpec((d, e), lambda i, j: (0, 0))],
        out_specs=[pl.BlockSpec((1, tm, d), lambda i, j: (i, j, 0)),
                   pl.BlockSpec((1, tm, e), lambda i, j: (i, j, 0))],
        out_shape=[jax.ShapeDtypeStruct((b, t, d), BF16),
                   jax.ShapeDtypeStruct((b, t, e), F32)],
        compiler_params=_cparams(("parallel", "parallel")),
        name="router",
    )(x, gain.reshape(1, d), scale, shift, w_router)


def _ffn_kernel(x_ref, wg_ref, wu_ref, wd_ref, g_ref, o_ref, *, fc):
    x = x_ref[0, 0]
    nf = wg_ref.shape[-1]
    for c in range(nf // fc):
        sl = slice(c * fc, (c + 1) * fc)
        hg = jnp.dot(x, wg_ref[0, :, sl], preferred_element_type=F32)
        hu = jnp.dot(x, wu_ref[0, :, sl], preferred_element_type=F32)
        hid = (_silu(hg) * hu).astype(BF16)
        part = jnp.dot(hid, wd_ref[0, sl, :], preferred_element_type=F32)
        if c == 0:
            o_ref[0, 0] = part
        else:
            o_ref[0, 0] += part
    o_ref[0, 0] = o_ref[0, 0] * g_ref[0, 0]


def _expert_ffn(xg, gates, w_gate, w_up, w_down, tm, fc):
    b, e, cap, d = xg.shape
    f = w_gate.shape[-1]
    tm = min(tm, cap)
    kern = functools.partial(_ffn_kernel, fc=fc)
    return pl.pallas_call(
        kern,
        grid=(b, e, cap // tm),
        in_specs=[pl.BlockSpec((1, 1, tm, d), lambda i, j, m: (i, j, m, 0)),
                  pl.BlockSpec((1, d, f), lambda i, j, m: (j, 0, 0)),
                  pl.BlockSpec((1, d, f), lambda i, j, m: (j, 0, 0)),
                  pl.BlockSpec((1, f, d), lambda i, j, m: (j, 0, 0)),
                  pl.BlockSpec((1, 1, tm, 1), lambda i, j, m: (i, j, m, 0))],
        out_specs=pl.BlockSpec((1, 1, tm, d), lambda i, j, m: (i, j, m, 0)),
        out_shape=jax.ShapeDtypeStruct((b, e, cap, d), F32),
        compiler_params=_cparams(("parallel", "parallel", "arbitrary")),
        name="expert_ffn",
    )(xg, w_gate, w_up, w_down, gates)


def _moe(x, gain, scale, shift, gate_out, w_router, w_gate, w_up, w_down):
    b, t, d = x.shape
    cap = CAP_FACTOR * t // N_EXPERTS
    hn, logits = _router(x, gain, scale, shift, w_router, min(256, t))
    aff = jax.nn.softmax(logits, axis=-1)
    g, idx = lax.top_k(jnp.swapaxes(aff, 1, 2), cap)
    bidx = jnp.arange(b)[:, None, None]
    xg = hn[bidx, idx]
    y = _expert_ffn(xg, g[..., None], w_gate, w_up, w_down, 256, 512)
    upd = jnp.zeros((b, t, d), F32).at[bidx, idx].add(y)
    return x + gate_out * upd


def _final_kernel(x_ref, g_ref, o_ref):
    o_ref[0] = _rms(x_ref[0], g_ref[...])


def _final_norm(x, gain, tm):
    b, t, d = x.shape
    return pl.pallas_call(
        _final_kernel,
        grid=(b, t // tm),
        in_specs=[pl.BlockSpec((1, tm, d), lambda i, j: (i, j, 0)),
                  pl.BlockSpec((1, d), lambda i, j: (0, 0))],
        out_specs=pl.BlockSpec((1, tm, d), lambda i, j: (i, j, 0)),
        out_shape=jax.ShapeDtypeStruct((b, t, d), F32),
        compiler_params=_cparams(("parallel", "parallel")),
        name="final_norm",
    )(x, gain.reshape(1, d))


def kernel(x, c, ctx, c_ctx, w_mod, b_mod, norm_mix, norm_ffn, norm_out, w_in, w_out_rec,
           conv_w, conv_b, lb_gamma, dt_bias, a_log, d_skip, hgrn_norm, mamba_norm,
           w_dq, q_norm, w_uq, w_dkv, kv_norm, w_ukv, w_kr, w_o,
           w_router, w_gate, w_up, w_down):
    B, T, D = x.shape
    TC = ctx.shape[1]
    lb = jnp.cumsum(jax.nn.softmax(lb_gamma.astype(F32), axis=0), axis=0)

    cvec = jnp.zeros((16, D), F32).at[:B].set(c).at[B].set(c_ctx)
    mods = _modulation(cvec, w_mod, b_mod)

    x_lat, x_ctx = x, ctx
    for l in range(DEPTH):
        need_ctx = l < DEPTH - 1
        m_lat = [mods[l, :B, i * D:(i + 1) * D][:, None, :] for i in range(6)]
        m_ctx = [jnp.broadcast_to(mods[l, B, i * D:(i + 1) * D][None, None, :], (B, 1, D))
                 for i in range(6)]
        if l % 2 == 0:
            e = l // 2
            w_in_p = jnp.pad(w_in[e], ((0, 0), (0, IN_COLS_PAD - IN_COLS))).astype(BF16)
            xs = jnp.concatenate([x_ctx, x_lat], axis=1)
            pair = lambda i: jnp.concatenate([m_ctx[i], m_lat[i]], axis=1)
            p = _modproj(xs, norm_mix[l], pair(1), pair(0), w_in_p, 256, TC // 256)
            ohf, omf, ohb, omb = _scan_mixers(p, TC // CHUNK, lb[e], conv_w[e], conv_b[e],
                                              dt_bias[e], a_log[e], d_skip[e])
            xs = _merge_mixers(xs, pair(2), ohf, ohb, omf, omb, p, hgrn_norm[e], mamba_norm[e],
                               w_out_rec[e].astype(BF16), TC // 256, 256)
            x_ctx, x_lat = xs[:, :TC], xs[:, TC:]
        else:
            j = l // 2
            cos_l, sin_l = _rope_tables(T)
            cos_c = jnp.concatenate([jnp.ones((TC, ROPE), F32), jnp.zeros((TC, ROPE), F32)], -1)
            sin_c = jnp.zeros((TC, 128), F32)
            w_cat = jnp.concatenate([w_dq[j], w_dkv[j], w_kr[j], _swap_cols(w_kr[j])],
                                    axis=1).astype(BF16)
            wq = w_uq[j].reshape(Q_LORA, H_C, NOPE + ROPE)
            wq_h = jnp.concatenate([wq, _swap_cols(wq[..., NOPE:])], axis=-1)
            wq_h = jnp.transpose(wq_h, (1, 0, 2)).astype(BF16)
            wkv_h = jnp.transpose(w_ukv[j].reshape(KV_LORA, H_C, NOPE + VH), (1, 0, 2)).astype(BF16)
            cq_l, ckv_l, kr_l = _mla_down(x_lat, norm_mix[l], m_lat[1], m_lat[0], w_cat,
                                          q_norm[j], kv_norm[j], cos_l, sin_l, 256)
            _, ckv_c, kr_c = _mla_down(x_ctx, norm_mix[l], m_ctx[1], m_ctx[0], w_cat,
                                       q_norm[j], kv_norm[j], cos_c, sin_c, 256)
            ckv = jnp.concatenate([ckv_c, ckv_l], axis=1)
            kr = jnp.concatenate([kr_c, kr_l], axis=1)
            qh = _q_up(cq_l, wq_h, cos_l, sin_l, 512)
            kh, vh = _kv_up(ckv, kr, wkv_h, 768)
            o = _attention(qh, kh, vh, 256, 768)
            x_lat = _linear_residual(x_lat, m_lat[2], o, w_o[j].astype(BF16), 512)
        x_lat = _moe(x_lat, norm_ffn[l], m_lat[4], m_lat[3], m_lat[5], w_router[l],
                     w_gate[l].astype(BF16), w_up[l].astype(BF16), w_down[l].astype(BF16))
        if need_ctx:
            x_ctx = _moe(x_ctx, norm_ffn[l], m_ctx[4], m_ctx[3], m_ctx[5], w_router[l],
                         w_gate[l].astype(BF16), w_up[l].astype(BF16), w_down[l].astype(BF16))
    return _final_norm(x_lat, norm_out, 512)
```

```python
import functools
import math

import jax
import jax.numpy as jnp
from jax import lax
from jax.experimental import pallas as pl
from jax.experimental.pallas import tpu as pltpu

F32 = jnp.float32
BF16 = jnp.bfloat16

D_MODEL = 1024
DEPTH = 2
GRID_W = 64
EPS = 1e-6
CHUNK = 64
HA, DKA, DVA = 8, 64, 64
WA = HA * DVA
HB, PB = 8, 64
D_INNER = HB * PB
N_GROUPS, D_STATE = 2, 128
CONV_W = 5
REC_SPLITS = (HA * DKA, HA * DKA, HA * DKA, WA, WA, D_INNER, D_INNER,
              N_GROUPS * D_STATE, N_GROUPS * D_STATE, HB, HB)
IN_COLS = sum(REC_SPLITS)
IN_COLS_PAD = 4224
H_C, NOPE, ROPE, VH = 8, 128, 64, 128
Q_LORA, KV_LORA = 384, 256
ROPE_THETA = 10000.0
ATTN_SCALE = 1.0 / math.sqrt(NOPE + ROPE)
QK_SCALE = ATTN_SCALE * math.log2(math.e)
QK_DIM = 256
N_EXPERTS = 16
EXPERT_FF = 1024
CAP_FACTOR = 2

VMEM_LIMIT = 56 * 1024 * 1024


def _cparams(sem):
    return pltpu.CompilerParams(dimension_semantics=sem, vmem_limit_bytes=VMEM_LIMIT)


def _silu(v):
    return v * jax.nn.sigmoid(v)


def _modnorm(x, gain, scale, shift):
    ms = jnp.mean(x * x, axis=-1, keepdims=True)
    return (x * lax.rsqrt(ms + EPS) * gain) * (1.0 + scale) + shift


def _mod_kernel(s_ref, w_ref, b_ref, o_ref):
    s = _silu(s_ref[...])
    o_ref[0] = jnp.dot(s, w_ref[0], preferred_element_type=F32,
                       precision=lax.Precision.HIGHEST) + b_ref[0]


def _modulation(cvec, w_mod, b_mod):
    n = w_mod.shape[-1]
    tn = 1536
    return pl.pallas_call(
        _mod_kernel,
        grid=(DEPTH, n // tn),
        in_specs=[pl.BlockSpec((16, D_MODEL), lambda l, j: (0, 0)),
                  pl.BlockSpec((1, D_MODEL, tn), lambda l, j: (l, 0, j)),
                  pl.BlockSpec((1, 1, tn), lambda l, j: (l, 0, j))],
        out_specs=pl.BlockSpec((1, 16, tn), lambda l, j: (l, 0, j)),
        out_shape=jax.ShapeDtypeStruct((DEPTH, 16, n), F32),
        compiler_params=_cparams(("parallel", "parallel")),
        name="modulation",
    )(cvec, w_mod, b_mod.reshape(DEPTH, 1, n))


def _modproj_kernel(x_ref, g_ref, sc_ref, sh_ref, w_ref, o_ref):
    a = _modnorm(x_ref[0], g_ref[...], sc_ref[0, 0], sh_ref[0, 0]).astype(BF16)
    o_ref[0] = jnp.dot(a, w_ref[...], preferred_element_type=F32).astype(o_ref.dtype)


def _modproj(x, gain, scale2, shift2, w, tm, nct_tiles, out_dtype=F32):
    b, t, d = x.shape
    n = w.shape[1]
    sel = lambda i, j: (i, jnp.where(j < nct_tiles, 0, 1), 0, 0)
    return pl.pallas_call(
        _modproj_kernel,
        grid=(b, t // tm),
        in_specs=[pl.BlockSpec((1, tm, d), lambda i, j: (i, j, 0)),
                  pl.BlockSpec((1, d), lambda i, j: (0, 0)),
                  pl.BlockSpec((1, 1, 1, d), sel),
                  pl.BlockSpec((1, 1, 1, d), sel),
                  pl.BlockSpec((d, n), lambda i, j: (0, 0))],
        out_specs=pl.BlockSpec((1, tm, n), lambda i, j: (i, j, 0)),
        out_shape=jax.ShapeDtypeStruct((b, t, n), out_dtype),
        compiler_params=_cparams(("parallel", "parallel")),
        name="modproj",
    )(x, gain.reshape(1, d), scale2.reshape(b, 2, 1, d), shift2.reshape(b, 2, 1, d), w)


def _linres_kernel(x_ref, gate_ref, y_ref, w_ref, o_ref):
    acc = jnp.dot(y_ref[0].astype(BF16), w_ref[...], preferred_element_type=F32)
    o_ref[0] = x_ref[0] + gate_ref[0] * acc


def _linear_residual(x, gate, y, w, tm):
    b, t, d = x.shape
    k = y.shape[-1]
    return pl.pallas_call(
        _linres_kernel,
        grid=(b, t // tm),
        in_specs=[pl.BlockSpec((1, tm, d), lambda i, j: (i, j, 0)),
                  pl.BlockSpec((1, 1, d), lambda i, j: (i, 0, 0)),
                  pl.BlockSpec((1, tm, k), lambda i, j: (i, j, 0)),
                  pl.BlockSpec((k, d), lambda i, j: (0, 0))],
        out_specs=pl.BlockSpec((1, tm, d), lambda i, j: (i, j, 0)),
        out_shape=jax.ShapeDtypeStruct((b, t, d), F32),
        compiler_params=_cparams(("parallel", "parallel")),
        name="linear_residual",
    )(x, gate, y, w)


L = CHUNK
COL_Q, COL_FF, COL_FB, COL_I, COL_G, COL_Z, COL_X, COL_DT = 0, 512, 1024, 1536, 2048, 2560, 3072, 4096


def _rms(x, gain):
    return x * lax.rsqrt(jnp.mean(x * x, axis=-1, keepdims=True) + EPS) * gain


def _split3(x):
    a = x.astype(BF16)
    r = x - a.astype(F32)
    b = r.astype(BF16)
    c = (r - b.astype(F32)).astype(BF16)
    return a, b, c


def _cumsum_exact(tri3, x):
    a, b, c = _split3(x)
    return jnp.dot(tri3, jnp.concatenate([a, b, c], axis=0), preferred_element_type=F32)


def _expand_exact(x, e3):
    a, b, c = _split3(x)
    return jnp.dot(jnp.concatenate([a, b, c], axis=1), e3, preferred_element_type=F32)


def _dot_nt(a, b):
    return lax.dot_general(a, b, (((1,), (1,)), ((), ())), preferred_element_type=F32)


def _dot_tn(a, b):
    return lax.dot_general(a, b, (((0,), (0,)), ((), ())), preferred_element_type=F32)


def _scan_dir(rev, q_ref, f_ref, v_ref, x_ref, xp_ref, xn_ref, dt_ref, pv, nv,
              lb_row, cw_ref, cbias_ref, dtb_ref, aexp_ref, dskip_ref,
              oh_ref, om_ref, sg_ref, sm_ref, u_scr):
    d = 1 if rev else 0
    ti = lax.broadcasted_iota(jnp.int32, (L, L), 0)
    si = lax.broadcasted_iota(jnp.int32, (L, L), 1)
    keep = (si >= ti) if rev else (si <= ti)
    tri = keep.astype(BF16)
    tri3 = jnp.concatenate([tri, tri, tri], axis=1)
    t2 = lax.broadcasted_iota(jnp.int32, (L, 128), 0)
    s2 = lax.broadcasted_iota(jnp.int32, (L, 128), 1) % L
    keep2 = (s2 >= t2) if rev else (s2 <= t2)
    ref_row = (L - 1 - L // 2) if rev else L // 2
    last_row = 0 if rev else L - 1
    lane = lax.broadcasted_iota(jnp.int32, (1, 128), 1)
    lo = lane < 64
    r128 = lax.broadcasted_iota(jnp.int32, (128, 128), 0)
    c128 = lax.broadcasted_iota(jnp.int32, (128, 128), 1)
    blockdiag = (r128 < 64) == (c128 < 64)
    zero_b = jnp.zeros((), BF16)

    def pair_rows(a):
        return jnp.concatenate([jnp.where(lo, a, zero_b), jnp.where(lo, zero_b, a)], axis=0)

    q = q_ref[0]
    qs = _silu(q)
    f = lb_row + (1.0 - lb_row) * jax.nn.sigmoid(f_ref[0])
    k = 1.0 - f
    la = jnp.log(f)
    b = _cumsum_exact(tri3, la)
    bref = b[ref_row:ref_row + 1, :]
    blast = b[last_row:last_row + 1, :]
    qd = (qs * jnp.exp(b - bref)).astype(BF16)
    kd = (k * jnp.exp(bref - b)).astype(BF16)
    kl = (k * jnp.exp(blast - b)).astype(BF16)
    qb = (qs * jnp.exp(b)).astype(BF16)
    dec_col = jnp.broadcast_to(jnp.exp(blast), (8, 512)).T
    vb = v_ref[0].astype(BF16)
    for p in range(HA // 2):
        sl = slice(128 * p, 128 * (p + 1))
        qd_p, kd_p, kl_p, qb_p, v_p = qd[:, sl], kd[:, sl], kl[:, sl], qb[:, sl], vb[:, sl]
        st = sg_ref[d, p]
        att = _dot_nt(qd_p, pair_rows(kd_p))
        att = jnp.where(keep2, att, 0.0).astype(BF16)
        lhs = jnp.concatenate([att, qb_p], axis=1)
        rhs = jnp.concatenate([pair_rows(v_p), st.astype(BF16)], axis=0)
        oh_ref[0, :, sl] = jnp.dot(lhs, rhs, preferred_element_type=F32)
        upd = _dot_tn(kl_p, v_p)
        sg_ref[d, p] = st * dec_col[sl, 0:1] + jnp.where(blockdiag, upd, 0.0)

    u_scr[0:8, :] = xp_ref[0] * pv
    u_scr[8:8 + L, :] = x_ref[0]
    u_scr[8 + L:16 + L, :] = xn_ref[0] * nv
    y = cbias_ref[...] + cw_ref[0:1, :] * u_scr[6:6 + L, :]
    for j in range(1, 5):
        y = y + cw_ref[j:j + 1, :] * u_scr[6 + j:6 + j + L, :]
    xbc = _silu(y)
    xm = xbc[:, :512]
    bm = xbc[:, 512:768].astype(BF16)
    cm = xbc[:, 768:1024].astype(BF16)
    draw = dt_ref[0] + dtb_ref[...]
    dt = jnp.maximum(draw, 0.0) + jnp.log(1.0 + jnp.exp(-jnp.abs(draw)))
    la_m = -dt * aexp_ref[...]
    cum = _cumsum_exact(tri3, la_m)
    er = lax.broadcasted_iota(jnp.int32, (128, 512), 0)
    ec = lax.broadcasted_iota(jnp.int32, (128, 512), 1)
    e1 = (er == (ec // 64) + 8 * d).astype(BF16)
    e1 = jnp.concatenate([e1, e1, e1], axis=0)
    both = _expand_exact(jnp.concatenate([dt, cum], axis=0), e1)
    dt_e, cum_e = both[:L], both[L:]
    cum_t = cum.T
    xdt = xm * dt_e
    clast = cum_e[last_row:last_row + 1, :]
    wx = (jnp.exp(clast - cum_e) * xdt).astype(BF16)
    dec_m = jnp.exp(clast)
    ecum = jnp.exp(cum_e)
    xdt_b = xdt.astype(BF16)
    for g in range(2):
        bm_g = bm[:, 128 * g:128 * (g + 1)]
        cm_g = cm[:, 128 * g:128 * (g + 1)]
        gl = slice(256 * g, 256 * (g + 1))
        cb2 = _dot_nt(cm_g, jnp.concatenate([bm_g, bm_g], axis=0))
        sm = sm_ref[d, g]
        y_int = jnp.dot(cm_g, sm.astype(BF16), preferred_element_type=F32) * ecum[:, gl]
        for pp in range(2):
            h0 = 4 * g + 2 * pp + 8 * d
            sl = slice(256 * g + 128 * pp, 256 * g + 128 * (pp + 1))
            crow = jnp.concatenate([cum_t[h0:h0 + 1, :], cum_t[h0 + 1:h0 + 2, :]], axis=1)
            lm = jnp.exp(jnp.where(keep2, cum_e[:, sl] - crow, -jnp.inf))
            yp = jnp.dot((cb2 * lm).astype(BF16), pair_rows(xdt_b[:, sl]), preferred_element_type=F32)
            yp = yp + y_int[:, 128 * pp:128 * (pp + 1)]
            if not rev:
                yp = yp + dskip_ref[:, sl] * xm[:, sl]
            om_ref[0, :, sl] = yp
        sm_ref[d, g] = sm * dec_m[:, gl] + _dot_tn(bm_g, wx[:, gl])


def _scan_kernel(qf, ff, vf, xf, xpf, xnf, dtf, qb, fb, vb_, xb, xpb, xnb, dtb,
                 lb_ref, cw_ref, cbias_ref, dtbias_ref, aexp_ref, dskip_ref,
                 ohf, omf, ohb, omb, sg_ref, sm_ref, u_scr, *, nct, nc):
    c = pl.program_id(1)

    @pl.when(c == 0)
    def _():
        sg_ref[...] = jnp.zeros(sg_ref.shape, F32)
        sm_ref[...] = jnp.zeros(sm_ref.shape, F32)

    cb = jnp.where(c < nct, nct - 1 - c, nc + nct - 1 - c)

    def edge_flags(ch):
        pv = jnp.where((ch == 0) | (ch == nct), 0.0, 1.0)
        nv = jnp.where((ch == nct - 1) | (ch == nc - 1), 0.0, 1.0)
        return pv, nv

    pvf, nvf = edge_flags(c)
    pvb, nvb = edge_flags(cb)
    _scan_dir(False, qf, ff, vf, xf, xpf, xnf, dtf, pvf, nvf, lb_ref[0:1, :], cw_ref, cbias_ref,
              dtbias_ref, aexp_ref, dskip_ref, ohf, omf, sg_ref, sm_ref, u_scr)
    _scan_dir(True, qb, fb, vb_, xb, xpb, xnb, dtb, pvb, nvb, lb_ref[1:2, :], cw_ref, cbias_ref,
              dtbias_ref, aexp_ref, dskip_ref, ohb, omb, sg_ref, sm_ref, u_scr)


def _scan_mixers(p, nct, lb, conv_w, conv_b, dt_bias, a_log, d_skip):
    bsz, s, _ = p.shape
    nc = s // L
    nb8 = s // 8

    def bmap(c):
        return jnp.where(c < nct, nct - 1 - c, nc + nct - 1 - c)

    def specs(cmap, fcol):
        colblk = lambda col, w: col // w
        return [
            pl.BlockSpec((1, L, 512), lambda i, c: (i, cmap(c), colblk(COL_Q, 512))),
            pl.BlockSpec((1, L, 512), lambda i, c: (i, cmap(c), colblk(fcol, 512))),
            pl.BlockSpec((1, L, 512), lambda i, c: (i, cmap(c), colblk(COL_I, 512))),
            pl.BlockSpec((1, L, 1024), lambda i, c: (i, cmap(c), colblk(COL_X, 1024))),
            pl.BlockSpec((1, 8, 1024), lambda i, c: (i, jnp.maximum(cmap(c) * 8 - 1, 0), colblk(COL_X, 1024))),
            pl.BlockSpec((1, 8, 1024), lambda i, c: (i, jnp.minimum(cmap(c) * 8 + 8, nb8 - 1), colblk(COL_X, 1024))),
            pl.BlockSpec((1, L, 128), lambda i, c: (i, cmap(c), colblk(COL_DT, 128))),
        ]

    ident = lambda c: c
    const2 = lambda i, c: (0, 0)
    dtb = jnp.zeros((1, 128), F32).at[0, :16].set(dt_bias.reshape(16))
    aexp = jnp.zeros((1, 128), F32).at[0, :16].set(jnp.exp(a_log.reshape(16)))
    dsk = jnp.repeat(d_skip, 64).reshape(1, 512)
    kern = functools.partial(_scan_kernel, nct=nct, nc=nc)
    out_sd = jax.ShapeDtypeStruct((bsz, s, 512), F32)
    return pl.pallas_call(
        kern,
        grid=(bsz, nc),
        in_specs=specs(ident, COL_FF) + specs(bmap, COL_FB) + [
            pl.BlockSpec((2, 512), const2), pl.BlockSpec((5, 1024), const2),
            pl.BlockSpec((1, 1024), const2), pl.BlockSpec((1, 128), const2),
            pl.BlockSpec((1, 128), const2), pl.BlockSpec((1, 512), const2)],
        out_specs=[pl.BlockSpec((1, L, 512), lambda i, c: (i, c, 0)),
                   pl.BlockSpec((1, L, 512), lambda i, c: (i, c, 0)),
                   pl.BlockSpec((1, L, 512), lambda i, c: (i, bmap(c), 0)),
                   pl.BlockSpec((1, L, 512), lambda i, c: (i, bmap(c), 0))],
        out_shape=[out_sd, out_sd, out_sd, out_sd],
        scratch_shapes=[pltpu.VMEM((2, HA // 2, 128, 128), F32),
                        pltpu.VMEM((2, 2, 128, 256), F32),
                        pltpu.VMEM((L + 16, 1024), F32)],
        compiler_params=_cparams(("parallel", "arbitrary")),
        name="scan_mixers",
    )(*([p] * 14), lb, conv_w, conv_b.reshape(1, 1024), dtb, aexp, dsk)


def _merge_kernel(ohf, ohb, omf, omb, g_ref, z_ref, x_ref, gate_ref, hn_ref, mn_ref, w_ref, o_ref):
    oh = ohf[0] + ohb[0]
    sq = oh * oh
    hi = sq.astype(BF16)
    lo = (sq - hi.astype(F32)).astype(BF16)
    r = lax.broadcasted_iota(jnp.int32, (512, 512), 0) // 64
    c = lax.broadcasted_iota(jnp.int32, (512, 512), 1) // 64
    avg = jnp.where(r == c, 1.0 / 64, 0.0).astype(BF16)
    ms = jnp.dot(hi, avg, preferred_element_type=F32) + jnp.dot(lo, avg, preferred_element_type=F32)
    oh = oh * lax.rsqrt(ms + EPS) * hn_ref[...] * jax.nn.sigmoid(g_ref[0])
    y = (omf[0] + omb[0]) * _silu(z_ref[0])
    y = y * lax.rsqrt(jnp.mean(y * y, axis=-1, keepdims=True) + EPS) * mn_ref[...]
    cat = jnp.concatenate([oh, y], axis=1).astype(BF16)
    o_ref[0] = x_ref[0] + gate_ref[0, 0] * jnp.dot(cat, w_ref[...], preferred_element_type=F32)


def _merge_mixers(x, gate2, ohf, ohb, omf, omb, p, hgrn_norm, mamba_norm, w_out, nct_tiles, tm):
    b, s, d = x.shape
    sel = lambda i, j: (i, jnp.where(j < nct_tiles, 0, 1), 0, 0)
    row = lambda i, j: (i, j, 0)
    const2 = lambda i, j: (0, 0)
    return pl.pallas_call(
        _merge_kernel,
        grid=(b, s // tm),
        in_specs=[pl.BlockSpec((1, tm, 512), row)] * 4 + [
            pl.BlockSpec((1, tm, 512), lambda i, j: (i, j, COL_G // 512)),
            pl.BlockSpec((1, tm, 512), lambda i, j: (i, j, COL_Z // 512)),
            pl.BlockSpec((1, tm, d), row),
            pl.BlockSpec((1, 1, 1, d), sel),
            pl.BlockSpec((1, 512), const2), pl.BlockSpec((1, 512), const2),
            pl.BlockSpec((2 * 512, d), const2)],
        out_specs=pl.BlockSpec((1, tm, d), row),
        out_shape=jax.ShapeDtypeStruct((b, s, d), F32),
        compiler_params=_cparams(("parallel", "parallel")),
        name="merge_mixers",
    )(ohf, ohb, omf, omb, p, p, x, gate2.reshape(b, 2, 1, d), hgrn_norm.reshape(1, 512),
      mamba_norm.reshape(1, 512), w_out)


def _rope_tables(t):
    rows = t // GRID_W
    row = jnp.repeat(jnp.arange(rows, dtype=jnp.int32), GRID_W)
    col = jnp.tile(jnp.arange(GRID_W, dtype=jnp.int32), rows)
    nf = ROPE // 4
    inv_freq = ROPE_THETA ** (-jnp.arange(nf, dtype=F32) / nf)
    pos = jnp.stack([row, col], axis=-1).astype(F32)
    ang = pos[..., None] * inv_freq
    cos, sin = jnp.cos(ang), jnp.sin(ang)
    cos64 = jnp.broadcast_to(cos[:, :, None, :], (t, 2, 2, nf)).reshape(t, ROPE)
    sin64 = jnp.broadcast_to(sin[:, :, None, :], (t, 2, 2, nf)).reshape(t, ROPE)
    zero = jnp.zeros((t, ROPE), F32)
    return jnp.concatenate([cos64, zero], -1), jnp.concatenate([sin64, zero], -1)


def _swap_cols(w):
    nf = ROPE // 4
    wr = w.reshape(w.shape[:-1] + (2, 2, nf))
    return jnp.stack([-wr[..., 1, :], wr[..., 0, :]], axis=-2).reshape(w.shape)


def _rope_tile(tile, cos_t, sin_t):
    return tile * cos_t + pltpu.roll(tile, ROPE, axis=1) * sin_t


def _mla_down_kernel(x_ref, g_ref, sc_ref, sh_ref, w_ref, qn_ref, kvn_ref, cos_ref, sin_ref,
                     cq_ref, ckv_ref, kr_ref):
    a = _modnorm(x_ref[0], g_ref[...], sc_ref[0], sh_ref[0]).astype(BF16)
    c = jnp.dot(a, w_ref[...], preferred_element_type=F32)
    cq = c[:, :Q_LORA]
    ckv = c[:, Q_LORA:Q_LORA + KV_LORA]
    cq_ref[0] = _rms(cq, qn_ref[...]).astype(BF16)
    ckv_ref[0] = _rms(ckv, kvn_ref[...]).astype(BF16)
    kr_ref[0] = _rope_tile(c[:, Q_LORA + KV_LORA:], cos_ref[...], sin_ref[...]).astype(BF16)


def _mla_down(x, gain, scale, shift, w_cat, q_norm, kv_norm, cos_t, sin_t, tm):
    b, t, d = x.shape
    n = w_cat.shape[1]
    return pl.pallas_call(
        _mla_down_kernel,
        grid=(b, t // tm),
        in_specs=[pl.BlockSpec((1, tm, d), lambda i, j: (i, j, 0)),
                  pl.BlockSpec((1, d), lambda i, j: (0, 0)),
                  pl.BlockSpec((1, 1, d), lambda i, j: (i, 0, 0)),
                  pl.BlockSpec((1, 1, d), lambda i, j: (i, 0, 0)),
                  pl.BlockSpec((d, n), lambda i, j: (0, 0)),
                  pl.BlockSpec((1, Q_LORA), lambda i, j: (0, 0)),
                  pl.BlockSpec((1, KV_LORA), lambda i, j: (0, 0)),
                  pl.BlockSpec((tm, 128), lambda i, j: (j, 0)),
                  pl.BlockSpec((tm, 128), lambda i, j: (j, 0))],
        out_specs=[pl.BlockSpec((1, tm, Q_LORA), lambda i, j: (i, j, 0)),
                   pl.BlockSpec((1, tm, KV_LORA), lambda i, j: (i, j, 0)),
                   pl.BlockSpec((1, tm, 128), lambda i, j: (i, j, 0))],
        out_shape=[jax.ShapeDtypeStruct((b, t, Q_LORA), BF16),
                   jax.ShapeDtypeStruct((b, t, KV_LORA), BF16),
                   jax.ShapeDtypeStruct((b, t, 128), BF16)],
        compiler_params=_cparams(("parallel", "parallel")),
        name="mla_down",
    )(x, gain.reshape(1, d), scale, shift, w_cat, q_norm.reshape(1, -1), kv_norm.reshape(1, -1),
      cos_t, sin_t)


def _q_up_kernel(cq_ref, w_ref, cos_ref, sin_ref, q_ref):
    cq = cq_ref[0]
    for h in range(H_C):
        q = jnp.dot(cq, w_ref[h], preferred_element_type=F32)
        qn = q[:, :NOPE] * QK_SCALE
        qr = _rope_tile(q[:, NOPE:], cos_ref[...], sin_ref[...]) * QK_SCALE
        q_ref[0, h] = jnp.concatenate([qn, qr], axis=1).astype(BF16)


def _q_up(cq, w_uq_h, cos_t, sin_t, tm):
    b, t, r = cq.shape
    return pl.pallas_call(
        _q_up_kernel,
        grid=(b, t // tm),
        in_specs=[pl.BlockSpec((1, tm, r), lambda i, j: (i, j, 0)),
                  pl.BlockSpec((H_C, r, QK_DIM), lambda i, j: (0, 0, 0)),
                  pl.BlockSpec((tm, 128), lambda i, j: (j, 0)),
                  pl.BlockSpec((tm, 128), lambda i, j: (j, 0))],
        out_specs=pl.BlockSpec((1, H_C, tm, QK_DIM), lambda i, j: (i, 0, j, 0)),
        out_shape=jax.ShapeDtypeStruct((b, H_C, t, QK_DIM), BF16),
        compiler_params=_cparams(("parallel", "parallel")),
        name="q_up",
    )(cq, w_uq_h, cos_t, sin_t)


def _kv_up_kernel(ckv_ref, kr_ref, w_ref, k_ref, v_ref):
    ckv = ckv_ref[0]
    kr = kr_ref[0]
    for h in range(H_C):
        kv = jnp.dot(ckv, w_ref[h], preferred_element_type=F32)
        k_ref[0, h] = jnp.concatenate([kv[:, :NOPE].astype(BF16), kr], axis=1)
        v_ref[0, h] = kv[:, NOPE:].astype(BF16)


def _kv_up(ckv, kr, w_ukv_h, tm):
    b, s, r = ckv.shape
    return pl.pallas_call(
        _kv_up_kernel,
        grid=(b, s // tm),
        in_specs=[pl.BlockSpec((1, tm, r), lambda i, j: (i, j, 0)),
                  pl.BlockSpec((1, tm, 128), lambda i, j: (i, j, 0)),
                  pl.BlockSpec((H_C, r, NOPE + VH), lambda i, j: (0, 0, 0))],
        out_specs=[pl.BlockSpec((1, H_C, tm, QK_DIM), lambda i, j: (i, 0, j, 0)),
                   pl.BlockSpec((1, H_C, tm, VH), lambda i, j: (i, 0, j, 0))],
        out_shape=[jax.ShapeDtypeStruct((b, H_C, s, QK_DIM), BF16),
                   jax.ShapeDtypeStruct((b, H_C, s, VH), BF16)],
        compiler_params=_cparams(("parallel", "parallel")),
        name="kv_up",
    )(ckv, kr, w_ukv_h)


def _attn_kernel(qa_ref, qb_ref, q0_ref, k_ref, v_ref, o_ref, s0_scr, s1_scr, m_scr, *, tk, nk, tq):
    n = pl.program_id(2)

    def scores(q, j):
        ks = k_ref[0, 0, j * tk:(j + 1) * tk, :]
        return lax.dot_general(q, ks, (((1,), (1,)), ((), ())), preferred_element_type=F32)

    def fold_max(mrun, s):
        for c in range(tk // 128):
            mrun = jnp.maximum(mrun, s[:, c * 128:(c + 1) * 128])
        return mrun

    @pl.when(n == 0)
    def _():
        q0 = q0_ref[0, 0]
        mrun = jnp.full((tq, 128), -jnp.inf, F32)
        for j in range(nk):
            s = scores(q0, j)
            s0_scr[:, j * tk:(j + 1) * tk] = s
            mrun = fold_max(mrun, s)
        m_scr[...] = jnp.max(mrun, axis=-1, keepdims=True)

    def fused(qn, s_read, s_write, m):
        mrun = jnp.full((tq, 128), -jnp.inf, F32)
        lrun = jnp.zeros((tq, 128), F32)
        acc = jnp.zeros((tq, VH), F32)
        for j in range(nk):
            sn = scores(qn, j)
            s_write[:, j * tk:(j + 1) * tk] = sn
            mrun = fold_max(mrun, sn)
            p = jnp.exp2(s_read[:, j * tk:(j + 1) * tk] - m)
            for c in range(tk // 128):
                lrun = lrun + p[:, c * 128:(c + 1) * 128]
            acc = acc + jnp.dot(p.astype(BF16), v_ref[0, 0, j * tk:(j + 1) * tk, :],
                                preferred_element_type=F32)
        out = acc / jnp.sum(lrun, axis=-1, keepdims=True)
        return out, jnp.max(mrun, axis=-1, keepdims=True)

    out_a, m1 = fused(qa_ref[0, 0], s0_scr, s1_scr, m_scr[...])
    o_ref[0, 0:tq, :] = out_a.astype(o_ref.dtype)
    out_b, m2 = fused(qb_ref[0, 0], s1_scr, s0_scr, m1)
    o_ref[0, tq:2 * tq, :] = out_b.astype(o_ref.dtype)
    m_scr[...] = m2


def _attention(q, k, v, tq, tk):
    b, h, t, _ = q.shape
    s = k.shape[2]
    nq = t // tq
    assert t % (2 * tq) == 0 and s % tk == 0
    kern = functools.partial(_attn_kernel, tk=tk, nk=s // tk, tq=tq)
    return pl.pallas_call(
        kern,
        grid=(b, h, nq // 2),
        in_specs=[pl.BlockSpec((1, 1, tq, QK_DIM), lambda i, j, n: (i, j, 2 * n + 1, 0)),
                  pl.BlockSpec((1, 1, tq, QK_DIM),
                               lambda i, j, n: (i, j, jnp.minimum(2 * n + 2, nq - 1), 0)),
                  pl.BlockSpec((1, 1, tq, QK_DIM), lambda i, j, n: (i, j, 0, 0)),
                  pl.BlockSpec((1, 1, s, QK_DIM), lambda i, j, n: (i, j, 0, 0)),
                  pl.BlockSpec((1, 1, s, VH), lambda i, j, n: (i, j, 0, 0))],
        out_specs=pl.BlockSpec((1, 2 * tq, VH), lambda i, j, n: (i, n, j)),
        out_shape=jax.ShapeDtypeStruct((b, t, h * VH), BF16),
        scratch_shapes=[pltpu.VMEM((tq, s), F32), pltpu.VMEM((tq, s), F32),
                        pltpu.VMEM((tq, 1), F32)],
        compiler_params=_cparams(("parallel", "parallel", "arbitrary")),
        name="attention",
    )(q, q, q, k, v)


def _router_kernel(x_ref, g_ref, sc_ref, sh_ref, w_ref, h_ref, lg_ref):
    hn = _modnorm(x_ref[0], g_ref[...], sc_ref[0], sh_ref[0])
    h_ref[0] = hn.astype(BF16)
    lg_ref[0] = jnp.dot(hn, w_ref[...], preferred_element_type=F32,
                        precision=lax.Precision.HIGHEST)


def _router(x, gain, scale, shift, w_router, tm):
    b, t, d = x.shape
    e = w_router.shape[1]
    return pl.pallas_call(
        _router_kernel,
        grid=(b, t // tm),
        in_specs=[pl.BlockSpec((1, tm, d), lambda i, j: (i, j, 0)),
                  pl.BlockSpec((1, d), lambda i, j: (0, 0)),
                  pl.BlockSpec((1, 1, d), lambda i, j: (i, 0, 0)),
                  pl.BlockSpec((1, 1, d), lambda i, j: (i, 0, 0)),
                  pl.BlockSpec((d, e), lambda i, j: (0, 0))],
        out_specs=[pl.BlockSpec((1, tm, d), lambda i, j: (i, j, 0)),
                   pl.BlockSpec((1, tm, e), lambda i, j: (i, j, 0))],
        out_shape=[jax.ShapeDtypeStruct((b, t, d), BF16),
                   jax.ShapeDtypeStruct((b, t, e), F32)],
        compiler_params=_cparams(("parallel", "parallel")),
        name="router",
    )(x, gain.reshape(1, d), scale, shift, w_router)


def _ffn_kernel(x_ref, wg_ref, wu_ref, wd_ref, g_ref, o_ref, acc_scr, *, fc):
    x = x_ref[0, 0]
    nf = wg_ref.shape[-1]
    for c in range(nf // fc):
        sl = slice(c * fc, (c + 1) * fc)
        hg = jnp.dot(x, wg_ref[0, :, sl], preferred_element_type=F32)
        hu = jnp.dot(x, wu_ref[0, :, sl], preferred_element_type=F32)
        hid = (_silu(hg) * hu).astype(BF16)
        part = jnp.dot(hid, wd_ref[0, sl, :], preferred_element_type=F32)
        if c == 0:
            acc_scr[...] = part
        else:
            acc_scr[...] += part
    o_ref[0, 0] = (acc_scr[...] * g_ref[0, 0]).astype(o_ref.dtype)


def _expert_ffn(xg, gates, w_gate, w_up, w_down, tm, fc):
    b, e, cap, d = xg.shape
    f = w_gate.shape[-1]
    tm = min(tm, cap)
    kern = functools.partial(_ffn_kernel, fc=fc)
    return pl.pallas_call(
        kern,
        grid=(b, e, cap // tm),
        in_specs=[pl.BlockSpec((1, 1, tm, d), lambda i, j, m: (i, j, m, 0)),
                  pl.BlockSpec((1, d, f), lambda i, j, m: (j, 0, 0)),
                  pl.BlockSpec((1, d, f), lambda i, j, m: (j, 0, 0)),
                  pl.BlockSpec((1, f, d), lambda i, j, m: (j, 0, 0)),
                  pl.BlockSpec((1, 1, tm, 1), lambda i, j, m: (i, j, m, 0))],
        out_specs=pl.BlockSpec((1, 1, tm, d), lambda i, j, m: (i, j, m, 0)),
        out_shape=jax.ShapeDtypeStruct((b, e, cap, d), BF16),
        scratch_shapes=[pltpu.VMEM((tm, d), F32)],
        compiler_params=_cparams(("parallel", "parallel", "arbitrary")),
        name="expert_ffn",
    )(xg, w_gate, w_up, w_down, gates)


SLOT_WIN = 256
TOK_BLK = 128


def _combine_kernel(lo_ref, idx_ref, y_ref, x_ref, gate_ref, o_ref, *, nblk, nrow):
    b, j, e = pl.program_id(0), pl.program_id(1), pl.program_id(2)

    @pl.when(e == 0)
    def _():
        o_ref[...] = jnp.zeros(o_ref.shape, F32)

    tok = lax.broadcasted_iota(jnp.int32, (TOK_BLK, 1), 0)
    for blk in range(nblk):
        gblk = j * nblk + blk
        w = jnp.minimum(lo_ref[b, e, gblk] // TOK_BLK, nrow - 2)
        ids = jnp.concatenate([idx_ref[0, 0, pl.ds(w, 1), :], idx_ref[0, 0, pl.ds(w + 1, 1), :]],
                              axis=1)
        onehot = (ids == tok + gblk * TOK_BLK).astype(BF16)
        yw = y_ref[0, 0, pl.ds(pl.multiple_of(w * TOK_BLK, TOK_BLK), SLOT_WIN), :]
        rows = slice(blk * TOK_BLK, (blk + 1) * TOK_BLK)
        o_ref[0, rows, :] += jnp.dot(onehot, yw, preferred_element_type=F32)

    @pl.when(e == pl.num_programs(2) - 1)
    def _():
        o_ref[0] = x_ref[0] + gate_ref[0] * o_ref[0]


def _combine(x, gate_out, idx_s, y):
    b, t, d = x.shape
    e, cap = idx_s.shape[1:]
    capp = max(cap, SLOT_WIN)
    if capp != cap:
        idx_s = jnp.pad(idx_s, ((0, 0), (0, 0), (0, capp - cap)), constant_values=-1)
        y = jnp.pad(y, ((0, 0), (0, 0), (0, capp - cap), (0, 0)))
    nrow = capp // TOK_BLK
    ts = min(2048, t)
    nblk = ts // TOK_BLK
    bounds = jnp.arange(t // TOK_BLK, dtype=jnp.int32) * TOK_BLK
    valid = idx_s >= 0
    lo = jnp.sum((valid[..., None] & (idx_s[..., None] < bounds)).astype(jnp.int32), axis=2)
    kern = functools.partial(_combine_kernel, nblk=nblk, nrow=nrow)
    grid_spec = pltpu.PrefetchScalarGridSpec(
        num_scalar_prefetch=1,
        grid=(b, t // ts, e),
        in_specs=[pl.BlockSpec((1, 1, nrow, TOK_BLK), lambda i, j, k, lo_r: (i, k, 0, 0)),
                  pl.BlockSpec((1, 1, capp, d), lambda i, j, k, lo_r: (i, k, 0, 0)),
                  pl.BlockSpec((1, ts, d), lambda i, j, k, lo_r: (i, j, 0)),
                  pl.BlockSpec((1, 1, d), lambda i, j, k, lo_r: (i, 0, 0))],
        out_specs=pl.BlockSpec((1, ts, d), lambda i, j, k, lo_r: (i, j, 0)),
    )
    return pl.pallas_call(
        kern,
        grid_spec=grid_spec,
        out_shape=jax.ShapeDtypeStruct((b, t, d), F32),
        compiler_params=_cparams(("parallel", "parallel", "arbitrary")),
        name="moe_combine",
    )(lo, idx_s.reshape(b, e, nrow, TOK_BLK), y, x, gate_out)


def _moe(x, gain, scale, shift, gate_out, w_router, w_gate, w_up, w_down):
    b, t, d = x.shape
    cap = CAP_FACTOR * t // N_EXPERTS
    hn, logits = _router(x, gain, scale, shift, w_router, min(256, t))
    aff = jax.nn.softmax(logits, axis=-1)
    g, idx = lax.top_k(jnp.swapaxes(aff, 1, 2), cap)
    idx_s, g_s = lax.sort((idx, g), dimension=2, num_keys=1)
    bidx = jnp.arange(b)[:, None, None]
    xg = hn[bidx, idx_s]
    y = _expert_ffn(xg, g_s[..., None], w_gate, w_up, w_down, 256, 512)
    return _combine(x, gate_out, idx_s, y)


def _final_kernel(x_ref, g_ref, o_ref):
    o_ref[0] = _rms(x_ref[0], g_ref[...])


def _final_norm(x, gain, tm):
    b, t, d = x.shape
    return pl.pallas_call(
        _final_kernel,
        grid=(b, t // tm),
        in_specs=[pl.BlockSpec((1, tm, d), lambda i, j: (i, j, 0)),
                  pl.BlockSpec((1, d), lambda i, j: (0, 0))],
        out_specs=pl.BlockSpec((1, tm, d), lambda i, j: (i, j, 0)),
        out_shape=jax.ShapeDtypeStruct((b, t, d), F32),
        compiler_params=_cparams(("parallel", "parallel")),
        name="final_norm",
    )(x, gain.reshape(1, d))


def kernel(x, c, ctx, c_ctx, w_mod, b_mod, norm_mix, norm_ffn, norm_out, w_in, w_out_rec,
           conv_w, conv_b, lb_gamma, dt_bias, a_log, d_skip, hgrn_norm, mamba_norm,
           w_dq, q_norm, w_uq, w_dkv, kv_norm, w_ukv, w_kr, w_o,
           w_router, w_gate, w_up, w_down):
    B, T, D = x.shape
    TC = ctx.shape[1]
    lb = jnp.cumsum(jax.nn.softmax(lb_gamma.astype(F32), axis=0), axis=0)

    cvec = jnp.zeros((16, D), F32).at[:B].set(c).at[B].set(c_ctx)
    mods = _modulation(cvec, w_mod, b_mod)

    x_lat, x_ctx = x, ctx
    for l in range(DEPTH):
        need_ctx = l < DEPTH - 1
        m_lat = [mods[l, :B, i * D:(i + 1) * D][:, None, :] for i in range(6)]
        m_ctx = [jnp.broadcast_to(mods[l, B, i * D:(i + 1) * D][None, None, :], (B, 1, D))
                 for i in range(6)]
        if l % 2 == 0:
            e = l // 2
            w_in_p = jnp.pad(w_in[e], ((0, 0), (0, IN_COLS_PAD - IN_COLS))).astype(BF16)
            xs = jnp.concatenate([x_ctx, x_lat], axis=1)
            pair = lambda i: jnp.concatenate([m_ctx[i], m_lat[i]], axis=1)
            p = _modproj(xs, norm_mix[l], pair(1), pair(0), w_in_p, 256, TC // 256)
            ohf, omf, ohb, omb = _scan_mixers(p, TC // CHUNK, lb[e], conv_w[e], conv_b[e],
                                              dt_bias[e], a_log[e], d_skip[e])
            xs = _merge_mixers(xs, pair(2), ohf, ohb, omf, omb, p, hgrn_norm[e], mamba_norm[e],
                               w_out_rec[e].astype(BF16), TC // 256, 256)
            x_ctx, x_lat = xs[:, :TC], xs[:, TC:]
        else:
            j = l // 2
            cos_l, sin_l = _rope_tables(T)
            cos_c = jnp.concatenate([jnp.ones((TC, ROPE), F32), jnp.zeros((TC, ROPE), F32)], -1)
            sin_c = jnp.zeros((TC, 128), F32)
            w_cat = jnp.concatenate([w_dq[j], w_dkv[j], w_kr[j], _swap_cols(w_kr[j])],
                                    axis=1).astype(BF16)
            wq = w_uq[j].reshape(Q_LORA, H_C, NOPE + ROPE)
            wq_h = jnp.concatenate([wq, _swap_cols(wq[..., NOPE:])], axis=-1)
            wq_h = jnp.transpose(wq_h, (1, 0, 2)).astype(BF16)
            wkv_h = jnp.transpose(w_ukv[j].reshape(KV_LORA, H_C, NOPE + VH), (1, 0, 2)).astype(BF16)
            cq_l, ckv_l, kr_l = _mla_down(x_lat, norm_mix[l], m_lat[1], m_lat[0], w_cat,
                                          q_norm[j], kv_norm[j], cos_l, sin_l, 256)
            _, ckv_c, kr_c = _mla_down(x_ctx, norm_mix[l], m_ctx[1], m_ctx[0], w_cat,
                                       q_norm[j], kv_norm[j], cos_c, sin_c, 256)
            ckv = jnp.concatenate([ckv_c, ckv_l], axis=1)
            kr = jnp.concatenate([kr_c, kr_l], axis=1)
            qh = _q_up(cq_l, wq_h, cos_l, sin_l, 512)
            kh, vh = _kv_up(ckv, kr, wkv_h, 768)
            o = _attention(qh, kh, vh, 256, 768)
            x_lat = _linear_residual(x_lat, m_lat[2], o, w_o[j].astype(BF16), 512)
        x_lat = _moe(x_lat, norm_ffn[l], m_lat[4], m_lat[3], m_lat[5], w_router[l],
                     w_gate[l].astype(BF16), w_up[l].astype(BF16), w_down[l].astype(BF16))
        if need_ctx:
            x_ctx = _moe(x_ctx, norm_ffn[l], m_ctx[4], m_ctx[3], m_ctx[5], w_router[l],
                         w_gate[l].astype(BF16), w_up[l].astype(BF16), w_down[l].astype(BF16))
    return _final_norm(x_lat, norm_out, 512)
```

```python
import functools
import math

import jax
import jax.numpy as jnp
from jax import lax
from jax.experimental import pallas as pl
from jax.experimental.pallas import tpu as pltpu

F32 = jnp.float32
BF16 = jnp.bfloat16

D_MODEL = 1024
DEPTH = 2
GRID_W = 64
EPS = 1e-6
CHUNK = 64
HA, DKA, DVA = 8, 64, 64
WA = HA * DVA
HB, PB = 8, 64
D_INNER = HB * PB
N_GROUPS, D_STATE = 2, 128
CONV_W = 5
REC_SPLITS = (HA * DKA, HA * DKA, HA * DKA, WA, WA, D_INNER, D_INNER,
              N_GROUPS * D_STATE, N_GROUPS * D_STATE, HB, HB)
IN_COLS = sum(REC_SPLITS)
IN_COLS_PAD = 4224
H_C, NOPE, ROPE, VH = 8, 128, 64, 128
Q_LORA, KV_LORA = 384, 256
ROPE_THETA = 10000.0
ATTN_SCALE = 1.0 / math.sqrt(NOPE + ROPE)
QK_SCALE = ATTN_SCALE * math.log2(math.e)
QK_DIM = 256
N_EXPERTS = 16
EXPERT_FF = 1024
CAP_FACTOR = 2

VMEM_LIMIT = 56 * 1024 * 1024


def _cparams(sem):
    return pltpu.CompilerParams(dimension_semantics=sem, vmem_limit_bytes=VMEM_LIMIT)


def _silu(v):
    return v * jax.nn.sigmoid(v)


def _modnorm(x, gain, scale, shift):
    ms = jnp.mean(x * x, axis=-1, keepdims=True)
    return (x * lax.rsqrt(ms + EPS) * gain) * (1.0 + scale) + shift


def _mod_kernel(s_ref, w_ref, b_ref, o_ref):
    s = _silu(s_ref[...])
    o_ref[0] = jnp.dot(s, w_ref[0], preferred_element_type=F32,
                       precision=lax.Precision.HIGHEST) + b_ref[0]


def _modulation(cvec, w_mod, b_mod):
    n = w_mod.shape[-1]
    tn = 1536
    return pl.pallas_call(
        _mod_kernel,
        grid=(DEPTH, n // tn),
        in_specs=[pl.BlockSpec((16, D_MODEL), lambda l, j: (0, 0)),
                  pl.BlockSpec((1, D_MODEL, tn), lambda l, j: (l, 0, j)),
                  pl.BlockSpec((1, 1, tn), lambda l, j: (l, 0, j))],
        out_specs=pl.BlockSpec((1, 16, tn), lambda l, j: (l, 0, j)),
        out_shape=jax.ShapeDtypeStruct((DEPTH, 16, n), F32),
        compiler_params=_cparams(("parallel", "parallel")),
        name="modulation",
    )(cvec, w_mod, b_mod.reshape(DEPTH, 1, n))


def _modproj_kernel(x_ref, g_ref, sc_ref, sh_ref, w_ref, o_ref):
    a = _modnorm(x_ref[0], g_ref[...], sc_ref[0, 0], sh_ref[0, 0]).astype(BF16)
    o_ref[0] = jnp.dot(a, w_ref[...], preferred_element_type=F32).astype(o_ref.dtype)


def _modproj(x, gain, scale2, shift2, w, tm, nct_tiles, out_dtype=F32):
    b, t, d = x.shape
    n = w.shape[1]
    sel = lambda i, j: (i, jnp.where(j < nct_tiles, 0, 1), 0, 0)
    return pl.pallas_call(
        _modproj_kernel,
        grid=(b, t // tm),
        in_specs=[pl.BlockSpec((1, tm, d), lambda i, j: (i, j, 0)),
                  pl.BlockSpec((1, d), lambda i, j: (0, 0)),
                  pl.BlockSpec((1, 1, 1, d), sel),
                  pl.BlockSpec((1, 1, 1, d), sel),
                  pl.BlockSpec((d, n), lambda i, j: (0, 0))],
        out_specs=pl.BlockSpec((1, tm, n), lambda i, j: (i, j, 0)),
        out_shape=jax.ShapeDtypeStruct((b, t, n), out_dtype),
        compiler_params=_cparams(("parallel", "parallel")),
        name="modproj",
    )(x, gain.reshape(1, d), scale2.reshape(b, 2, 1, d), shift2.reshape(b, 2, 1, d), w)


def _linres_kernel(x_ref, gate_ref, y_ref, w_ref, o_ref):
    acc = jnp.dot(y_ref[0].astype(BF16), w_ref[...], preferred_element_type=F32)
    o_ref[0] = x_ref[0] + gate_ref[0] * acc


def _linear_residual(x, gate, y, w, tm):
    b, t, d = x.shape
    k = y.shape[-1]
    return pl.pallas_call(
        _linres_kernel,
        grid=(b, t // tm),
        in_specs=[pl.BlockSpec((1, tm, d), lambda i, j: (i, j, 0)),
                  pl.BlockSpec((1, 1, d), lambda i, j: (i, 0, 0)),
                  pl.BlockSpec((1, tm, k), lambda i, j: (i, j, 0)),
                  pl.BlockSpec((k, d), lambda i, j: (0, 0))],
        out_specs=pl.BlockSpec((1, tm, d), lambda i, j: (i, j, 0)),
        out_shape=jax.ShapeDtypeStruct((b, t, d), F32),
        compiler_params=_cparams(("parallel", "parallel")),
        name="linear_residual",
    )(x, gate, y, w)


L = CHUNK
COL_Q, COL_FF, COL_FB, COL_I, COL_G, COL_Z, COL_X, COL_DT = 0, 512, 1024, 1536, 2048, 2560, 3072, 4096


def _rms(x, gain):
    return x * lax.rsqrt(jnp.mean(x * x, axis=-1, keepdims=True) + EPS) * gain


def _split3(x):
    a = x.astype(BF16)
    r = x - a.astype(F32)
    b = r.astype(BF16)
    c = (r - b.astype(F32)).astype(BF16)
    return a, b, c


def _cumsum_exact(tri3, x):
    a, b, c = _split3(x)
    return jnp.dot(tri3, jnp.concatenate([a, b, c], axis=0), preferred_element_type=F32)


def _expand_exact(x, e3):
    a, b, c = _split3(x)
    return jnp.dot(jnp.concatenate([a, b, c], axis=1), e3, preferred_element_type=F32)


def _dot_nt(a, b):
    return lax.dot_general(a, b, (((1,), (1,)), ((), ())), preferred_element_type=F32)


def _dot_tn(a, b):
    return lax.dot_general(a, b, (((0,), (0,)), ((), ())), preferred_element_type=F32)


def _scan_dir(rev, q_ref, f_ref, v_ref, x_ref, xp_ref, xn_ref, dt_ref, pv, nv,
              lb_row, cw_ref, cbias_ref, dtb_ref, aexp_ref, dskip_ref,
              oh_ref, om_ref, sg_ref, sm_ref, u_scr):
    d = 1 if rev else 0
    ti = lax.broadcasted_iota(jnp.int32, (L, L), 0)
    si = lax.broadcasted_iota(jnp.int32, (L, L), 1)
    keep = (si >= ti) if rev else (si <= ti)
    tri = keep.astype(BF16)
    tri3 = jnp.concatenate([tri, tri, tri], axis=1)
    t2 = lax.broadcasted_iota(jnp.int32, (L, 128), 0)
    s2 = lax.broadcasted_iota(jnp.int32, (L, 128), 1) % L
    keep2 = (s2 >= t2) if rev else (s2 <= t2)
    ref_row = (L - 1 - L // 2) if rev else L // 2
    last_row = 0 if rev else L - 1
    lane = lax.broadcasted_iota(jnp.int32, (1, 128), 1)
    lo = lane < 64
    r128 = lax.broadcasted_iota(jnp.int32, (128, 128), 0)
    c128 = lax.broadcasted_iota(jnp.int32, (128, 128), 1)
    blockdiag = (r128 < 64) == (c128 < 64)
    zero_b = jnp.zeros((), BF16)

    def pair_rows(a):
        return jnp.concatenate([jnp.where(lo, a, zero_b), jnp.where(lo, zero_b, a)], axis=0)

    q = q_ref[0]
    qs = _silu(q)
    f = lb_row + (1.0 - lb_row) * jax.nn.sigmoid(f_ref[0])
    k = 1.0 - f
    la = jnp.log(f)
    b = _cumsum_exact(tri3, la)
    bref = b[ref_row:ref_row + 1, :]
    blast = b[last_row:last_row + 1, :]
    qd = (qs * jnp.exp(b - bref)).astype(BF16)
    kd = (k * jnp.exp(bref - b)).astype(BF16)
    kl = (k * jnp.exp(blast - b)).astype(BF16)
    qb = (qs * jnp.exp(b)).astype(BF16)
    dec_col = jnp.broadcast_to(jnp.exp(blast), (8, 512)).T
    vb = v_ref[0].astype(BF16)
    for p in range(HA // 2):
        sl = slice(128 * p, 128 * (p + 1))
        qd_p, kd_p, kl_p, qb_p, v_p = qd[:, sl], kd[:, sl], kl[:, sl], qb[:, sl], vb[:, sl]
        st = sg_ref[d, p]
        att = _dot_nt(qd_p, pair_rows(kd_p))
        att = jnp.where(keep2, att, 0.0).astype(BF16)
        lhs = jnp.concatenate([att, qb_p], axis=1)
        rhs = jnp.concatenate([pair_rows(v_p), st.astype(BF16)], axis=0)
        oh_ref[0, :, sl] = jnp.dot(lhs, rhs, preferred_element_type=F32)
        upd = _dot_tn(kl_p, v_p)
        sg_ref[d, p] = st * dec_col[sl, 0:1] + jnp.where(blockdiag, upd, 0.0)

    u_scr[0:8, :] = xp_ref[0] * pv
    u_scr[8:8 + L, :] = x_ref[0]
    u_scr[8 + L:16 + L, :] = xn_ref[0] * nv
    y = cbias_ref[...] + cw_ref[0:1, :] * u_scr[6:6 + L, :]
    for j in range(1, 5):
        y = y + cw_ref[j:j + 1, :] * u_scr[6 + j:6 + j + L, :]
    xbc = _silu(y)
    xm = xbc[:, :512]
    bm = xbc[:, 512:768].astype(BF16)
    cm = xbc[:, 768:1024].astype(BF16)
    draw = dt_ref[0] + dtb_ref[...]
    dt = jnp.maximum(draw, 0.0) + jnp.log(1.0 + jnp.exp(-jnp.abs(draw)))
    la_m = -dt * aexp_ref[...]
    cum = _cumsum_exact(tri3, la_m)
    er = lax.broadcasted_iota(jnp.int32, (128, 512), 0)
    ec = lax.broadcasted_iota(jnp.int32, (128, 512), 1)
    e1 = (er == (ec // 64) + 8 * d).astype(BF16)
    e1 = jnp.concatenate([e1, e1, e1], axis=0)
    both = _expand_exact(jnp.concatenate([dt, cum], axis=0), e1)
    dt_e, cum_e = both[:L], both[L:]
    cum_t = cum.T
    xdt = xm * dt_e
    clast = cum_e[last_row:last_row + 1, :]
    wx = (jnp.exp(clast - cum_e) * xdt).astype(BF16)
    dec_m = jnp.exp(clast)
    ecum = jnp.exp(cum_e)
    xdt_b = xdt.astype(BF16)
    for g in range(2):
        bm_g = bm[:, 128 * g:128 * (g + 1)]
        cm_g = cm[:, 128 * g:128 * (g + 1)]
        gl = slice(256 * g, 256 * (g + 1))
        cb2 = _dot_nt(cm_g, jnp.concatenate([bm_g, bm_g], axis=0))
        sm = sm_ref[d, g]
        y_int = jnp.dot(cm_g, sm.astype(BF16), preferred_element_type=F32) * ecum[:, gl]
        for pp in range(2):
            h0 = 4 * g + 2 * pp + 8 * d
            sl = slice(256 * g + 128 * pp, 256 * g + 128 * (pp + 1))
            crow = jnp.concatenate([cum_t[h0:h0 + 1, :], cum_t[h0 + 1:h0 + 2, :]], axis=1)
            lm = jnp.exp(jnp.where(keep2, cum_e[:, sl] - crow, -jnp.inf))
            yp = jnp.dot((cb2 * lm).astype(BF16), pair_rows(xdt_b[:, sl]), preferred_element_type=F32)
            yp = yp + y_int[:, 128 * pp:128 * (pp + 1)]
            if not rev:
                yp = yp + dskip_ref[:, sl] * xm[:, sl]
            om_ref[0, :, sl] = yp
        sm_ref[d, g] = sm * dec_m[:, gl] + _dot_tn(bm_g, wx[:, gl])


def _scan_kernel(qf, ff, vf, xf, xpf, xnf, dtf, qb, fb, vb_, xb, xpb, xnb, dtb,
                 lb_ref, cw_ref, cbias_ref, dtbias_ref, aexp_ref, dskip_ref,
                 ohf, omf, ohb, omb, sg_ref, sm_ref, u_scr, *, nct, nc):
    c = pl.program_id(1)

    @pl.when(c == 0)
    def _():
        sg_ref[...] = jnp.zeros(sg_ref.shape, F32)
        sm_ref[...] = jnp.zeros(sm_ref.shape, F32)

    cb = jnp.where(c < nct, nct - 1 - c, nc + nct - 1 - c)

    def edge_flags(ch):
        pv = jnp.where((ch == 0) | (ch == nct), 0.0, 1.0)
        nv = jnp.where((ch == nct - 1) | (ch == nc - 1), 0.0, 1.0)
        return pv, nv

    pvf, nvf = edge_flags(c)
    pvb, nvb = edge_flags(cb)
    _scan_dir(False, qf, ff, vf, xf, xpf, xnf, dtf, pvf, nvf, lb_ref[0:1, :], cw_ref, cbias_ref,
              dtbias_ref, aexp_ref, dskip_ref, ohf, omf, sg_ref, sm_ref, u_scr)
    _scan_dir(True, qb, fb, vb_, xb, xpb, xnb, dtb, pvb, nvb, lb_ref[1:2, :], cw_ref, cbias_ref,
              dtbias_ref, aexp_ref, dskip_ref, ohb, omb, sg_ref, sm_ref, u_scr)


def _scan_mixers(p, nct, lb, conv_w, conv_b, dt_bias, a_log, d_skip):
    bsz, s, _ = p.shape
    nc = s // L
    nb8 = s // 8

    def bmap(c):
        return jnp.where(c < nct, nct - 1 - c, nc + nct - 1 - c)

    def specs(cmap, fcol):
        colblk = lambda col, w: col // w
        return [
            pl.BlockSpec((1, L, 512), lambda i, c: (i, cmap(c), colblk(COL_Q, 512))),
            pl.BlockSpec((1, L, 512), lambda i, c: (i, cmap(c), colblk(fcol, 512))),
            pl.BlockSpec((1, L, 512), lambda i, c: (i, cmap(c), colblk(COL_I, 512))),
            pl.BlockSpec((1, L, 1024), lambda i, c: (i, cmap(c), colblk(COL_X, 1024))),
            pl.BlockSpec((1, 8, 1024), lambda i, c: (i, jnp.maximum(cmap(c) * 8 - 1, 0), colblk(COL_X, 1024))),
            pl.BlockSpec((1, 8, 1024), lambda i, c: (i, jnp.minimum(cmap(c) * 8 + 8, nb8 - 1), colblk(COL_X, 1024))),
            pl.BlockSpec((1, L, 128), lambda i, c: (i, cmap(c), colblk(COL_DT, 128))),
        ]

    ident = lambda c: c
    const2 = lambda i, c: (0, 0)
    dtb = jnp.zeros((1, 128), F32).at[0, :16].set(dt_bias.reshape(16))
    aexp = jnp.zeros((1, 128), F32).at[0, :16].set(jnp.exp(a_log.reshape(16)))
    dsk = jnp.repeat(d_skip, 64).reshape(1, 512)
    kern = functools.partial(_scan_kernel, nct=nct, nc=nc)
    out_sd = jax.ShapeDtypeStruct((bsz, s, 512), F32)
    return pl.pallas_call(
        kern,
        grid=(bsz, nc),
        in_specs=specs(ident, COL_FF) + specs(bmap, COL_FB) + [
            pl.BlockSpec((2, 512), const2), pl.BlockSpec((5, 1024), const2),
            pl.BlockSpec((1, 1024), const2), pl.BlockSpec((1, 128), const2),
            pl.BlockSpec((1, 128), const2), pl.BlockSpec((1, 512), const2)],
        out_specs=[pl.BlockSpec((1, L, 512), lambda i, c: (i, c, 0)),
                   pl.BlockSpec((1, L, 512), lambda i, c: (i, c, 0)),
                   pl.BlockSpec((1, L, 512), lambda i, c: (i, bmap(c), 0)),
                   pl.BlockSpec((1, L, 512), lambda i, c: (i, bmap(c), 0))],
        out_shape=[out_sd, out_sd, out_sd, out_sd],
        scratch_shapes=[pltpu.VMEM((2, HA // 2, 128, 128), F32),
                        pltpu.VMEM((2, 2, 128, 256), F32),
                        pltpu.VMEM((L + 16, 1024), F32)],
        compiler_params=_cparams(("parallel", "arbitrary")),
        name="scan_mixers",
    )(*([p] * 14), lb, conv_w, conv_b.reshape(1, 1024), dtb, aexp, dsk)


def _merge_kernel(ohf, ohb, omf, omb, g_ref, z_ref, x_ref, gate_ref, hn_ref, mn_ref, w_ref,
                  oc_ref, ol_ref, *, nct_tiles):
    oh = ohf[0] + ohb[0]
    sq = oh * oh
    hi = sq.astype(BF16)
    lo = (sq - hi.astype(F32)).astype(BF16)
    r = lax.broadcasted_iota(jnp.int32, (512, 512), 0) // 64
    c = lax.broadcasted_iota(jnp.int32, (512, 512), 1) // 64
    avg = jnp.where(r == c, 1.0 / 64, 0.0).astype(BF16)
    ms = jnp.dot(hi, avg, preferred_element_type=F32) + jnp.dot(lo, avg, preferred_element_type=F32)
    oh = oh * lax.rsqrt(ms + EPS) * hn_ref[...] * jax.nn.sigmoid(g_ref[0])
    y = (omf[0] + omb[0]) * _silu(z_ref[0])
    y = y * lax.rsqrt(jnp.mean(y * y, axis=-1, keepdims=True) + EPS) * mn_ref[...]
    cat = jnp.concatenate([oh, y], axis=1).astype(BF16)
    out = x_ref[0] + gate_ref[0, 0] * jnp.dot(cat, w_ref[...], preferred_element_type=F32)
    j = pl.program_id(1)

    @pl.when(j < nct_tiles)
    def _():
        oc_ref[0] = out

    @pl.when(j >= nct_tiles)
    def _():
        ol_ref[0] = out


def _merge_mixers(x, gate2, ohf, ohb, omf, omb, p, hgrn_norm, mamba_norm, w_out, nct_tiles, tm):
    b, s, d = x.shape
    sel = lambda i, j: (i, jnp.where(j < nct_tiles, 0, 1), 0, 0)
    row = lambda i, j: (i, j, 0)
    const2 = lambda i, j: (0, 0)
    tc = nct_tiles * tm
    return pl.pallas_call(
        functools.partial(_merge_kernel, nct_tiles=nct_tiles),
        grid=(b, s // tm),
        in_specs=[pl.BlockSpec((1, tm, 512), row)] * 4 + [
            pl.BlockSpec((1, tm, 512), lambda i, j: (i, j, COL_G // 512)),
            pl.BlockSpec((1, tm, 512), lambda i, j: (i, j, COL_Z // 512)),
            pl.BlockSpec((1, tm, d), row),
            pl.BlockSpec((1, 1, 1, d), sel),
            pl.BlockSpec((1, 512), const2), pl.BlockSpec((1, 512), const2),
            pl.BlockSpec((2 * 512, d), const2)],
        out_specs=[pl.BlockSpec((1, tm, d), lambda i, j: (i, jnp.minimum(j, nct_tiles - 1), 0)),
                   pl.BlockSpec((1, tm, d), lambda i, j: (i, jnp.maximum(j - nct_tiles, 0), 0))],
        out_shape=[jax.ShapeDtypeStruct((b, tc, d), F32),
                   jax.ShapeDtypeStruct((b, s - tc, d), F32)],
        compiler_params=_cparams(("parallel", "arbitrary")),
        name="merge_mixers",
    )(ohf, ohb, omf, omb, p, p, x, gate2.reshape(b, 2, 1, d), hgrn_norm.reshape(1, 512),
      mamba_norm.reshape(1, 512), w_out)


def _rope_tables(t):
    rows = t // GRID_W
    row = jnp.repeat(jnp.arange(rows, dtype=jnp.int32), GRID_W)
    col = jnp.tile(jnp.arange(GRID_W, dtype=jnp.int32), rows)
    nf = ROPE // 4
    inv_freq = ROPE_THETA ** (-jnp.arange(nf, dtype=F32) / nf)
    pos = jnp.stack([row, col], axis=-1).astype(F32)
    ang = pos[..., None] * inv_freq
    cos, sin = jnp.cos(ang), jnp.sin(ang)
    cos64 = jnp.broadcast_to(cos[:, :, None, :], (t, 2, 2, nf)).reshape(t, ROPE)
    sin64 = jnp.broadcast_to(sin[:, :, None, :], (t, 2, 2, nf)).reshape(t, ROPE)
    zero = jnp.zeros((t, ROPE), F32)
    return jnp.concatenate([cos64, zero], -1), jnp.concatenate([sin64, zero], -1)


def _swap_cols(w):
    nf = ROPE // 4
    wr = w.reshape(w.shape[:-1] + (2, 2, nf))
    return jnp.stack([-wr[..., 1, :], wr[..., 0, :]], axis=-2).reshape(w.shape)


def _rope_tile(tile, cos_t, sin_t):
    return tile * cos_t + pltpu.roll(tile, ROPE, axis=1) * sin_t


def _mla_down_kernel(x_ref, g_ref, sc_ref, sh_ref, w_ref, qn_ref, kvn_ref, cos_ref, sin_ref,
                     cq_ref, ckv_ref, kr_ref):
    a = _modnorm(x_ref[0], g_ref[...], sc_ref[0], sh_ref[0]).astype(BF16)
    c = jnp.dot(a, w_ref[...], preferred_element_type=F32)
    cq = c[:, :Q_LORA]
    ckv = c[:, Q_LORA:Q_LORA + KV_LORA]
    cq_ref[0] = _rms(cq, qn_ref[...]).astype(BF16)
    ckv_ref[0] = _rms(ckv, kvn_ref[...]).astype(BF16)
    kr_ref[0] = _rope_tile(c[:, Q_LORA + KV_LORA:], cos_ref[...], sin_ref[...]).astype(BF16)


def _mla_down(x, gain, scale, shift, w_cat, q_norm, kv_norm, cos_t, sin_t, tm):
    b, t, d = x.shape
    n = w_cat.shape[1]
    return pl.pallas_call(
        _mla_down_kernel,
        grid=(b, t // tm),
        in_specs=[pl.BlockSpec((1, tm, d), lambda i, j: (i, j, 0)),
                  pl.BlockSpec((1, d), lambda i, j: (0, 0)),
                  pl.BlockSpec((1, 1, d), lambda i, j: (i, 0, 0)),
                  pl.BlockSpec((1, 1, d), lambda i, j: (i, 0, 0)),
                  pl.BlockSpec((d, n), lambda i, j: (0, 0)),
                  pl.BlockSpec((1, Q_LORA), lambda i, j: (0, 0)),
                  pl.BlockSpec((1, KV_LORA), lambda i, j: (0, 0)),
                  pl.BlockSpec((tm, 128), lambda i, j: (j, 0)),
                  pl.BlockSpec((tm, 128), lambda i, j: (j, 0))],
        out_specs=[pl.BlockSpec((1, tm, Q_LORA), lambda i, j: (i, j, 0)),
                   pl.BlockSpec((1, tm, KV_LORA), lambda i, j: (i, j, 0)),
                   pl.BlockSpec((1, tm, 128), lambda i, j: (i, j, 0))],
        out_shape=[jax.ShapeDtypeStruct((b, t, Q_LORA), BF16),
                   jax.ShapeDtypeStruct((b, t, KV_LORA), BF16),
                   jax.ShapeDtypeStruct((b, t, 128), BF16)],
        compiler_params=_cparams(("parallel", "parallel")),
        name="mla_down",
    )(x, gain.reshape(1, d), scale, shift, w_cat, q_norm.reshape(1, -1), kv_norm.reshape(1, -1),
      cos_t, sin_t)


def _q_up_kernel(cq_ref, w_ref, cos_ref, sin_ref, q_ref):
    cq = cq_ref[0]
    for h in range(H_C):
        q = jnp.dot(cq, w_ref[h], preferred_element_type=F32)
        qn = q[:, :NOPE] * QK_SCALE
        qr = _rope_tile(q[:, NOPE:], cos_ref[...], sin_ref[...]) * QK_SCALE
        q_ref[0, h] = jnp.concatenate([qn, qr], axis=1).astype(BF16)


def _q_up(cq, w_uq_h, cos_t, sin_t, tm):
    b, t, r = cq.shape
    return pl.pallas_call(
        _q_up_kernel,
        grid=(b, t // tm),
        in_specs=[pl.BlockSpec((1, tm, r), lambda i, j: (i, j, 0)),
                  pl.BlockSpec((H_C, r, QK_DIM), lambda i, j: (0, 0, 0)),
                  pl.BlockSpec((tm, 128), lambda i, j: (j, 0)),
                  pl.BlockSpec((tm, 128), lambda i, j: (j, 0))],
        out_specs=pl.BlockSpec((1, H_C, tm, QK_DIM), lambda i, j: (i, 0, j, 0)),
        out_shape=jax.ShapeDtypeStruct((b, H_C, t, QK_DIM), BF16),
        compiler_params=_cparams(("parallel", "parallel")),
        name="q_up",
    )(cq, w_uq_h, cos_t, sin_t)


def _kv_up_kernel(ckv_ref, kr_ref, w_ref, k_ref, v_ref):
    ckv = ckv_ref[0]
    kr = kr_ref[0]
    for h in range(H_C):
        kv = jnp.dot(ckv, w_ref[h], preferred_element_type=F32)
        k_ref[0, h] = jnp.concatenate([kv[:, :NOPE].astype(BF16), kr], axis=1)
        v_ref[0, h] = kv[:, NOPE:].astype(BF16)


def _kv_up(ckv, kr, w_ukv_h, tm):
    b, s, r = ckv.shape
    return pl.pallas_call(
        _kv_up_kernel,
        grid=(b, s // tm),
        in_specs=[pl.BlockSpec((1, tm, r), lambda i, j: (i, j, 0)),
                  pl.BlockSpec((1, tm, 128), lambda i, j: (i, j, 0)),
                  pl.BlockSpec((H_C, r, NOPE + VH), lambda i, j: (0, 0, 0))],
        out_specs=[pl.BlockSpec((1, H_C, tm, QK_DIM), lambda i, j: (i, 0, j, 0)),
                   pl.BlockSpec((1, H_C, tm, VH), lambda i, j: (i, 0, j, 0))],
        out_shape=[jax.ShapeDtypeStruct((b, H_C, s, QK_DIM), BF16),
                   jax.ShapeDtypeStruct((b, H_C, s, VH), BF16)],
        compiler_params=_cparams(("parallel", "parallel")),
        name="kv_up",
    )(ckv, kr, w_ukv_h)


def _attn_kernel(qa_ref, qb_ref, q0_ref, k_ref, v_ref, o_ref, s0_scr, s1_scr, m_scr, *, tk, nk, tq):
    n = pl.program_id(2)

    def scores(q, j):
        ks = k_ref[0, 0, j * tk:(j + 1) * tk, :]
        return lax.dot_general(q, ks, (((1,), (1,)), ((), ())), preferred_element_type=F32)

    def fold_max(mrun, s):
        for c in range(tk // 128):
            mrun = jnp.maximum(mrun, s[:, c * 128:(c + 1) * 128])
        return mrun

    @pl.when(n == 0)
    def _():
        q0 = q0_ref[0, 0]
        mrun = jnp.full((tq, 128), -jnp.inf, F32)
        for j in range(nk):
            s = scores(q0, j)
            s0_scr[:, j * tk:(j + 1) * tk] = s
            mrun = fold_max(mrun, s)
        m_scr[...] = jnp.max(mrun, axis=-1, keepdims=True)

    def fused(qn, s_read, s_write, m):
        mrun = jnp.full((tq, 128), -jnp.inf, F32)
        lrun = jnp.zeros((tq, 128), F32)
        acc = jnp.zeros((tq, VH), F32)
        for j in range(nk):
            sn = scores(qn, j)
            s_write[:, j * tk:(j + 1) * tk] = sn
            mrun = fold_max(mrun, sn)
            p = jnp.exp2(s_read[:, j * tk:(j + 1) * tk] - m)
            for c in range(tk // 128):
                lrun = lrun + p[:, c * 128:(c + 1) * 128]
            acc = acc + jnp.dot(p.astype(BF16), v_ref[0, 0, j * tk:(j + 1) * tk, :],
                                preferred_element_type=F32)
        out = acc / jnp.sum(lrun, axis=-1, keepdims=True)
        return out, jnp.max(mrun, axis=-1, keepdims=True)

    out_a, m1 = fused(qa_ref[0, 0], s0_scr, s1_scr, m_scr[...])
    o_ref[0, 0:tq, :] = out_a.astype(o_ref.dtype)
    out_b, m2 = fused(qb_ref[0, 0], s1_scr, s0_scr, m1)
    o_ref[0, tq:2 * tq, :] = out_b.astype(o_ref.dtype)
    m_scr[...] = m2


def _attention(q, k, v, tq, tk):
    b, h, t, _ = q.shape
    s = k.shape[2]
    nq = t // tq
    assert t % (2 * tq) == 0 and s % tk == 0
    kern = functools.partial(_attn_kernel, tk=tk, nk=s // tk, tq=tq)
    return pl.pallas_call(
        kern,
        grid=(b, h, nq // 2),
        in_specs=[pl.BlockSpec((1, 1, tq, QK_DIM), lambda i, j, n: (i, j, 2 * n + 1, 0)),
                  pl.BlockSpec((1, 1, tq, QK_DIM),
                               lambda i, j, n: (i, j, jnp.minimum(2 * n + 2, nq - 1), 0)),
                  pl.BlockSpec((1, 1, tq, QK_DIM), lambda i, j, n: (i, j, 0, 0)),
                  pl.BlockSpec((1, 1, s, QK_DIM), lambda i, j, n: (i, j, 0, 0)),
                  pl.BlockSpec((1, 1, s, VH), lambda i, j, n: (i, j, 0, 0))],
        out_specs=pl.BlockSpec((1, 2 * tq, VH), lambda i, j, n: (i, n, j)),
        out_shape=jax.ShapeDtypeStruct((b, t, h * VH), BF16),
        scratch_shapes=[pltpu.VMEM((tq, s), F32), pltpu.VMEM((tq, s), F32),
                        pltpu.VMEM((tq, 1), F32)],
        compiler_params=_cparams(("parallel", "parallel", "arbitrary")),
        name="attention",
    )(q, q, q, k, v)


def _router_kernel(x_ref, g_ref, sc_ref, sh_ref, w_ref, h_ref, lg_ref):
    hn = _modnorm(x_ref[0], g_ref[...], sc_ref[0], sh_ref[0])
    h_ref[0] = hn.astype(BF16)
    lg_ref[0] = jnp.dot(hn, w_ref[...], preferred_element_type=F32,
                        precision=lax.Precision.HIGHEST)


def _router(x, gain, scale, shift, w_router, tm):
    b, t, d = x.shape
    e = w_router.shape[1]
    return pl.pallas_call(
        _router_kernel,
        grid=(b, t // tm),
        in_specs=[pl.BlockSpec((1, tm, d), lambda i, j: (i, j, 0)),
                  pl.BlockSpec((1, d), lambda i, j: (0, 0)),
                  pl.BlockSpec((1, 1, d), lambda i, j: (i, 0, 0)),
                  pl.BlockSpec((1, 1, d), lambda i, j: (i, 0, 0)),
                  pl.BlockSpec((d, e), lambda i, j: (0, 0))],
        out_specs=[pl.BlockSpec((1, tm, d), lambda i, j: (i, j, 0)),
                   pl.BlockSpec((1, tm, e), lambda i, j: (i, j, 0))],
        out_shape=[jax.ShapeDtypeStruct((b, t, d), BF16),
                   jax.ShapeDtypeStruct((b, t, e), F32)],
        compiler_params=_cparams(("parallel", "parallel")),
        name="router",
    )(x, gain.reshape(1, d), scale, shift, w_router)


def _ffn_kernel(x_ref, wg_ref, wu_ref, wd_ref, g_ref, o_ref, acc_scr, wgb, wub, wdb, *, fc):
    @pl.when(pl.program_id(2) == 0)
    def _():
        wgb[...] = wg_ref[0].astype(BF16)
        wub[...] = wu_ref[0].astype(BF16)
        wdb[...] = wd_ref[0].astype(BF16)

    x = x_ref[0, 0]
    nf = wg_ref.shape[-1]
    for c in range(nf // fc):
        sl = slice(c * fc, (c + 1) * fc)
        hg = jnp.dot(x, wgb[:, sl], preferred_element_type=F32)
        hu = jnp.dot(x, wub[:, sl], preferred_element_type=F32)
        hid = (_silu(hg) * hu).astype(BF16)
        part = jnp.dot(hid, wdb[sl, :], preferred_element_type=F32)
        if c == 0:
            acc_scr[...] = part
        else:
            acc_scr[...] += part
    o_ref[0, 0] = (acc_scr[...] * g_ref[0, 0]).astype(o_ref.dtype)


def _expert_ffn(xg, gates, w_gate, w_up, w_down, tm, fc):
    b, e, cap, d = xg.shape
    f = w_gate.shape[-1]
    tm = min(tm, cap)
    kern = functools.partial(_ffn_kernel, fc=fc)
    return pl.pallas_call(
        kern,
        grid=(b, e, cap // tm),
        in_specs=[pl.BlockSpec((1, 1, tm, d), lambda i, j, m: (i, j, m, 0)),
                  pl.BlockSpec((1, d, f), lambda i, j, m: (j, 0, 0)),
                  pl.BlockSpec((1, d, f), lambda i, j, m: (j, 0, 0)),
                  pl.BlockSpec((1, f, d), lambda i, j, m: (j, 0, 0)),
                  pl.BlockSpec((1, 1, tm, 1), lambda i, j, m: (i, j, m, 0))],
        out_specs=pl.BlockSpec((1, 1, tm, d), lambda i, j, m: (i, j, m, 0)),
        out_shape=jax.ShapeDtypeStruct((b, e, cap, d), BF16),
        scratch_shapes=[pltpu.VMEM((tm, d), F32), pltpu.VMEM((d, f), BF16),
                        pltpu.VMEM((d, f), BF16), pltpu.VMEM((f, d), BF16)],
        compiler_params=_cparams(("parallel", "parallel", "arbitrary")),
        name="expert_ffn",
    )(xg, w_gate, w_up, w_down, gates)


SLOT_WIN = 256
TOK_BLK = 128


def _combine_kernel(lo_ref, idx_ref, y_ref, x_ref, gate_ref, fg_ref, o_ref, *, nblk, nrow,
                    final_norm):
    b, j, e = pl.program_id(0), pl.program_id(1), pl.program_id(2)

    @pl.when(e == 0)
    def _():
        o_ref[...] = jnp.zeros(o_ref.shape, F32)

    tok = lax.broadcasted_iota(jnp.int32, (TOK_BLK, 1), 0)
    for blk in range(nblk):
        gblk = j * nblk + blk
        w = jnp.minimum(lo_ref[b, e, gblk] // TOK_BLK, nrow - 2)
        ids = jnp.concatenate([idx_ref[0, 0, pl.ds(w, 1), :], idx_ref[0, 0, pl.ds(w + 1, 1), :]],
                              axis=1)
        onehot = (ids == tok + gblk * TOK_BLK).astype(BF16)
        yw = y_ref[0, 0, pl.ds(pl.multiple_of(w * TOK_BLK, TOK_BLK), SLOT_WIN), :]
        rows = slice(blk * TOK_BLK, (blk + 1) * TOK_BLK)
        o_ref[0, rows, :] += jnp.dot(onehot, yw, preferred_element_type=F32)

    @pl.when(e == pl.num_programs(2) - 1)
    def _():
        out = x_ref[0] + gate_ref[0] * o_ref[0]
        o_ref[0] = _rms(out, fg_ref[...]) if final_norm else out


def _combine(x, gate_out, idx_s, y, final_gain=None):
    b, t, d = x.shape
    e, cap = idx_s.shape[1:]
    capp = max(cap, SLOT_WIN)
    if capp != cap:
        idx_s = jnp.pad(idx_s, ((0, 0), (0, 0), (0, capp - cap)), constant_values=-1)
        y = jnp.pad(y, ((0, 0), (0, 0), (0, capp - cap), (0, 0)))
    nrow = capp // TOK_BLK
    ts = min(2048, t)
    nblk = ts // TOK_BLK
    bounds = jnp.arange(t // TOK_BLK, dtype=jnp.int32) * TOK_BLK
    valid = idx_s >= 0
    lo = jnp.sum((valid[..., None] & (idx_s[..., None] < bounds)).astype(jnp.int32), axis=2)
    kern = functools.partial(_combine_kernel, nblk=nblk, nrow=nrow,
                             final_norm=final_gain is not None)
    fg = jnp.ones((1, d), F32) if final_gain is None else final_gain.reshape(1, d)
    grid_spec = pltpu.PrefetchScalarGridSpec(
        num_scalar_prefetch=1,
        grid=(b, t // ts, e),
        in_specs=[pl.BlockSpec((1, 1, nrow, TOK_BLK), lambda i, j, k, lo_r: (i, k, 0, 0)),
                  pl.BlockSpec((1, 1, capp, d), lambda i, j, k, lo_r: (i, k, 0, 0)),
                  pl.BlockSpec((1, ts, d), lambda i, j, k, lo_r: (i, j, 0)),
                  pl.BlockSpec((1, 1, d), lambda i, j, k, lo_r: (i, 0, 0)),
                  pl.BlockSpec((1, d), lambda i, j, k, lo_r: (0, 0))],
        out_specs=pl.BlockSpec((1, ts, d), lambda i, j, k, lo_r: (i, j, 0)),
    )
    return pl.pallas_call(
        kern,
        grid_spec=grid_spec,
        out_shape=jax.ShapeDtypeStruct((b, t, d), F32),
        compiler_params=_cparams(("parallel", "parallel", "arbitrary")),
        name="moe_combine",
    )(lo, idx_s.reshape(b, e, nrow, TOK_BLK), y, x, gate_out, fg)


def _moe(x, gain, scale, shift, gate_out, w_router, w_gate, w_up, w_down, final_gain=None):
    b, t, d = x.shape
    cap = CAP_FACTOR * t // N_EXPERTS
    hn, logits = _router(x, gain, scale, shift, w_router, min(256, t))
    aff = jax.nn.softmax(logits, axis=-1)
    g, idx = lax.top_k(jnp.swapaxes(aff, 1, 2), cap)
    idx_s, g_s = lax.sort((idx, g), dimension=2, num_keys=1)
    bidx = jnp.arange(b)[:, None, None]
    xg = hn[bidx, idx_s]
    y = _expert_ffn(xg, g_s[..., None], w_gate, w_up, w_down, 256, 512)
    return _combine(x, gate_out, idx_s, y, final_gain)


def kernel(x, c, ctx, c_ctx, w_mod, b_mod, norm_mix, norm_ffn, norm_out, w_in, w_out_rec,
           conv_w, conv_b, lb_gamma, dt_bias, a_log, d_skip, hgrn_norm, mamba_norm,
           w_dq, q_norm, w_uq, w_dkv, kv_norm, w_ukv, w_kr, w_o,
           w_router, w_gate, w_up, w_down):
    B, T, D = x.shape
    TC = ctx.shape[1]
    lb = jnp.cumsum(jax.nn.softmax(lb_gamma.astype(F32), axis=0), axis=0)

    cvec = jnp.zeros((16, D), F32).at[:B].set(c).at[B].set(c_ctx)
    mods = _modulation(cvec, w_mod, b_mod)

    x_lat, x_ctx = x, ctx
    for l in range(DEPTH):
        need_ctx = l < DEPTH - 1
        m_lat = [mods[l, :B, i * D:(i + 1) * D][:, None, :] for i in range(6)]
        m_ctx = [jnp.broadcast_to(mods[l, B, i * D:(i + 1) * D][None, None, :], (B, 1, D))
                 for i in range(6)]
        if l % 2 == 0:
            e = l // 2
            w_in_p = jnp.pad(w_in[e], ((0, 0), (0, IN_COLS_PAD - IN_COLS))).astype(BF16)
            xs = jnp.concatenate([x_ctx, x_lat], axis=1)
            pair = lambda i: jnp.concatenate([m_ctx[i], m_lat[i]], axis=1)
            p = _modproj(xs, norm_mix[l], pair(1), pair(0), w_in_p, 256, TC // 256)
            ohf, omf, ohb, omb = _scan_mixers(p, TC // CHUNK, lb[e], conv_w[e], conv_b[e],
                                              dt_bias[e], a_log[e], d_skip[e])
            x_ctx, x_lat = _merge_mixers(xs, pair(2), ohf, ohb, omf, omb, p, hgrn_norm[e],
                                         mamba_norm[e], w_out_rec[e].astype(BF16), TC // 256, 256)
        else:
            j = l // 2
            cos_l, sin_l = _rope_tables(T)
            cos_c = jnp.concatenate([jnp.ones((TC, ROPE), F32), jnp.zeros((TC, ROPE), F32)], -1)
            sin_c = jnp.zeros((TC, 128), F32)
            w_cat = jnp.concatenate([w_dq[j], w_dkv[j], w_kr[j], _swap_cols(w_kr[j])],
                                    axis=1).astype(BF16)
            wq = w_uq[j].reshape(Q_LORA, H_C, NOPE + ROPE)
            wq_h = jnp.concatenate([wq, _swap_cols(wq[..., NOPE:])], axis=-1)
            wq_h = jnp.transpose(wq_h, (1, 0, 2)).astype(BF16)
            wkv_h = jnp.transpose(w_ukv[j].reshape(KV_LORA, H_C, NOPE + VH), (1, 0, 2)).astype(BF16)
            cq_l, ckv_l, kr_l = _mla_down(x_lat, norm_mix[l], m_lat[1], m_lat[0], w_cat,
                                          q_norm[j], kv_norm[j], cos_l, sin_l, 256)
            _, ckv_c, kr_c = _mla_down(x_ctx, norm_mix[l], m_ctx[1], m_ctx[0], w_cat,
                                       q_norm[j], kv_norm[j], cos_c, sin_c, 256)
            ckv = jnp.concatenate([ckv_c, ckv_l], axis=1)
            kr = jnp.concatenate([kr_c, kr_l], axis=1)
            qh = _q_up(cq_l, wq_h, cos_l, sin_l, 512)
            kh, vh = _kv_up(ckv, kr, wkv_h, 768)
            o = _attention(qh, kh, vh, 256, 768)
            x_lat = _linear_residual(x_lat, m_lat[2], o, w_o[j].astype(BF16), 512)
        x_lat = _moe(x_lat, norm_ffn[l], m_lat[4], m_lat[3], m_lat[5], w_router[l],
                     w_gate[l], w_up[l], w_down[l], None if need_ctx else norm_out)
        if need_ctx:
            x_ctx = _moe(x_ctx, norm_ffn[l], m_ctx[4], m_ctx[3], m_ctx[5], w_router[l],
                         w_gate[l], w_up[l], w_down[l])
    return x_lat
```

```python
import functools
import math

import jax
import jax.numpy as jnp
from jax import lax
from jax.experimental import pallas as pl
from jax.experimental.pallas import tpu as pltpu

F32 = jnp.float32
BF16 = jnp.bfloat16

D_MODEL = 1024
DEPTH = 2
GRID_W = 64
EPS = 1e-6
CHUNK = 64
HA, DKA, DVA = 8, 64, 64
WA = HA * DVA
HB, PB = 8, 64
D_INNER = HB * PB
N_GROUPS, D_STATE = 2, 128
CONV_W = 5
REC_SPLITS = (HA * DKA, HA * DKA, HA * DKA, WA, WA, D_INNER, D_INNER,
              N_GROUPS * D_STATE, N_GROUPS * D_STATE, HB, HB)
IN_COLS = sum(REC_SPLITS)
IN_COLS_PAD = 4224
H_C, NOPE, ROPE, VH = 8, 128, 64, 128
Q_LORA, KV_LORA = 384, 256
ROPE_THETA = 10000.0
ATTN_SCALE = 1.0 / math.sqrt(NOPE + ROPE)
QK_SCALE = ATTN_SCALE * math.log2(math.e)
QK_DIM = 256
N_EXPERTS = 16
EXPERT_FF = 1024
CAP_FACTOR = 2

VMEM_LIMIT = 56 * 1024 * 1024


def _cparams(sem):
    return pltpu.CompilerParams(dimension_semantics=sem, vmem_limit_bytes=VMEM_LIMIT)


def _silu(v):
    return v * jax.nn.sigmoid(v)


def _modnorm(x, gain, scale, shift):
    ms = jnp.mean(x * x, axis=-1, keepdims=True)
    return (x * lax.rsqrt(ms + EPS) * gain) * (1.0 + scale) + shift


def _mod_kernel(s_ref, w_ref, b_ref, o_ref):
    s = _silu(s_ref[...])
    o_ref[0] = jnp.dot(s, w_ref[0], preferred_element_type=F32,
                       precision=lax.Precision.HIGHEST) + b_ref[0]


def _modulation(cvec, w_mod, b_mod):
    n = w_mod.shape[-1]
    tn = 1536
    return pl.pallas_call(
        _mod_kernel,
        grid=(DEPTH, n // tn),
        in_specs=[pl.BlockSpec((16, D_MODEL), lambda l, j: (0, 0)),
                  pl.BlockSpec((1, D_MODEL, tn), lambda l, j: (l, 0, j)),
                  pl.BlockSpec((1, 1, tn), lambda l, j: (l, 0, j))],
        out_specs=pl.BlockSpec((1, 16, tn), lambda l, j: (l, 0, j)),
        out_shape=jax.ShapeDtypeStruct((DEPTH, 16, n), F32),
        compiler_params=_cparams(("parallel", "parallel")),
        name="modulation",
    )(cvec, w_mod, b_mod.reshape(DEPTH, 1, n))


def _modproj_kernel(x_ref, g_ref, sc_ref, sh_ref, w_ref, o_ref):
    a = _modnorm(x_ref[0], g_ref[...], sc_ref[0, 0], sh_ref[0, 0]).astype(BF16)
    o_ref[0] = jnp.dot(a, w_ref[...], preferred_element_type=F32).astype(o_ref.dtype)


def _modproj(x, gain, scale2, shift2, w, tm, nct_tiles, out_dtype=F32):
    b, t, d = x.shape
    n = w.shape[1]
    sel = lambda i, j: (i, jnp.where(j < nct_tiles, 0, 1), 0, 0)
    return pl.pallas_call(
        _modproj_kernel,
        grid=(b, t // tm),
        in_specs=[pl.BlockSpec((1, tm, d), lambda i, j: (i, j, 0)),
                  pl.BlockSpec((1, d), lambda i, j: (0, 0)),
                  pl.BlockSpec((1, 1, 1, d), sel),
                  pl.BlockSpec((1, 1, 1, d), sel),
                  pl.BlockSpec((d, n), lambda i, j: (0, 0))],
        out_specs=pl.BlockSpec((1, tm, n), lambda i, j: (i, j, 0)),
        out_shape=jax.ShapeDtypeStruct((b, t, n), out_dtype),
        compiler_params=_cparams(("parallel", "parallel")),
        name="modproj",
    )(x, gain.reshape(1, d), scale2.reshape(b, 2, 1, d), shift2.reshape(b, 2, 1, d), w)


def _linres_kernel(x_ref, gate_ref, y_ref, w_ref, o_ref):
    acc = jnp.dot(y_ref[0].astype(BF16), w_ref[...], preferred_element_type=F32)
    o_ref[0] = x_ref[0] + gate_ref[0] * acc


def _linear_residual(x, gate, y, w, tm):
    b, t, d = x.shape
    k = y.shape[-1]
    return pl.pallas_call(
        _linres_kernel,
        grid=(b, t // tm),
        in_specs=[pl.BlockSpec((1, tm, d), lambda i, j: (i, j, 0)),
                  pl.BlockSpec((1, 1, d), lambda i, j: (i, 0, 0)),
                  pl.BlockSpec((1, tm, k), lambda i, j: (i, j, 0)),
                  pl.BlockSpec((k, d), lambda i, j: (0, 0))],
        out_specs=pl.BlockSpec((1, tm, d), lambda i, j: (i, j, 0)),
        out_shape=jax.ShapeDtypeStruct((b, t, d), F32),
        compiler_params=_cparams(("parallel", "parallel")),
        name="linear_residual",
    )(x, gate, y, w)


L = CHUNK
COL_Q, COL_FF, COL_FB, COL_I, COL_G, COL_Z, COL_X, COL_DT = 0, 512, 1024, 1536, 2048, 2560, 3072, 4096


def _rms(x, gain):
    return x * lax.rsqrt(jnp.mean(x * x, axis=-1, keepdims=True) + EPS) * gain


def _split3(x):
    a = x.astype(BF16)
    r = x - a.astype(F32)
    b = r.astype(BF16)
    c = (r - b.astype(F32)).astype(BF16)
    return a, b, c


def _cumsum_exact(tri3, x):
    a, b, c = _split3(x)
    return jnp.dot(tri3, jnp.concatenate([a, b, c], axis=0), preferred_element_type=F32)


def _expand_exact(x, e3):
    a, b, c = _split3(x)
    return jnp.dot(jnp.concatenate([a, b, c], axis=1), e3, preferred_element_type=F32)


def _dot_nt(a, b):
    return lax.dot_general(a, b, (((1,), (1,)), ((), ())), preferred_element_type=F32)


def _dot_tn(a, b):
    return lax.dot_general(a, b, (((0,), (0,)), ((), ())), preferred_element_type=F32)


def _scan_dir(rev, q_ref, f_ref, v_ref, x_ref, xp_ref, xn_ref, dt_ref, pv, nv,
              lb_row, cw_ref, cbias_ref, dtb_ref, aexp_ref, dskip_ref,
              oh_ref, om_ref, sg_ref, sm_ref, u_scr):
    d = 1 if rev else 0
    ti = lax.broadcasted_iota(jnp.int32, (L, L), 0)
    si = lax.broadcasted_iota(jnp.int32, (L, L), 1)
    keep = (si >= ti) if rev else (si <= ti)
    tri = keep.astype(BF16)
    tri3 = jnp.concatenate([tri, tri, tri], axis=1)
    t2 = lax.broadcasted_iota(jnp.int32, (L, 128), 0)
    s2 = lax.broadcasted_iota(jnp.int32, (L, 128), 1) % L
    keep2 = (s2 >= t2) if rev else (s2 <= t2)
    ref_row = (L - 1 - L // 2) if rev else L // 2
    last_row = 0 if rev else L - 1
    lane = lax.broadcasted_iota(jnp.int32, (1, 128), 1)
    lo = lane < 64
    r128 = lax.broadcasted_iota(jnp.int32, (128, 128), 0)
    c128 = lax.broadcasted_iota(jnp.int32, (128, 128), 1)
    blockdiag = (r128 < 64) == (c128 < 64)
    zero_b = jnp.zeros((), BF16)

    def pair_rows(a):
        return jnp.concatenate([jnp.where(lo, a, zero_b), jnp.where(lo, zero_b, a)], axis=0)

    q = q_ref[0]
    qs = _silu(q)
    f = lb_row + (1.0 - lb_row) * jax.nn.sigmoid(f_ref[0])
    k = 1.0 - f
    la = jnp.log(f)
    b = _cumsum_exact(tri3, la)
    bref = b[ref_row:ref_row + 1, :]
    blast = b[last_row:last_row + 1, :]
    qd = (qs * jnp.exp(b - bref)).astype(BF16)
    kd = (k * jnp.exp(bref - b)).astype(BF16)
    kl = (k * jnp.exp(blast - b)).astype(BF16)
    qb = (qs * jnp.exp(b)).astype(BF16)
    dec_col = jnp.broadcast_to(jnp.exp(blast), (8, 512)).T
    vb = v_ref[0].astype(BF16)
    for p in range(HA // 2):
        sl = slice(128 * p, 128 * (p + 1))
        qd_p, kd_p, kl_p, qb_p, v_p = qd[:, sl], kd[:, sl], kl[:, sl], qb[:, sl], vb[:, sl]
        st = sg_ref[d, p]
        att = _dot_nt(qd_p, pair_rows(kd_p))
        att = jnp.where(keep2, att, 0.0).astype(BF16)
        lhs = jnp.concatenate([att, qb_p], axis=1)
        rhs = jnp.concatenate([pair_rows(v_p), st.astype(BF16)], axis=0)
        oh_ref[0, :, sl] = jnp.dot(lhs, rhs, preferred_element_type=F32)
        upd = _dot_tn(kl_p, v_p)
        sg_ref[d, p] = st * dec_col[sl, 0:1] + jnp.where(blockdiag, upd, 0.0)

    u_scr[0:8, :] = xp_ref[0] * pv
    u_scr[8:8 + L, :] = x_ref[0]
    u_scr[8 + L:16 + L, :] = xn_ref[0] * nv
    y = cbias_ref[...] + cw_ref[0:1, :] * u_scr[6:6 + L, :]
    for j in range(1, 5):
        y = y + cw_ref[j:j + 1, :] * u_scr[6 + j:6 + j + L, :]
    xbc = _silu(y)
    xm = xbc[:, :512]
    bm = xbc[:, 512:768].astype(BF16)
    cm = xbc[:, 768:1024].astype(BF16)
    draw = dt_ref[0] + dtb_ref[...]
    dt = jnp.maximum(draw, 0.0) + jnp.log(1.0 + jnp.exp(-jnp.abs(draw)))
    la_m = -dt * aexp_ref[...]
    cum = _cumsum_exact(tri3, la_m)
    er = lax.broadcasted_iota(jnp.int32, (128, 512), 0)
    ec = lax.broadcasted_iota(jnp.int32, (128, 512), 1)
    e1 = (er == (ec // 64) + 8 * d).astype(BF16)
    e1 = jnp.concatenate([e1, e1, e1], axis=0)
    both = _expand_exact(jnp.concatenate([dt, cum], axis=0), e1)
    dt_e, cum_e = both[:L], both[L:]
    cum_t = cum.T
    xdt = xm * dt_e
    clast = cum_e[last_row:last_row + 1, :]
    wx = (jnp.exp(clast - cum_e) * xdt).astype(BF16)
    dec_m = jnp.exp(clast)
    ecum = jnp.exp(cum_e)
    xdt_b = xdt.astype(BF16)
    for g in range(2):
        bm_g = bm[:, 128 * g:128 * (g + 1)]
        cm_g = cm[:, 128 * g:128 * (g + 1)]
        gl = slice(256 * g, 256 * (g + 1))
        cb2 = _dot_nt(cm_g, jnp.concatenate([bm_g, bm_g], axis=0))
        sm = sm_ref[d, g]
        y_int = jnp.dot(cm_g, sm.astype(BF16), preferred_element_type=F32) * ecum[:, gl]
        for pp in range(2):
            h0 = 4 * g + 2 * pp + 8 * d
            sl = slice(256 * g + 128 * pp, 256 * g + 128 * (pp + 1))
            crow = jnp.concatenate([cum_t[h0:h0 + 1, :], cum_t[h0 + 1:h0 + 2, :]], axis=1)
            lm = jnp.exp(jnp.where(keep2, cum_e[:, sl] - crow, -jnp.inf))
            yp = jnp.dot((cb2 * lm).astype(BF16), pair_rows(xdt_b[:, sl]), preferred_element_type=F32)
            yp = yp + y_int[:, 128 * pp:128 * (pp + 1)]
            if not rev:
                yp = yp + dskip_ref[:, sl] * xm[:, sl]
            om_ref[0, :, sl] = yp
        sm_ref[d, g] = sm * dec_m[:, gl] + _dot_tn(bm_g, wx[:, gl])


def _scan_kernel(qf, ff, vf, xf, xpf, xnf, dtf, qb, fb, vb_, xb, xpb, xnb, dtb,
                 lb_ref, cw_ref, cbias_ref, dtbias_ref, aexp_ref, dskip_ref,
                 ohf, omf, ohb, omb, sg_ref, sm_ref, u_scr, *, nct, nc):
    c = pl.program_id(1)

    @pl.when(c == 0)
    def _():
        sg_ref[...] = jnp.zeros(sg_ref.shape, F32)
        sm_ref[...] = jnp.zeros(sm_ref.shape, F32)

    cb = jnp.where(c < nct, nct - 1 - c, nc + nct - 1 - c)

    def edge_flags(ch):
        pv = jnp.where((ch == 0) | (ch == nct), 0.0, 1.0)
        nv = jnp.where((ch == nct - 1) | (ch == nc - 1), 0.0, 1.0)
        return pv, nv

    pvf, nvf = edge_flags(c)
    pvb, nvb = edge_flags(cb)
    _scan_dir(False, qf, ff, vf, xf, xpf, xnf, dtf, pvf, nvf, lb_ref[0:1, :], cw_ref, cbias_ref,
              dtbias_ref, aexp_ref, dskip_ref, ohf, omf, sg_ref, sm_ref, u_scr)
    _scan_dir(True, qb, fb, vb_, xb, xpb, xnb, dtb, pvb, nvb, lb_ref[1:2, :], cw_ref, cbias_ref,
              dtbias_ref, aexp_ref, dskip_ref, ohb, omb, sg_ref, sm_ref, u_scr)


def _scan_mixers(p, nct, lb, conv_w, conv_b, dt_bias, a_log, d_skip):
    bsz, s, _ = p.shape
    nc = s // L
    nb8 = s // 8

    def bmap(c):
        return jnp.where(c < nct, nct - 1 - c, nc + nct - 1 - c)

    def specs(cmap, fcol):
        colblk = lambda col, w: col // w
        return [
            pl.BlockSpec((1, L, 512), lambda i, c: (i, cmap(c), colblk(COL_Q, 512))),
            pl.BlockSpec((1, L, 512), lambda i, c: (i, cmap(c), colblk(fcol, 512))),
            pl.BlockSpec((1, L, 512), lambda i, c: (i, cmap(c), colblk(COL_I, 512))),
            pl.BlockSpec((1, L, 1024), lambda i, c: (i, cmap(c), colblk(COL_X, 1024))),
            pl.BlockSpec((1, 8, 1024), lambda i, c: (i, jnp.maximum(cmap(c) * 8 - 1, 0), colblk(COL_X, 1024))),
            pl.BlockSpec((1, 8, 1024), lambda i, c: (i, jnp.minimum(cmap(c) * 8 + 8, nb8 - 1), colblk(COL_X, 1024))),
            pl.BlockSpec((1, L, 128), lambda i, c: (i, cmap(c), colblk(COL_DT, 128))),
        ]

    ident = lambda c: c
    const2 = lambda i, c: (0, 0)
    dtb = jnp.zeros((1, 128), F32).at[0, :16].set(dt_bias.reshape(16))
    aexp = jnp.zeros((1, 128), F32).at[0, :16].set(jnp.exp(a_log.reshape(16)))
    dsk = jnp.repeat(d_skip, 64).reshape(1, 512)
    kern = functools.partial(_scan_kernel, nct=nct, nc=nc)
    out_sd = jax.ShapeDtypeStruct((bsz, s, 512), F32)
    return pl.pallas_call(
        kern,
        grid=(bsz, nc),
        in_specs=specs(ident, COL_FF) + specs(bmap, COL_FB) + [
            pl.BlockSpec((2, 512), const2), pl.BlockSpec((5, 1024), const2),
            pl.BlockSpec((1, 1024), const2), pl.BlockSpec((1, 128), const2),
            pl.BlockSpec((1, 128), const2), pl.BlockSpec((1, 512), const2)],
        out_specs=[pl.BlockSpec((1, L, 512), lambda i, c: (i, c, 0)),
                   pl.BlockSpec((1, L, 512), lambda i, c: (i, c, 0)),
                   pl.BlockSpec((1, L, 512), lambda i, c: (i, bmap(c), 0)),
                   pl.BlockSpec((1, L, 512), lambda i, c: (i, bmap(c), 0))],
        out_shape=[out_sd, out_sd, out_sd, out_sd],
        scratch_shapes=[pltpu.VMEM((2, HA // 2, 128, 128), F32),
                        pltpu.VMEM((2, 2, 128, 256), F32),
                        pltpu.VMEM((L + 16, 1024), F32)],
        compiler_params=_cparams(("parallel", "arbitrary")),
        name="scan_mixers",
    )(*([p] * 14), lb, conv_w, conv_b.reshape(1, 1024), dtb, aexp, dsk)


def _merge_kernel(ohf, ohb, omf, omb, g_ref, z_ref, x_ref, gate_ref, hn_ref, mn_ref, w_ref,
                  oc_ref, ol_ref, *, nct_tiles):
    oh = ohf[0] + ohb[0]
    sq = oh * oh
    hi = sq.astype(BF16)
    lo = (sq - hi.astype(F32)).astype(BF16)
    r = lax.broadcasted_iota(jnp.int32, (512, 512), 0) // 64
    c = lax.broadcasted_iota(jnp.int32, (512, 512), 1) // 64
    avg = jnp.where(r == c, 1.0 / 64, 0.0).astype(BF16)
    ms = jnp.dot(hi, avg, preferred_element_type=F32) + jnp.dot(lo, avg, preferred_element_type=F32)
    oh = oh * lax.rsqrt(ms + EPS) * hn_ref[...] * jax.nn.sigmoid(g_ref[0])
    y = (omf[0] + omb[0]) * _silu(z_ref[0])
    y = y * lax.rsqrt(jnp.mean(y * y, axis=-1, keepdims=True) + EPS) * mn_ref[...]
    cat = jnp.concatenate([oh, y], axis=1).astype(BF16)
    out = x_ref[0] + gate_ref[0, 0] * jnp.dot(cat, w_ref[...], preferred_element_type=F32)
    j = pl.program_id(1)

    @pl.when(j < nct_tiles)
    def _():
        oc_ref[0] = out

    @pl.when(j >= nct_tiles)
    def _():
        ol_ref[0] = out


def _merge_mixers(x, gate2, ohf, ohb, omf, omb, p, hgrn_norm, mamba_norm, w_out, nct_tiles, tm):
    b, s, d = x.shape
    sel = lambda i, j: (i, jnp.where(j < nct_tiles, 0, 1), 0, 0)
    row = lambda i, j: (i, j, 0)
    const2 = lambda i, j: (0, 0)
    tc = nct_tiles * tm
    return pl.pallas_call(
        functools.partial(_merge_kernel, nct_tiles=nct_tiles),
        grid=(b, s // tm),
        in_specs=[pl.BlockSpec((1, tm, 512), row)] * 4 + [
            pl.BlockSpec((1, tm, 512), lambda i, j: (i, j, COL_G // 512)),
            pl.BlockSpec((1, tm, 512), lambda i, j: (i, j, COL_Z // 512)),
            pl.BlockSpec((1, tm, d), row),
            pl.BlockSpec((1, 1, 1, d), sel),
            pl.BlockSpec((1, 512), const2), pl.BlockSpec((1, 512), const2),
            pl.BlockSpec((2 * 512, d), const2)],
        out_specs=[pl.BlockSpec((1, tm, d), lambda i, j: (i, jnp.minimum(j, nct_tiles - 1), 0)),
                   pl.BlockSpec((1, tm, d), lambda i, j: (i, jnp.maximum(j - nct_tiles, 0), 0))],
        out_shape=[jax.ShapeDtypeStruct((b, tc, d), F32),
                   jax.ShapeDtypeStruct((b, s - tc, d), F32)],
        compiler_params=_cparams(("parallel", "arbitrary")),
        name="merge_mixers",
    )(ohf, ohb, omf, omb, p, p, x, gate2.reshape(b, 2, 1, d), hgrn_norm.reshape(1, 512),
      mamba_norm.reshape(1, 512), w_out)


def _rope_tables(t):
    rows = t // GRID_W
    row = jnp.repeat(jnp.arange(rows, dtype=jnp.int32), GRID_W)
    col = jnp.tile(jnp.arange(GRID_W, dtype=jnp.int32), rows)
    nf = ROPE // 4
    inv_freq = ROPE_THETA ** (-jnp.arange(nf, dtype=F32) / nf)
    pos = jnp.stack([row, col], axis=-1).astype(F32)
    ang = pos[..., None] * inv_freq
    cos, sin = jnp.cos(ang), jnp.sin(ang)
    cos64 = jnp.broadcast_to(cos[:, :, None, :], (t, 2, 2, nf)).reshape(t, ROPE)
    sin64 = jnp.broadcast_to(sin[:, :, None, :], (t, 2, 2, nf)).reshape(t, ROPE)
    zero = jnp.zeros((t, ROPE), F32)
    return jnp.concatenate([cos64, zero], -1), jnp.concatenate([sin64, zero], -1)


def _swap_cols(w):
    nf = ROPE // 4
    wr = w.reshape(w.shape[:-1] + (2, 2, nf))
    return jnp.stack([-wr[..., 1, :], wr[..., 0, :]], axis=-2).reshape(w.shape)


def _rope_tile(tile, cos_t, sin_t):
    return tile * cos_t + pltpu.roll(tile, ROPE, axis=1) * sin_t


def _mla_down_kernel(x_ref, g_ref, sc_ref, sh_ref, w_ref, qn_ref, kvn_ref, cos_ref, sin_ref,
                     cq_ref, ckv_ref, kr_ref):
    a = _modnorm(x_ref[0], g_ref[...], sc_ref[0], sh_ref[0]).astype(BF16)
    c = jnp.dot(a, w_ref[...], preferred_element_type=F32)
    cq = c[:, :Q_LORA]
    ckv = c[:, Q_LORA:Q_LORA + KV_LORA]
    cq_ref[0] = _rms(cq, qn_ref[...]).astype(BF16)
    ckv_ref[0] = _rms(ckv, kvn_ref[...]).astype(BF16)
    kr_ref[0] = _rope_tile(c[:, Q_LORA + KV_LORA:], cos_ref[...], sin_ref[...]).astype(BF16)


def _mla_down(x, gain, scale, shift, w_cat, q_norm, kv_norm, cos_t, sin_t, tm):
    b, t, d = x.shape
    n = w_cat.shape[1]
    return pl.pallas_call(
        _mla_down_kernel,
        grid=(b, t // tm),
        in_specs=[pl.BlockSpec((1, tm, d), lambda i, j: (i, j, 0)),
                  pl.BlockSpec((1, d), lambda i, j: (0, 0)),
                  pl.BlockSpec((1, 1, d), lambda i, j: (i, 0, 0)),
                  pl.BlockSpec((1, 1, d), lambda i, j: (i, 0, 0)),
                  pl.BlockSpec((d, n), lambda i, j: (0, 0)),
                  pl.BlockSpec((1, Q_LORA), lambda i, j: (0, 0)),
                  pl.BlockSpec((1, KV_LORA), lambda i, j: (0, 0)),
                  pl.BlockSpec((tm, 128), lambda i, j: (j, 0)),
                  pl.BlockSpec((tm, 128), lambda i, j: (j, 0))],
        out_specs=[pl.BlockSpec((1, tm, Q_LORA), lambda i, j: (i, j, 0)),
                   pl.BlockSpec((1, tm, KV_LORA), lambda i, j: (i, j, 0)),
                   pl.BlockSpec((1, tm, 128), lambda i, j: (i, j, 0))],
        out_shape=[jax.ShapeDtypeStruct((b, t, Q_LORA), BF16),
                   jax.ShapeDtypeStruct((b, t, KV_LORA), BF16),
                   jax.ShapeDtypeStruct((b, t, 128), BF16)],
        compiler_params=_cparams(("parallel", "parallel")),
        name="mla_down",
    )(x, gain.reshape(1, d), scale, shift, w_cat, q_norm.reshape(1, -1), kv_norm.reshape(1, -1),
      cos_t, sin_t)


def _q_up_kernel(cq_ref, w_ref, cos_ref, sin_ref, q_ref):
    cq = cq_ref[0]
    for h in range(H_C):
        q = jnp.dot(cq, w_ref[h], preferred_element_type=F32)
        qn = q[:, :NOPE] * QK_SCALE
        qr = _rope_tile(q[:, NOPE:], cos_ref[...], sin_ref[...]) * QK_SCALE
        q_ref[0, h] = jnp.concatenate([qn, qr], axis=1).astype(BF16)


def _q_up(cq, w_uq_h, cos_t, sin_t, tm):
    b, t, r = cq.shape
    return pl.pallas_call(
        _q_up_kernel,
        grid=(b, t // tm),
        in_specs=[pl.BlockSpec((1, tm, r), lambda i, j: (i, j, 0)),
                  pl.BlockSpec((H_C, r, QK_DIM), lambda i, j: (0, 0, 0)),
                  pl.BlockSpec((tm, 128), lambda i, j: (j, 0)),
                  pl.BlockSpec((tm, 128), lambda i, j: (j, 0))],
        out_specs=pl.BlockSpec((1, H_C, tm, QK_DIM), lambda i, j: (i, 0, j, 0)),
        out_shape=jax.ShapeDtypeStruct((b, H_C, t, QK_DIM), BF16),
        compiler_params=_cparams(("parallel", "parallel")),
        name="q_up",
    )(cq, w_uq_h, cos_t, sin_t)


def _kv_up_kernel(ckv_ref, kr_ref, w_ref, k_ref, v_ref):
    ckv = ckv_ref[0]
    kr = kr_ref[0]
    for h in range(H_C):
        kv = jnp.dot(ckv, w_ref[h], preferred_element_type=F32)
        k_ref[0, h] = jnp.concatenate([kv[:, :NOPE].astype(BF16), kr], axis=1)
        v_ref[0, h] = kv[:, NOPE:].astype(BF16)


def _kv_up(ckv, kr, w_ukv_h, tm):
    b, s, r = ckv.shape
    return pl.pallas_call(
        _kv_up_kernel,
        grid=(b, s // tm),
        in_specs=[pl.BlockSpec((1, tm, r), lambda i, j: (i, j, 0)),
                  pl.BlockSpec((1, tm, 128), lambda i, j: (i, j, 0)),
                  pl.BlockSpec((H_C, r, NOPE + VH), lambda i, j: (0, 0, 0))],
        out_specs=[pl.BlockSpec((1, H_C, tm, QK_DIM), lambda i, j: (i, 0, j, 0)),
                   pl.BlockSpec((1, H_C, tm, VH), lambda i, j: (i, 0, j, 0))],
        out_shape=[jax.ShapeDtypeStruct((b, H_C, s, QK_DIM), BF16),
                   jax.ShapeDtypeStruct((b, H_C, s, VH), BF16)],
        compiler_params=_cparams(("parallel", "parallel")),
        name="kv_up",
    )(ckv, kr, w_ukv_h)


def _attn_kernel(qa_ref, qb_ref, q0_ref, k_ref, v_ref, o_ref, s0_scr, s1_scr, m_scr, *, tk, nk, tq):
    n = pl.program_id(2)

    def scores(q, j):
        ks = k_ref[0, 0, j * tk:(j + 1) * tk, :]
        return lax.dot_general(q, ks, (((1,), (1,)), ((), ())), preferred_element_type=F32)

    def fold_max(mrun, s):
        for c in range(tk // 128):
            mrun = jnp.maximum(mrun, s[:, c * 128:(c + 1) * 128])
        return mrun

    @pl.when(n == 0)
    def _():
        q0 = q0_ref[0, 0]
        mrun = jnp.full((tq, 128), -jnp.inf, F32)
        for j in range(nk):
            s = scores(q0, j)
            s0_scr[:, j * tk:(j + 1) * tk] = s
            mrun = fold_max(mrun, s)
        m_scr[...] = jnp.max(mrun, axis=-1, keepdims=True)

    def fused(qn, s_read, s_write, m):
        mrun = jnp.full((tq, 128), -jnp.inf, F32)
        lrun = jnp.zeros((tq, 128), F32)
        acc = jnp.zeros((tq, VH), F32)
        for j in range(nk):
            sn = scores(qn, j)
            s_write[:, j * tk:(j + 1) * tk] = sn
            mrun = fold_max(mrun, sn)
            p = jnp.exp2(s_read[:, j * tk:(j + 1) * tk] - m)
            for c in range(tk // 128):
                lrun = lrun + p[:, c * 128:(c + 1) * 128]
            acc = acc + jnp.dot(p.astype(BF16), v_ref[0, 0, j * tk:(j + 1) * tk, :],
                                preferred_element_type=F32)
        out = acc / jnp.sum(lrun, axis=-1, keepdims=True)
        return out, jnp.max(mrun, axis=-1, keepdims=True)

    out_a, m1 = fused(qa_ref[0, 0], s0_scr, s1_scr, m_scr[...])
    o_ref[0, 0:tq, :] = out_a.astype(o_ref.dtype)
    out_b, m2 = fused(qb_ref[0, 0], s1_scr, s0_scr, m1)
    o_ref[0, tq:2 * tq, :] = out_b.astype(o_ref.dtype)
    m_scr[...] = m2


def _attention(q, k, v, tq, tk):
    b, h, t, _ = q.shape
    s = k.shape[2]
    nq = t // tq
    assert t % (2 * tq) == 0 and s % tk == 0
    kern = functools.partial(_attn_kernel, tk=tk, nk=s // tk, tq=tq)
    return pl.pallas_call(
        kern,
        grid=(b, h, nq // 2),
        in_specs=[pl.BlockSpec((1, 1, tq, QK_DIM), lambda i, j, n: (i, j, 2 * n + 1, 0)),
                  pl.BlockSpec((1, 1, tq, QK_DIM),
                               lambda i, j, n: (i, j, jnp.minimum(2 * n + 2, nq - 1), 0)),
                  pl.BlockSpec((1, 1, tq, QK_DIM), lambda i, j, n: (i, j, 0, 0)),
                  pl.BlockSpec((1, 1, s, QK_DIM), lambda i, j, n: (i, j, 0, 0)),
                  pl.BlockSpec((1, 1, s, VH), lambda i, j, n: (i, j, 0, 0))],
        out_specs=pl.BlockSpec((1, 2 * tq, VH), lambda i, j, n: (i, n, j)),
        out_shape=jax.ShapeDtypeStruct((b, t, h * VH), BF16),
        scratch_shapes=[pltpu.VMEM((tq, s), F32), pltpu.VMEM((tq, s), F32),
                        pltpu.VMEM((tq, 1), F32)],
        compiler_params=_cparams(("parallel", "parallel", "arbitrary")),
        name="attention",
    )(q, q, q, k, v)


def _router_kernel(x_ref, g_ref, sc_ref, sh_ref, w_ref, h_ref, lg_ref):
    hn = _modnorm(x_ref[0], g_ref[...], sc_ref[0], sh_ref[0])
    h_ref[0] = hn.astype(BF16)
    lg_ref[0] = jnp.dot(hn, w_ref[...], preferred_element_type=F32,
                        precision=lax.Precision.HIGHEST)


def _router(x, gain, scale, shift, w_router, tm):
    b, t, d = x.shape
    e = w_router.shape[1]
    return pl.pallas_call(
        _router_kernel,
        grid=(b, t // tm),
        in_specs=[pl.BlockSpec((1, tm, d), lambda i, j: (i, j, 0)),
                  pl.BlockSpec((1, d), lambda i, j: (0, 0)),
                  pl.BlockSpec((1, 1, d), lambda i, j: (i, 0, 0)),
                  pl.BlockSpec((1, 1, d), lambda i, j: (i, 0, 0)),
                  pl.BlockSpec((d, e), lambda i, j: (0, 0))],
        out_specs=[pl.BlockSpec((1, tm, d), lambda i, j: (i, j, 0)),
                   pl.BlockSpec((1, tm, e), lambda i, j: (i, j, 0))],
        out_shape=[jax.ShapeDtypeStruct((b, t, d), BF16),
                   jax.ShapeDtypeStruct((b, t, e), F32)],
        compiler_params=_cparams(("parallel", "parallel")),
        name="router",
    )(x, gain.reshape(1, d), scale, shift, w_router)


def _ffn_kernel(x_ref, wg_ref, wu_ref, wd_ref, g_ref, o_ref, acc_scr, *, fc):
    x = x_ref[0, 0]
    nf = wg_ref.shape[-1]
    for c in range(nf // fc):
        sl = slice(c * fc, (c + 1) * fc)
        hg = jnp.dot(x, wg_ref[0, :, sl], preferred_element_type=F32)
        hu = jnp.dot(x, wu_ref[0, :, sl], preferred_element_type=F32)
        hid = (_silu(hg) * hu).astype(BF16)
        part = jnp.dot(hid, wd_ref[0, sl, :], preferred_element_type=F32)
        if c == 0:
            acc_scr[...] = part
        else:
            acc_scr[...] += part
    o_ref[0, 0] = (acc_scr[...] * g_ref[0, 0]).astype(o_ref.dtype)


def _expert_ffn(xg, gates, w_gate, w_up, w_down, tm, fc):
    b, e, cap, d = xg.shape
    f = w_gate.shape[-1]
    tm = min(tm, cap)
    kern = functools.partial(_ffn_kernel, fc=fc)
    return pl.pallas_call(
        kern,
        grid=(e, b, cap // tm),
        in_specs=[pl.BlockSpec((1, 1, tm, d), lambda j, i, m: (i, j, m, 0)),
                  pl.BlockSpec((1, d, f), lambda j, i, m: (j, 0, 0)),
                  pl.BlockSpec((1, d, f), lambda j, i, m: (j, 0, 0)),
                  pl.BlockSpec((1, f, d), lambda j, i, m: (j, 0, 0)),
                  pl.BlockSpec((1, 1, tm, 1), lambda j, i, m: (i, j, m, 0))],
        out_specs=pl.BlockSpec((1, 1, tm, d), lambda j, i, m: (i, j, m, 0)),
        out_shape=jax.ShapeDtypeStruct((b, e, cap, d), BF16),
        scratch_shapes=[pltpu.VMEM((tm, d), F32)],
        compiler_params=_cparams(("parallel", "parallel", "arbitrary")),
        name="expert_ffn",
    )(xg, w_gate, w_up, w_down, gates)


SLOT_WIN = 256
TOK_BLK = 128
COMBINE_EXPERTS = 2


def _combine_kernel(lo_ref, idx_ref, y_ref, x_ref, gate_ref, fg_ref, o_ref, *, nblk, nrow,
                    final_norm):
    b, j, eg = pl.program_id(0), pl.program_id(1), pl.program_id(2)

    @pl.when(eg == 0)
    def _():
        o_ref[...] = jnp.zeros(o_ref.shape, F32)

    tok = lax.broadcasted_iota(jnp.int32, (TOK_BLK, 1), 0)
    for blk in range(nblk):
        gblk = j * nblk + blk
        part = None
        for k in range(COMBINE_EXPERTS):
            w = jnp.minimum(lo_ref[b, eg * COMBINE_EXPERTS + k, gblk] // TOK_BLK, nrow - 2)
            ids = jnp.concatenate([idx_ref[0, k, pl.ds(w, 1), :],
                                   idx_ref[0, k, pl.ds(w + 1, 1), :]], axis=1)
            onehot = (ids == tok + gblk * TOK_BLK).astype(BF16)
            yw = y_ref[0, k, pl.ds(pl.multiple_of(w * TOK_BLK, TOK_BLK), SLOT_WIN), :]
            d = jnp.dot(onehot, yw, preferred_element_type=F32)
            part = d if part is None else part + d
        rows = slice(blk * TOK_BLK, (blk + 1) * TOK_BLK)
        o_ref[0, rows, :] += part

    @pl.when(eg == pl.num_programs(2) - 1)
    def _():
        out = x_ref[0] + gate_ref[0] * o_ref[0]
        o_ref[0] = _rms(out, fg_ref[...]) if final_norm else out


def _combine(x, gate_out, idx_s, y, final_gain=None):
    b, t, d = x.shape
    e, cap = idx_s.shape[1:]
    capp = max(cap, SLOT_WIN)
    if capp != cap:
        idx_s = jnp.pad(idx_s, ((0, 0), (0, 0), (0, capp - cap)), constant_values=-1)
        y = jnp.pad(y, ((0, 0), (0, 0), (0, capp - cap), (0, 0)))
    nrow = capp // TOK_BLK
    ts = min(2048, t)
    nblk = ts // TOK_BLK
    bounds = jnp.arange(t // TOK_BLK, dtype=jnp.int32) * TOK_BLK
    valid = idx_s >= 0
    lo = jnp.sum((valid[..., None] & (idx_s[..., None] < bounds)).astype(jnp.int32), axis=2)
    kern = functools.partial(_combine_kernel, nblk=nblk, nrow=nrow,
                             final_norm=final_gain is not None)
    fg = jnp.ones((1, d), F32) if final_gain is None else final_gain.reshape(1, d)
    grid_spec = pltpu.PrefetchScalarGridSpec(
        num_scalar_prefetch=1,
        grid=(b, t // ts, e // COMBINE_EXPERTS),
        in_specs=[pl.BlockSpec((1, COMBINE_EXPERTS, nrow, TOK_BLK), lambda i, j, k, lo_r: (i, k, 0, 0)),
                  pl.BlockSpec((1, COMBINE_EXPERTS, capp, d), lambda i, j, k, lo_r: (i, k, 0, 0)),
                  pl.BlockSpec((1, ts, d), lambda i, j, k, lo_r: (i, j, 0)),
                  pl.BlockSpec((1, 1, d), lambda i, j, k, lo_r: (i, 0, 0)),
                  pl.BlockSpec((1, d), lambda i, j, k, lo_r: (0, 0))],
        out_specs=pl.BlockSpec((1, ts, d), lambda i, j, k, lo_r: (i, j, 0)),
    )
    return pl.pallas_call(
        kern,
        grid_spec=grid_spec,
        out_shape=jax.ShapeDtypeStruct((b, t, d), F32),
        compiler_params=_cparams(("parallel", "parallel", "arbitrary")),
        name="moe_combine",
    )(lo, idx_s.reshape(b, e, nrow, TOK_BLK), y, x, gate_out, fg)


def _moe(x, gain, scale, shift, gate_out, w_router, w_gate, w_up, w_down, final_gain=None):
    b, t, d = x.shape
    cap = CAP_FACTOR * t // N_EXPERTS
    hn, logits = _router(x, gain, scale, shift, w_router, min(256, t))
    aff = jax.nn.softmax(logits, axis=-1)
    g, idx = lax.top_k(jnp.swapaxes(aff, 1, 2), cap)
    idx_s, g_s = lax.sort((idx, g), dimension=2, num_keys=1)
    bidx = jnp.arange(b)[:, None, None]
    xg = hn[bidx, idx_s]
    y = _expert_ffn(xg, g_s[..., None], w_gate, w_up, w_down, 256, 512)
    return _combine(x, gate_out, idx_s, y, final_gain)


def kernel(x, c, ctx, c_ctx, w_mod, b_mod, norm_mix, norm_ffn, norm_out, w_in, w_out_rec,
           conv_w, conv_b, lb_gamma, dt_bias, a_log, d_skip, hgrn_norm, mamba_norm,
           w_dq, q_norm, w_uq, w_dkv, kv_norm, w_ukv, w_kr, w_o,
           w_router, w_gate, w_up, w_down):
    B, T, D = x.shape
    TC = ctx.shape[1]
    lb = jnp.cumsum(jax.nn.softmax(lb_gamma.astype(F32), axis=0), axis=0)

    cvec = jnp.zeros((16, D), F32).at[:B].set(c).at[B].set(c_ctx)
    mods = _modulation(cvec, w_mod, b_mod)

    x_lat, x_ctx = x, ctx
    for l in range(DEPTH):
        need_ctx = l < DEPTH - 1
        m_lat = [mods[l, :B, i * D:(i + 1) * D][:, None, :] for i in range(6)]
        m_ctx = [jnp.broadcast_to(mods[l, B, i * D:(i + 1) * D][None, None, :], (B, 1, D))
                 for i in range(6)]
        if l % 2 == 0:
            e = l // 2
            w_in_p = jnp.pad(w_in[e], ((0, 0), (0, IN_COLS_PAD - IN_COLS))).astype(BF16)
            xs = jnp.concatenate([x_ctx, x_lat], axis=1)
            pair = lambda i: jnp.concatenate([m_ctx[i], m_lat[i]], axis=1)
            p = _modproj(xs, norm_mix[l], pair(1), pair(0), w_in_p, 256, TC // 256)
            ohf, omf, ohb, omb = _scan_mixers(p, TC // CHUNK, lb[e], conv_w[e], conv_b[e],
                                              dt_bias[e], a_log[e], d_skip[e])
            x_ctx, x_lat = _merge_mixers(xs, pair(2), ohf, ohb, omf, omb, p, hgrn_norm[e],
                                         mamba_norm[e], w_out_rec[e].astype(BF16), TC // 256, 256)
        else:
            j = l // 2
            cos_l, sin_l = _rope_tables(T)
            cos_c = jnp.concatenate([jnp.ones((TC, ROPE), F32), jnp.zeros((TC, ROPE), F32)], -1)
            sin_c = jnp.zeros((TC, 128), F32)
            w_cat = jnp.concatenate([w_dq[j], w_dkv[j], w_kr[j], _swap_cols(w_kr[j])],
                                    axis=1).astype(BF16)
            wq = w_uq[j].reshape(Q_LORA, H_C, NOPE + ROPE)
            wq_h = jnp.concatenate([wq, _swap_cols(wq[..., NOPE:])], axis=-1)
            wq_h = jnp.transpose(wq_h, (1, 0, 2)).astype(BF16)
            wkv_h = jnp.transpose(w_ukv[j].reshape(KV_LORA, H_C, NOPE + VH), (1, 0, 2)).astype(BF16)
            cq_l, ckv_l, kr_l = _mla_down(x_lat, norm_mix[l], m_lat[1], m_lat[0], w_cat,
                                          q_norm[j], kv_norm[j], cos_l, sin_l, 256)
            _, ckv_c, kr_c = _mla_down(x_ctx, norm_mix[l], m_ctx[1], m_ctx[0], w_cat,
                                       q_norm[j], kv_norm[j], cos_c, sin_c, 256)
            ckv = jnp.concatenate([ckv_c, ckv_l], axis=1)
            kr = jnp.concatenate([kr_c, kr_l], axis=1)
            qh = _q_up(cq_l, wq_h, cos_l, sin_l, 512)
            kh, vh = _kv_up(ckv, kr, wkv_h, 768)
            o = _attention(qh, kh, vh, 256, 768)
            x_lat = _linear_residual(x_lat, m_lat[2], o, w_o[j].astype(BF16), 512)
        wg_b, wu_b, wd_b = w_gate[l].astype(BF16), w_up[l].astype(BF16), w_down[l].astype(BF16)
        x_lat = _moe(x_lat, norm_ffn[l], m_lat[4], m_lat[3], m_lat[5], w_router[l],
                     wg_b, wu_b, wd_b, None if need_ctx else norm_out)
        if need_ctx:
            x_ctx = _moe(x_ctx, norm_ffn[l], m_ctx[4], m_ctx[3], m_ctx[5], w_router[l],
                         wg_b, wu_b, wd_b)
    return x_lat
```

```python
import functools
import math

import jax
import jax.numpy as jnp
from jax import lax
from jax.experimental import pallas as pl
from jax.experimental.pallas import tpu as pltpu

F32 = jnp.float32
BF16 = jnp.bfloat16

D_MODEL = 1024
DEPTH = 2
GRID_W = 64
EPS = 1e-6
CHUNK = 64
HA, DKA, DVA = 8, 64, 64
WA = HA * DVA
HB, PB = 8, 64
D_INNER = HB * PB
N_GROUPS, D_STATE = 2, 128
CONV_W = 5
REC_SPLITS = (HA * DKA, HA * DKA, HA * DKA, WA, WA, D_INNER, D_INNER,
              N_GROUPS * D_STATE, N_GROUPS * D_STATE, HB, HB)
IN_COLS = sum(REC_SPLITS)
IN_COLS_PAD = 4224
H_C, NOPE, ROPE, VH = 8, 128, 64, 128
Q_LORA, KV_LORA = 384, 256
ROPE_THETA = 10000.0
ATTN_SCALE = 1.0 / math.sqrt(NOPE + ROPE)
QK_SCALE = ATTN_SCALE * math.log2(math.e)
QK_DIM = 256
N_EXPERTS = 16
EXPERT_FF = 1024
CAP_FACTOR = 2

VMEM_LIMIT = 56 * 1024 * 1024


def _cparams(sem):
    return pltpu.CompilerParams(dimension_semantics=sem, vmem_limit_bytes=VMEM_LIMIT)


def _silu(v):
    return v * jax.nn.sigmoid(v)


def _modnorm(x, gain, scale, shift):
    ms = jnp.mean(x * x, axis=-1, keepdims=True)
    return (x * lax.rsqrt(ms + EPS) * gain) * (1.0 + scale) + shift


def _mod_kernel(s_ref, w_ref, b_ref, o_ref):
    s = _silu(s_ref[...])
    o_ref[0] = jnp.dot(s, w_ref[0], preferred_element_type=F32,
                       precision=lax.Precision.HIGHEST) + b_ref[0]


def _modulation(cvec, w_mod, b_mod):
    n = w_mod.shape[-1]
    tn = 1536
    return pl.pallas_call(
        _mod_kernel,
        grid=(DEPTH, n // tn),
        in_specs=[pl.BlockSpec((16, D_MODEL), lambda l, j: (0, 0)),
                  pl.BlockSpec((1, D_MODEL, tn), lambda l, j: (l, 0, j)),
                  pl.BlockSpec((1, 1, tn), lambda l, j: (l, 0, j))],
        out_specs=pl.BlockSpec((1, 16, tn), lambda l, j: (l, 0, j)),
        out_shape=jax.ShapeDtypeStruct((DEPTH, 16, n), F32),
        compiler_params=_cparams(("parallel", "parallel")),
        name="modulation",
    )(cvec, w_mod, b_mod.reshape(DEPTH, 1, n))


def _modproj_kernel(x_ref, g_ref, sc_ref, sh_ref, w_ref, o_ref):
    a = _modnorm(x_ref[0], g_ref[...], sc_ref[0, 0], sh_ref[0, 0]).astype(BF16)
    o_ref[0] = jnp.dot(a, w_ref[...], preferred_element_type=F32).astype(o_ref.dtype)


def _modproj(x, gain, scale2, shift2, w, tm, nct_tiles, out_dtype=F32):
    b, t, d = x.shape
    n = w.shape[1]
    sel = lambda i, j: (i, jnp.where(j < nct_tiles, 0, 1), 0, 0)
    return pl.pallas_call(
        _modproj_kernel,
        grid=(b, t // tm),
        in_specs=[pl.BlockSpec((1, tm, d), lambda i, j: (i, j, 0)),
                  pl.BlockSpec((1, d), lambda i, j: (0, 0)),
                  pl.BlockSpec((1, 1, 1, d), sel),
                  pl.BlockSpec((1, 1, 1, d), sel),
                  pl.BlockSpec((d, n), lambda i, j: (0, 0))],
        out_specs=pl.BlockSpec((1, tm, n), lambda i, j: (i, j, 0)),
        out_shape=jax.ShapeDtypeStruct((b, t, n), out_dtype),
        compiler_params=_cparams(("parallel", "parallel")),
        name="modproj",
    )(x, gain.reshape(1, d), scale2.reshape(b, 2, 1, d), shift2.reshape(b, 2, 1, d), w)


def _linres_kernel(x_ref, gate_ref, y_ref, w_ref, o_ref):
    acc = jnp.dot(y_ref[0].astype(BF16), w_ref[...], preferred_element_type=F32)
    o_ref[0] = x_ref[0] + gate_ref[0] * acc


def _linear_residual(x, gate, y, w, tm):
    b, t, d = x.shape
    k = y.shape[-1]
    return pl.pallas_call(
        _linres_kernel,
        grid=(b, t // tm),
        in_specs=[pl.BlockSpec((1, tm, d), lambda i, j: (i, j, 0)),
                  pl.BlockSpec((1, 1, d), lambda i, j: (i, 0, 0)),
                  pl.BlockSpec((1, tm, k), lambda i, j: (i, j, 0)),
                  pl.BlockSpec((k, d), lambda i, j: (0, 0))],
        out_specs=pl.BlockSpec((1, tm, d), lambda i, j: (i, j, 0)),
        out_shape=jax.ShapeDtypeStruct((b, t, d), F32),
        compiler_params=_cparams(("parallel", "parallel")),
        name="linear_residual",
    )(x, gate, y, w)


L = CHUNK
COL_Q, COL_FF, COL_FB, COL_I, COL_G, COL_Z, COL_X, COL_DT = 0, 512, 1024, 1536, 2048, 2560, 3072, 4096


def _rms(x, gain):
    return x * lax.rsqrt(jnp.mean(x * x, axis=-1, keepdims=True) + EPS) * gain


def _split3(x):
    a = x.astype(BF16)
    r = x - a.astype(F32)
    b = r.astype(BF16)
    c = (r - b.astype(F32)).astype(BF16)
    return a, b, c


def _cumsum_exact(tri3, x):
    a, b, c = _split3(x)
    return jnp.dot(tri3, jnp.concatenate([a, b, c], axis=0), preferred_element_type=F32)


def _expand_exact(x, e3):
    a, b, c = _split3(x)
    return jnp.dot(jnp.concatenate([a, b, c], axis=1), e3, preferred_element_type=F32)


def _dot_nt(a, b):
    return lax.dot_general(a, b, (((1,), (1,)), ((), ())), preferred_element_type=F32)


def _dot_tn(a, b):
    return lax.dot_general(a, b, (((0,), (0,)), ((), ())), preferred_element_type=F32)


def _scan_dir(rev, q_ref, f_ref, v_ref, x_ref, xp_ref, xn_ref, dt_ref, pv, nv,
              lb_row, cw_ref, cbias_ref, dtb_ref, aexp_ref, dskip_ref,
              oh_ref, om_ref, sg_ref, sm_ref, u_scr):
    d = 1 if rev else 0
    ti = lax.broadcasted_iota(jnp.int32, (L, L), 0)
    si = lax.broadcasted_iota(jnp.int32, (L, L), 1)
    keep = (si >= ti) if rev else (si <= ti)
    tri = keep.astype(BF16)
    tri3 = jnp.concatenate([tri, tri, tri], axis=1)
    t2 = lax.broadcasted_iota(jnp.int32, (L, 128), 0)
    s2 = lax.broadcasted_iota(jnp.int32, (L, 128), 1) % L
    keep2 = (s2 >= t2) if rev else (s2 <= t2)
    ref_row = (L - 1 - L // 2) if rev else L // 2
    last_row = 0 if rev else L - 1
    lane = lax.broadcasted_iota(jnp.int32, (1, 128), 1)
    lo = lane < 64
    r128 = lax.broadcasted_iota(jnp.int32, (128, 128), 0)
    c128 = lax.broadcasted_iota(jnp.int32, (128, 128), 1)
    blockdiag = (r128 < 64) == (c128 < 64)
    zero_b = jnp.zeros((), BF16)

    def pair_rows(a):
        return jnp.concatenate([jnp.where(lo, a, zero_b), jnp.where(lo, zero_b, a)], axis=0)

    q = q_ref[0]
    qs = _silu(q)
    f = lb_row + (1.0 - lb_row) * jax.nn.sigmoid(f_ref[0])
    k = 1.0 - f
    la = jnp.log(f)
    b = _cumsum_exact(tri3, la)
    bref = b[ref_row:ref_row + 1, :]
    blast = b[last_row:last_row + 1, :]
    qd = (qs * jnp.exp(b - bref)).astype(BF16)
    kd = (k * jnp.exp(bref - b)).astype(BF16)
    kl = (k * jnp.exp(blast - b)).astype(BF16)
    qb = (qs * jnp.exp(b)).astype(BF16)
    dec_col = jnp.broadcast_to(jnp.exp(blast), (8, 512)).T
    vb = v_ref[0].astype(BF16)
    for p in range(HA // 2):
        sl = slice(128 * p, 128 * (p + 1))
        qd_p, kd_p, kl_p, qb_p, v_p = qd[:, sl], kd[:, sl], kl[:, sl], qb[:, sl], vb[:, sl]
        st = sg_ref[d, p]
        att = _dot_nt(qd_p, pair_rows(kd_p))
        att = jnp.where(keep2, att, 0.0).astype(BF16)
        lhs = jnp.concatenate([att, qb_p], axis=1)
        rhs = jnp.concatenate([pair_rows(v_p), st.astype(BF16)], axis=0)
        oh_ref[0, :, sl] = jnp.dot(lhs, rhs, preferred_element_type=F32)
        upd = _dot_tn(kl_p, v_p)
        sg_ref[d, p] = st * dec_col[sl, 0:1] + jnp.where(blockdiag, upd, 0.0)

    u_scr[0:8, :] = xp_ref[0] * pv
    u_scr[8:8 + L, :] = x_ref[0]
    u_scr[8 + L:16 + L, :] = xn_ref[0] * nv
    u = u_scr[...]
    y = cbias_ref[...] + cw_ref[2:3, :] * u[8:8 + L]
    for j in (0, 1, 3, 4):
        y = y + cw_ref[j:j + 1, :] * pltpu.roll(u, (2 - j) % (L + 16), axis=0)[8:8 + L]
    xbc = _silu(y)
    xm = xbc[:, :512]
    bm = xbc[:, 512:768].astype(BF16)
    cm = xbc[:, 768:1024].astype(BF16)
    draw = dt_ref[0] + dtb_ref[...]
    dt = jnp.maximum(draw, 0.0) + jnp.log(1.0 + jnp.exp(-jnp.abs(draw)))
    la_m = -dt * aexp_ref[...]
    cum = _cumsum_exact(tri3, la_m)
    er = lax.broadcasted_iota(jnp.int32, (128, 512), 0)
    ec = lax.broadcasted_iota(jnp.int32, (128, 512), 1)
    e1 = (er == (ec // 64) + 8 * d).astype(BF16)
    e1 = jnp.concatenate([e1, e1, e1], axis=0)
    both = _expand_exact(jnp.concatenate([dt, cum], axis=0), e1)
    dt_e, cum_e = both[:L], both[L:]
    cum_t = cum.T
    xdt = xm * dt_e
    clast = cum_e[last_row:last_row + 1, :]
    wx = (jnp.exp(clast - cum_e) * xdt).astype(BF16)
    dec_m = jnp.exp(clast)
    ecum = jnp.exp(cum_e)
    xdt_b = xdt.astype(BF16)
    for g in range(2):
        bm_g = bm[:, 128 * g:128 * (g + 1)]
        cm_g = cm[:, 128 * g:128 * (g + 1)]
        gl = slice(256 * g, 256 * (g + 1))
        cb2 = _dot_nt(cm_g, jnp.concatenate([bm_g, bm_g], axis=0))
        sm = sm_ref[d, g]
        y_int = jnp.dot(cm_g, sm.astype(BF16), preferred_element_type=F32) * ecum[:, gl]
        for pp in range(2):
            h0 = 4 * g + 2 * pp + 8 * d
            sl = slice(256 * g + 128 * pp, 256 * g + 128 * (pp + 1))
            crow = jnp.concatenate([cum_t[h0:h0 + 1, :], cum_t[h0 + 1:h0 + 2, :]], axis=1)
            lm = jnp.exp(jnp.where(keep2, cum_e[:, sl] - crow, -jnp.inf))
            yp = jnp.dot((cb2 * lm).astype(BF16), pair_rows(xdt_b[:, sl]), preferred_element_type=F32)
            yp = yp + y_int[:, 128 * pp:128 * (pp + 1)]
            if not rev:
                yp = yp + dskip_ref[:, sl] * xm[:, sl]
            om_ref[0, :, sl] = yp
        sm_ref[d, g] = sm * dec_m[:, gl] + _dot_tn(bm_g, wx[:, gl])


def _scan_kernel(qf, ff, vf, xf, xpf, xnf, dtf, qb, fb, vb_, xb, xpb, xnb, dtb,
                 lb_ref, cw_ref, cbias_ref, dtbias_ref, aexp_ref, dskip_ref,
                 ohf, omf, ohb, omb, sg_ref, sm_ref, u_scr, *, nct, nc):
    c = pl.program_id(1)

    @pl.when(c == 0)
    def _():
        sg_ref[...] = jnp.zeros(sg_ref.shape, F32)
        sm_ref[...] = jnp.zeros(sm_ref.shape, F32)

    cb = jnp.where(c < nct, nct - 1 - c, nc + nct - 1 - c)

    def edge_flags(ch):
        pv = jnp.where((ch == 0) | (ch == nct), 0.0, 1.0)
        nv = jnp.where((ch == nct - 1) | (ch == nc - 1), 0.0, 1.0)
        return pv, nv

    pvf, nvf = edge_flags(c)
    pvb, nvb = edge_flags(cb)
    _scan_dir(False, qf, ff, vf, xf, xpf, xnf, dtf, pvf, nvf, lb_ref[0:1, :], cw_ref, cbias_ref,
              dtbias_ref, aexp_ref, dskip_ref, ohf, omf, sg_ref, sm_ref, u_scr)
    _scan_dir(True, qb, fb, vb_, xb, xpb, xnb, dtb, pvb, nvb, lb_ref[1:2, :], cw_ref, cbias_ref,
              dtbias_ref, aexp_ref, dskip_ref, ohb, omb, sg_ref, sm_ref, u_scr)


def _scan_mixers(p, nct, lb, conv_w, conv_b, dt_bias, a_log, d_skip):
    bsz, s, _ = p.shape
    nc = s // L
    nb8 = s // 8

    def bmap(c):
        return jnp.where(c < nct, nct - 1 - c, nc + nct - 1 - c)

    def specs(cmap, fcol):
        colblk = lambda col, w: col // w
        return [
            pl.BlockSpec((1, L, 512), lambda i, c: (i, cmap(c), colblk(COL_Q, 512))),
            pl.BlockSpec((1, L, 512), lambda i, c: (i, cmap(c), colblk(fcol, 512))),
            pl.BlockSpec((1, L, 512), lambda i, c: (i, cmap(c), colblk(COL_I, 512))),
            pl.BlockSpec((1, L, 1024), lambda i, c: (i, cmap(c), colblk(COL_X, 1024))),
            pl.BlockSpec((1, 8, 1024), lambda i, c: (i, jnp.maximum(cmap(c) * 8 - 1, 0), colblk(COL_X, 1024))),
            pl.BlockSpec((1, 8, 1024), lambda i, c: (i, jnp.minimum(cmap(c) * 8 + 8, nb8 - 1), colblk(COL_X, 1024))),
            pl.BlockSpec((1, L, 128), lambda i, c: (i, cmap(c), colblk(COL_DT, 128))),
        ]

    ident = lambda c: c
    const2 = lambda i, c: (0, 0)
    dtb = jnp.zeros((1, 128), F32).at[0, :16].set(dt_bias.reshape(16))
    aexp = jnp.zeros((1, 128), F32).at[0, :16].set(jnp.exp(a_log.reshape(16)))
    dsk = jnp.repeat(d_skip, 64).reshape(1, 512)
    kern = functools.partial(_scan_kernel, nct=nct, nc=nc)
    out_sd = jax.ShapeDtypeStruct((bsz, s, 512), F32)
    return pl.pallas_call(
        kern,
        grid=(bsz, nc),
        in_specs=specs(ident, COL_FF) + specs(bmap, COL_FB) + [
            pl.BlockSpec((2, 512), const2), pl.BlockSpec((5, 1024), const2),
            pl.BlockSpec((1, 1024), const2), pl.BlockSpec((1, 128), const2),
            pl.BlockSpec((1, 128), const2), pl.BlockSpec((1, 512), const2)],
        out_specs=[pl.BlockSpec((1, L, 512), lambda i, c: (i, c, 0)),
                   pl.BlockSpec((1, L, 512), lambda i, c: (i, c, 0)),
                   pl.BlockSpec((1, L, 512), lambda i, c: (i, bmap(c), 0)),
                   pl.BlockSpec((1, L, 512), lambda i, c: (i, bmap(c), 0))],
        out_shape=[out_sd, out_sd, out_sd, out_sd],
        scratch_shapes=[pltpu.VMEM((2, HA // 2, 128, 128), F32),
                        pltpu.VMEM((2, 2, 128, 256), F32),
                        pltpu.VMEM((L + 16, 1024), F32)],
        compiler_params=_cparams(("parallel", "arbitrary")),
        name="scan_mixers",
    )(*([p] * 14), lb, conv_w, conv_b.reshape(1, 1024), dtb, aexp, dsk)


def _merge_kernel(ohf, ohb, omf, omb, g_ref, z_ref, x_ref, gate_ref, hn_ref, mn_ref, w_ref,
                  oc_ref, ol_ref, *, nct_tiles):
    oh = ohf[0] + ohb[0]
    sq = oh * oh
    hi = sq.astype(BF16)
    lo = (sq - hi.astype(F32)).astype(BF16)
    r = lax.broadcasted_iota(jnp.int32, (512, 512), 0) // 64
    c = lax.broadcasted_iota(jnp.int32, (512, 512), 1) // 64
    avg = jnp.where(r == c, 1.0 / 64, 0.0).astype(BF16)
    ms = jnp.dot(hi, avg, preferred_element_type=F32) + jnp.dot(lo, avg, preferred_element_type=F32)
    oh = oh * lax.rsqrt(ms + EPS) * hn_ref[...] * jax.nn.sigmoid(g_ref[0])
    y = (omf[0] + omb[0]) * _silu(z_ref[0])
    y = y * lax.rsqrt(jnp.mean(y * y, axis=-1, keepdims=True) + EPS) * mn_ref[...]
    cat = jnp.concatenate([oh, y], axis=1).astype(BF16)
    out = x_ref[0] + gate_ref[0, 0] * jnp.dot(cat, w_ref[...], preferred_element_type=F32)
    j = pl.program_id(1)

    @pl.when(j < nct_tiles)
    def _():
        oc_ref[0] = out

    @pl.when(j >= nct_tiles)
    def _():
        ol_ref[0] = out


def _merge_mixers(x, gate2, ohf, ohb, omf, omb, p, hgrn_norm, mamba_norm, w_out, nct_tiles, tm):
    b, s, d = x.shape
    sel = lambda i, j: (i, jnp.where(j < nct_tiles, 0, 1), 0, 0)
    row = lambda i, j: (i, j, 0)
    const2 = lambda i, j: (0, 0)
    tc = nct_tiles * tm
    return pl.pallas_call(
        functools.partial(_merge_kernel, nct_tiles=nct_tiles),
        grid=(b, s // tm),
        in_specs=[pl.BlockSpec((1, tm, 512), row)] * 4 + [
            pl.BlockSpec((1, tm, 512), lambda i, j: (i, j, COL_G // 512)),
            pl.BlockSpec((1, tm, 512), lambda i, j: (i, j, COL_Z // 512)),
            pl.BlockSpec((1, tm, d), row),
            pl.BlockSpec((1, 1, 1, d), sel),
            pl.BlockSpec((1, 512), const2), pl.BlockSpec((1, 512), const2),
            pl.BlockSpec((2 * 512, d), const2)],
        out_specs=[pl.BlockSpec((1, tm, d), lambda i, j: (i, jnp.minimum(j, nct_tiles - 1), 0)),
                   pl.BlockSpec((1, tm, d), lambda i, j: (i, jnp.maximum(j - nct_tiles, 0), 0))],
        out_shape=[jax.ShapeDtypeStruct((b, tc, d), F32),
                   jax.ShapeDtypeStruct((b, s - tc, d), F32)],
        compiler_params=_cparams(("parallel", "arbitrary")),
        name="merge_mixers",
    )(ohf, ohb, omf, omb, p, p, x, gate2.reshape(b, 2, 1, d), hgrn_norm.reshape(1, 512),
      mamba_norm.reshape(1, 512), w_out)


def _rope_tables(t):
    rows = t // GRID_W
    row = jnp.repeat(jnp.arange(rows, dtype=jnp.int32), GRID_W)
    col = jnp.tile(jnp.arange(GRID_W, dtype=jnp.int32), rows)
    nf = ROPE // 4
    inv_freq = ROPE_THETA ** (-jnp.arange(nf, dtype=F32) / nf)
    pos = jnp.stack([row, col], axis=-1).astype(F32)
    ang = pos[..., None] * inv_freq
    cos, sin = jnp.cos(ang), jnp.sin(ang)
    cos64 = jnp.broadcast_to(cos[:, :, None, :], (t, 2, 2, nf)).reshape(t, ROPE)
    sin64 = jnp.broadcast_to(sin[:, :, None, :], (t, 2, 2, nf)).reshape(t, ROPE)
    zero = jnp.zeros((t, ROPE), F32)
    return jnp.concatenate([cos64, zero], -1), jnp.concatenate([sin64, zero], -1)


def _swap_cols(w):
    nf = ROPE // 4
    wr = w.reshape(w.shape[:-1] + (2, 2, nf))
    return jnp.stack([-wr[..., 1, :], wr[..., 0, :]], axis=-2).reshape(w.shape)


def _rope_tile(tile, cos_t, sin_t):
    return tile * cos_t + pltpu.roll(tile, ROPE, axis=1) * sin_t


def _mla_down_kernel(x_ref, g_ref, sc_ref, sh_ref, w_ref, qn_ref, kvn_ref, cos_ref, sin_ref,
                     cq_ref, ckv_ref, kr_ref):
    a = _modnorm(x_ref[0], g_ref[...], sc_ref[0], sh_ref[0]).astype(BF16)
    c = jnp.dot(a, w_ref[...], preferred_element_type=F32)
    cq = c[:, :Q_LORA]
    ckv = c[:, Q_LORA:Q_LORA + KV_LORA]
    cq_ref[0] = _rms(cq, qn_ref[...]).astype(BF16)
    ckv_ref[0] = _rms(ckv, kvn_ref[...]).astype(BF16)
    kr_ref[0] = _rope_tile(c[:, Q_LORA + KV_LORA:], cos_ref[...], sin_ref[...]).astype(BF16)


def _mla_down(x, gain, scale, shift, w_cat, q_norm, kv_norm, cos_t, sin_t, tm):
    b, t, d = x.shape
    n = w_cat.shape[1]
    return pl.pallas_call(
        _mla_down_kernel,
        grid=(b, t // tm),
        in_specs=[pl.BlockSpec((1, tm, d), lambda i, j: (i, j, 0)),
                  pl.BlockSpec((1, d), lambda i, j: (0, 0)),
                  pl.BlockSpec((1, 1, d), lambda i, j: (i, 0, 0)),
                  pl.BlockSpec((1, 1, d), lambda i, j: (i, 0, 0)),
                  pl.BlockSpec((d, n), lambda i, j: (0, 0)),
                  pl.BlockSpec((1, Q_LORA), lambda i, j: (0, 0)),
                  pl.BlockSpec((1, KV_LORA), lambda i, j: (0, 0)),
                  pl.BlockSpec((tm, 128), lambda i, j: (j, 0)),
                  pl.BlockSpec((tm, 128), lambda i, j: (j, 0))],
        out_specs=[pl.BlockSpec((1, tm, Q_LORA), lambda i, j: (i, j, 0)),
                   pl.BlockSpec((1, tm, KV_LORA), lambda i, j: (i, j, 0)),
                   pl.BlockSpec((1, tm, 128), lambda i, j: (i, j, 0))],
        out_shape=[jax.ShapeDtypeStruct((b, t, Q_LORA), BF16),
                   jax.ShapeDtypeStruct((b, t, KV_LORA), BF16),
                   jax.ShapeDtypeStruct((b, t, 128), BF16)],
        compiler_params=_cparams(("parallel", "parallel")),
        name="mla_down",
    )(x, gain.reshape(1, d), scale, shift, w_cat, q_norm.reshape(1, -1), kv_norm.reshape(1, -1),
      cos_t, sin_t)


def _q_up_kernel(cq_ref, w_ref, cos_ref, sin_ref, q_ref):
    cq = cq_ref[0]
    for h in range(H_C):
        q = jnp.dot(cq, w_ref[h], preferred_element_type=F32)
        qn = q[:, :NOPE] * QK_SCALE
        qr = _rope_tile(q[:, NOPE:], cos_ref[...], sin_ref[...]) * QK_SCALE
        q_ref[0, h] = jnp.concatenate([qn, qr], axis=1).astype(BF16)


def _q_up(cq, w_uq_h, cos_t, sin_t, tm):
    b, t, r = cq.shape
    return pl.pallas_call(
        _q_up_kernel,
        grid=(b, t // tm),
        in_specs=[pl.BlockSpec((1, tm, r), lambda i, j: (i, j, 0)),
                  pl.BlockSpec((H_C, r, QK_DIM), lambda i, j: (0, 0, 0)),
                  pl.BlockSpec((tm, 128), lambda i, j: (j, 0)),
                  pl.BlockSpec((tm, 128), lambda i, j: (j, 0))],
        out_specs=pl.BlockSpec((1, H_C, tm, QK_DIM), lambda i, j: (i, 0, j, 0)),
        out_shape=jax.ShapeDtypeStruct((b, H_C, t, QK_DIM), BF16),
        compiler_params=_cparams(("parallel", "parallel")),
        name="q_up",
    )(cq, w_uq_h, cos_t, sin_t)


def _kv_up_kernel(ckv_ref, kr_ref, w_ref, k_ref, v_ref):
    ckv = ckv_ref[0]
    kr = kr_ref[0]
    for h in range(H_C):
        kv = jnp.dot(ckv, w_ref[h], preferred_element_type=F32)
        k_ref[0, h] = jnp.concatenate([kv[:, :NOPE].astype(BF16), kr], axis=1)
        v_ref[0, h] = kv[:, NOPE:].astype(BF16)


def _kv_up(ckv, kr, w_ukv_h, tm):
    b, s, r = ckv.shape
    return pl.pallas_call(
        _kv_up_kernel,
        grid=(b, s // tm),
        in_specs=[pl.BlockSpec((1, tm, r), lambda i, j: (i, j, 0)),
                  pl.BlockSpec((1, tm, 128), lambda i, j: (i, j, 0)),
                  pl.BlockSpec((H_C, r, NOPE + VH), lambda i, j: (0, 0, 0))],
        out_specs=[pl.BlockSpec((1, H_C, tm, QK_DIM), lambda i, j: (i, 0, j, 0)),
                   pl.BlockSpec((1, H_C, tm, VH), lambda i, j: (i, 0, j, 0))],
        out_shape=[jax.ShapeDtypeStruct((b, H_C, s, QK_DIM), BF16),
                   jax.ShapeDtypeStruct((b, H_C, s, VH), BF16)],
        compiler_params=_cparams(("parallel", "parallel")),
        name="kv_up",
    )(ckv, kr, w_ukv_h)


def _attn_kernel(qa_ref, qb_ref, q0_ref, k_ref, v_ref, o_ref, s0_scr, s1_scr, m_scr, *, tk, nk, tq):
    n = pl.program_id(2)

    def scores(q, j):
        ks = k_ref[0, 0, j * tk:(j + 1) * tk, :]
        return lax.dot_general(q, ks, (((1,), (1,)), ((), ())), preferred_element_type=F32)

    def fold_max(mrun, s):
        for c in range(tk // 128):
            mrun = jnp.maximum(mrun, s[:, c * 128:(c + 1) * 128])
        return mrun

    @pl.when(n == 0)
    def _():
        q0 = q0_ref[0, 0]
        mrun = jnp.full((tq, 128), -jnp.inf, F32)
        for j in range(nk):
            s = scores(q0, j)
            s0_scr[:, j * tk:(j + 1) * tk] = s
            mrun = fold_max(mrun, s)
        m_scr[...] = jnp.max(mrun, axis=-1, keepdims=True)

    def fused(qn, s_read, s_write, m):
        mrun = jnp.full((tq, 128), -jnp.inf, F32)
        lrun = jnp.zeros((tq, 128), F32)
        acc = jnp.zeros((tq, VH), F32)
        for j in range(nk):
            sn = scores(qn, j)
            s_write[:, j * tk:(j + 1) * tk] = sn
            mrun = fold_max(mrun, sn)
            p = jnp.exp2(s_read[:, j * tk:(j + 1) * tk] - m)
            for c in range(tk // 128):
                lrun = lrun + p[:, c * 128:(c + 1) * 128]
            acc = acc + jnp.dot(p.astype(BF16), v_ref[0, 0, j * tk:(j + 1) * tk, :],
                                preferred_element_type=F32)
        out = acc / jnp.sum(lrun, axis=-1, keepdims=True)
        return out, jnp.max(mrun, axis=-1, keepdims=True)

    out_a, m1 = fused(qa_ref[0, 0], s0_scr, s1_scr, m_scr[...])
    o_ref[0, 0:tq, :] = out_a.astype(o_ref.dtype)
    out_b, m2 = fused(qb_ref[0, 0], s1_scr, s0_scr, m1)
    o_ref[0, tq:2 * tq, :] = out_b.astype(o_ref.dtype)
    m_scr[...] = m2


def _attention(q, k, v, tq, tk):
    b, h, t, _ = q.shape
    s = k.shape[2]
    nq = t // tq
    assert t % (2 * tq) == 0 and s % tk == 0
    kern = functools.partial(_attn_kernel, tk=tk, nk=s // tk, tq=tq)
    return pl.pallas_call(
        kern,
        grid=(b, h, nq // 2),
        in_specs=[pl.BlockSpec((1, 1, tq, QK_DIM), lambda i, j, n: (i, j, 2 * n + 1, 0)),
                  pl.BlockSpec((1, 1, tq, QK_DIM),
                               lambda i, j, n: (i, j, jnp.minimum(2 * n + 2, nq - 1), 0)),
                  pl.BlockSpec((1, 1, tq, QK_DIM), lambda i, j, n: (i, j, 0, 0)),
                  pl.BlockSpec((1, 1, s, QK_DIM), lambda i, j, n: (i, j, 0, 0)),
                  pl.BlockSpec((1, 1, s, VH), lambda i, j, n: (i, j, 0, 0))],
        out_specs=pl.BlockSpec((1, 2 * tq, VH), lambda i, j, n: (i, n, j)),
        out_shape=jax.ShapeDtypeStruct((b, t, h * VH), BF16),
        scratch_shapes=[pltpu.VMEM((tq, s), F32), pltpu.VMEM((tq, s), F32),
                        pltpu.VMEM((tq, 1), F32)],
        compiler_params=_cparams(("parallel", "parallel", "arbitrary")),
        name="attention",
    )(q, q, q, k, v)


def _router_kernel(x_ref, g_ref, sc_ref, sh_ref, w_ref, h_ref, lg_ref):
    hn = _modnorm(x_ref[0], g_ref[...], sc_ref[0], sh_ref[0])
    a = hn.astype(BF16)
    h_ref[0] = a
    b = (hn - a.astype(F32)).astype(BF16)
    lg_ref[0] = jnp.dot(jnp.concatenate([a, b, a], axis=1), w_ref[...], preferred_element_type=F32)


def _router(x, gain, scale, shift, w_router, tm):
    b, t, d = x.shape
    e = w_router.shape[1]
    w_hi = w_router.astype(BF16)
    w_lo = (w_router - w_hi.astype(F32)).astype(BF16)
    w3 = jnp.concatenate([w_hi, w_hi, w_lo], axis=0)
    return pl.pallas_call(
        _router_kernel,
        grid=(b, t // tm),
        in_specs=[pl.BlockSpec((1, tm, d), lambda i, j: (i, j, 0)),
                  pl.BlockSpec((1, d), lambda i, j: (0, 0)),
                  pl.BlockSpec((1, 1, d), lambda i, j: (i, 0, 0)),
                  pl.BlockSpec((1, 1, d), lambda i, j: (i, 0, 0)),
                  pl.BlockSpec((3 * d, e), lambda i, j: (0, 0))],
        out_specs=[pl.BlockSpec((1, tm, d), lambda i, j: (i, j, 0)),
                   pl.BlockSpec((1, tm, e), lambda i, j: (i, j, 0))],
        out_shape=[jax.ShapeDtypeStruct((b, t, d), BF16),
                   jax.ShapeDtypeStruct((b, t, e), F32)],
        compiler_params=_cparams(("parallel", "parallel")),
        name="router",
    )(x, gain.reshape(1, d), scale, shift, w3)


def _ffn_kernel(x_ref, wg_ref, wu_ref, wd_ref, g_ref, o_ref, acc_scr, *, fc):
    x = x_ref[0, 0]
    nf = wg_ref.shape[-1]
    for c in range(nf // fc):
        sl = slice(c * fc, (c + 1) * fc)
        hg = jnp.dot(x, wg_ref[0, :, sl], preferred_element_type=F32)
        hu = jnp.dot(x, wu_ref[0, :, sl], preferred_element_type=F32)
        hid = (_silu(hg) * hu).astype(BF16)
        part = jnp.dot(hid, wd_ref[0, sl, :], preferred_element_type=F32)
        if c == 0:
            acc_scr[...] = part
        else:
            acc_scr[...] += part
    o_ref[0, 0] = (acc_scr[...] * g_ref[0, 0]).astype(o_ref.dtype)


def _expert_ffn(xg, gates, w_gate, w_up, w_down, tm, fc):
    b, e, cap, d = xg.shape
    f = w_gate.shape[-1]
    tm = min(tm, cap)
    kern = functools.partial(_ffn_kernel, fc=fc)
    return pl.pallas_call(
        kern,
        grid=(e, b, cap // tm),
        in_specs=[pl.BlockSpec((1, 1, tm, d), lambda j, i, m: (i, j, m, 0)),
                  pl.BlockSpec((1, d, f), lambda j, i, m: (j, 0, 0)),
                  pl.BlockSpec((1, d, f), lambda j, i, m: (j, 0, 0)),
                  pl.BlockSpec((1, f, d), lambda j, i, m: (j, 0, 0)),
                  pl.BlockSpec((1, 1, tm, 1), lambda j, i, m: (i, j, m, 0))],
        out_specs=pl.BlockSpec((1, 1, tm, d), lambda j, i, m: (i, j, m, 0)),
        out_shape=jax.ShapeDtypeStruct((b, e, cap, d), BF16),
        scratch_shapes=[pltpu.VMEM((tm, d), F32)],
        compiler_params=_cparams(("parallel", "parallel", "arbitrary")),
        name="expert_ffn",
    )(xg, w_gate, w_up, w_down, gates)


SLOT_WIN = 256
TOK_BLK = 128
COMBINE_EXPERTS = 2


def _combine_kernel(lo_ref, idx_ref, y_ref, x_ref, gate_ref, fg_ref, o_ref, *, nblk, nrow,
                    final_norm):
    b, j, eg = pl.program_id(0), pl.program_id(1), pl.program_id(2)

    @pl.when(eg == 0)
    def _():
        o_ref[...] = jnp.zeros(o_ref.shape, F32)

    tok = lax.broadcasted_iota(jnp.int32, (TOK_BLK, 1), 0)
    for blk in range(nblk):
        gblk = j * nblk + blk
        part = None
        for k in range(COMBINE_EXPERTS):
            w = jnp.minimum(lo_ref[b, eg * COMBINE_EXPERTS + k, gblk] // TOK_BLK, nrow - 2)
            ids = jnp.concatenate([idx_ref[0, k, pl.ds(w, 1), :],
                                   idx_ref[0, k, pl.ds(w + 1, 1), :]], axis=1)
            onehot = (ids == tok + gblk * TOK_BLK).astype(BF16)
            yw = y_ref[0, k, pl.ds(pl.multiple_of(w * TOK_BLK, TOK_BLK), SLOT_WIN), :]
            d = jnp.dot(onehot, yw, preferred_element_type=F32)
            part = d if part is None else part + d
        rows = slice(blk * TOK_BLK, (blk + 1) * TOK_BLK)
        o_ref[0, rows, :] += part

    @pl.when(eg == pl.num_programs(2) - 1)
    def _():
        out = x_ref[0] + gate_ref[0] * o_ref[0]
        o_ref[0] = _rms(out, fg_ref[...]) if final_norm else out


def _combine(x, gate_out, idx_s, y, final_gain=None):
    b, t, d = x.shape
    e, cap = idx_s.shape[1:]
    capp = max(cap, SLOT_WIN)
    if capp != cap:
        idx_s = jnp.pad(idx_s, ((0, 0), (0, 0), (0, capp - cap)), constant_values=-1)
        y = jnp.pad(y, ((0, 0), (0, 0), (0, capp - cap), (0, 0)))
    nrow = capp // TOK_BLK
    ts = min(2048, t)
    nblk = ts // TOK_BLK
    bounds = jnp.arange(t // TOK_BLK, dtype=jnp.int32) * TOK_BLK
    valid = idx_s >= 0
    lo = jnp.sum((valid[..., None] & (idx_s[..., None] < bounds)).astype(jnp.int32), axis=2)
    kern = functools.partial(_combine_kernel, nblk=nblk, nrow=nrow,
                             final_norm=final_gain is not None)
    fg = jnp.ones((1, d), F32) if final_gain is None else final_gain.reshape(1, d)
    grid_spec = pltpu.PrefetchScalarGridSpec(
        num_scalar_prefetch=1,
        grid=(b, t // ts, e // COMBINE_EXPERTS),
        in_specs=[pl.BlockSpec((1, COMBINE_EXPERTS, nrow, TOK_BLK), lambda i, j, k, lo_r: (i, k, 0, 0)),
                  pl.BlockSpec((1, COMBINE_EXPERTS, capp, d), lambda i, j, k, lo_r: (i, k, 0, 0)),
                  pl.BlockSpec((1, ts, d), lambda i, j, k, lo_r: (i, j, 0)),
                  pl.BlockSpec((1, 1, d), lambda i, j, k, lo_r: (i, 0, 0)),
                  pl.BlockSpec((1, d), lambda i, j, k, lo_r: (0, 0))],
        out_specs=pl.BlockSpec((1, ts, d), lambda i, j, k, lo_r: (i, j, 0)),
    )
    return pl.pallas_call(
        kern,
        grid_spec=grid_spec,
        out_shape=jax.ShapeDtypeStruct((b, t, d), F32),
        compiler_params=_cparams(("parallel", "parallel", "arbitrary")),
        name="moe_combine",
    )(lo, idx_s.reshape(b, e, nrow, TOK_BLK), y, x, gate_out, fg)


def _moe(x, gain, scale, shift, gate_out, w_router, w_gate, w_up, w_down, final_gain=None):
    b, t, d = x.shape
    cap = CAP_FACTOR * t // N_EXPERTS
    hn, logits = _router(x, gain, scale, shift, w_router, min(256, t))
    aff = jax.nn.softmax(logits, axis=-1)
    g, idx = lax.top_k(jnp.swapaxes(aff, 1, 2), cap)
    idx_s, g_s = lax.sort((idx, g), dimension=2, num_keys=1)
    bidx = jnp.arange(b)[:, None, None]
    xg = hn[bidx, idx_s]
    y = _expert_ffn(xg, g_s[..., None], w_gate, w_up, w_down, 256, 512)
    return _combine(x, gate_out, idx_s, y, final_gain)


def kernel(x, c, ctx, c_ctx, w_mod, b_mod, norm_mix, norm_ffn, norm_out, w_in, w_out_rec,
           conv_w, conv_b, lb_gamma, dt_bias, a_log, d_skip, hgrn_norm, mamba_norm,
           w_dq, q_norm, w_uq, w_dkv, kv_norm, w_ukv, w_kr, w_o,
           w_router, w_gate, w_up, w_down):
    B, T, D = x.shape
    TC = ctx.shape[1]
    lb = jnp.cumsum(jax.nn.softmax(lb_gamma.astype(F32), axis=0), axis=0)

    cvec = jnp.zeros((16, D), F32).at[:B].set(c).at[B].set(c_ctx)
    mods = _modulation(cvec, w_mod, b_mod)

    x_lat, x_ctx = x, ctx
    for l in range(DEPTH):
        need_ctx = l < DEPTH - 1
        m_lat = [mods[l, :B, i * D:(i + 1) * D][:, None, :] for i in range(6)]
        m_ctx = [jnp.broadcast_to(mods[l, B, i * D:(i + 1) * D][None, None, :], (B, 1, D))
                 for i in range(6)]
        if l % 2 == 0:
            e = l // 2
            w_in_p = jnp.pad(w_in[e], ((0, 0), (0, IN_COLS_PAD - IN_COLS))).astype(BF16)
            xs = jnp.concatenate([x_ctx, x_lat], axis=1)
            pair = lambda i: jnp.concatenate([m_ctx[i], m_lat[i]], axis=1)
            p = _modproj(xs, norm_mix[l], pair(1), pair(0), w_in_p, 256, TC // 256)
            ohf, omf, ohb, omb = _scan_mixers(p, TC // CHUNK, lb[e], conv_w[e], conv_b[e],
                                              dt_bias[e], a_log[e], d_skip[e])
            x_ctx, x_lat = _merge_mixers(xs, pair(2), ohf, ohb, omf, omb, p, hgrn_norm[e],
                                         mamba_norm[e], w_out_rec[e].astype(BF16), TC // 256, 256)
        else:
            j = l // 2
            cos_l, sin_l = _rope_tables(T)
            cos_c = jnp.concatenate([jnp.ones((TC, ROPE), F32), jnp.zeros((TC, ROPE), F32)], -1)
            sin_c = jnp.zeros((TC, 128), F32)
            w_cat = jnp.concatenate([w_dq[j], w_dkv[j], w_kr[j], _swap_cols(w_kr[j])],
                                    axis=1).astype(BF16)
            wq = w_uq[j].reshape(Q_LORA, H_C, NOPE + ROPE)
            wq_h = jnp.concatenate([wq, _swap_cols(wq[..., NOPE:])], axis=-1)
            wq_h = jnp.transpose(wq_h, (1, 0, 2)).astype(BF16)
            wkv_h = jnp.transpose(w_ukv[j].reshape(KV_LORA, H_C, NOPE + VH), (1, 0, 2)).astype(BF16)
            cq_l, ckv_l, kr_l = _mla_down(x_lat, norm_mix[l], m_lat[1], m_lat[0], w_cat,
                                          q_norm[j], kv_norm[j], cos_l, sin_l, 256)
            _, ckv_c, kr_c = _mla_down(x_ctx, norm_mix[l], m_ctx[1], m_ctx[0], w_cat,
                                       q_norm[j], kv_norm[j], cos_c, sin_c, 256)
            ckv = jnp.concatenate([ckv_c, ckv_l], axis=1)
            kr = jnp.concatenate([kr_c, kr_l], axis=1)
            qh = _q_up(cq_l, wq_h, cos_l, sin_l, 512)
            kh, vh = _kv_up(ckv, kr, wkv_h, 768)
            o = _attention(qh, kh, vh, 256, 768)
            x_lat = _linear_residual(x_lat, m_lat[2], o, w_o[j].astype(BF16), 512)
        wg_b, wu_b, wd_b = w_gate[l].astype(BF16), w_up[l].astype(BF16), w_down[l].astype(BF16)
        x_lat = _moe(x_lat, norm_ffn[l], m_lat[4], m_lat[3], m_lat[5], w_router[l],
                     wg_b, wu_b, wd_b, None if need_ctx else norm_out)
        if need_ctx:
            x_ctx = _moe(x_ctx, norm_ffn[l], m_ctx[4], m_ctx[3], m_ctx[5], w_router[l],
                         wg_b, wu_b, wd_b)
    return x_lat
```

```python
import functools
import math

import jax
import jax.numpy as jnp
from jax import lax
from jax.experimental import pallas as pl
from jax.experimental.pallas import tpu as pltpu

F32 = jnp.float32
BF16 = jnp.bfloat16

D_MODEL = 1024
DEPTH = 2
GRID_W = 64
EPS = 1e-6
CHUNK = 64
HA, DKA, DVA = 8, 64, 64
WA = HA * DVA
HB, PB = 8, 64
D_INNER = HB * PB
N_GROUPS, D_STATE = 2, 128
CONV_W = 5
REC_SPLITS = (HA * DKA, HA * DKA, HA * DKA, WA, WA, D_INNER, D_INNER,
              N_GROUPS * D_STATE, N_GROUPS * D_STATE, HB, HB)
IN_COLS = sum(REC_SPLITS)
IN_COLS_PAD = 4224
H_C, NOPE, ROPE, VH = 8, 128, 64, 128
Q_LORA, KV_LORA = 384, 256
ROPE_THETA = 10000.0
ATTN_SCALE = 1.0 / math.sqrt(NOPE + ROPE)
QK_SCALE = ATTN_SCALE * math.log2(math.e)
QK_DIM = 256
N_EXPERTS = 16
EXPERT_FF = 1024
CAP_FACTOR = 2

VMEM_LIMIT = 56 * 1024 * 1024


def _cparams(sem):
    return pltpu.CompilerParams(dimension_semantics=sem, vmem_limit_bytes=VMEM_LIMIT)


def _silu(v):
    return v * jax.nn.sigmoid(v)


def _modnorm(x, gain, scale, shift):
    ms = jnp.mean(x * x, axis=-1, keepdims=True)
    return (x * lax.rsqrt(ms + EPS) * gain) * (1.0 + scale) + shift


def _mod_kernel(s_ref, w_ref, b_ref, o_ref):
    s = _silu(s_ref[...])
    o_ref[0] = jnp.dot(s, w_ref[0], preferred_element_type=F32,
                       precision=lax.Precision.HIGHEST) + b_ref[0]


def _modulation(cvec, w_mod, b_mod):
    n = w_mod.shape[-1]
    tn = 1536
    return pl.pallas_call(
        _mod_kernel,
        grid=(DEPTH, n // tn),
        in_specs=[pl.BlockSpec((16, D_MODEL), lambda l, j: (0, 0)),
                  pl.BlockSpec((1, D_MODEL, tn), lambda l, j: (l, 0, j)),
                  pl.BlockSpec((1, 1, tn), lambda l, j: (l, 0, j))],
        out_specs=pl.BlockSpec((1, 16, tn), lambda l, j: (l, 0, j)),
        out_shape=jax.ShapeDtypeStruct((DEPTH, 16, n), F32),
        compiler_params=_cparams(("parallel", "parallel")),
        name="modulation",
    )(cvec, w_mod, b_mod.reshape(DEPTH, 1, n))


def _ctx_lat_specs(tm, d, nct_tiles):
    return [pl.BlockSpec((1, tm, d), lambda i, j: (i, jnp.minimum(j, nct_tiles - 1), 0)),
            pl.BlockSpec((1, tm, d), lambda i, j: (i, jnp.maximum(j - nct_tiles, 0), 0))]


def _modproj_kernel(xc_ref, xl_ref, g_ref, sc_ref, sh_ref, w_ref, o_ref, *, nct_tiles):
    x = jnp.where(pl.program_id(1) < nct_tiles, xc_ref[0], xl_ref[0])
    a = _modnorm(x, g_ref[...], sc_ref[0, 0], sh_ref[0, 0]).astype(BF16)
    o_ref[0] = jnp.dot(a, w_ref[...], preferred_element_type=F32).astype(o_ref.dtype)


def _modproj(x_ctx, x_lat, gain, scale2, shift2, w, tm, out_dtype=F32):
    b, tc, d = x_ctx.shape
    t = tc + x_lat.shape[1]
    nct_tiles = tc // tm
    n = w.shape[1]
    sel = lambda i, j: (i, jnp.where(j < nct_tiles, 0, 1), 0, 0)
    return pl.pallas_call(
        functools.partial(_modproj_kernel, nct_tiles=nct_tiles),
        grid=(b, t // tm),
        in_specs=_ctx_lat_specs(tm, d, nct_tiles) + [
                  pl.BlockSpec((1, d), lambda i, j: (0, 0)),
                  pl.BlockSpec((1, 1, 1, d), sel),
                  pl.BlockSpec((1, 1, 1, d), sel),
                  pl.BlockSpec((d, n), lambda i, j: (0, 0))],
        out_specs=pl.BlockSpec((1, tm, n), lambda i, j: (i, j, 0)),
        out_shape=jax.ShapeDtypeStruct((b, t, n), out_dtype),
        compiler_params=_cparams(("parallel", "parallel")),
        name="modproj",
    )(x_ctx, x_lat, gain.reshape(1, d), scale2.reshape(b, 2, 1, d), shift2.reshape(b, 2, 1, d), w)


def _linres_kernel(x_ref, gate_ref, y_ref, w_ref, o_ref):
    acc = jnp.dot(y_ref[0].astype(BF16), w_ref[...], preferred_element_type=F32)
    o_ref[0] = x_ref[0] + gate_ref[0] * acc


def _linear_residual(x, gate, y, w, tm):
    b, t, d = x.shape
    k = y.shape[-1]
    return pl.pallas_call(
        _linres_kernel,
        grid=(b, t // tm),
        in_specs=[pl.BlockSpec((1, tm, d), lambda i, j: (i, j, 0)),
                  pl.BlockSpec((1, 1, d), lambda i, j: (i, 0, 0)),
                  pl.BlockSpec((1, tm, k), lambda i, j: (i, j, 0)),
                  pl.BlockSpec((k, d), lambda i, j: (0, 0))],
        out_specs=pl.BlockSpec((1, tm, d), lambda i, j: (i, j, 0)),
        out_shape=jax.ShapeDtypeStruct((b, t, d), F32),
        compiler_params=_cparams(("parallel", "parallel")),
        name="linear_residual",
    )(x, gate, y, w)


L = CHUNK
COL_Q, COL_FF, COL_FB, COL_I, COL_G, COL_Z, COL_X, COL_DT = 0, 512, 1024, 1536, 2048, 2560, 3072, 4096


def _rms(x, gain):
    return x * lax.rsqrt(jnp.mean(x * x, axis=-1, keepdims=True) + EPS) * gain


def _split3(x):
    a = x.astype(BF16)
    r = x - a.astype(F32)
    b = r.astype(BF16)
    c = (r - b.astype(F32)).astype(BF16)
    return a, b, c


def _cumsum_exact(tri3, x):
    a, b, c = _split3(x)
    return jnp.dot(tri3, jnp.concatenate([a, b, c], axis=0), preferred_element_type=F32)


def _expand_exact(x, e3):
    a, b, c = _split3(x)
    return jnp.dot(jnp.concatenate([a, b, c], axis=1), e3, preferred_element_type=F32)


def _dot_nt(a, b):
    return lax.dot_general(a, b, (((1,), (1,)), ((), ())), preferred_element_type=F32)


def _dot_tn(a, b):
    return lax.dot_general(a, b, (((0,), (0,)), ((), ())), preferred_element_type=F32)


def _scan_dir(rev, q_ref, f_ref, v_ref, x_ref, xp_ref, xn_ref, dt_ref, pv, nv,
              lb_row, cw_ref, cbias_ref, dtb_ref, aexp_ref, dskip_ref,
              oh_ref, om_ref, sg_ref, sm_ref, u_scr):
    d = 1 if rev else 0
    ti = lax.broadcasted_iota(jnp.int32, (L, L), 0)
    si = lax.broadcasted_iota(jnp.int32, (L, L), 1)
    keep = (si >= ti) if rev else (si <= ti)
    tri = keep.astype(BF16)
    tri3 = jnp.concatenate([tri, tri, tri], axis=1)
    t2 = lax.broadcasted_iota(jnp.int32, (L, 128), 0)
    s2 = lax.broadcasted_iota(jnp.int32, (L, 128), 1) % L
    keep2 = (s2 >= t2) if rev else (s2 <= t2)
    ref_row = (L - 1 - L // 2) if rev else L // 2
    last_row = 0 if rev else L - 1
    lane = lax.broadcasted_iota(jnp.int32, (1, 128), 1)
    lo = lane < 64
    r128 = lax.broadcasted_iota(jnp.int32, (128, 128), 0)
    c128 = lax.broadcasted_iota(jnp.int32, (128, 128), 1)
    blockdiag = (r128 < 64) == (c128 < 64)
    zero_b = jnp.zeros((), BF16)

    def pair_rows(a):
        return jnp.concatenate([jnp.where(lo, a, zero_b), jnp.where(lo, zero_b, a)], axis=0)

    q = q_ref[0]
    qs = _silu(q)
    f = lb_row + (1.0 - lb_row) * jax.nn.sigmoid(f_ref[0])
    k = 1.0 - f
    la = jnp.log(f)
    b = _cumsum_exact(tri3, la)
    bref = b[ref_row:ref_row + 1, :]
    blast = b[last_row:last_row + 1, :]
    qd = (qs * jnp.exp(b - bref)).astype(BF16)
    kd = (k * jnp.exp(bref - b)).astype(BF16)
    kl = (k * jnp.exp(blast - b)).astype(BF16)
    qb = (qs * jnp.exp(b)).astype(BF16)
    dec_col = jnp.broadcast_to(jnp.exp(blast), (8, 512)).T
    vb = v_ref[0].astype(BF16)
    for p in range(HA // 2):
        sl = slice(128 * p, 128 * (p + 1))
        qd_p, kd_p, kl_p, qb_p, v_p = qd[:, sl], kd[:, sl], kl[:, sl], qb[:, sl], vb[:, sl]
        st = sg_ref[d, p]
        att = _dot_nt(qd_p, pair_rows(kd_p))
        att = jnp.where(keep2, att, 0.0).astype(BF16)
        lhs = jnp.concatenate([att, qb_p], axis=1)
        rhs = jnp.concatenate([pair_rows(v_p), st.astype(BF16)], axis=0)
        oh_ref[0, :, sl] = jnp.dot(lhs, rhs, preferred_element_type=F32)
        upd = _dot_tn(kl_p, v_p)
        sg_ref[d, p] = st * dec_col[sl, 0:1] + jnp.where(blockdiag, upd, 0.0)

    u_scr[0:8, :] = xp_ref[0] * pv
    u_scr[8:8 + L, :] = x_ref[0]
    u_scr[8 + L:16 + L, :] = xn_ref[0] * nv
    u = u_scr[...]
    y = cbias_ref[...] + cw_ref[2:3, :] * u[8:8 + L]
    for j in (0, 1, 3, 4):
        y = y + cw_ref[j:j + 1, :] * pltpu.roll(u, (2 - j) % (L + 16), axis=0)[8:8 + L]
    xbc = _silu(y)
    xm = xbc[:, :512]
    bm = xbc[:, 512:768].astype(BF16)
    cm = xbc[:, 768:1024].astype(BF16)
    draw = dt_ref[0] + dtb_ref[...]
    dt = jnp.maximum(draw, 0.0) + jnp.log(1.0 + jnp.exp(-jnp.abs(draw)))
    la_m = -dt * aexp_ref[...]
    cum = _cumsum_exact(tri3, la_m)
    er = lax.broadcasted_iota(jnp.int32, (128, 512), 0)
    ec = lax.broadcasted_iota(jnp.int32, (128, 512), 1)
    e1 = (er == (ec // 64) + 8 * d).astype(BF16)
    e1 = jnp.concatenate([e1, e1, e1], axis=0)
    both = _expand_exact(jnp.concatenate([dt, cum], axis=0), e1)
    dt_e, cum_e = both[:L], both[L:]
    cum_t = cum.T
    xdt = xm * dt_e
    clast = cum_e[last_row:last_row + 1, :]
    wx = (jnp.exp(clast - cum_e) * xdt).astype(BF16)
    dec_m = jnp.exp(clast)
    ecum = jnp.exp(cum_e)
    xdt_b = xdt.astype(BF16)
    for g in range(2):
        bm_g = bm[:, 128 * g:128 * (g + 1)]
        cm_g = cm[:, 128 * g:128 * (g + 1)]
        gl = slice(256 * g, 256 * (g + 1))
        cb2 = _dot_nt(cm_g, jnp.concatenate([bm_g, bm_g], axis=0))
        sm = sm_ref[d, g]
        y_int = jnp.dot(cm_g, sm.astype(BF16), preferred_element_type=F32) * ecum[:, gl]
        for pp in range(2):
            h0 = 4 * g + 2 * pp + 8 * d
            sl = slice(256 * g + 128 * pp, 256 * g + 128 * (pp + 1))
            crow = jnp.concatenate([cum_t[h0:h0 + 1, :], cum_t[h0 + 1:h0 + 2, :]], axis=1)
            lm = jnp.exp(jnp.where(keep2, cum_e[:, sl] - crow, -jnp.inf))
            yp = jnp.dot((cb2 * lm).astype(BF16), pair_rows(xdt_b[:, sl]), preferred_element_type=F32)
            yp = yp + y_int[:, 128 * pp:128 * (pp + 1)]
            if not rev:
                yp = yp + dskip_ref[:, sl] * xm[:, sl]
            om_ref[0, :, sl] = yp
        sm_ref[d, g] = sm * dec_m[:, gl] + _dot_tn(bm_g, wx[:, gl])


def _scan_kernel(qf, ff, vf, xf, xpf, xnf, dtf, qb, fb, vb_, xb, xpb, xnb, dtb,
                 lb_ref, cw_ref, cbias_ref, dtbias_ref, aexp_ref, dskip_ref,
                 ohf, omf, ohb, omb, sg_ref, sm_ref, u_scr, *, nct, nc):
    c = pl.program_id(1)

    @pl.when(c == 0)
    def _():
        sg_ref[...] = jnp.zeros(sg_ref.shape, F32)
        sm_ref[...] = jnp.zeros(sm_ref.shape, F32)

    cb = jnp.where(c < nct, nct - 1 - c, nc + nct - 1 - c)

    def edge_flags(ch):
        pv = jnp.where((ch == 0) | (ch == nct), 0.0, 1.0)
        nv = jnp.where((ch == nct - 1) | (ch == nc - 1), 0.0, 1.0)
        return pv, nv

    pvf, nvf = edge_flags(c)
    pvb, nvb = edge_flags(cb)
    _scan_dir(False, qf, ff, vf, xf, xpf, xnf, dtf, pvf, nvf, lb_ref[0:1, :], cw_ref, cbias_ref,
              dtbias_ref, aexp_ref, dskip_ref, ohf, omf, sg_ref, sm_ref, u_scr)
    _scan_dir(True, qb, fb, vb_, xb, xpb, xnb, dtb, pvb, nvb, lb_ref[1:2, :], cw_ref, cbias_ref,
              dtbias_ref, aexp_ref, dskip_ref, ohb, omb, sg_ref, sm_ref, u_scr)


def _scan_mixers(p, nct, lb, conv_w, conv_b, dt_bias, a_log, d_skip):
    bsz, s, _ = p.shape
    nc = s // L
    nb8 = s // 8

    def bmap(c):
        return jnp.where(c < nct, nct - 1 - c, nc + nct - 1 - c)

    def specs(cmap, fcol):
        colblk = lambda col, w: col // w
        return [
            pl.BlockSpec((1, L, 512), lambda i, c: (i, cmap(c), colblk(COL_Q, 512))),
            pl.BlockSpec((1, L, 512), lambda i, c: (i, cmap(c), colblk(fcol, 512))),
            pl.BlockSpec((1, L, 512), lambda i, c: (i, cmap(c), colblk(COL_I, 512))),
            pl.BlockSpec((1, L, 1024), lambda i, c: (i, cmap(c), colblk(COL_X, 1024))),
            pl.BlockSpec((1, 8, 1024), lambda i, c: (i, jnp.maximum(cmap(c) * 8 - 1, 0), colblk(COL_X, 1024))),
            pl.BlockSpec((1, 8, 1024), lambda i, c: (i, jnp.minimum(cmap(c) * 8 + 8, nb8 - 1), colblk(COL_X, 1024))),
            pl.BlockSpec((1, L, 128), lambda i, c: (i, cmap(c), colblk(COL_DT, 128))),
        ]

    ident = lambda c: c
    const2 = lambda i, c: (0, 0)
    dtb = jnp.zeros((1, 128), F32).at[0, :16].set(dt_bias.reshape(16))
    aexp = jnp.zeros((1, 128), F32).at[0, :16].set(jnp.exp(a_log.reshape(16)))
    dsk = jnp.repeat(d_skip, 64).reshape(1, 512)
    kern = functools.partial(_scan_kernel, nct=nct, nc=nc)
    out_sd = jax.ShapeDtypeStruct((bsz, s, 512), F32)
    return pl.pallas_call(
        kern,
        grid=(bsz, nc),
        in_specs=specs(ident, COL_FF) + specs(bmap, COL_FB) + [
            pl.BlockSpec((2, 512), const2), pl.BlockSpec((5, 1024), const2),
            pl.BlockSpec((1, 1024), const2), pl.BlockSpec((1, 128), const2),
            pl.BlockSpec((1, 128), const2), pl.BlockSpec((1, 512), const2)],
        out_specs=[pl.BlockSpec((1, L, 512), lambda i, c: (i, c, 0)),
                   pl.BlockSpec((1, L, 512), lambda i, c: (i, c, 0)),
                   pl.BlockSpec((1, L, 512), lambda i, c: (i, bmap(c), 0)),
                   pl.BlockSpec((1, L, 512), lambda i, c: (i, bmap(c), 0))],
        out_shape=[out_sd, out_sd, out_sd, out_sd],
        scratch_shapes=[pltpu.VMEM((2, HA // 2, 128, 128), F32),
                        pltpu.VMEM((2, 2, 128, 256), F32),
                        pltpu.VMEM((L + 16, 1024), F32)],
        compiler_params=_cparams(("parallel", "arbitrary")),
        name="scan_mixers",
    )(*([p] * 14), lb, conv_w, conv_b.reshape(1, 1024), dtb, aexp, dsk)


def _merge_kernel(ohf, ohb, omf, omb, g_ref, z_ref, xc_ref, xl_ref, gate_ref, hn_ref, mn_ref, w_ref,
                  oc_ref, ol_ref, *, nct_tiles):
    oh = ohf[0] + ohb[0]
    sq = oh * oh
    hi = sq.astype(BF16)
    lo = (sq - hi.astype(F32)).astype(BF16)
    r = lax.broadcasted_iota(jnp.int32, (512, 512), 0) // 64
    c = lax.broadcasted_iota(jnp.int32, (512, 512), 1) // 64
    avg = jnp.where(r == c, 1.0 / 64, 0.0).astype(BF16)
    ms = jnp.dot(hi, avg, preferred_element_type=F32) + jnp.dot(lo, avg, preferred_element_type=F32)
    oh = oh * lax.rsqrt(ms + EPS) * hn_ref[...] * jax.nn.sigmoid(g_ref[0])
    y = (omf[0] + omb[0]) * _silu(z_ref[0])
    y = y * lax.rsqrt(jnp.mean(y * y, axis=-1, keepdims=True) + EPS) * mn_ref[...]
    cat = jnp.concatenate([oh, y], axis=1).astype(BF16)
    j = pl.program_id(1)
    x = jnp.where(j < nct_tiles, xc_ref[0], xl_ref[0])
    out = x + gate_ref[0, 0] * jnp.dot(cat, w_ref[...], preferred_element_type=F32)

    @pl.when(j < nct_tiles)
    def _():
        oc_ref[0] = out

    @pl.when(j >= nct_tiles)
    def _():
        ol_ref[0] = out


def _merge_mixers(x_ctx, x_lat, gate2, ohf, ohb, omf, omb, p, hgrn_norm, mamba_norm, w_out, tm):
    b, tc, d = x_ctx.shape
    s = tc + x_lat.shape[1]
    nct_tiles = tc // tm
    sel = lambda i, j: (i, jnp.where(j < nct_tiles, 0, 1), 0, 0)
    row = lambda i, j: (i, j, 0)
    const2 = lambda i, j: (0, 0)
    return pl.pallas_call(
        functools.partial(_merge_kernel, nct_tiles=nct_tiles),
        grid=(b, s // tm),
        in_specs=[pl.BlockSpec((1, tm, 512), row)] * 4 + [
            pl.BlockSpec((1, tm, 512), lambda i, j: (i, j, COL_G // 512)),
            pl.BlockSpec((1, tm, 512), lambda i, j: (i, j, COL_Z // 512))] + _ctx_lat_specs(
                tm, d, nct_tiles) + [
            pl.BlockSpec((1, 1, 1, d), sel),
            pl.BlockSpec((1, 512), const2), pl.BlockSpec((1, 512), const2),
            pl.BlockSpec((2 * 512, d), const2)],
        out_specs=[pl.BlockSpec((1, tm, d), lambda i, j: (i, jnp.minimum(j, nct_tiles - 1), 0)),
                   pl.BlockSpec((1, tm, d), lambda i, j: (i, jnp.maximum(j - nct_tiles, 0), 0))],
        out_shape=[jax.ShapeDtypeStruct((b, tc, d), F32),
                   jax.ShapeDtypeStruct((b, s - tc, d), F32)],
        compiler_params=_cparams(("parallel", "arbitrary")),
        name="merge_mixers",
    )(ohf, ohb, omf, omb, p, p, x_ctx, x_lat, gate2.reshape(b, 2, 1, d), hgrn_norm.reshape(1, 512),
      mamba_norm.reshape(1, 512), w_out)


def _rope_tables(t):
    rows = t // GRID_W
    row = jnp.repeat(jnp.arange(rows, dtype=jnp.int32), GRID_W)
    col = jnp.tile(jnp.arange(GRID_W, dtype=jnp.int32), rows)
    nf = ROPE // 4
    inv_freq = ROPE_THETA ** (-jnp.arange(nf, dtype=F32) / nf)
    pos = jnp.stack([row, col], axis=-1).astype(F32)
    ang = pos[..., None] * inv_freq
    cos, sin = jnp.cos(ang), jnp.sin(ang)
    cos64 = jnp.broadcast_to(cos[:, :, None, :], (t, 2, 2, nf)).reshape(t, ROPE)
    sin64 = jnp.broadcast_to(sin[:, :, None, :], (t, 2, 2, nf)).reshape(t, ROPE)
    zero = jnp.zeros((t, ROPE), F32)
    return jnp.concatenate([cos64, zero], -1), jnp.concatenate([sin64, zero], -1)


def _swap_cols(w):
    nf = ROPE // 4
    wr = w.reshape(w.shape[:-1] + (2, 2, nf))
    return jnp.stack([-wr[..., 1, :], wr[..., 0, :]], axis=-2).reshape(w.shape)


def _rope_tile(tile, cos_t, sin_t):
    return tile * cos_t + pltpu.roll(tile, ROPE, axis=1) * sin_t


def _mla_down_kernel(x_ref, g_ref, sc_ref, sh_ref, w_ref, qn_ref, kvn_ref, cos_ref, sin_ref,
                     cq_ref, ckv_ref, kr_ref):
    a = _modnorm(x_ref[0], g_ref[...], sc_ref[0], sh_ref[0]).astype(BF16)
    c = jnp.dot(a, w_ref[...], preferred_element_type=F32)
    cq = c[:, :Q_LORA]
    ckv = c[:, Q_LORA:Q_LORA + KV_LORA]
    cq_ref[0] = _rms(cq, qn_ref[...]).astype(BF16)
    ckv_ref[0] = _rms(ckv, kvn_ref[...]).astype(BF16)
    kr_ref[0] = _rope_tile(c[:, Q_LORA + KV_LORA:], cos_ref[...], sin_ref[...]).astype(BF16)


def _mla_down(x, gain, scale, shift, w_cat, q_norm, kv_norm, cos_t, sin_t, tm):
    b, t, d = x.shape
    n = w_cat.shape[1]
    return pl.pallas_call(
        _mla_down_kernel,
        grid=(b, t // tm),
        in_specs=[pl.BlockSpec((1, tm, d), lambda i, j: (i, j, 0)),
                  pl.BlockSpec((1, d), lambda i, j: (0, 0)),
                  pl.BlockSpec((1, 1, d), lambda i, j: (i, 0, 0)),
                  pl.BlockSpec((1, 1, d), lambda i, j: (i, 0, 0)),
                  pl.BlockSpec((d, n), lambda i, j: (0, 0)),
                  pl.BlockSpec((1, Q_LORA), lambda i, j: (0, 0)),
                  pl.BlockSpec((1, KV_LORA), lambda i, j: (0, 0)),
                  pl.BlockSpec((tm, 128), lambda i, j: (j, 0)),
                  pl.BlockSpec((tm, 128), lambda i, j: (j, 0))],
        out_specs=[pl.BlockSpec((1, tm, Q_LORA), lambda i, j: (i, j, 0)),
                   pl.BlockSpec((1, tm, KV_LORA), lambda i, j: (i, j, 0)),
                   pl.BlockSpec((1, tm, 128), lambda i, j: (i, j, 0))],
        out_shape=[jax.ShapeDtypeStruct((b, t, Q_LORA), BF16),
                   jax.ShapeDtypeStruct((b, t, KV_LORA), BF16),
                   jax.ShapeDtypeStruct((b, t, 128), BF16)],
        compiler_params=_cparams(("parallel", "parallel")),
        name="mla_down",
    )(x, gain.reshape(1, d), scale, shift, w_cat, q_norm.reshape(1, -1), kv_norm.reshape(1, -1),
      cos_t, sin_t)


def _q_up_kernel(cq_ref, w_ref, cos_ref, sin_ref, q_ref):
    cq = cq_ref[0]
    for h in range(H_C):
        q = jnp.dot(cq, w_ref[h], preferred_element_type=F32)
        qn = q[:, :NOPE] * QK_SCALE
        qr = _rope_tile(q[:, NOPE:], cos_ref[...], sin_ref[...]) * QK_SCALE
        q_ref[0, h] = jnp.concatenate([qn, qr], axis=1).astype(BF16)


def _q_up(cq, w_uq_h, cos_t, sin_t, tm):
    b, t, r = cq.shape
    return pl.pallas_call(
        _q_up_kernel,
        grid=(b, t // tm),
        in_specs=[pl.BlockSpec((1, tm, r), lambda i, j: (i, j, 0)),
                  pl.BlockSpec((H_C, r, QK_DIM), lambda i, j: (0, 0, 0)),
                  pl.BlockSpec((tm, 128), lambda i, j: (j, 0)),
                  pl.BlockSpec((tm, 128), lambda i, j: (j, 0))],
        out_specs=pl.BlockSpec((1, H_C, tm, QK_DIM), lambda i, j: (i, 0, j, 0)),
        out_shape=jax.ShapeDtypeStruct((b, H_C, t, QK_DIM), BF16),
        compiler_params=_cparams(("parallel", "parallel")),
        name="q_up",
    )(cq, w_uq_h, cos_t, sin_t)


def _kv_up_kernel(ckv_ref, kr_ref, w_ref, k_ref, v_ref):
    ckv = ckv_ref[0]
    kr = kr_ref[0]
    for h in range(H_C):
        kv = jnp.dot(ckv, w_ref[h], preferred_element_type=F32)
        k_ref[0, h] = jnp.concatenate([kv[:, :NOPE].astype(BF16), kr], axis=1)
        v_ref[0, h] = kv[:, NOPE:].astype(BF16)


def _kv_up(ckv, kr, w_ukv_h, tm):
    b, s, r = ckv.shape
    return pl.pallas_call(
        _kv_up_kernel,
        grid=(b, s // tm),
        in_specs=[pl.BlockSpec((1, tm, r), lambda i, j: (i, j, 0)),
                  pl.BlockSpec((1, tm, 128), lambda i, j: (i, j, 0)),
                  pl.BlockSpec((H_C, r, NOPE + VH), lambda i, j: (0, 0, 0))],
        out_specs=[pl.BlockSpec((1, H_C, tm, QK_DIM), lambda i, j: (i, 0, j, 0)),
                   pl.BlockSpec((1, H_C, tm, VH), lambda i, j: (i, 0, j, 0))],
        out_shape=[jax.ShapeDtypeStruct((b, H_C, s, QK_DIM), BF16),
                   jax.ShapeDtypeStruct((b, H_C, s, VH), BF16)],
        compiler_params=_cparams(("parallel", "parallel")),
        name="kv_up",
    )(ckv, kr, w_ukv_h)


def _attn_kernel(qa_ref, qb_ref, q0_ref, k_ref, v_ref, o_ref, s0_scr, s1_scr, m_scr, *, tk, nk, tq):
    n = pl.program_id(2)

    def scores(q, j):
        ks = k_ref[0, 0, j * tk:(j + 1) * tk, :]
        return lax.dot_general(q, ks, (((1,), (1,)), ((), ())), preferred_element_type=F32)

    def fold_max(mrun, s):
        for c in range(tk // 128):
            mrun = jnp.maximum(mrun, s[:, c * 128:(c + 1) * 128])
        return mrun

    @pl.when(n == 0)
    def _():
        q0 = q0_ref[0, 0]
        mrun = jnp.full((tq, 128), -jnp.inf, F32)
        for j in range(nk):
            s = scores(q0, j)
            s0_scr[:, j * tk:(j + 1) * tk] = s
            mrun = fold_max(mrun, s)
        m_scr[...] = jnp.max(mrun, axis=-1, keepdims=True)

    def fused(qn, s_read, s_write, m):
        mrun = jnp.full((tq, 128), -jnp.inf, F32)
        lrun = jnp.zeros((tq, 128), F32)
        acc = jnp.zeros((tq, VH), F32)
        for j in range(nk):
            sn = scores(qn, j)
            s_write[:, j * tk:(j + 1) * tk] = sn
            mrun = fold_max(mrun, sn)
            p = jnp.exp2(s_read[:, j * tk:(j + 1) * tk] - m)
            for c in range(tk // 128):
                lrun = lrun + p[:, c * 128:(c + 1) * 128]
            acc = acc + jnp.dot(p.astype(BF16), v_ref[0, 0, j * tk:(j + 1) * tk, :],
                                preferred_element_type=F32)
        out = acc / jnp.sum(lrun, axis=-1, keepdims=True)
        return out, jnp.max(mrun, axis=-1, keepdims=True)

    out_a, m1 = fused(qa_ref[0, 0], s0_scr, s1_scr, m_scr[...])
    o_ref[0, 0:tq, :] = out_a.astype(o_ref.dtype)
    out_b, m2 = fused(qb_ref[0, 0], s1_scr, s0_scr, m1)
    o_ref[0, tq:2 * tq, :] = out_b.astype(o_ref.dtype)
    m_scr[...] = m2


def _attention(q, k, v, tq, tk):
    b, h, t, _ = q.shape
    s = k.shape[2]
    nq = t // tq
    assert t % (2 * tq) == 0 and s % tk == 0
    kern = functools.partial(_attn_kernel, tk=tk, nk=s // tk, tq=tq)
    return pl.pallas_call(
        kern,
        grid=(b, h, nq // 2),
        in_specs=[pl.BlockSpec((1, 1, tq, QK_DIM), lambda i, j, n: (i, j, 2 * n + 1, 0)),
                  pl.BlockSpec((1, 1, tq, QK_DIM),
                               lambda i, j, n: (i, j, jnp.minimum(2 * n + 2, nq - 1), 0)),
                  pl.BlockSpec((1, 1, tq, QK_DIM), lambda i, j, n: (i, j, 0, 0)),
                  pl.BlockSpec((1, 1, s, QK_DIM), lambda i, j, n: (i, j, 0, 0)),
                  pl.BlockSpec((1, 1, s, VH), lambda i, j, n: (i, j, 0, 0))],
        out_specs=pl.BlockSpec((1, 2 * tq, VH), lambda i, j, n: (i, n, j)),
        out_shape=jax.ShapeDtypeStruct((b, t, h * VH), BF16),
        scratch_shapes=[pltpu.VMEM((tq, s), F32), pltpu.VMEM((tq, s), F32),
                        pltpu.VMEM((tq, 1), F32)],
        compiler_params=_cparams(("parallel", "parallel", "arbitrary")),
        name="attention",
    )(q, q, q, k, v)


def _router_kernel(x_ref, g_ref, sc_ref, sh_ref, w_ref, h_ref, lg_ref):
    hn = _modnorm(x_ref[0], g_ref[...], sc_ref[0], sh_ref[0])
    a = hn.astype(BF16)
    h_ref[0] = a
    b = (hn - a.astype(F32)).astype(BF16)
    lg_ref[0] = jnp.dot(jnp.concatenate([a, b, a], axis=1), w_ref[...], preferred_element_type=F32)


def _router(x, gain, scale, shift, w_router, tm):
    b, t, d = x.shape
    e = w_router.shape[1]
    w_hi = w_router.astype(BF16)
    w_lo = (w_router - w_hi.astype(F32)).astype(BF16)
    w3 = jnp.concatenate([w_hi, w_hi, w_lo], axis=0)
    return pl.pallas_call(
        _router_kernel,
        grid=(b, t // tm),
        in_specs=[pl.BlockSpec((1, tm, d), lambda i, j: (i, j, 0)),
                  pl.BlockSpec((1, d), lambda i, j: (0, 0)),
                  pl.BlockSpec((1, 1, d), lambda i, j: (i, 0, 0)),
                  pl.BlockSpec((1, 1, d), lambda i, j: (i, 0, 0)),
                  pl.BlockSpec((3 * d, e), lambda i, j: (0, 0))],
        out_specs=[pl.BlockSpec((1, tm, d), lambda i, j: (i, j, 0)),
                   pl.BlockSpec((1, tm, e), lambda i, j: (i, j, 0))],
        out_shape=[jax.ShapeDtypeStruct((b, t, d), BF16),
                   jax.ShapeDtypeStruct((b, t, e), F32)],
        compiler_params=_cparams(("parallel", "parallel")),
        name="router",
    )(x, gain.reshape(1, d), scale, shift, w3)


def _ffn_kernel(x_ref, wg_ref, wu_ref, wd_ref, g_ref, o_ref, acc_scr, wgb, wub, wdb, *, fc):
    @pl.when((pl.program_id(1) == 0) & (pl.program_id(2) == 0))
    def _():
        wgb[...] = wg_ref[0, 0].astype(BF16)
        wub[...] = wu_ref[0, 0].astype(BF16)
        wdb[...] = wd_ref[0, 0].astype(BF16)

    x = x_ref[0, 0]
    nf = wgb.shape[-1]
    for c in range(nf // fc):
        sl = slice(c * fc, (c + 1) * fc)
        hg = jnp.dot(x, wgb[:, sl], preferred_element_type=F32)
        hu = jnp.dot(x, wub[:, sl], preferred_element_type=F32)
        hid = (_silu(hg) * hu).astype(BF16)
        part = jnp.dot(hid, wdb[sl, :], preferred_element_type=F32)
        if c == 0:
            acc_scr[...] = part
        else:
            acc_scr[...] += part
    o_ref[0, 0] = (acc_scr[...] * g_ref[0, 0]).astype(o_ref.dtype)


def _expert_ffn(xg, gates, w_gate, w_up, w_down, layer, tm, fc):
    b, e, cap, d = xg.shape
    f = w_gate.shape[-1]
    tm = min(tm, cap)
    kern = functools.partial(_ffn_kernel, fc=fc)
    return pl.pallas_call(
        kern,
        grid=(e, b, cap // tm),
        in_specs=[pl.BlockSpec((1, 1, tm, d), lambda j, i, m: (i, j, m, 0)),
                  pl.BlockSpec((1, 1, d, f), lambda j, i, m: (layer, j, 0, 0)),
                  pl.BlockSpec((1, 1, d, f), lambda j, i, m: (layer, j, 0, 0)),
                  pl.BlockSpec((1, 1, f, d), lambda j, i, m: (layer, j, 0, 0)),
                  pl.BlockSpec((1, 1, tm, 1), lambda j, i, m: (i, j, m, 0))],
        out_specs=pl.BlockSpec((1, 1, tm, d), lambda j, i, m: (i, j, m, 0)),
        out_shape=jax.ShapeDtypeStruct((b, e, cap, d), BF16),
        scratch_shapes=[pltpu.VMEM((tm, d), F32), pltpu.VMEM((d, f), BF16),
                        pltpu.VMEM((d, f), BF16), pltpu.VMEM((f, d), BF16)],
        compiler_params=_cparams(("arbitrary", "arbitrary", "arbitrary")),
        name="expert_ffn",
    )(xg, w_gate, w_up, w_down, gates)


SLOT_WIN = 256
TOK_BLK = 128
COMBINE_EXPERTS = 2


def _combine_kernel(lo_ref, idx_ref, y_ref, x_ref, gate_ref, fg_ref, o_ref, *, nblk, nrow,
                    final_norm):
    b, j, eg = pl.program_id(0), pl.program_id(1), pl.program_id(2)

    @pl.when(eg == 0)
    def _():
        o_ref[...] = jnp.zeros(o_ref.shape, F32)

    tok = lax.broadcasted_iota(jnp.int32, (TOK_BLK, 1), 0)
    for blk in range(nblk):
        gblk = j * nblk + blk
        part = None
        for k in range(COMBINE_EXPERTS):
            w = jnp.minimum(lo_ref[b, eg * COMBINE_EXPERTS + k, gblk] // TOK_BLK, nrow - 2)
            ids = jnp.concatenate([idx_ref[0, k, pl.ds(w, 1), :],
                                   idx_ref[0, k, pl.ds(w + 1, 1), :]], axis=1)
            onehot = (ids == tok + gblk * TOK_BLK).astype(BF16)
            yw = y_ref[0, k, pl.ds(pl.multiple_of(w * TOK_BLK, TOK_BLK), SLOT_WIN), :]
            d = jnp.dot(onehot, yw, preferred_element_type=F32)
            part = d if part is None else part + d
        rows = slice(blk * TOK_BLK, (blk + 1) * TOK_BLK)
        o_ref[0, rows, :] += part

    @pl.when(eg == pl.num_programs(2) - 1)
    def _():
        out = x_ref[0] + gate_ref[0] * o_ref[0]
        o_ref[0] = _rms(out, fg_ref[...]) if final_norm else out


def _combine(x, gate_out, idx_s, y, final_gain=None):
    b, t, d = x.shape
    e, cap = idx_s.shape[1:]
    capp = max(cap, SLOT_WIN)
    if capp != cap:
        idx_s = jnp.pad(idx_s, ((0, 0), (0, 0), (0, capp - cap)), constant_values=-1)
        y = jnp.pad(y, ((0, 0), (0, 0), (0, capp - cap), (0, 0)))
    nrow = capp // TOK_BLK
    ts = min(2048, t)
    nblk = ts // TOK_BLK
    bounds = jnp.arange(t // TOK_BLK, dtype=jnp.int32) * TOK_BLK
    valid = idx_s >= 0
    lo = jnp.sum((valid[..., None] & (idx_s[..., None] < bounds)).astype(jnp.int32), axis=2)
    kern = functools.partial(_combine_kernel, nblk=nblk, nrow=nrow,
                             final_norm=final_gain is not None)
    fg = jnp.ones((1, d), F32) if final_gain is None else final_gain.reshape(1, d)
    grid_spec = pltpu.PrefetchScalarGridSpec(
        num_scalar_prefetch=1,
        grid=(b, t // ts, e // COMBINE_EXPERTS),
        in_specs=[pl.BlockSpec((1, COMBINE_EXPERTS, nrow, TOK_BLK), lambda i, j, k, lo_r: (i, k, 0, 0)),
                  pl.BlockSpec((1, COMBINE_EXPERTS, capp, d), lambda i, j, k, lo_r: (i, k, 0, 0)),
                  pl.BlockSpec((1, ts, d), lambda i, j, k, lo_r: (i, j, 0)),
                  pl.BlockSpec((1, 1, d), lambda i, j, k, lo_r: (i, 0, 0)),
                  pl.BlockSpec((1, d), lambda i, j, k, lo_r: (0, 0))],
        out_specs=pl.BlockSpec((1, ts, d), lambda i, j, k, lo_r: (i, j, 0)),
    )
    return pl.pallas_call(
        kern,
        grid_spec=grid_spec,
        out_shape=jax.ShapeDtypeStruct((b, t, d), F32),
        compiler_params=_cparams(("parallel", "parallel", "arbitrary")),
        name="moe_combine",
    )(lo, idx_s.reshape(b, e, nrow, TOK_BLK), y, x, gate_out, fg)


def _moe(x, gain, scale, shift, gate_out, w_router, w_gate, w_up, w_down, layer, final_gain=None):
    b, t, d = x.shape
    cap = CAP_FACTOR * t // N_EXPERTS
    hn, logits = _router(x, gain, scale, shift, w_router, min(256, t))
    aff = jax.nn.softmax(logits, axis=-1)
    g, idx = lax.top_k(jnp.swapaxes(aff, 1, 2), cap)
    idx_s, g_s = lax.sort((idx, g), dimension=2, num_keys=1)
    bidx = jnp.arange(b)[:, None, None]
    xg = hn[bidx, idx_s]
    y = _expert_ffn(xg, g_s[..., None], w_gate, w_up, w_down, layer, 256, 512)
    return _combine(x, gate_out, idx_s, y, final_gain)


def kernel(x, c, ctx, c_ctx, w_mod, b_mod, norm_mix, norm_ffn, norm_out, w_in, w_out_rec,
           conv_w, conv_b, lb_gamma, dt_bias, a_log, d_skip, hgrn_norm, mamba_norm,
           w_dq, q_norm, w_uq, w_dkv, kv_norm, w_ukv, w_kr, w_o,
           w_router, w_gate, w_up, w_down):
    B, T, D = x.shape
    TC = ctx.shape[1]
    lb = jnp.cumsum(jax.nn.softmax(lb_gamma.astype(F32), axis=0), axis=0)

    cvec = jnp.zeros((16, D), F32).at[:B].set(c).at[B].set(c_ctx)
    mods = _modulation(cvec, w_mod, b_mod)

    x_lat, x_ctx = x, ctx
    for l in range(DEPTH):
        need_ctx = l < DEPTH - 1
        m_lat = [mods[l, :B, i * D:(i + 1) * D][:, None, :] for i in range(6)]
        m_ctx = [jnp.broadcast_to(mods[l, B, i * D:(i + 1) * D][None, None, :], (B, 1, D))
                 for i in range(6)]
        if l % 2 == 0:
            e = l // 2
            w_in_p = jnp.pad(w_in[e], ((0, 0), (0, IN_COLS_PAD - IN_COLS))).astype(BF16)
            pair = lambda i: jnp.concatenate([m_ctx[i], m_lat[i]], axis=1)
            p = _modproj(x_ctx, x_lat, norm_mix[l], pair(1), pair(0), w_in_p, 256)
            ohf, omf, ohb, omb = _scan_mixers(p, TC // CHUNK, lb[e], conv_w[e], conv_b[e],
                                              dt_bias[e], a_log[e], d_skip[e])
            x_ctx, x_lat = _merge_mixers(x_ctx, x_lat, pair(2), ohf, ohb, omf, omb, p, hgrn_norm[e],
                                         mamba_norm[e], w_out_rec[e].astype(BF16), 256)
        else:
            j = l // 2
            cos_l, sin_l = _rope_tables(T)
            cos_c = jnp.concatenate([jnp.ones((TC, ROPE), F32), jnp.zeros((TC, ROPE), F32)], -1)
            sin_c = jnp.zeros((TC, 128), F32)
            w_cat = jnp.concatenate([w_dq[j], w_dkv[j], w_kr[j], _swap_cols(w_kr[j])],
                                    axis=1).astype(BF16)
            wq = w_uq[j].reshape(Q_LORA, H_C, NOPE + ROPE)
            wq_h = jnp.concatenate([wq, _swap_cols(wq[..., NOPE:])], axis=-1)
            wq_h = jnp.transpose(wq_h, (1, 0, 2)).astype(BF16)
            wkv_h = jnp.transpose(w_ukv[j].reshape(KV_LORA, H_C, NOPE + VH), (1, 0, 2)).astype(BF16)
            cq_l, ckv_l, kr_l = _mla_down(x_lat, norm_mix[l], m_lat[1], m_lat[0], w_cat,
                                          q_norm[j], kv_norm[j], cos_l, sin_l, 256)
            _, ckv_c, kr_c = _mla_down(x_ctx, norm_mix[l], m_ctx[1], m_ctx[0], w_cat,
                                       q_norm[j], kv_norm[j], cos_c, sin_c, 256)
            ckv = jnp.concatenate([ckv_c, ckv_l], axis=1)
            kr = jnp.concatenate([kr_c, kr_l], axis=1)
            qh = _q_up(cq_l, wq_h, cos_l, sin_l, 512)
            kh, vh = _kv_up(ckv, kr, wkv_h, 768)
            o = _attention(qh, kh, vh, 256, 768)
            x_lat = _linear_residual(x_lat, m_lat[2], o, w_o[j].astype(BF16), 512)
        x_lat = _moe(x_lat, norm_ffn[l], m_lat[4], m_lat[3], m_lat[5], w_router[l],
                     w_gate, w_up, w_down, l, None if need_ctx else norm_out)
        if need_ctx:
            x_ctx = _moe(x_ctx, norm_ffn[l], m_ctx[4], m_ctx[3], m_ctx[5], w_router[l],
                         w_gate, w_up, w_down, l)
    return x_lat
```

```python
import functools
import math

import jax
import jax.numpy as jnp
from jax import lax
from jax.experimental import pallas as pl
from jax.experimental.pallas import tpu as pltpu

F32 = jnp.float32
BF16 = jnp.bfloat16

D_MODEL = 1024
DEPTH = 2
GRID_W = 64
EPS = 1e-6
CHUNK = 64
HA, DKA, DVA = 8, 64, 64
WA = HA * DVA
HB, PB = 8, 64
D_INNER = HB * PB
N_GROUPS, D_STATE = 2, 128
CONV_W = 5
REC_SPLITS = (HA * DKA, HA * DKA, HA * DKA, WA, WA, D_INNER, D_INNER,
              N_GROUPS * D_STATE, N_GROUPS * D_STATE, HB, HB)
IN_COLS = sum(REC_SPLITS)
IN_COLS_PAD = 4224
H_C, NOPE, ROPE, VH = 8, 128, 64, 128
Q_LORA, KV_LORA = 384, 256
ROPE_THETA = 10000.0
ATTN_SCALE = 1.0 / math.sqrt(NOPE + ROPE)
QK_SCALE = ATTN_SCALE * math.log2(math.e)
QK_DIM = 256
N_EXPERTS = 16
EXPERT_FF = 1024
CAP_FACTOR = 2

VMEM_LIMIT = 56 * 1024 * 1024


def _cparams(sem):
    return pltpu.CompilerParams(dimension_semantics=sem, vmem_limit_bytes=VMEM_LIMIT)


def _silu(v):
    return v * jax.nn.sigmoid(v)


def _modnorm(x, gain, scale, shift):
    ms = jnp.mean(x * x, axis=-1, keepdims=True)
    return (x * lax.rsqrt(ms + EPS) * gain) * (1.0 + scale) + shift


def _mod_kernel(s_ref, w_ref, b_ref, o_ref):
    s = _silu(s_ref[...])
    o_ref[0] = jnp.dot(s, w_ref[0], preferred_element_type=F32,
                       precision=lax.Precision.HIGHEST) + b_ref[0]


def _modulation(cvec, w_mod, b_mod):
    n = w_mod.shape[-1]
    tn = 1536
    return pl.pallas_call(
        _mod_kernel,
        grid=(DEPTH, n // tn),
        in_specs=[pl.BlockSpec((16, D_MODEL), lambda l, j: (0, 0)),
                  pl.BlockSpec((1, D_MODEL, tn), lambda l, j: (l, 0, j)),
                  pl.BlockSpec((1, 1, tn), lambda l, j: (l, 0, j))],
        out_specs=pl.BlockSpec((1, 16, tn), lambda l, j: (l, 0, j)),
        out_shape=jax.ShapeDtypeStruct((DEPTH, 16, n), F32),
        compiler_params=_cparams(("parallel", "parallel")),
        name="modulation",
    )(cvec, w_mod, b_mod.reshape(DEPTH, 1, n))


def _ctx_lat_specs(tm, d, nct_tiles):
    return [pl.BlockSpec((1, tm, d), lambda i, j: (i, jnp.minimum(j, nct_tiles - 1), 0)),
            pl.BlockSpec((1, tm, d), lambda i, j: (i, jnp.maximum(j - nct_tiles, 0), 0))]


def _modproj_kernel(xc_ref, xl_ref, g_ref, sc_ref, sh_ref, w_ref, o_ref, *, nct_tiles):
    x = jnp.where(pl.program_id(1) < nct_tiles, xc_ref[0], xl_ref[0])
    a = _modnorm(x, g_ref[...], sc_ref[0, 0], sh_ref[0, 0]).astype(BF16)
    o_ref[0] = jnp.dot(a, w_ref[...], preferred_element_type=F32).astype(o_ref.dtype)


def _modproj(x_ctx, x_lat, gain, scale2, shift2, w, tm, out_dtype=F32):
    b, tc, d = x_ctx.shape
    t = tc + x_lat.shape[1]
    nct_tiles = tc // tm
    n = w.shape[1]
    sel = lambda i, j: (i, jnp.where(j < nct_tiles, 0, 1), 0, 0)
    return pl.pallas_call(
        functools.partial(_modproj_kernel, nct_tiles=nct_tiles),
        grid=(b, t // tm),
        in_specs=_ctx_lat_specs(tm, d, nct_tiles) + [
                  pl.BlockSpec((1, d), lambda i, j: (0, 0)),
                  pl.BlockSpec((1, 1, 1, d), sel),
                  pl.BlockSpec((1, 1, 1, d), sel),
                  pl.BlockSpec((d, n), lambda i, j: (0, 0))],
        out_specs=pl.BlockSpec((1, tm, n), lambda i, j: (i, j, 0)),
        out_shape=jax.ShapeDtypeStruct((b, t, n), out_dtype),
        compiler_params=_cparams(("parallel", "parallel")),
        name="modproj",
    )(x_ctx, x_lat, gain.reshape(1, d), scale2.reshape(b, 2, 1, d), shift2.reshape(b, 2, 1, d), w)


def _linres_kernel(x_ref, gate_ref, y_ref, w_ref, o_ref):
    acc = jnp.dot(y_ref[0].astype(BF16), w_ref[...], preferred_element_type=F32)
    o_ref[0] = x_ref[0] + gate_ref[0] * acc


def _linear_residual(x, gate, y, w, tm):
    b, t, d = x.shape
    k = y.shape[-1]
    return pl.pallas_call(
        _linres_kernel,
        grid=(b, t // tm),
        in_specs=[pl.BlockSpec((1, tm, d), lambda i, j: (i, j, 0)),
                  pl.BlockSpec((1, 1, d), lambda i, j: (i, 0, 0)),
                  pl.BlockSpec((1, tm, k), lambda i, j: (i, j, 0)),
                  pl.BlockSpec((k, d), lambda i, j: (0, 0))],
        out_specs=pl.BlockSpec((1, tm, d), lambda i, j: (i, j, 0)),
        out_shape=jax.ShapeDtypeStruct((b, t, d), F32),
        compiler_params=_cparams(("parallel", "parallel")),
        name="linear_residual",
    )(x, gate, y, w)


L = CHUNK
COL_Q, COL_FF, COL_FB, COL_I, COL_G, COL_Z, COL_X, COL_DT = 0, 512, 1024, 1536, 2048, 2560, 3072, 4096


def _rms(x, gain):
    return x * lax.rsqrt(jnp.mean(x * x, axis=-1, keepdims=True) + EPS) * gain


def _split3(x):
    a = x.astype(BF16)
    r = x - a.astype(F32)
    b = r.astype(BF16)
    c = (r - b.astype(F32)).astype(BF16)
    return a, b, c


def _cumsum_exact(tri3, x):
    a, b, c = _split3(x)
    return jnp.dot(tri3, jnp.concatenate([a, b, c], axis=0), preferred_element_type=F32)


def _expand_exact(x, e3):
    a, b, c = _split3(x)
    return jnp.dot(jnp.concatenate([a, b, c], axis=1), e3, preferred_element_type=F32)


def _dot_nt(a, b):
    return lax.dot_general(a, b, (((1,), (1,)), ((), ())), preferred_element_type=F32)


def _dot_tn(a, b):
    return lax.dot_general(a, b, (((0,), (0,)), ((), ())), preferred_element_type=F32)


def _scan_dir(rev, q_ref, f_ref, v_ref, x_ref, xp_ref, xn_ref, dt_ref, pv, nv,
              lb_row, cw_ref, cbias_ref, dtb_ref, aexp_ref, dskip_ref,
              oh_ref, om_ref, sg_ref, sm_ref, u_scr):
    d = 1 if rev else 0
    ti = lax.broadcasted_iota(jnp.int32, (L, L), 0)
    si = lax.broadcasted_iota(jnp.int32, (L, L), 1)
    keep = (si >= ti) if rev else (si <= ti)
    tri = keep.astype(BF16)
    tri3 = jnp.concatenate([tri, tri, tri], axis=1)
    t2 = lax.broadcasted_iota(jnp.int32, (L, 128), 0)
    s2 = lax.broadcasted_iota(jnp.int32, (L, 128), 1) % L
    keep2 = (s2 >= t2) if rev else (s2 <= t2)
    ref_row = (L - 1 - L // 2) if rev else L // 2
    last_row = 0 if rev else L - 1
    lane = lax.broadcasted_iota(jnp.int32, (1, 128), 1)
    lo = lane < 64
    r128 = lax.broadcasted_iota(jnp.int32, (128, 128), 0)
    c128 = lax.broadcasted_iota(jnp.int32, (128, 128), 1)
    blockdiag = (r128 < 64) == (c128 < 64)
    zero_b = jnp.zeros((), BF16)

    def pair_rows(a):
        return jnp.concatenate([jnp.where(lo, a, zero_b), jnp.where(lo, zero_b, a)], axis=0)

    q = q_ref[0]
    qs = _silu(q)
    f = lb_row + (1.0 - lb_row) * jax.nn.sigmoid(f_ref[0])
    k = 1.0 - f
    la = jnp.log(f)
    b = _cumsum_exact(tri3, la)
    bref = b[ref_row:ref_row + 1, :]
    blast = b[last_row:last_row + 1, :]
    qd = (qs * jnp.exp(b - bref)).astype(BF16)
    kd = (k * jnp.exp(bref - b)).astype(BF16)
    kl = (k * jnp.exp(blast - b)).astype(BF16)
    qb = (qs * jnp.exp(b)).astype(BF16)
    dec_col = jnp.broadcast_to(jnp.exp(blast), (8, 512)).T
    vb = v_ref[0].astype(BF16)
    for p in range(HA // 2):
        sl = slice(128 * p, 128 * (p + 1))
        qd_p, kd_p, kl_p, qb_p, v_p = qd[:, sl], kd[:, sl], kl[:, sl], qb[:, sl], vb[:, sl]
        st = sg_ref[d, p]
        att = _dot_nt(qd_p, pair_rows(kd_p))
        att = jnp.where(keep2, att, 0.0).astype(BF16)
        lhs = jnp.concatenate([att, qb_p], axis=1)
        rhs = jnp.concatenate([pair_rows(v_p), st.astype(BF16)], axis=0)
        oh_ref[0, :, sl] = jnp.dot(lhs, rhs, preferred_element_type=F32)
        upd = _dot_tn(kl_p, v_p)
        sg_ref[d, p] = st * dec_col[sl, 0:1] + jnp.where(blockdiag, upd, 0.0)

    u_scr[0:8, :] = xp_ref[0] * pv
    u_scr[8:8 + L, :] = x_ref[0]
    u_scr[8 + L:16 + L, :] = xn_ref[0] * nv
    u = u_scr[...]
    y = cbias_ref[...] + cw_ref[2:3, :] * u[8:8 + L]
    for j in (0, 1, 3, 4):
        y = y + cw_ref[j:j + 1, :] * pltpu.roll(u, (2 - j) % (L + 16), axis=0)[8:8 + L]
    xbc = _silu(y)
    xm = xbc[:, :512]
    bm = xbc[:, 512:768].astype(BF16)
    cm = xbc[:, 768:1024].astype(BF16)
    draw = dt_ref[0] + dtb_ref[...]
    dt = jnp.maximum(draw, 0.0) + jnp.log(1.0 + jnp.exp(-jnp.abs(draw)))
    la_m = -dt * aexp_ref[...]
    cum = _cumsum_exact(tri3, la_m)
    er = lax.broadcasted_iota(jnp.int32, (128, 512), 0)
    ec = lax.broadcasted_iota(jnp.int32, (128, 512), 1)
    e1 = (er == (ec // 64) + 8 * d).astype(BF16)
    e1 = jnp.concatenate([e1, e1, e1], axis=0)
    both = _expand_exact(jnp.concatenate([dt, cum], axis=0), e1)
    dt_e, cum_e = both[:L], both[L:]
    cum_t = cum.T
    xdt = xm * dt_e
    clast = cum_e[last_row:last_row + 1, :]
    wx = (jnp.exp(clast - cum_e) * xdt).astype(BF16)
    dec_m = jnp.exp(clast)
    ecum = jnp.exp(cum_e)
    xdt_b = xdt.astype(BF16)
    for g in range(2):
        bm_g = bm[:, 128 * g:128 * (g + 1)]
        cm_g = cm[:, 128 * g:128 * (g + 1)]
        gl = slice(256 * g, 256 * (g + 1))
        cb2 = _dot_nt(cm_g, jnp.concatenate([bm_g, bm_g], axis=0))
        sm = sm_ref[d, g]
        y_int = jnp.dot(cm_g, sm.astype(BF16), preferred_element_type=F32) * ecum[:, gl]
        for pp in range(2):
            h0 = 4 * g + 2 * pp + 8 * d
            sl = slice(256 * g + 128 * pp, 256 * g + 128 * (pp + 1))
            crow = jnp.concatenate([cum_t[h0:h0 + 1, :], cum_t[h0 + 1:h0 + 2, :]], axis=1)
            lm = jnp.exp(jnp.where(keep2, cum_e[:, sl] - crow, -jnp.inf))
            yp = jnp.dot((cb2 * lm).astype(BF16), pair_rows(xdt_b[:, sl]), preferred_element_type=F32)
            yp = yp + y_int[:, 128 * pp:128 * (pp + 1)]
            if not rev:
                yp = yp + dskip_ref[:, sl] * xm[:, sl]
            om_ref[0, :, sl] = yp
        sm_ref[d, g] = sm * dec_m[:, gl] + _dot_tn(bm_g, wx[:, gl])


def _scan_kernel(qf, ff, vf, xf, xpf, xnf, dtf, qb, fb, vb_, xb, xpb, xnb, dtb,
                 lb_ref, cw_ref, cbias_ref, dtbias_ref, aexp_ref, dskip_ref,
                 ohf, omf, ohb, omb, sg_ref, sm_ref, u_scr, *, nct, nc):
    c = pl.program_id(1)

    @pl.when(c == 0)
    def _():
        sg_ref[...] = jnp.zeros(sg_ref.shape, F32)
        sm_ref[...] = jnp.zeros(sm_ref.shape, F32)

    cb = jnp.where(c < nct, nct - 1 - c, nc + nct - 1 - c)

    def edge_flags(ch):
        pv = jnp.where((ch == 0) | (ch == nct), 0.0, 1.0)
        nv = jnp.where((ch == nct - 1) | (ch == nc - 1), 0.0, 1.0)
        return pv, nv

    pvf, nvf = edge_flags(c)
    pvb, nvb = edge_flags(cb)
    _scan_dir(False, qf, ff, vf, xf, xpf, xnf, dtf, pvf, nvf, lb_ref[0:1, :], cw_ref, cbias_ref,
              dtbias_ref, aexp_ref, dskip_ref, ohf, omf, sg_ref, sm_ref, u_scr)
    _scan_dir(True, qb, fb, vb_, xb, xpb, xnb, dtb, pvb, nvb, lb_ref[1:2, :], cw_ref, cbias_ref,
              dtbias_ref, aexp_ref, dskip_ref, ohb, omb, sg_ref, sm_ref, u_scr)


def _scan_mixers(p, nct, lb, conv_w, conv_b, dt_bias, a_log, d_skip):
    bsz, s, _ = p.shape
    nc = s // L
    nb8 = s // 8

    def bmap(c):
        return jnp.where(c < nct, nct - 1 - c, nc + nct - 1 - c)

    def specs(cmap, fcol):
        colblk = lambda col, w: col // w
        return [
            pl.BlockSpec((1, L, 512), lambda i, c: (i, cmap(c), colblk(COL_Q, 512))),
            pl.BlockSpec((1, L, 512), lambda i, c: (i, cmap(c), colblk(fcol, 512))),
            pl.BlockSpec((1, L, 512), lambda i, c: (i, cmap(c), colblk(COL_I, 512))),
            pl.BlockSpec((1, L, 1024), lambda i, c: (i, cmap(c), colblk(COL_X, 1024))),
            pl.BlockSpec((1, 8, 1024), lambda i, c: (i, jnp.maximum(cmap(c) * 8 - 1, 0), colblk(COL_X, 1024))),
            pl.BlockSpec((1, 8, 1024), lambda i, c: (i, jnp.minimum(cmap(c) * 8 + 8, nb8 - 1), colblk(COL_X, 1024))),
            pl.BlockSpec((1, L, 128), lambda i, c: (i, cmap(c), colblk(COL_DT, 128))),
        ]

    ident = lambda c: c
    const2 = lambda i, c: (0, 0)
    dtb = jnp.zeros((1, 128), F32).at[0, :16].set(dt_bias.reshape(16))
    aexp = jnp.zeros((1, 128), F32).at[0, :16].set(jnp.exp(a_log.reshape(16)))
    dsk = jnp.repeat(d_skip, 64).reshape(1, 512)
    kern = functools.partial(_scan_kernel, nct=nct, nc=nc)
    out_sd = jax.ShapeDtypeStruct((bsz, s, 512), F32)
    return pl.pallas_call(
        kern,
        grid=(bsz, nc),
        in_specs=specs(ident, COL_FF) + specs(bmap, COL_FB) + [
            pl.BlockSpec((2, 512), const2), pl.BlockSpec((5, 1024), const2),
            pl.BlockSpec((1, 1024), const2), pl.BlockSpec((1, 128), const2),
            pl.BlockSpec((1, 128), const2), pl.BlockSpec((1, 512), const2)],
        out_specs=[pl.BlockSpec((1, L, 512), lambda i, c: (i, c, 0)),
                   pl.BlockSpec((1, L, 512), lambda i, c: (i, c, 0)),
                   pl.BlockSpec((1, L, 512), lambda i, c: (i, bmap(c), 0)),
                   pl.BlockSpec((1, L, 512), lambda i, c: (i, bmap(c), 0))],
        out_shape=[out_sd, out_sd, out_sd, out_sd],
        scratch_shapes=[pltpu.VMEM((2, HA // 2, 128, 128), F32),
                        pltpu.VMEM((2, 2, 128, 256), F32),
                        pltpu.VMEM((L + 16, 1024), F32)],
        compiler_params=_cparams(("parallel", "arbitrary")),
        name="scan_mixers",
    )(*([p] * 14), lb, conv_w, conv_b.reshape(1, 1024), dtb, aexp, dsk)


def _merge_kernel(ohf, ohb, omf, omb, g_ref, z_ref, xc_ref, xl_ref, gate_ref, hn_ref, mn_ref, w_ref,
                  oc_ref, ol_ref, *, nct_tiles):
    oh = ohf[0] + ohb[0]
    sq = oh * oh
    hi = sq.astype(BF16)
    lo = (sq - hi.astype(F32)).astype(BF16)
    r = lax.broadcasted_iota(jnp.int32, (512, 512), 0) // 64
    c = lax.broadcasted_iota(jnp.int32, (512, 512), 1) // 64
    avg = jnp.where(r == c, 1.0 / 64, 0.0).astype(BF16)
    ms = jnp.dot(hi, avg, preferred_element_type=F32) + jnp.dot(lo, avg, preferred_element_type=F32)
    oh = oh * lax.rsqrt(ms + EPS) * hn_ref[...] * jax.nn.sigmoid(g_ref[0])
    y = (omf[0] + omb[0]) * _silu(z_ref[0])
    y = y * lax.rsqrt(jnp.mean(y * y, axis=-1, keepdims=True) + EPS) * mn_ref[...]
    cat = jnp.concatenate([oh, y], axis=1).astype(BF16)
    j = pl.program_id(1)
    x = jnp.where(j < nct_tiles, xc_ref[0], xl_ref[0])
    out = x + gate_ref[0, 0] * jnp.dot(cat, w_ref[...], preferred_element_type=F32)

    @pl.when(j < nct_tiles)
    def _():
        oc_ref[0] = out

    @pl.when(j >= nct_tiles)
    def _():
        ol_ref[0] = out


def _merge_mixers(x_ctx, x_lat, gate2, ohf, ohb, omf, omb, p, hgrn_norm, mamba_norm, w_out, tm):
    b, tc, d = x_ctx.shape
    s = tc + x_lat.shape[1]
    nct_tiles = tc // tm
    sel = lambda i, j: (i, jnp.where(j < nct_tiles, 0, 1), 0, 0)
    row = lambda i, j: (i, j, 0)
    const2 = lambda i, j: (0, 0)
    return pl.pallas_call(
        functools.partial(_merge_kernel, nct_tiles=nct_tiles),
        grid=(b, s // tm),
        in_specs=[pl.BlockSpec((1, tm, 512), row)] * 4 + [
            pl.BlockSpec((1, tm, 512), lambda i, j: (i, j, COL_G // 512)),
            pl.BlockSpec((1, tm, 512), lambda i, j: (i, j, COL_Z // 512))] + _ctx_lat_specs(
                tm, d, nct_tiles) + [
            pl.BlockSpec((1, 1, 1, d), sel),
            pl.BlockSpec((1, 512), const2), pl.BlockSpec((1, 512), const2),
            pl.BlockSpec((2 * 512, d), const2)],
        out_specs=[pl.BlockSpec((1, tm, d), lambda i, j: (i, jnp.minimum(j, nct_tiles - 1), 0)),
                   pl.BlockSpec((1, tm, d), lambda i, j: (i, jnp.maximum(j - nct_tiles, 0), 0))],
        out_shape=[jax.ShapeDtypeStruct((b, tc, d), F32),
                   jax.ShapeDtypeStruct((b, s - tc, d), F32)],
        compiler_params=_cparams(("parallel", "arbitrary")),
        name="merge_mixers",
    )(ohf, ohb, omf, omb, p, p, x_ctx, x_lat, gate2.reshape(b, 2, 1, d), hgrn_norm.reshape(1, 512),
      mamba_norm.reshape(1, 512), w_out)


def _rope_tables(t):
    rows = t // GRID_W
    row = jnp.repeat(jnp.arange(rows, dtype=jnp.int32), GRID_W)
    col = jnp.tile(jnp.arange(GRID_W, dtype=jnp.int32), rows)
    nf = ROPE // 4
    inv_freq = ROPE_THETA ** (-jnp.arange(nf, dtype=F32) / nf)
    pos = jnp.stack([row, col], axis=-1).astype(F32)
    ang = pos[..., None] * inv_freq
    cos, sin = jnp.cos(ang), jnp.sin(ang)
    cos64 = jnp.broadcast_to(cos[:, :, None, :], (t, 2, 2, nf)).reshape(t, ROPE)
    sin64 = jnp.broadcast_to(sin[:, :, None, :], (t, 2, 2, nf)).reshape(t, ROPE)
    zero = jnp.zeros((t, ROPE), F32)
    return jnp.concatenate([cos64, zero], -1), jnp.concatenate([sin64, zero], -1)


def _swap_cols(w):
    nf = ROPE // 4
    wr = w.reshape(w.shape[:-1] + (2, 2, nf))
    return jnp.stack([-wr[..., 1, :], wr[..., 0, :]], axis=-2).reshape(w.shape)


def _rope_tile(tile, cos_t, sin_t):
    return tile * cos_t + pltpu.roll(tile, ROPE, axis=1) * sin_t


def _mla_down_kernel(x_ref, g_ref, sc_ref, sh_ref, w_ref, qn_ref, kvn_ref, cos_ref, sin_ref,
                     cq_ref, ckv_ref, kr_ref):
    a = _modnorm(x_ref[0], g_ref[...], sc_ref[0], sh_ref[0]).astype(BF16)
    c = jnp.dot(a, w_ref[...], preferred_element_type=F32)
    cq = c[:, :Q_LORA]
    ckv = c[:, Q_LORA:Q_LORA + KV_LORA]
    cq_ref[0] = _rms(cq, qn_ref[...]).astype(BF16)
    ckv_ref[0] = _rms(ckv, kvn_ref[...]).astype(BF16)
    kr_ref[0] = _rope_tile(c[:, Q_LORA + KV_LORA:], cos_ref[...], sin_ref[...]).astype(BF16)


def _mla_down(x, gain, scale, shift, w_cat, q_norm, kv_norm, cos_t, sin_t, tm):
    b, t, d = x.shape
    n = w_cat.shape[1]
    return pl.pallas_call(
        _mla_down_kernel,
        grid=(b, t // tm),
        in_specs=[pl.BlockSpec((1, tm, d), lambda i, j: (i, j, 0)),
                  pl.BlockSpec((1, d), lambda i, j: (0, 0)),
                  pl.BlockSpec((1, 1, d), lambda i, j: (i, 0, 0)),
                  pl.BlockSpec((1, 1, d), lambda i, j: (i, 0, 0)),
                  pl.BlockSpec((d, n), lambda i, j: (0, 0)),
                  pl.BlockSpec((1, Q_LORA), lambda i, j: (0, 0)),
                  pl.BlockSpec((1, KV_LORA), lambda i, j: (0, 0)),
                  pl.BlockSpec((tm, 128), lambda i, j: (j, 0)),
                  pl.BlockSpec((tm, 128), lambda i, j: (j, 0))],
        out_specs=[pl.BlockSpec((1, tm, Q_LORA), lambda i, j: (i, j, 0)),
                   pl.BlockSpec((1, tm, KV_LORA), lambda i, j: (i, j, 0)),
                   pl.BlockSpec((1, tm, 128), lambda i, j: (i, j, 0))],
        out_shape=[jax.ShapeDtypeStruct((b, t, Q_LORA), BF16),
                   jax.ShapeDtypeStruct((b, t, KV_LORA), BF16),
                   jax.ShapeDtypeStruct((b, t, 128), BF16)],
        compiler_params=_cparams(("parallel", "parallel")),
        name="mla_down",
    )(x, gain.reshape(1, d), scale, shift, w_cat, q_norm.reshape(1, -1), kv_norm.reshape(1, -1),
      cos_t, sin_t)


def _q_up_kernel(cq_ref, w_ref, cos_ref, sin_ref, q_ref):
    cq = cq_ref[0]
    for h in range(H_C):
        q = jnp.dot(cq, w_ref[h], preferred_element_type=F32)
        qn = q[:, :NOPE] * QK_SCALE
        qr = _rope_tile(q[:, NOPE:], cos_ref[...], sin_ref[...]) * QK_SCALE
        q_ref[0, h] = jnp.concatenate([qn, qr], axis=1).astype(BF16)


def _q_up(cq, w_uq_h, cos_t, sin_t, tm):
    b, t, r = cq.shape
    return pl.pallas_call(
        _q_up_kernel,
        grid=(b, t // tm),
        in_specs=[pl.BlockSpec((1, tm, r), lambda i, j: (i, j, 0)),
                  pl.BlockSpec((H_C, r, QK_DIM), lambda i, j: (0, 0, 0)),
                  pl.BlockSpec((tm, 128), lambda i, j: (j, 0)),
                  pl.BlockSpec((tm, 128), lambda i, j: (j, 0))],
        out_specs=pl.BlockSpec((1, H_C, tm, QK_DIM), lambda i, j: (i, 0, j, 0)),
        out_shape=jax.ShapeDtypeStruct((b, H_C, t, QK_DIM), BF16),
        compiler_params=_cparams(("parallel", "parallel")),
        name="q_up",
    )(cq, w_uq_h, cos_t, sin_t)


def _kv_up_kernel(ckv_ref, kr_ref, w_ref, k_ref, v_ref):
    ckv = ckv_ref[0]
    kr = kr_ref[0]
    for h in range(H_C):
        kv = jnp.dot(ckv, w_ref[h], preferred_element_type=F32)
        k_ref[0, h] = jnp.concatenate([kv[:, :NOPE].astype(BF16), kr], axis=1)
        v_ref[0, h] = kv[:, NOPE:].astype(BF16)


def _kv_up(ckv, kr, w_ukv_h, tm):
    b, s, r = ckv.shape
    return pl.pallas_call(
        _kv_up_kernel,
        grid=(b, s // tm),
        in_specs=[pl.BlockSpec((1, tm, r), lambda i, j: (i, j, 0)),
                  pl.BlockSpec((1, tm, 128), lambda i, j: (i, j, 0)),
                  pl.BlockSpec((H_C, r, NOPE + VH), lambda i, j: (0, 0, 0))],
        out_specs=[pl.BlockSpec((1, H_C, tm, QK_DIM), lambda i, j: (i, 0, j, 0)),
                   pl.BlockSpec((1, H_C, tm, VH), lambda i, j: (i, 0, j, 0))],
        out_shape=[jax.ShapeDtypeStruct((b, H_C, s, QK_DIM), BF16),
                   jax.ShapeDtypeStruct((b, H_C, s, VH), BF16)],
        compiler_params=_cparams(("parallel", "parallel")),
        name="kv_up",
    )(ckv, kr, w_ukv_h)


def _attn_kernel(qa_ref, qb_ref, q0_ref, k_ref, v_ref, o_ref, s0_scr, s1_scr, m_scr, *, tk, nk, tq):
    n = pl.program_id(2)

    def scores(q, j):
        ks = k_ref[0, 0, j * tk:(j + 1) * tk, :]
        return lax.dot_general(q, ks, (((1,), (1,)), ((), ())), preferred_element_type=F32)

    def fold_max(mrun, s):
        for c in range(tk // 128):
            mrun = jnp.maximum(mrun, s[:, c * 128:(c + 1) * 128])
        return mrun

    @pl.when(n == 0)
    def _():
        q0 = q0_ref[0, 0]
        mrun = jnp.full((tq, 128), -jnp.inf, F32)
        for j in range(nk):
            s = scores(q0, j)
            s0_scr[:, j * tk:(j + 1) * tk] = s
            mrun = fold_max(mrun, s)
        m_scr[...] = jnp.max(mrun, axis=-1, keepdims=True)

    def fused(qn, s_read, s_write, m):
        mrun = jnp.full((tq, 128), -jnp.inf, F32)
        lrun = jnp.zeros((tq, 128), F32)
        acc = jnp.zeros((tq, VH), F32)
        for j in range(nk):
            sn = scores(qn, j)
            s_write[:, j * tk:(j + 1) * tk] = sn
            mrun = fold_max(mrun, sn)
            p = jnp.exp2(s_read[:, j * tk:(j + 1) * tk] - m)
            for c in range(tk // 128):
                lrun = lrun + p[:, c * 128:(c + 1) * 128]
            acc = acc + jnp.dot(p.astype(BF16), v_ref[0, 0, j * tk:(j + 1) * tk, :],
                                preferred_element_type=F32)
        out = acc / jnp.sum(lrun, axis=-1, keepdims=True)
        return out, jnp.max(mrun, axis=-1, keepdims=True)

    out_a, m1 = fused(qa_ref[0, 0], s0_scr, s1_scr, m_scr[...])
    o_ref[0, 0:tq, :] = out_a.astype(o_ref.dtype)
    out_b, m2 = fused(qb_ref[0, 0], s1_scr, s0_scr, m1)
    o_ref[0, tq:2 * tq, :] = out_b.astype(o_ref.dtype)
    m_scr[...] = m2


def _attention(q, k, v, tq, tk):
    b, h, t, _ = q.shape
    s = k.shape[2]
    nq = t // tq
    assert t % (2 * tq) == 0 and s % tk == 0
    kern = functools.partial(_attn_kernel, tk=tk, nk=s // tk, tq=tq)
    return pl.pallas_call(
        kern,
        grid=(b, h, nq // 2),
        in_specs=[pl.BlockSpec((1, 1, tq, QK_DIM), lambda i, j, n: (i, j, 2 * n + 1, 0)),
                  pl.BlockSpec((1, 1, tq, QK_DIM),
                               lambda i, j, n: (i, j, jnp.minimum(2 * n + 2, nq - 1), 0)),
                  pl.BlockSpec((1, 1, tq, QK_DIM), lambda i, j, n: (i, j, 0, 0)),
                  pl.BlockSpec((1, 1, s, QK_DIM), lambda i, j, n: (i, j, 0, 0)),
                  pl.BlockSpec((1, 1, s, VH), lambda i, j, n: (i, j, 0, 0))],
        out_specs=pl.BlockSpec((1, 2 * tq, VH), lambda i, j, n: (i, n, j)),
        out_shape=jax.ShapeDtypeStruct((b, t, h * VH), BF16),
        scratch_shapes=[pltpu.VMEM((tq, s), F32), pltpu.VMEM((tq, s), F32),
                        pltpu.VMEM((tq, 1), F32)],
        compiler_params=_cparams(("parallel", "parallel", "arbitrary")),
        name="attention",
    )(q, q, q, k, v)


def _router_body(x, gain, scale, shift, w_ref, h_ref, lg_ref):
    hn = _modnorm(x, gain, scale, shift)
    a = hn.astype(BF16)
    h_ref[0] = a
    b = (hn - a.astype(F32)).astype(BF16)
    lg_ref[0] = jnp.dot(jnp.concatenate([a, b, a], axis=1), w_ref[...], preferred_element_type=F32)


def _router_kernel(x_ref, g_ref, sc_ref, sh_ref, w_ref, h_ref, lg_ref):
    _router_body(x_ref[0], g_ref[...], sc_ref[0], sh_ref[0], w_ref, h_ref, lg_ref)


def _router2_kernel(xc_ref, xl_ref, g_ref, sc_ref, sh_ref, w_ref, h_ref, lg_ref, *, nct_tiles):
    x = jnp.where(pl.program_id(1) < nct_tiles, xc_ref[0], xl_ref[0])
    _router_body(x, g_ref[...], sc_ref[0, 0], sh_ref[0, 0], w_ref, h_ref, lg_ref)


def _router_weights(w_router):
    w_hi = w_router.astype(BF16)
    w_lo = (w_router - w_hi.astype(F32)).astype(BF16)
    return jnp.concatenate([w_hi, w_hi, w_lo], axis=0)


def _router2(x_ctx, x_lat, gain, scale2, shift2, w_router, tm):
    b, tc, d = x_ctx.shape
    t = tc + x_lat.shape[1]
    nct_tiles = tc // tm
    e = w_router.shape[1]
    sel = lambda i, j: (i, jnp.where(j < nct_tiles, 0, 1), 0, 0)
    return pl.pallas_call(
        functools.partial(_router2_kernel, nct_tiles=nct_tiles),
        grid=(b, t // tm),
        in_specs=_ctx_lat_specs(tm, d, nct_tiles) + [
                  pl.BlockSpec((1, d), lambda i, j: (0, 0)),
                  pl.BlockSpec((1, 1, 1, d), sel),
                  pl.BlockSpec((1, 1, 1, d), sel),
                  pl.BlockSpec((3 * d, e), lambda i, j: (0, 0))],
        out_specs=[pl.BlockSpec((1, tm, d), lambda i, j: (i, j, 0)),
                   pl.BlockSpec((1, tm, e), lambda i, j: (i, j, 0))],
        out_shape=[jax.ShapeDtypeStruct((b, t, d), BF16),
                   jax.ShapeDtypeStruct((b, t, e), F32)],
        compiler_params=_cparams(("parallel", "parallel")),
        name="router",
    )(x_ctx, x_lat, gain.reshape(1, d), scale2.reshape(b, 2, 1, d), shift2.reshape(b, 2, 1, d),
      _router_weights(w_router))


def _router(x, gain, scale, shift, w_router, tm):
    b, t, d = x.shape
    e = w_router.shape[1]
    w3 = _router_weights(w_router)
    return pl.pallas_call(
        _router_kernel,
        grid=(b, t // tm),
        in_specs=[pl.BlockSpec((1, tm, d), lambda i, j: (i, j, 0)),
                  pl.BlockSpec((1, d), lambda i, j: (0, 0)),
                  pl.BlockSpec((1, 1, d), lambda i, j: (i, 0, 0)),
                  pl.BlockSpec((1, 1, d), lambda i, j: (i, 0, 0)),
                  pl.BlockSpec((3 * d, e), lambda i, j: (0, 0))],
        out_specs=[pl.BlockSpec((1, tm, d), lambda i, j: (i, j, 0)),
                   pl.BlockSpec((1, tm, e), lambda i, j: (i, j, 0))],
        out_shape=[jax.ShapeDtypeStruct((b, t, d), BF16),
                   jax.ShapeDtypeStruct((b, t, e), F32)],
        compiler_params=_cparams(("parallel", "parallel")),
        name="router",
    )(x, gain.reshape(1, d), scale, shift, w3)


def _ffn_kernel(x_ref, wg_ref, wu_ref, wd_ref, g_ref, o_ref, acc_scr, wgb, wub, wdb, *, fc):
    @pl.when((pl.program_id(1) == 0) & (pl.program_id(2) == 0))
    def _():
        wgb[...] = wg_ref[0, 0].astype(BF16)
        wub[...] = wu_ref[0, 0].astype(BF16)
        wdb[...] = wd_ref[0, 0].astype(BF16)

    x = x_ref[0, 0]
    nf = wgb.shape[-1]
    for c in range(nf // fc):
        sl = slice(c * fc, (c + 1) * fc)
        hg = jnp.dot(x, wgb[:, sl], preferred_element_type=F32)
        hu = jnp.dot(x, wub[:, sl], preferred_element_type=F32)
        hid = (_silu(hg) * hu).astype(BF16)
        part = jnp.dot(hid, wdb[sl, :], preferred_element_type=F32)
        if c == 0:
            acc_scr[...] = part
        else:
            acc_scr[...] += part
    o_ref[0, 0] = (acc_scr[...] * g_ref[0, 0]).astype(o_ref.dtype)


def _expert_ffn(xg, gates, w_gate, w_up, w_down, layer, tm, fc):
    b, e, cap, d = xg.shape
    f = w_gate.shape[-1]
    tm = min(tm, cap)
    kern = functools.partial(_ffn_kernel, fc=fc)
    return pl.pallas_call(
        kern,
        grid=(e, b, cap // tm),
        in_specs=[pl.BlockSpec((1, 1, tm, d), lambda j, i, m: (i, j, m, 0)),
                  pl.BlockSpec((1, 1, d, f), lambda j, i, m: (layer, j, 0, 0)),
                  pl.BlockSpec((1, 1, d, f), lambda j, i, m: (layer, j, 0, 0)),
                  pl.BlockSpec((1, 1, f, d), lambda j, i, m: (layer, j, 0, 0)),
                  pl.BlockSpec((1, 1, tm, 1), lambda j, i, m: (i, j, m, 0))],
        out_specs=pl.BlockSpec((1, 1, tm, d), lambda j, i, m: (i, j, m, 0)),
        out_shape=jax.ShapeDtypeStruct((b, e, cap, d), BF16),
        scratch_shapes=[pltpu.VMEM((tm, d), F32), pltpu.VMEM((d, f), BF16),
                        pltpu.VMEM((d, f), BF16), pltpu.VMEM((f, d), BF16)],
        compiler_params=_cparams(("arbitrary", "arbitrary", "arbitrary")),
        name="expert_ffn",
    )(xg, w_gate, w_up, w_down, gates)


SLOT_WIN = 256
TOK_BLK = 128
COMBINE_EXPERTS = 2


def _combine_kernel(lo_ref, idx_ref, y_ref, x_ref, gate_ref, fg_ref, o_ref, *, nblk, nrow,
                    final_norm):
    b, j, eg = pl.program_id(0), pl.program_id(1), pl.program_id(2)

    @pl.when(eg == 0)
    def _():
        o_ref[...] = jnp.zeros(o_ref.shape, F32)

    tok = lax.broadcasted_iota(jnp.int32, (TOK_BLK, 1), 0)
    for blk in range(nblk):
        gblk = j * nblk + blk
        part = None
        for k in range(COMBINE_EXPERTS):
            w = jnp.minimum(lo_ref[b, eg * COMBINE_EXPERTS + k, gblk] // TOK_BLK, nrow - 2)
            ids = jnp.concatenate([idx_ref[0, k, pl.ds(w, 1), :],
                                   idx_ref[0, k, pl.ds(w + 1, 1), :]], axis=1)
            onehot = (ids == tok + gblk * TOK_BLK).astype(BF16)
            yw = y_ref[0, k, pl.ds(pl.multiple_of(w * TOK_BLK, TOK_BLK), SLOT_WIN), :]
            d = jnp.dot(onehot, yw, preferred_element_type=F32)
            part = d if part is None else part + d
        rows = slice(blk * TOK_BLK, (blk + 1) * TOK_BLK)
        o_ref[0, rows, :] += part

    @pl.when(eg == pl.num_programs(2) - 1)
    def _():
        out = x_ref[0] + gate_ref[0] * o_ref[0]
        o_ref[0] = _rms(out, fg_ref[...]) if final_norm else out


def _combine(x, gate_out, idx_s, y, final_gain=None):
    b, t, d = x.shape
    e, cap = idx_s.shape[1:]
    capp = max(cap, SLOT_WIN)
    if capp != cap:
        idx_s = jnp.pad(idx_s, ((0, 0), (0, 0), (0, capp - cap)), constant_values=-1)
        y = jnp.pad(y[:, :, :cap], ((0, 0), (0, 0), (0, capp - cap), (0, 0)))
    nrow = capp // TOK_BLK
    ts = min(2048, t)
    nblk = ts // TOK_BLK
    bounds = jnp.arange(t // TOK_BLK, dtype=jnp.int32) * TOK_BLK
    valid = idx_s >= 0
    lo = jnp.sum((valid[..., None] & (idx_s[..., None] < bounds)).astype(jnp.int32), axis=2)
    kern = functools.partial(_combine_kernel, nblk=nblk, nrow=nrow,
                             final_norm=final_gain is not None)
    fg = jnp.ones((1, d), F32) if final_gain is None else final_gain.reshape(1, d)
    grid_spec = pltpu.PrefetchScalarGridSpec(
        num_scalar_prefetch=1,
        grid=(b, t // ts, e // COMBINE_EXPERTS),
        in_specs=[pl.BlockSpec((1, COMBINE_EXPERTS, nrow, TOK_BLK), lambda i, j, k, lo_r: (i, k, 0, 0)),
                  pl.BlockSpec((1, COMBINE_EXPERTS, capp, d), lambda i, j, k, lo_r: (i, k, 0, 0)),
                  pl.BlockSpec((1, ts, d), lambda i, j, k, lo_r: (i, j, 0)),
                  pl.BlockSpec((1, 1, d), lambda i, j, k, lo_r: (i, 0, 0)),
                  pl.BlockSpec((1, d), lambda i, j, k, lo_r: (0, 0))],
        out_specs=pl.BlockSpec((1, ts, d), lambda i, j, k, lo_r: (i, j, 0)),
    )
    return pl.pallas_call(
        kern,
        grid_spec=grid_spec,
        out_shape=jax.ShapeDtypeStruct((b, t, d), F32),
        compiler_params=_cparams(("parallel", "parallel", "arbitrary")),
        name="moe_combine",
    )(lo, idx_s.reshape(b, e, nrow, TOK_BLK), y, x, gate_out, fg)


def _route(logits):
    t = logits.shape[1]
    cap = CAP_FACTOR * t // N_EXPERTS
    aff = jax.nn.softmax(logits, axis=-1)
    g, idx = lax.top_k(jnp.swapaxes(aff, 1, 2), cap)
    return lax.sort((idx, g), dimension=2, num_keys=1)


def _moe(x, gain, scale, shift, gate_out, w_router, w_gate, w_up, w_down, layer, final_gain=None):
    b, t, d = x.shape
    hn, logits = _router(x, gain, scale, shift, w_router, min(256, t))
    idx_s, g_s = _route(logits)
    bidx = jnp.arange(b)[:, None, None]
    xg = hn[bidx, idx_s]
    y = _expert_ffn(xg, g_s[..., None], w_gate, w_up, w_down, layer, 256, 512)
    return _combine(x, gate_out, idx_s, y, final_gain)


def _moe_two_streams(x_ctx, x_lat, gain, scale2, shift2, gate_ctx, gate_lat, w_router, w_gate, w_up,
                     w_down, layer):
    b, tc, d = x_ctx.shape
    hn, logits = _router2(x_ctx, x_lat, gain, scale2, shift2, w_router, 256)
    idx_c, g_c = _route(logits[:, :tc])
    idx_l, g_l = _route(logits[:, tc:])
    idx_cat = jnp.concatenate([idx_l + tc, idx_c], axis=2)
    g_cat = jnp.concatenate([g_l, g_c], axis=2)
    bidx = jnp.arange(b)[:, None, None]
    xg = hn[bidx, idx_cat]
    cap = idx_cat.shape[2]
    tm = next(m for m in (256, 352, 176, 96, 32, 16) if cap % m == 0)
    y = _expert_ffn(xg, g_cat[..., None], w_gate, w_up, w_down, layer, tm, 512)
    new_lat = _combine(x_lat, gate_lat, idx_l, y)
    new_ctx = _combine(x_ctx, gate_ctx, idx_c, y[:, :, idx_l.shape[2]:])
    return new_ctx, new_lat


def kernel(x, c, ctx, c_ctx, w_mod, b_mod, norm_mix, norm_ffn, norm_out, w_in, w_out_rec,
           conv_w, conv_b, lb_gamma, dt_bias, a_log, d_skip, hgrn_norm, mamba_norm,
           w_dq, q_norm, w_uq, w_dkv, kv_norm, w_ukv, w_kr, w_o,
           w_router, w_gate, w_up, w_down):
    B, T, D = x.shape
    TC = ctx.shape[1]
    lb = jnp.cumsum(jax.nn.softmax(lb_gamma.astype(F32), axis=0), axis=0)

    cvec = jnp.zeros((16, D), F32).at[:B].set(c).at[B].set(c_ctx)
    mods = _modulation(cvec, w_mod, b_mod)

    x_lat, x_ctx = x, ctx
    for l in range(DEPTH):
        need_ctx = l < DEPTH - 1
        m_lat = [mods[l, :B, i * D:(i + 1) * D][:, None, :] for i in range(6)]
        m_ctx = [jnp.broadcast_to(mods[l, B, i * D:(i + 1) * D][None, None, :], (B, 1, D))
                 for i in range(6)]
        if l % 2 == 0:
            e = l // 2
            w_in_p = jnp.pad(w_in[e], ((0, 0), (0, IN_COLS_PAD - IN_COLS))).astype(BF16)
            pair = lambda i: jnp.concatenate([m_ctx[i], m_lat[i]], axis=1)
            p = _modproj(x_ctx, x_lat, norm_mix[l], pair(1), pair(0), w_in_p, 256)
            ohf, omf, ohb, omb = _scan_mixers(p, TC // CHUNK, lb[e], conv_w[e], conv_b[e],
                                              dt_bias[e], a_log[e], d_skip[e])
            x_ctx, x_lat = _merge_mixers(x_ctx, x_lat, pair(2), ohf, ohb, omf, omb, p, hgrn_norm[e],
                                         mamba_norm[e], w_out_rec[e].astype(BF16), 256)
        else:
            j = l // 2
            cos_l, sin_l = _rope_tables(T)
            cos_c = jnp.concatenate([jnp.ones((TC, ROPE), F32), jnp.zeros((TC, ROPE), F32)], -1)
            sin_c = jnp.zeros((TC, 128), F32)
            w_cat = jnp.concatenate([w_dq[j], w_dkv[j], w_kr[j], _swap_cols(w_kr[j])],
                                    axis=1).astype(BF16)
            wq = w_uq[j].reshape(Q_LORA, H_C, NOPE + ROPE)
            wq_h = jnp.concatenate([wq, _swap_cols(wq[..., NOPE:])], axis=-1)
            wq_h = jnp.transpose(wq_h, (1, 0, 2)).astype(BF16)
            wkv_h = jnp.transpose(w_ukv[j].reshape(KV_LORA, H_C, NOPE + VH), (1, 0, 2)).astype(BF16)
            cq_l, ckv_l, kr_l = _mla_down(x_lat, norm_mix[l], m_lat[1], m_lat[0], w_cat,
                                          q_norm[j], kv_norm[j], cos_l, sin_l, 256)
            _, ckv_c, kr_c = _mla_down(x_ctx, norm_mix[l], m_ctx[1], m_ctx[0], w_cat,
                                       q_norm[j], kv_norm[j], cos_c, sin_c, 256)
            ckv = jnp.concatenate([ckv_c, ckv_l], axis=1)
            kr = jnp.concatenate([kr_c, kr_l], axis=1)
            qh = _q_up(cq_l, wq_h, cos_l, sin_l, 512)
            kh, vh = _kv_up(ckv, kr, wkv_h, 768)
            o = _attention(qh, kh, vh, 256, 768)
            x_lat = _linear_residual(x_lat, m_lat[2], o, w_o[j].astype(BF16), 512)
        if need_ctx:
            pair = lambda i: jnp.concatenate([m_ctx[i], m_lat[i]], axis=1)
            x_ctx, x_lat = _moe_two_streams(x_ctx, x_lat, norm_ffn[l], pair(4), pair(3), m_ctx[5],
                                            m_lat[5], w_router[l], w_gate, w_up, w_down, l)
        else:
            x_lat = _moe(x_lat, norm_ffn[l], m_lat[4], m_lat[3], m_lat[5], w_router[l],
                         w_gate, w_up, w_down, l, norm_out)
    return x_lat
```

```python
import functools
import math

import jax
import jax.numpy as jnp
from jax import lax
from jax.experimental import pallas as pl
from jax.experimental.pallas import tpu as pltpu

F32 = jnp.float32
BF16 = jnp.bfloat16

D_MODEL = 1024
DEPTH = 2
GRID_W = 64
EPS = 1e-6
CHUNK = 64
HA, DKA, DVA = 8, 64, 64
WA = HA * DVA
HB, PB = 8, 64
D_INNER = HB * PB
N_GROUPS, D_STATE = 2, 128
CONV_W = 5
REC_SPLITS = (HA * DKA, HA * DKA, HA * DKA, WA, WA, D_INNER, D_INNER,
              N_GROUPS * D_STATE, N_GROUPS * D_STATE, HB, HB)
IN_COLS = sum(REC_SPLITS)
IN_COLS_PAD = 4224
H_C, NOPE, ROPE, VH = 8, 128, 64, 128
Q_LORA, KV_LORA = 384, 256
ROPE_THETA = 10000.0
ATTN_SCALE = 1.0 / math.sqrt(NOPE + ROPE)
QK_SCALE = ATTN_SCALE * math.log2(math.e)
QK_DIM = 256
N_EXPERTS = 16
EXPERT_FF = 1024
CAP_FACTOR = 2

VMEM_LIMIT = 56 * 1024 * 1024


def _cparams(sem):
    return pltpu.CompilerParams(dimension_semantics=sem, vmem_limit_bytes=VMEM_LIMIT)


def _silu(v):
    return v * jax.nn.sigmoid(v)


def _modnorm(x, gain, scale, shift):
    ms = jnp.mean(x * x, axis=-1, keepdims=True)
    return (x * lax.rsqrt(ms + EPS) * gain) * (1.0 + scale) + shift


def _mod_kernel(s_ref, w_ref, b_ref, o_ref):
    s = _silu(s_ref[...])
    o_ref[0] = jnp.dot(s, w_ref[0], preferred_element_type=F32,
                       precision=lax.Precision.HIGHEST) + b_ref[0]


def _modulation(cvec, w_mod, b_mod):
    n = w_mod.shape[-1]
    tn = 1536
    return pl.pallas_call(
        _mod_kernel,
        grid=(DEPTH, n // tn),
        in_specs=[pl.BlockSpec((16, D_MODEL), lambda l, j: (0, 0)),
                  pl.BlockSpec((1, D_MODEL, tn), lambda l, j: (l, 0, j)),
                  pl.BlockSpec((1, 1, tn), lambda l, j: (l, 0, j))],
        out_specs=pl.BlockSpec((1, 16, tn), lambda l, j: (l, 0, j)),
        out_shape=jax.ShapeDtypeStruct((DEPTH, 16, n), F32),
        compiler_params=_cparams(("parallel", "parallel")),
        name="modulation",
    )(cvec, w_mod, b_mod.reshape(DEPTH, 1, n))


def _ctx_lat_specs(tm, d, nct_tiles):
    return [pl.BlockSpec((1, tm, d), lambda i, j: (i, jnp.minimum(j, nct_tiles - 1), 0)),
            pl.BlockSpec((1, tm, d), lambda i, j: (i, jnp.maximum(j - nct_tiles, 0), 0))]


def _modproj_kernel(xc_ref, xl_ref, g_ref, sc_ref, sh_ref, w_ref, o_ref, *, nct_tiles):
    x = jnp.where(pl.program_id(1) < nct_tiles, xc_ref[0], xl_ref[0])
    a = _modnorm(x, g_ref[...], sc_ref[0, 0], sh_ref[0, 0]).astype(BF16)
    o_ref[0] = jnp.dot(a, w_ref[...], preferred_element_type=F32).astype(o_ref.dtype)


def _modproj(x_ctx, x_lat, gain, scale2, shift2, w, tm, out_dtype=F32):
    b, tc, d = x_ctx.shape
    t = tc + x_lat.shape[1]
    nct_tiles = tc // tm
    n = w.shape[1]
    sel = lambda i, j: (i, jnp.where(j < nct_tiles, 0, 1), 0, 0)
    return pl.pallas_call(
        functools.partial(_modproj_kernel, nct_tiles=nct_tiles),
        grid=(b, t // tm),
        in_specs=_ctx_lat_specs(tm, d, nct_tiles) + [
                  pl.BlockSpec((1, d), lambda i, j: (0, 0)),
                  pl.BlockSpec((1, 1, 1, d), sel),
                  pl.BlockSpec((1, 1, 1, d), sel),
                  pl.BlockSpec((d, n), lambda i, j: (0, 0))],
        out_specs=pl.BlockSpec((1, tm, n), lambda i, j: (i, j, 0)),
        out_shape=jax.ShapeDtypeStruct((b, t, n), out_dtype),
        compiler_params=_cparams(("parallel", "parallel")),
        name="modproj",
    )(x_ctx, x_lat, gain.reshape(1, d), scale2.reshape(b, 2, 1, d), shift2.reshape(b, 2, 1, d), w)


def _router_body(x, gain, scale, shift, w_ref, h_ref, lg_ref):
    hn = _modnorm(x, gain, scale, shift)
    a = hn.astype(BF16)
    h_ref[0] = a
    b = (hn - a.astype(F32)).astype(BF16)
    lg_ref[0] = jnp.dot(jnp.concatenate([a, b, a], axis=1), w_ref[...], preferred_element_type=F32)


def _router_weights(w_router):
    w_hi = w_router.astype(BF16)
    w_lo = (w_router - w_hi.astype(F32)).astype(BF16)
    return jnp.concatenate([w_hi, w_hi, w_lo], axis=0)


def _linres_kernel(x_ref, gate_ref, y_ref, w_ref, fg_ref, fsc_ref, fsh_ref, rw_ref,
                   o_ref, h_ref, lg_ref):
    acc = jnp.dot(y_ref[0].astype(BF16), w_ref[...], preferred_element_type=F32)
    out = x_ref[0] + gate_ref[0] * acc
    o_ref[0] = out
    _router_body(out, fg_ref[...], fsc_ref[0], fsh_ref[0], rw_ref, h_ref, lg_ref)


def _linear_residual(x, gate, y, w, ffn_gain, ffn_scale, ffn_shift, w_router, tm):
    b, t, d = x.shape
    k = y.shape[-1]
    e = w_router.shape[1]
    row = lambda i, j: (i, j, 0)
    vec = lambda i, j: (i, 0, 0)
    const2 = lambda i, j: (0, 0)
    return pl.pallas_call(
        _linres_kernel,
        grid=(b, t // tm),
        in_specs=[pl.BlockSpec((1, tm, d), row),
                  pl.BlockSpec((1, 1, d), vec),
                  pl.BlockSpec((1, tm, k), row),
                  pl.BlockSpec((k, d), const2),
                  pl.BlockSpec((1, d), const2),
                  pl.BlockSpec((1, 1, d), vec),
                  pl.BlockSpec((1, 1, d), vec),
                  pl.BlockSpec((3 * d, e), const2)],
        out_specs=[pl.BlockSpec((1, tm, d), row), pl.BlockSpec((1, tm, d), row),
                   pl.BlockSpec((1, tm, e), row)],
        out_shape=[jax.ShapeDtypeStruct((b, t, d), F32), jax.ShapeDtypeStruct((b, t, d), BF16),
                   jax.ShapeDtypeStruct((b, t, e), F32)],
        compiler_params=_cparams(("parallel", "parallel")),
        name="linear_residual",
    )(x, gate, y, w, ffn_gain.reshape(1, d), ffn_scale, ffn_shift, _router_weights(w_router))


L = CHUNK
COL_Q, COL_FF, COL_FB, COL_I, COL_G, COL_Z, COL_X, COL_DT = 0, 512, 1024, 1536, 2048, 2560, 3072, 4096


def _rms(x, gain):
    return x * lax.rsqrt(jnp.mean(x * x, axis=-1, keepdims=True) + EPS) * gain


def _split3(x):
    a = x.astype(BF16)
    r = x - a.astype(F32)
    b = r.astype(BF16)
    c = (r - b.astype(F32)).astype(BF16)
    return a, b, c


def _cumsum_exact(tri3, x):
    a, b, c = _split3(x)
    return jnp.dot(tri3, jnp.concatenate([a, b, c], axis=0), preferred_element_type=F32)


def _expand_exact(x, e3):
    a, b, c = _split3(x)
    return jnp.dot(jnp.concatenate([a, b, c], axis=1), e3, preferred_element_type=F32)


def _dot_nt(a, b):
    return lax.dot_general(a, b, (((1,), (1,)), ((), ())), preferred_element_type=F32)


def _dot_tn(a, b):
    return lax.dot_general(a, b, (((0,), (0,)), ((), ())), preferred_element_type=F32)


def _scan_dir(rev, q_ref, f_ref, v_ref, x_ref, xp_ref, xn_ref, dt_ref, pv, nv,
              lb_row, cw_ref, cbias_ref, dtb_ref, aexp_ref, dskip_ref,
              oh_ref, om_ref, sg_ref, sm_ref, u_scr):
    d = 1 if rev else 0
    ti = lax.broadcasted_iota(jnp.int32, (L, L), 0)
    si = lax.broadcasted_iota(jnp.int32, (L, L), 1)
    keep = (si >= ti) if rev else (si <= ti)
    tri = keep.astype(BF16)
    tri3 = jnp.concatenate([tri, tri, tri], axis=1)
    t2 = lax.broadcasted_iota(jnp.int32, (L, 128), 0)
    s2 = lax.broadcasted_iota(jnp.int32, (L, 128), 1) % L
    keep2 = (s2 >= t2) if rev else (s2 <= t2)
    ref_row = (L - 1 - L // 2) if rev else L // 2
    last_row = 0 if rev else L - 1
    lane = lax.broadcasted_iota(jnp.int32, (1, 128), 1)
    lo = lane < 64
    r128 = lax.broadcasted_iota(jnp.int32, (128, 128), 0)
    c128 = lax.broadcasted_iota(jnp.int32, (128, 128), 1)
    blockdiag = (r128 < 64) == (c128 < 64)
    zero_b = jnp.zeros((), BF16)

    def pair_rows(a):
        return jnp.concatenate([jnp.where(lo, a, zero_b), jnp.where(lo, zero_b, a)], axis=0)

    q = q_ref[0]
    qs = _silu(q)
    f = lb_row + (1.0 - lb_row) * jax.nn.sigmoid(f_ref[0])
    k = 1.0 - f
    la = jnp.log(f)
    b = _cumsum_exact(tri3, la)
    bref = b[ref_row:ref_row + 1, :]
    blast = b[last_row:last_row + 1, :]
    qd = (qs * jnp.exp(b - bref)).astype(BF16)
    kd = (k * jnp.exp(bref - b)).astype(BF16)
    kl = (k * jnp.exp(blast - b)).astype(BF16)
    qb = (qs * jnp.exp(b)).astype(BF16)
    dec_col = jnp.broadcast_to(jnp.exp(blast), (8, 512)).T
    vb = v_ref[0].astype(BF16)
    for p in range(HA // 2):
        sl = slice(128 * p, 128 * (p + 1))
        qd_p, kd_p, kl_p, qb_p, v_p = qd[:, sl], kd[:, sl], kl[:, sl], qb[:, sl], vb[:, sl]
        st = sg_ref[d, p]
        att = _dot_nt(qd_p, pair_rows(kd_p))
        att = jnp.where(keep2, att, 0.0).astype(BF16)
        lhs = jnp.concatenate([att, qb_p], axis=1)
        rhs = jnp.concatenate([pair_rows(v_p), st.astype(BF16)], axis=0)
        oh_ref[0, :, sl] = jnp.dot(lhs, rhs, preferred_element_type=F32).astype(oh_ref.dtype)
        upd = _dot_tn(kl_p, v_p)
        sg_ref[d, p] = st * dec_col[sl, 0:1] + jnp.where(blockdiag, upd, 0.0)

    u_scr[0:8, :] = xp_ref[0] * pv
    u_scr[8:8 + L, :] = x_ref[0]
    u_scr[8 + L:16 + L, :] = xn_ref[0] * nv
    u = u_scr[...]
    y = cbias_ref[...] + cw_ref[2:3, :] * u[8:8 + L]
    for j in (0, 1, 3, 4):
        y = y + cw_ref[j:j + 1, :] * pltpu.roll(u, (2 - j) % (L + 16), axis=0)[8:8 + L]
    xbc = _silu(y)
    xm = xbc[:, :512]
    bm = xbc[:, 512:768].astype(BF16)
    cm = xbc[:, 768:1024].astype(BF16)
    draw = dt_ref[0] + dtb_ref[...]
    dt = jnp.maximum(draw, 0.0) + jnp.log(1.0 + jnp.exp(-jnp.abs(draw)))
    la_m = -dt * aexp_ref[...]
    cum = _cumsum_exact(tri3, la_m)
    er = lax.broadcasted_iota(jnp.int32, (128, 512), 0)
    ec = lax.broadcasted_iota(jnp.int32, (128, 512), 1)
    e1 = (er == (ec // 64) + 8 * d).astype(BF16)
    e1 = jnp.concatenate([e1, e1, e1], axis=0)
    both = _expand_exact(jnp.concatenate([dt, cum], axis=0), e1)
    dt_e, cum_e = both[:L], both[L:]
    cum_t = cum.T
    xdt = xm * dt_e
    clast = cum_e[last_row:last_row + 1, :]
    wx = (jnp.exp(clast - cum_e) * xdt).astype(BF16)
    dec_m = jnp.exp(clast)
    ecum = jnp.exp(cum_e)
    xdt_b = xdt.astype(BF16)
    for g in range(2):
        bm_g = bm[:, 128 * g:128 * (g + 1)]
        cm_g = cm[:, 128 * g:128 * (g + 1)]
        gl = slice(256 * g, 256 * (g + 1))
        cb2 = _dot_nt(cm_g, jnp.concatenate([bm_g, bm_g], axis=0))
        sm = sm_ref[d, g]
        y_int = jnp.dot(cm_g, sm.astype(BF16), preferred_element_type=F32) * ecum[:, gl]
        for pp in range(2):
            h0 = 4 * g + 2 * pp + 8 * d
            sl = slice(256 * g + 128 * pp, 256 * g + 128 * (pp + 1))
            crow = jnp.concatenate([cum_t[h0:h0 + 1, :], cum_t[h0 + 1:h0 + 2, :]], axis=1)
            lm = jnp.exp(jnp.where(keep2, cum_e[:, sl] - crow, -jnp.inf))
            yp = jnp.dot((cb2 * lm).astype(BF16), pair_rows(xdt_b[:, sl]), preferred_element_type=F32)
            yp = yp + y_int[:, 128 * pp:128 * (pp + 1)]
            if not rev:
                yp = yp + dskip_ref[:, sl] * xm[:, sl]
            om_ref[0, :, sl] = yp.astype(om_ref.dtype)
        sm_ref[d, g] = sm * dec_m[:, gl] + _dot_tn(bm_g, wx[:, gl])


def _scan_kernel(qf, ff, vf, xf, xpf, xnf, dtf, qb, fb, vb_, xb, xpb, xnb, dtb,
                 lb_ref, cw_ref, cbias_ref, dtbias_ref, aexp_ref, dskip_ref,
                 ohf, omf, ohb, omb, sg_ref, sm_ref, u_scr, *, nct, nc):
    c = pl.program_id(1)

    @pl.when(c == 0)
    def _():
        sg_ref[...] = jnp.zeros(sg_ref.shape, F32)
        sm_ref[...] = jnp.zeros(sm_ref.shape, F32)

    cb = jnp.where(c < nct, nct - 1 - c, nc + nct - 1 - c)

    def edge_flags(ch):
        pv = jnp.where((ch == 0) | (ch == nct), 0.0, 1.0)
        nv = jnp.where((ch == nct - 1) | (ch == nc - 1), 0.0, 1.0)
        return pv, nv

    pvf, nvf = edge_flags(c)
    pvb, nvb = edge_flags(cb)
    _scan_dir(False, qf, ff, vf, xf, xpf, xnf, dtf, pvf, nvf, lb_ref[0:1, :], cw_ref, cbias_ref,
              dtbias_ref, aexp_ref, dskip_ref, ohf, omf, sg_ref, sm_ref, u_scr)
    _scan_dir(True, qb, fb, vb_, xb, xpb, xnb, dtb, pvb, nvb, lb_ref[1:2, :], cw_ref, cbias_ref,
              dtbias_ref, aexp_ref, dskip_ref, ohb, omb, sg_ref, sm_ref, u_scr)


def _scan_mixers(p, nct, lb, conv_w, conv_b, dt_bias, a_log, d_skip):
    bsz, s, _ = p.shape
    nc = s // L
    nb8 = s // 8

    def bmap(c):
        return jnp.where(c < nct, nct - 1 - c, nc + nct - 1 - c)

    def specs(cmap, fcol):
        colblk = lambda col, w: col // w
        return [
            pl.BlockSpec((1, L, 512), lambda i, c: (i, cmap(c), colblk(COL_Q, 512))),
            pl.BlockSpec((1, L, 512), lambda i, c: (i, cmap(c), colblk(fcol, 512))),
            pl.BlockSpec((1, L, 512), lambda i, c: (i, cmap(c), colblk(COL_I, 512))),
            pl.BlockSpec((1, L, 1024), lambda i, c: (i, cmap(c), colblk(COL_X, 1024))),
            pl.BlockSpec((1, 8, 1024), lambda i, c: (i, jnp.maximum(cmap(c) * 8 - 1, 0), colblk(COL_X, 1024))),
            pl.BlockSpec((1, 8, 1024), lambda i, c: (i, jnp.minimum(cmap(c) * 8 + 8, nb8 - 1), colblk(COL_X, 1024))),
            pl.BlockSpec((1, L, 128), lambda i, c: (i, cmap(c), colblk(COL_DT, 128))),
        ]

    ident = lambda c: c
    const2 = lambda i, c: (0, 0)
    dtb = jnp.zeros((1, 128), F32).at[0, :16].set(dt_bias.reshape(16))
    aexp = jnp.zeros((1, 128), F32).at[0, :16].set(jnp.exp(a_log.reshape(16)))
    dsk = jnp.repeat(d_skip, 64).reshape(1, 512)
    kern = functools.partial(_scan_kernel, nct=nct, nc=nc)
    out_sd = jax.ShapeDtypeStruct((bsz, s, 512), BF16)
    return pl.pallas_call(
        kern,
        grid=(bsz, nc),
        in_specs=specs(ident, COL_FF) + specs(bmap, COL_FB) + [
            pl.BlockSpec((2, 512), const2), pl.BlockSpec((5, 1024), const2),
            pl.BlockSpec((1, 1024), const2), pl.BlockSpec((1, 128), const2),
            pl.BlockSpec((1, 128), const2), pl.BlockSpec((1, 512), const2)],
        out_specs=[pl.BlockSpec((1, L, 512), lambda i, c: (i, c, 0)),
                   pl.BlockSpec((1, L, 512), lambda i, c: (i, c, 0)),
                   pl.BlockSpec((1, L, 512), lambda i, c: (i, bmap(c), 0)),
                   pl.BlockSpec((1, L, 512), lambda i, c: (i, bmap(c), 0))],
        out_shape=[out_sd, out_sd, out_sd, out_sd],
        scratch_shapes=[pltpu.VMEM((2, HA // 2, 128, 128), F32),
                        pltpu.VMEM((2, 2, 128, 256), F32),
                        pltpu.VMEM((L + 16, 1024), F32)],
        compiler_params=_cparams(("parallel", "arbitrary")),
        name="scan_mixers",
    )(*([p] * 14), lb, conv_w, conv_b.reshape(1, 1024), dtb, aexp, dsk)


def _merge_kernel(ohf, ohb, omf, omb, g_ref, z_ref, xc_ref, xl_ref, gate_ref, hn_ref, mn_ref, w_ref,
                  fg_ref, fsc_ref, fsh_ref, rw_ref, oc_ref, ol_ref, h_ref, lg_ref, *, nct_tiles):
    oh = ohf[0].astype(F32) + ohb[0].astype(F32)
    sq = oh * oh
    hi = sq.astype(BF16)
    lo = (sq - hi.astype(F32)).astype(BF16)
    r = lax.broadcasted_iota(jnp.int32, (512, 512), 0) // 64
    c = lax.broadcasted_iota(jnp.int32, (512, 512), 1) // 64
    avg = jnp.where(r == c, 1.0 / 64, 0.0).astype(BF16)
    ms = jnp.dot(hi, avg, preferred_element_type=F32) + jnp.dot(lo, avg, preferred_element_type=F32)
    oh = oh * lax.rsqrt(ms + EPS) * hn_ref[...] * jax.nn.sigmoid(g_ref[0])
    y = (omf[0].astype(F32) + omb[0].astype(F32)) * _silu(z_ref[0])
    y = y * lax.rsqrt(jnp.mean(y * y, axis=-1, keepdims=True) + EPS) * mn_ref[...]
    cat = jnp.concatenate([oh, y], axis=1).astype(BF16)
    j = pl.program_id(1)
    x = jnp.where(j < nct_tiles, xc_ref[0], xl_ref[0])
    out = x + gate_ref[0, 0] * jnp.dot(cat, w_ref[...], preferred_element_type=F32)
    _router_body(out, fg_ref[...], fsc_ref[0, 0], fsh_ref[0, 0], rw_ref, h_ref, lg_ref)

    @pl.when(j < nct_tiles)
    def _():
        oc_ref[0] = out

    @pl.when(j >= nct_tiles)
    def _():
        ol_ref[0] = out


def _merge_mixers(x_ctx, x_lat, gate2, ohf, ohb, omf, omb, p, hgrn_norm, mamba_norm, w_out,
                  ffn_gain, ffn_scale2, ffn_shift2, w_router, tm):
    b, tc, d = x_ctx.shape
    s = tc + x_lat.shape[1]
    nct_tiles = tc // tm
    e = w_router.shape[1]
    sel = lambda i, j: (i, jnp.where(j < nct_tiles, 0, 1), 0, 0)
    row = lambda i, j: (i, j, 0)
    const2 = lambda i, j: (0, 0)
    return pl.pallas_call(
        functools.partial(_merge_kernel, nct_tiles=nct_tiles),
        grid=(b, s // tm),
        in_specs=[pl.BlockSpec((1, tm, 512), row)] * 4 + [
            pl.BlockSpec((1, tm, 512), lambda i, j: (i, j, COL_G // 512)),
            pl.BlockSpec((1, tm, 512), lambda i, j: (i, j, COL_Z // 512))] + _ctx_lat_specs(
                tm, d, nct_tiles) + [
            pl.BlockSpec((1, 1, 1, d), sel),
            pl.BlockSpec((1, 512), const2), pl.BlockSpec((1, 512), const2),
            pl.BlockSpec((2 * 512, d), const2),
            pl.BlockSpec((1, d), const2),
            pl.BlockSpec((1, 1, 1, d), sel), pl.BlockSpec((1, 1, 1, d), sel),
            pl.BlockSpec((3 * d, e), const2)],
        out_specs=[pl.BlockSpec((1, tm, d), lambda i, j: (i, jnp.minimum(j, nct_tiles - 1), 0)),
                   pl.BlockSpec((1, tm, d), lambda i, j: (i, jnp.maximum(j - nct_tiles, 0), 0)),
                   pl.BlockSpec((1, tm, d), row), pl.BlockSpec((1, tm, e), row)],
        out_shape=[jax.ShapeDtypeStruct((b, tc, d), F32),
                   jax.ShapeDtypeStruct((b, s - tc, d), F32),
                   jax.ShapeDtypeStruct((b, s, d), BF16),
                   jax.ShapeDtypeStruct((b, s, e), F32)],
        compiler_params=_cparams(("parallel", "arbitrary")),
        name="merge_mixers",
    )(ohf, ohb, omf, omb, p, p, x_ctx, x_lat, gate2.reshape(b, 2, 1, d), hgrn_norm.reshape(1, 512),
      mamba_norm.reshape(1, 512), w_out, ffn_gain.reshape(1, d), ffn_scale2.reshape(b, 2, 1, d),
      ffn_shift2.reshape(b, 2, 1, d), _router_weights(w_router))


def _rope_tables(t):
    rows = t // GRID_W
    row = jnp.repeat(jnp.arange(rows, dtype=jnp.int32), GRID_W)
    col = jnp.tile(jnp.arange(GRID_W, dtype=jnp.int32), rows)
    nf = ROPE // 4
    inv_freq = ROPE_THETA ** (-jnp.arange(nf, dtype=F32) / nf)
    pos = jnp.stack([row, col], axis=-1).astype(F32)
    ang = pos[..., None] * inv_freq
    cos, sin = jnp.cos(ang), jnp.sin(ang)
    cos64 = jnp.broadcast_to(cos[:, :, None, :], (t, 2, 2, nf)).reshape(t, ROPE)
    sin64 = jnp.broadcast_to(sin[:, :, None, :], (t, 2, 2, nf)).reshape(t, ROPE)
    zero = jnp.zeros((t, ROPE), F32)
    return jnp.concatenate([cos64, zero], -1), jnp.concatenate([sin64, zero], -1)


def _swap_cols(w):
    nf = ROPE // 4
    wr = w.reshape(w.shape[:-1] + (2, 2, nf))
    return jnp.stack([-wr[..., 1, :], wr[..., 0, :]], axis=-2).reshape(w.shape)


def _rope_tile(tile, cos_t, sin_t):
    return tile * cos_t + pltpu.roll(tile, ROPE, axis=1) * sin_t


def _mla_down_kernel(x_ref, g_ref, sc_ref, sh_ref, w_ref, qn_ref, kvn_ref, cos_ref, sin_ref,
                     cq_ref, ckv_ref, kr_ref):
    a = _modnorm(x_ref[0], g_ref[...], sc_ref[0], sh_ref[0]).astype(BF16)
    c = jnp.dot(a, w_ref[...], preferred_element_type=F32)
    cq = c[:, :Q_LORA]
    ckv = c[:, Q_LORA:Q_LORA + KV_LORA]
    cq_ref[0] = _rms(cq, qn_ref[...]).astype(BF16)
    ckv_ref[0] = _rms(ckv, kvn_ref[...]).astype(BF16)
    kr_ref[0] = _rope_tile(c[:, Q_LORA + KV_LORA:], cos_ref[...], sin_ref[...]).astype(BF16)


def _mla_down(x, gain, scale, shift, w_cat, q_norm, kv_norm, cos_t, sin_t, tm):
    b, t, d = x.shape
    n = w_cat.shape[1]
    return pl.pallas_call(
        _mla_down_kernel,
        grid=(b, t // tm),
        in_specs=[pl.BlockSpec((1, tm, d), lambda i, j: (i, j, 0)),
                  pl.BlockSpec((1, d), lambda i, j: (0, 0)),
                  pl.BlockSpec((1, 1, d), lambda i, j: (i, 0, 0)),
                  pl.BlockSpec((1, 1, d), lambda i, j: (i, 0, 0)),
                  pl.BlockSpec((d, n), lambda i, j: (0, 0)),
                  pl.BlockSpec((1, Q_LORA), lambda i, j: (0, 0)),
                  pl.BlockSpec((1, KV_LORA), lambda i, j: (0, 0)),
                  pl.BlockSpec((tm, 128), lambda i, j: (j, 0)),
                  pl.BlockSpec((tm, 128), lambda i, j: (j, 0))],
        out_specs=[pl.BlockSpec((1, tm, Q_LORA), lambda i, j: (i, j, 0)),
                   pl.BlockSpec((1, tm, KV_LORA), lambda i, j: (i, j, 0)),
                   pl.BlockSpec((1, tm, 128), lambda i, j: (i, j, 0))],
        out_shape=[jax.ShapeDtypeStruct((b, t, Q_LORA), BF16),
                   jax.ShapeDtypeStruct((b, t, KV_LORA), BF16),
                   jax.ShapeDtypeStruct((b, t, 128), BF16)],
        compiler_params=_cparams(("parallel", "parallel")),
        name="mla_down",
    )(x, gain.reshape(1, d), scale, shift, w_cat, q_norm.reshape(1, -1), kv_norm.reshape(1, -1),
      cos_t, sin_t)


def _q_up_kernel(cq_ref, w_ref, cos_ref, sin_ref, q_ref):
    cq = cq_ref[0]
    for h in range(H_C):
        q = jnp.dot(cq, w_ref[h], preferred_element_type=F32)
        qn = q[:, :NOPE] * QK_SCALE
        qr = _rope_tile(q[:, NOPE:], cos_ref[...], sin_ref[...]) * QK_SCALE
        q_ref[0, h] = jnp.concatenate([qn, qr], axis=1).astype(BF16)


def _q_up(cq, w_uq_h, cos_t, sin_t, tm):
    b, t, r = cq.shape
    return pl.pallas_call(
        _q_up_kernel,
        grid=(b, t // tm),
        in_specs=[pl.BlockSpec((1, tm, r), lambda i, j: (i, j, 0)),
                  pl.BlockSpec((H_C, r, QK_DIM), lambda i, j: (0, 0, 0)),
                  pl.BlockSpec((tm, 128), lambda i, j: (j, 0)),
                  pl.BlockSpec((tm, 128), lambda i, j: (j, 0))],
        out_specs=pl.BlockSpec((1, H_C, tm, QK_DIM), lambda i, j: (i, 0, j, 0)),
        out_shape=jax.ShapeDtypeStruct((b, H_C, t, QK_DIM), BF16),
        compiler_params=_cparams(("parallel", "parallel")),
        name="q_up",
    )(cq, w_uq_h, cos_t, sin_t)


def _kv_up_kernel(ckv_ref, kr_ref, w_ref, k_ref, v_ref):
    ckv = ckv_ref[0]
    kr = kr_ref[0]
    for h in range(H_C):
        kv = jnp.dot(ckv, w_ref[h], preferred_element_type=F32)
        k_ref[0, h] = jnp.concatenate([kv[:, :NOPE].astype(BF16), kr], axis=1)
        v_ref[0, h] = kv[:, NOPE:].astype(BF16)


def _kv_up(ckv, kr, w_ukv_h, tm):
    b, s, r = ckv.shape
    return pl.pallas_call(
        _kv_up_kernel,
        grid=(b, s // tm),
        in_specs=[pl.BlockSpec((1, tm, r), lambda i, j: (i, j, 0)),
                  pl.BlockSpec((1, tm, 128), lambda i, j: (i, j, 0)),
                  pl.BlockSpec((H_C, r, NOPE + VH), lambda i, j: (0, 0, 0))],
        out_specs=[pl.BlockSpec((1, H_C, tm, QK_DIM), lambda i, j: (i, 0, j, 0)),
                   pl.BlockSpec((1, H_C, tm, VH), lambda i, j: (i, 0, j, 0))],
        out_shape=[jax.ShapeDtypeStruct((b, H_C, s, QK_DIM), BF16),
                   jax.ShapeDtypeStruct((b, H_C, s, VH), BF16)],
        compiler_params=_cparams(("parallel", "parallel")),
        name="kv_up",
    )(ckv, kr, w_ukv_h)


def _attn_kernel(qa_ref, qb_ref, q0_ref, k_ref, v_ref, o_ref, s0_scr, s1_scr, m_scr, *, tk, nk, tq):
    n = pl.program_id(2)

    def scores(q, j):
        ks = k_ref[0, 0, j * tk:(j + 1) * tk, :]
        return lax.dot_general(q, ks, (((1,), (1,)), ((), ())), preferred_element_type=F32)

    def fold_max(mrun, s):
        for c in range(tk // 128):
            mrun = jnp.maximum(mrun, s[:, c * 128:(c + 1) * 128])
        return mrun

    @pl.when(n == 0)
    def _():
        q0 = q0_ref[0, 0]
        mrun = jnp.full((tq, 128), -jnp.inf, F32)
        for j in range(nk):
            s = scores(q0, j)
            s0_scr[:, j * tk:(j + 1) * tk] = s
            mrun = fold_max(mrun, s)
        m_scr[...] = jnp.max(mrun, axis=-1, keepdims=True)

    def fused(qn, s_read, s_write, m):
        mrun = jnp.full((tq, 128), -jnp.inf, F32)
        lrun = jnp.zeros((tq, 128), F32)
        acc = jnp.zeros((tq, VH), F32)
        for j in range(nk):
            sn = scores(qn, j)
            s_write[:, j * tk:(j + 1) * tk] = sn
            mrun = fold_max(mrun, sn)
            p = jnp.exp2(s_read[:, j * tk:(j + 1) * tk] - m)
            for c in range(tk // 128):
                lrun = lrun + p[:, c * 128:(c + 1) * 128]
            acc = acc + jnp.dot(p.astype(BF16), v_ref[0, 0, j * tk:(j + 1) * tk, :],
                                preferred_element_type=F32)
        out = acc / jnp.sum(lrun, axis=-1, keepdims=True)
        return out, jnp.max(mrun, axis=-1, keepdims=True)

    out_a, m1 = fused(qa_ref[0, 0], s0_scr, s1_scr, m_scr[...])
    o_ref[0, 0:tq, :] = out_a.astype(o_ref.dtype)
    out_b, m2 = fused(qb_ref[0, 0], s1_scr, s0_scr, m1)
    o_ref[0, tq:2 * tq, :] = out_b.astype(o_ref.dtype)
    m_scr[...] = m2


def _attention(q, k, v, tq, tk):
    b, h, t, _ = q.shape
    s = k.shape[2]
    nq = t // tq
    assert t % (2 * tq) == 0 and s % tk == 0
    kern = functools.partial(_attn_kernel, tk=tk, nk=s // tk, tq=tq)
    return pl.pallas_call(
        kern,
        grid=(b, h, nq // 2),
        in_specs=[pl.BlockSpec((1, 1, tq, QK_DIM), lambda i, j, n: (i, j, 2 * n + 1, 0)),
                  pl.BlockSpec((1, 1, tq, QK_DIM),
                               lambda i, j, n: (i, j, jnp.minimum(2 * n + 2, nq - 1), 0)),
                  pl.BlockSpec((1, 1, tq, QK_DIM), lambda i, j, n: (i, j, 0, 0)),
                  pl.BlockSpec((1, 1, s, QK_DIM), lambda i, j, n: (i, j, 0, 0)),
                  pl.BlockSpec((1, 1, s, VH), lambda i, j, n: (i, j, 0, 0))],
        out_specs=pl.BlockSpec((1, 2 * tq, VH), lambda i, j, n: (i, n, j)),
        out_shape=jax.ShapeDtypeStruct((b, t, h * VH), BF16),
        scratch_shapes=[pltpu.VMEM((tq, s), F32), pltpu.VMEM((tq, s), F32),
                        pltpu.VMEM((tq, 1), F32)],
        compiler_params=_cparams(("parallel", "parallel", "arbitrary")),
        name="attention",
    )(q, q, q, k, v)


def _ffn_kernel(x_ref, wg_ref, wu_ref, wd_ref, g_ref, o_ref, acc_scr, wgb, wub, wdb, *, fc):
    @pl.when((pl.program_id(1) == 0) & (pl.program_id(2) == 0))
    def _():
        wgb[...] = wg_ref[0, 0].astype(BF16)
        wub[...] = wu_ref[0, 0].astype(BF16)
        wdb[...] = wd_ref[0, 0].astype(BF16)

    x = x_ref[0, 0]
    nf = wgb.shape[-1]
    for c in range(nf // fc):
        sl = slice(c * fc, (c + 1) * fc)
        hg = jnp.dot(x, wgb[:, sl], preferred_element_type=F32)
        hu = jnp.dot(x, wub[:, sl], preferred_element_type=F32)
        hid = (_silu(hg) * hu).astype(BF16)
        part = jnp.dot(hid, wdb[sl, :], preferred_element_type=F32)
        if c == 0:
            acc_scr[...] = part
        else:
            acc_scr[...] += part
    o_ref[0, 0] = (acc_scr[...] * g_ref[0, 0]).astype(o_ref.dtype)


def _expert_ffn(xg, gates, w_gate, w_up, w_down, layer, tm, fc):
    b, e, cap, d = xg.shape
    f = w_gate.shape[-1]
    tm = min(tm, cap)
    kern = functools.partial(_ffn_kernel, fc=fc)
    return pl.pallas_call(
        kern,
        grid=(e, b, cap // tm),
        in_specs=[pl.BlockSpec((1, 1, tm, d), lambda j, i, m: (i, j, m, 0)),
                  pl.BlockSpec((1, 1, d, f), lambda j, i, m: (layer, j, 0, 0)),
                  pl.BlockSpec((1, 1, d, f), lambda j, i, m: (layer, j, 0, 0)),
                  pl.BlockSpec((1, 1, f, d), lambda j, i, m: (layer, j, 0, 0)),
                  pl.BlockSpec((1, 1, tm, 1), lambda j, i, m: (i, j, m, 0))],
        out_specs=pl.BlockSpec((1, 1, tm, d), lambda j, i, m: (i, j, m, 0)),
        out_shape=jax.ShapeDtypeStruct((b, e, cap, d), BF16),
        scratch_shapes=[pltpu.VMEM((tm, d), F32), pltpu.VMEM((d, f), BF16),
                        pltpu.VMEM((d, f), BF16), pltpu.VMEM((f, d), BF16)],
        compiler_params=_cparams(("arbitrary", "arbitrary", "arbitrary")),
        name="expert_ffn",
    )(xg, w_gate, w_up, w_down, gates)


SLOT_WIN = 256
TOK_BLK = 128
COMBINE_EXPERTS = 2


def _combine_kernel(lo_ref, idx_ref, y_ref, x_ref, gate_ref, fg_ref, o_ref, *, nblk, nrow,
                    final_norm):
    b, j, eg = pl.program_id(0), pl.program_id(1), pl.program_id(2)

    @pl.when(eg == 0)
    def _():
        o_ref[...] = jnp.zeros(o_ref.shape, F32)

    tok = lax.broadcasted_iota(jnp.int32, (TOK_BLK, 1), 0)
    for blk in range(nblk):
        gblk = j * nblk + blk
        part = None
        for k in range(COMBINE_EXPERTS):
            w = jnp.minimum(lo_ref[b, eg * COMBINE_EXPERTS + k, gblk] // TOK_BLK, nrow - 2)
            ids = jnp.concatenate([idx_ref[0, k, pl.ds(w, 1), :],
                                   idx_ref[0, k, pl.ds(w + 1, 1), :]], axis=1)
            onehot = (ids == tok + gblk * TOK_BLK).astype(BF16)
            yw = y_ref[0, k, pl.ds(pl.multiple_of(w * TOK_BLK, TOK_BLK), SLOT_WIN), :]
            d = jnp.dot(onehot, yw, preferred_element_type=F32)
            part = d if part is None else part + d
        rows = slice(blk * TOK_BLK, (blk + 1) * TOK_BLK)
        o_ref[0, rows, :] += part

    @pl.when(eg == pl.num_programs(2) - 1)
    def _():
        out = x_ref[0] + gate_ref[0] * o_ref[0]
        o_ref[0] = _rms(out, fg_ref[...]) if final_norm else out


def _combine(x, gate_out, idx_s, y, final_gain=None):
    b, t, d = x.shape
    e, cap = idx_s.shape[1:]
    capp = max(cap, SLOT_WIN)
    if capp != cap:
        idx_s = jnp.pad(idx_s, ((0, 0), (0, 0), (0, capp - cap)), constant_values=-1)
        y = jnp.pad(y[:, :, :cap], ((0, 0), (0, 0), (0, capp - cap), (0, 0)))
    nrow = capp // TOK_BLK
    ts = min(2048, t)
    nblk = ts // TOK_BLK
    bounds = jnp.arange(t // TOK_BLK, dtype=jnp.int32) * TOK_BLK
    valid = idx_s >= 0
    lo = jnp.sum((valid[..., None] & (idx_s[..., None] < bounds)).astype(jnp.int32), axis=2)
    kern = functools.partial(_combine_kernel, nblk=nblk, nrow=nrow,
                             final_norm=final_gain is not None)
    fg = jnp.ones((1, d), F32) if final_gain is None else final_gain.reshape(1, d)
    grid_spec = pltpu.PrefetchScalarGridSpec(
        num_scalar_prefetch=1,
        grid=(b, t // ts, e // COMBINE_EXPERTS),
        in_specs=[pl.BlockSpec((1, COMBINE_EXPERTS, nrow, TOK_BLK), lambda i, j, k, lo_r: (i, k, 0, 0)),
                  pl.BlockSpec((1, COMBINE_EXPERTS, capp, d), lambda i, j, k, lo_r: (i, k, 0, 0)),
                  pl.BlockSpec((1, ts, d), lambda i, j, k, lo_r: (i, j, 0)),
                  pl.BlockSpec((1, 1, d), lambda i, j, k, lo_r: (i, 0, 0)),
                  pl.BlockSpec((1, d), lambda i, j, k, lo_r: (0, 0))],
        out_specs=pl.BlockSpec((1, ts, d), lambda i, j, k, lo_r: (i, j, 0)),
    )
    return pl.pallas_call(
        kern,
        grid_spec=grid_spec,
        out_shape=jax.ShapeDtypeStruct((b, t, d), F32),
        compiler_params=_cparams(("parallel", "parallel", "arbitrary")),
        name="moe_combine",
    )(lo, idx_s.reshape(b, e, nrow, TOK_BLK), y, x, gate_out, fg)


def _route(logits):
    t = logits.shape[1]
    cap = CAP_FACTOR * t // N_EXPERTS
    aff = jax.nn.softmax(logits, axis=-1)
    g, idx = lax.top_k(jnp.swapaxes(aff, 1, 2), cap)
    return lax.sort((idx, g), dimension=2, num_keys=1)


def _moe(x, hn, logits, gate_out, w_gate, w_up, w_down, layer, final_gain=None):
    b, t, d = x.shape
    idx_s, g_s = _route(logits)
    bidx = jnp.arange(b)[:, None, None]
    xg = hn[bidx, idx_s]
    y = _expert_ffn(xg, g_s[..., None], w_gate, w_up, w_down, layer, 256, 512)
    return _combine(x, gate_out, idx_s, y, final_gain)


def _moe_two_streams(x_ctx, x_lat, hn, logits, gate_ctx, gate_lat, w_gate, w_up, w_down, layer):
    b, tc, d = x_ctx.shape
    idx_c, g_c = _route(logits[:, :tc])
    idx_l, g_l = _route(logits[:, tc:])
    idx_cat = jnp.concatenate([idx_l + tc, idx_c], axis=2)
    g_cat = jnp.concatenate([g_l, g_c], axis=2)
    bidx = jnp.arange(b)[:, None, None]
    xg = hn[bidx, idx_cat]
    cap = idx_cat.shape[2]
    tm = next(m for m in (256, 352, 176, 96, 32, 16) if cap % m == 0)
    y = _expert_ffn(xg, g_cat[..., None], w_gate, w_up, w_down, layer, tm, 512)
    new_lat = _combine(x_lat, gate_lat, idx_l, y)
    new_ctx = _combine(x_ctx, gate_ctx, idx_c, y[:, :, idx_l.shape[2]:])
    return new_ctx, new_lat


def kernel(x, c, ctx, c_ctx, w_mod, b_mod, norm_mix, norm_ffn, norm_out, w_in, w_out_rec,
           conv_w, conv_b, lb_gamma, dt_bias, a_log, d_skip, hgrn_norm, mamba_norm,
           w_dq, q_norm, w_uq, w_dkv, kv_norm, w_ukv, w_kr, w_o,
           w_router, w_gate, w_up, w_down):
    B, T, D = x.shape
    TC = ctx.shape[1]
    lb = jnp.cumsum(jax.nn.softmax(lb_gamma.astype(F32), axis=0), axis=0)

    cvec = jnp.zeros((16, D), F32).at[:B].set(c).at[B].set(c_ctx)
    mods = _modulation(cvec, w_mod, b_mod)

    x_lat, x_ctx = x, ctx
    for l in range(DEPTH):
        need_ctx = l < DEPTH - 1
        m_lat = [mods[l, :B, i * D:(i + 1) * D][:, None, :] for i in range(6)]
        m_ctx = [jnp.broadcast_to(mods[l, B, i * D:(i + 1) * D][None, None, :], (B, 1, D))
                 for i in range(6)]
        if l % 2 == 0:
            e = l // 2
            w_in_p = jnp.pad(w_in[e], ((0, 0), (0, IN_COLS_PAD - IN_COLS))).astype(BF16)
            pair = lambda i: jnp.concatenate([m_ctx[i], m_lat[i]], axis=1)
            p = _modproj(x_ctx, x_lat, norm_mix[l], pair(1), pair(0), w_in_p, 256)
            ohf, omf, ohb, omb = _scan_mixers(p, TC // CHUNK, lb[e], conv_w[e], conv_b[e],
                                              dt_bias[e], a_log[e], d_skip[e])
            x_ctx, x_lat, hn, logits = _merge_mixers(
                x_ctx, x_lat, pair(2), ohf, ohb, omf, omb, p, hgrn_norm[e], mamba_norm[e],
                w_out_rec[e].astype(BF16), norm_ffn[l], pair(4), pair(3), w_router[l], 256)
            if not need_ctx:
                hn, logits = hn[:, TC:], logits[:, TC:]
        else:
            assert not need_ctx
            j = l // 2
            cos_l, sin_l = _rope_tables(T)
            cos_c = jnp.concatenate([jnp.ones((TC, ROPE), F32), jnp.zeros((TC, ROPE), F32)], -1)
            sin_c = jnp.zeros((TC, 128), F32)
            w_cat = jnp.concatenate([w_dq[j], w_dkv[j], w_kr[j], _swap_cols(w_kr[j])],
                                    axis=1).astype(BF16)
            wq = w_uq[j].reshape(Q_LORA, H_C, NOPE + ROPE)
            wq_h = jnp.concatenate([wq, _swap_cols(wq[..., NOPE:])], axis=-1)
            wq_h = jnp.transpose(wq_h, (1, 0, 2)).astype(BF16)
            wkv_h = jnp.transpose(w_ukv[j].reshape(KV_LORA, H_C, NOPE + VH), (1, 0, 2)).astype(BF16)
            cq_l, ckv_l, kr_l = _mla_down(x_lat, norm_mix[l], m_lat[1], m_lat[0], w_cat,
                                          q_norm[j], kv_norm[j], cos_l, sin_l, 256)
            _, ckv_c, kr_c = _mla_down(x_ctx, norm_mix[l], m_ctx[1], m_ctx[0], w_cat,
                                       q_norm[j], kv_norm[j], cos_c, sin_c, 256)
            ckv = jnp.concatenate([ckv_c, ckv_l], axis=1)
            kr = jnp.concatenate([kr_c, kr_l], axis=1)
            qh = _q_up(cq_l, wq_h, cos_l, sin_l, 512)
            kh, vh = _kv_up(ckv, kr, wkv_h, 768)
            o = _attention(qh, kh, vh, 256, 768)
            x_lat, hn, logits = _linear_residual(x_lat, m_lat[2], o, w_o[j].astype(BF16),
                                                 norm_ffn[l], m_lat[4], m_lat[3], w_router[l], 512)
        if need_ctx:
            x_ctx, x_lat = _moe_two_streams(x_ctx, x_lat, hn, logits, m_ctx[5], m_lat[5],
                                            w_gate, w_up, w_down, l)
        else:
            x_lat = _moe(x_lat, hn, logits, m_lat[5], w_gate, w_up, w_down, l, norm_out)
    return x_lat
```

```python
import functools
import math

import jax
import jax.numpy as jnp
from jax import lax
from jax.experimental import pallas as pl
from jax.experimental.pallas import tpu as pltpu

F32 = jnp.float32
BF16 = jnp.bfloat16

D_MODEL = 1024
DEPTH = 2
GRID_W = 64
EPS = 1e-6
CHUNK = 64
HA, DKA, DVA = 8, 64, 64
WA = HA * DVA
HB, PB = 8, 64
D_INNER = HB * PB
N_GROUPS, D_STATE = 2, 128
CONV_W = 5
REC_SPLITS = (HA * DKA, HA * DKA, HA * DKA, WA, WA, D_INNER, D_INNER,
              N_GROUPS * D_STATE, N_GROUPS * D_STATE, HB, HB)
IN_COLS = sum(REC_SPLITS)
IN_COLS_PAD = 4224
H_C, NOPE, ROPE, VH = 8, 128, 64, 128
Q_LORA, KV_LORA = 384, 256
ROPE_THETA = 10000.0
ATTN_SCALE = 1.0 / math.sqrt(NOPE + ROPE)
QK_SCALE = ATTN_SCALE * math.log2(math.e)
QK_DIM = 256
N_EXPERTS = 16
EXPERT_FF = 1024
CAP_FACTOR = 2

VMEM_LIMIT = 56 * 1024 * 1024


def _cparams(sem):
    return pltpu.CompilerParams(dimension_semantics=sem, vmem_limit_bytes=VMEM_LIMIT)


def _silu(v):
    return v * jax.nn.sigmoid(v)


def _modnorm(x, gain, scale, shift):
    ms = jnp.mean(x * x, axis=-1, keepdims=True)
    return (x * lax.rsqrt(ms + EPS) * gain) * (1.0 + scale) + shift


def _mod_kernel(s_ref, w_ref, b_ref, o_ref):
    s = _silu(s_ref[...])
    o_ref[0] = jnp.dot(s, w_ref[0], preferred_element_type=F32,
                       precision=lax.Precision.HIGHEST) + b_ref[0]


def _modulation(cvec, w_mod, b_mod):
    n = w_mod.shape[-1]
    tn = 1536
    return pl.pallas_call(
        _mod_kernel,
        grid=(DEPTH, n // tn),
        in_specs=[pl.BlockSpec((16, D_MODEL), lambda l, j: (0, 0)),
                  pl.BlockSpec((1, D_MODEL, tn), lambda l, j: (l, 0, j)),
                  pl.BlockSpec((1, 1, tn), lambda l, j: (l, 0, j))],
        out_specs=pl.BlockSpec((1, 16, tn), lambda l, j: (l, 0, j)),
        out_shape=jax.ShapeDtypeStruct((DEPTH, 16, n), F32),
        compiler_params=_cparams(("parallel", "parallel")),
        name="modulation",
    )(cvec, w_mod, b_mod.reshape(DEPTH, 1, n))


def _ctx_lat_specs(tm, d, nct_tiles):
    return [pl.BlockSpec((1, tm, d), lambda i, j: (i, jnp.minimum(j, nct_tiles - 1), 0)),
            pl.BlockSpec((1, tm, d), lambda i, j: (i, jnp.maximum(j - nct_tiles, 0), 0))]


def _modproj_kernel(xc_ref, xl_ref, g_ref, sc_ref, sh_ref, w_ref, o_ref, *, nct_tiles):
    x = jnp.where(pl.program_id(1) < nct_tiles, xc_ref[0], xl_ref[0])
    a = _modnorm(x, g_ref[...], sc_ref[0, 0], sh_ref[0, 0]).astype(BF16)
    o_ref[0] = jnp.dot(a, w_ref[...], preferred_element_type=F32).astype(o_ref.dtype)


def _modproj(x_ctx, x_lat, gain, scale2, shift2, w, tm, out_dtype=F32):
    b, tc, d = x_ctx.shape
    t = tc + x_lat.shape[1]
    nct_tiles = tc // tm
    n = w.shape[1]
    sel = lambda i, j: (i, jnp.where(j < nct_tiles, 0, 1), 0, 0)
    return pl.pallas_call(
        functools.partial(_modproj_kernel, nct_tiles=nct_tiles),
        grid=(b, t // tm),
        in_specs=_ctx_lat_specs(tm, d, nct_tiles) + [
                  pl.BlockSpec((1, d), lambda i, j: (0, 0)),
                  pl.BlockSpec((1, 1, 1, d), sel),
                  pl.BlockSpec((1, 1, 1, d), sel),
                  pl.BlockSpec((d, n), lambda i, j: (0, 0))],
        out_specs=pl.BlockSpec((1, tm, n), lambda i, j: (i, j, 0)),
        out_shape=jax.ShapeDtypeStruct((b, t, n), out_dtype),
        compiler_params=_cparams(("parallel", "parallel")),
        name="modproj",
    )(x_ctx, x_lat, gain.reshape(1, d), scale2.reshape(b, 2, 1, d), shift2.reshape(b, 2, 1, d), w)


def _router_body(x, gain, scale, shift, w_ref, h_ref, lg_ref):
    hn = _modnorm(x, gain, scale, shift)
    a = hn.astype(BF16)
    h_ref[0] = a
    b = (hn - a.astype(F32)).astype(BF16)
    lg_ref[0] = jnp.dot(jnp.concatenate([a, b, a], axis=1), w_ref[...], preferred_element_type=F32)


def _router_weights(w_router):
    w_hi = w_router.astype(BF16)
    w_lo = (w_router - w_hi.astype(F32)).astype(BF16)
    return jnp.concatenate([w_hi, w_hi, w_lo], axis=0)


def _linres_kernel(x_ref, gate_ref, y_ref, w_ref, fg_ref, fsc_ref, fsh_ref, rw_ref,
                   o_ref, h_ref, lg_ref):
    acc = jnp.dot(y_ref[0].astype(BF16), w_ref[...], preferred_element_type=F32)
    out = x_ref[0] + gate_ref[0] * acc
    o_ref[0] = out
    _router_body(out, fg_ref[...], fsc_ref[0], fsh_ref[0], rw_ref, h_ref, lg_ref)


def _linear_residual(x, gate, y, w, ffn_gain, ffn_scale, ffn_shift, w_router, tm):
    b, t, d = x.shape
    k = y.shape[-1]
    e = w_router.shape[1]
    row = lambda i, j: (i, j, 0)
    vec = lambda i, j: (i, 0, 0)
    const2 = lambda i, j: (0, 0)
    return pl.pallas_call(
        _linres_kernel,
        grid=(b, t // tm),
        in_specs=[pl.BlockSpec((1, tm, d), row),
                  pl.BlockSpec((1, 1, d), vec),
                  pl.BlockSpec((1, tm, k), row),
                  pl.BlockSpec((k, d), const2),
                  pl.BlockSpec((1, d), const2),
                  pl.BlockSpec((1, 1, d), vec),
                  pl.BlockSpec((1, 1, d), vec),
                  pl.BlockSpec((3 * d, e), const2)],
        out_specs=[pl.BlockSpec((1, tm, d), row), pl.BlockSpec((1, tm, d), row),
                   pl.BlockSpec((1, tm, e), row)],
        out_shape=[jax.ShapeDtypeStruct((b, t, d), F32), jax.ShapeDtypeStruct((b, t, d), BF16),
                   jax.ShapeDtypeStruct((b, t, e), F32)],
        compiler_params=_cparams(("parallel", "parallel")),
        name="linear_residual",
    )(x, gate, y, w, ffn_gain.reshape(1, d), ffn_scale, ffn_shift, _router_weights(w_router))


L = CHUNK
COL_Q, COL_FF, COL_FB, COL_I, COL_G, COL_Z, COL_X, COL_DT = 0, 512, 1024, 1536, 2048, 2560, 3072, 4096


def _rms(x, gain):
    return x * lax.rsqrt(jnp.mean(x * x, axis=-1, keepdims=True) + EPS) * gain


def _split3(x):
    a = x.astype(BF16)
    r = x - a.astype(F32)
    b = r.astype(BF16)
    c = (r - b.astype(F32)).astype(BF16)
    return a, b, c


def _cumsum_exact(tri3, x):
    a, b, c = _split3(x)
    return jnp.dot(tri3, jnp.concatenate([a, b, c], axis=0), preferred_element_type=F32)


def _expand_exact(x, e3):
    a, b, c = _split3(x)
    return jnp.dot(jnp.concatenate([a, b, c], axis=1), e3, preferred_element_type=F32)


def _dot_nt(a, b):
    return lax.dot_general(a, b, (((1,), (1,)), ((), ())), preferred_element_type=F32)


def _dot_tn(a, b):
    return lax.dot_general(a, b, (((0,), (0,)), ((), ())), preferred_element_type=F32)


def _scan_dir(rev, q_ref, f_ref, v_ref, x_ref, xp_ref, xn_ref, dt_ref, pv, nv,
              lb_row, cw_ref, cbias_ref, dtb_ref, aexp_ref, dskip_ref,
              oh_ref, om_ref, sg_ref, sm_ref, u_scr):
    d = 1 if rev else 0
    ti = lax.broadcasted_iota(jnp.int32, (L, L), 0)
    si = lax.broadcasted_iota(jnp.int32, (L, L), 1)
    keep = (si >= ti) if rev else (si <= ti)
    tri = keep.astype(BF16)
    tri3 = jnp.concatenate([tri, tri, tri], axis=1)
    t2 = lax.broadcasted_iota(jnp.int32, (L, 128), 0)
    s2 = lax.broadcasted_iota(jnp.int32, (L, 128), 1) % L
    keep2 = (s2 >= t2) if rev else (s2 <= t2)
    ref_row = (L - 1 - L // 2) if rev else L // 2
    last_row = 0 if rev else L - 1
    lane = lax.broadcasted_iota(jnp.int32, (1, 128), 1)
    lo = lane < 64
    r128 = lax.broadcasted_iota(jnp.int32, (128, 128), 0)
    c128 = lax.broadcasted_iota(jnp.int32, (128, 128), 1)
    blockdiag = (r128 < 64) == (c128 < 64)
    zero_b = jnp.zeros((), BF16)

    def pair_rows(a):
        return jnp.concatenate([jnp.where(lo, a, zero_b), jnp.where(lo, zero_b, a)], axis=0)

    q = q_ref[0]
    qs = _silu(q)
    f = lb_row + (1.0 - lb_row) * jax.nn.sigmoid(f_ref[0])
    k = 1.0 - f
    la = jnp.log(f)
    b = _cumsum_exact(tri3, la)
    bref = b[ref_row:ref_row + 1, :]
    blast = b[last_row:last_row + 1, :]
    qd = (qs * jnp.exp(b - bref)).astype(BF16)
    kd = (k * jnp.exp(bref - b)).astype(BF16)
    kl = (k * jnp.exp(blast - b)).astype(BF16)
    qb = (qs * jnp.exp(b)).astype(BF16)
    dec_col = jnp.broadcast_to(jnp.exp(blast), (8, 512)).T
    vb = v_ref[0].astype(BF16)
    for p in range(HA // 2):
        sl = slice(128 * p, 128 * (p + 1))
        qd_p, kd_p, kl_p, qb_p, v_p = qd[:, sl], kd[:, sl], kl[:, sl], qb[:, sl], vb[:, sl]
        st = sg_ref[d, p]
        att = _dot_nt(qd_p, pair_rows(kd_p))
        att = jnp.where(keep2, att, 0.0).astype(BF16)
        lhs = jnp.concatenate([att, qb_p], axis=1)
        rhs = jnp.concatenate([pair_rows(v_p), st.astype(BF16)], axis=0)
        oh_ref[0, :, sl] = jnp.dot(lhs, rhs, preferred_element_type=F32).astype(oh_ref.dtype)
        upd = _dot_tn(kl_p, v_p)
        sg_ref[d, p] = st * dec_col[sl, 0:1] + jnp.where(blockdiag, upd, 0.0)

    u_scr[0:8, :] = xp_ref[0] * pv
    u_scr[8:8 + L, :] = x_ref[0]
    u_scr[8 + L:16 + L, :] = xn_ref[0] * nv
    u = u_scr[...]
    y = cbias_ref[...] + cw_ref[2:3, :] * u[8:8 + L]
    for j in (0, 1, 3, 4):
        y = y + cw_ref[j:j + 1, :] * pltpu.roll(u, (2 - j) % (L + 16), axis=0)[8:8 + L]
    xbc = _silu(y)
    xm = xbc[:, :512]
    bm = xbc[:, 512:768].astype(BF16)
    cm = xbc[:, 768:1024].astype(BF16)
    draw = dt_ref[0] + dtb_ref[...]
    dt = jnp.maximum(draw, 0.0) + jnp.log(1.0 + jnp.exp(-jnp.abs(draw)))
    la_m = -dt * aexp_ref[...]
    cum = _cumsum_exact(tri3, la_m)
    er = lax.broadcasted_iota(jnp.int32, (128, 512), 0)
    ec = lax.broadcasted_iota(jnp.int32, (128, 512), 1)
    e1 = (er == (ec // 64) + 8 * d).astype(BF16)
    e1 = jnp.concatenate([e1, e1, e1], axis=0)
    both = _expand_exact(jnp.concatenate([dt, cum], axis=0), e1)
    dt_e, cum_e = both[:L], both[L:]
    cum_t = cum.T
    xdt = xm * dt_e
    clast = cum_e[last_row:last_row + 1, :]
    wx = (jnp.exp(clast - cum_e) * xdt).astype(BF16)
    dec_m = jnp.exp(clast)
    ecum = jnp.exp(cum_e)
    xdt_b = xdt.astype(BF16)
    for g in range(2):
        bm_g = bm[:, 128 * g:128 * (g + 1)]
        cm_g = cm[:, 128 * g:128 * (g + 1)]
        gl = slice(256 * g, 256 * (g + 1))
        cb2 = _dot_nt(cm_g, jnp.concatenate([bm_g, bm_g], axis=0))
        sm = sm_ref[d, g]
        y_int = jnp.dot(cm_g, sm.astype(BF16), preferred_element_type=F32) * ecum[:, gl]
        for pp in range(2):
            h0 = 4 * g + 2 * pp + 8 * d
            sl = slice(256 * g + 128 * pp, 256 * g + 128 * (pp + 1))
            crow = jnp.concatenate([cum_t[h0:h0 + 1, :], cum_t[h0 + 1:h0 + 2, :]], axis=1)
            lm = jnp.exp(jnp.where(keep2, cum_e[:, sl] - crow, -jnp.inf))
            yp = jnp.dot((cb2 * lm).astype(BF16), pair_rows(xdt_b[:, sl]), preferred_element_type=F32)
            yp = yp + y_int[:, 128 * pp:128 * (pp + 1)]
            if not rev:
                yp = yp + dskip_ref[:, sl] * xm[:, sl]
            om_ref[0, :, sl] = yp.astype(om_ref.dtype)
        sm_ref[d, g] = sm * dec_m[:, gl] + _dot_tn(bm_g, wx[:, gl])


def _scan_kernel(qf, ff, vf, xf, xpf, xnf, dtf, qb, fb, vb_, xb, xpb, xnb, dtb,
                 lb_ref, cw_ref, cbias_ref, dtbias_ref, aexp_ref, dskip_ref,
                 ohf, omf, ohb, omb, sg_ref, sm_ref, u_scr, *, nct, nc):
    c = pl.program_id(1)

    @pl.when(c == 0)
    def _():
        sg_ref[...] = jnp.zeros(sg_ref.shape, F32)
        sm_ref[...] = jnp.zeros(sm_ref.shape, F32)

    cb = jnp.where(c < nct, nct - 1 - c, nc + nct - 1 - c)

    def edge_flags(ch):
        pv = jnp.where((ch == 0) | (ch == nct), 0.0, 1.0)
        nv = jnp.where((ch == nct - 1) | (ch == nc - 1), 0.0, 1.0)
        return pv, nv

    pvf, nvf = edge_flags(c)
    pvb, nvb = edge_flags(cb)
    _scan_dir(False, qf, ff, vf, xf, xpf, xnf, dtf, pvf, nvf, lb_ref[0:1, :], cw_ref, cbias_ref,
              dtbias_ref, aexp_ref, dskip_ref, ohf, omf, sg_ref, sm_ref, u_scr)
    _scan_dir(True, qb, fb, vb_, xb, xpb, xnb, dtb, pvb, nvb, lb_ref[1:2, :], cw_ref, cbias_ref,
              dtbias_ref, aexp_ref, dskip_ref, ohb, omb, sg_ref, sm_ref, u_scr)


def _scan_mixers(p, nct, lb, conv_w, conv_b, dt_bias, a_log, d_skip):
    bsz, s, _ = p.shape
    nc = s // L
    nb8 = s // 8

    def bmap(c):
        return jnp.where(c < nct, nct - 1 - c, nc + nct - 1 - c)

    def specs(cmap, fcol):
        colblk = lambda col, w: col // w
        return [
            pl.BlockSpec((1, L, 512), lambda i, c: (i, cmap(c), colblk(COL_Q, 512))),
            pl.BlockSpec((1, L, 512), lambda i, c: (i, cmap(c), colblk(fcol, 512))),
            pl.BlockSpec((1, L, 512), lambda i, c: (i, cmap(c), colblk(COL_I, 512))),
            pl.BlockSpec((1, L, 1024), lambda i, c: (i, cmap(c), colblk(COL_X, 1024))),
            pl.BlockSpec((1, 8, 1024), lambda i, c: (i, jnp.maximum(cmap(c) * 8 - 1, 0), colblk(COL_X, 1024))),
            pl.BlockSpec((1, 8, 1024), lambda i, c: (i, jnp.minimum(cmap(c) * 8 + 8, nb8 - 1), colblk(COL_X, 1024))),
            pl.BlockSpec((1, L, 128), lambda i, c: (i, cmap(c), colblk(COL_DT, 128))),
        ]

    ident = lambda c: c
    const2 = lambda i, c: (0, 0)
    dtb = jnp.zeros((1, 128), F32).at[0, :16].set(dt_bias.reshape(16))
    aexp = jnp.zeros((1, 128), F32).at[0, :16].set(jnp.exp(a_log.reshape(16)))
    dsk = jnp.repeat(d_skip, 64).reshape(1, 512)
    kern = functools.partial(_scan_kernel, nct=nct, nc=nc)
    out_sd = jax.ShapeDtypeStruct((bsz, s, 512), BF16)
    return pl.pallas_call(
        kern,
        grid=(bsz, nc),
        in_specs=specs(ident, COL_FF) + specs(bmap, COL_FB) + [
            pl.BlockSpec((2, 512), const2), pl.BlockSpec((5, 1024), const2),
            pl.BlockSpec((1, 1024), const2), pl.BlockSpec((1, 128), const2),
            pl.BlockSpec((1, 128), const2), pl.BlockSpec((1, 512), const2)],
        out_specs=[pl.BlockSpec((1, L, 512), lambda i, c: (i, c, 0)),
                   pl.BlockSpec((1, L, 512), lambda i, c: (i, c, 0)),
                   pl.BlockSpec((1, L, 512), lambda i, c: (i, bmap(c), 0)),
                   pl.BlockSpec((1, L, 512), lambda i, c: (i, bmap(c), 0))],
        out_shape=[out_sd, out_sd, out_sd, out_sd],
        scratch_shapes=[pltpu.VMEM((2, HA // 2, 128, 128), F32),
                        pltpu.VMEM((2, 2, 128, 256), F32),
                        pltpu.VMEM((L + 16, 1024), F32)],
        compiler_params=_cparams(("parallel", "arbitrary")),
        name="scan_mixers",
    )(*([p] * 14), lb, conv_w, conv_b.reshape(1, 1024), dtb, aexp, dsk)


def _merge_kernel(ohf, ohb, omf, omb, g_ref, z_ref, xc_ref, xl_ref, gate_ref, hn_ref, mn_ref, w_ref,
                  fg_ref, fsc_ref, fsh_ref, rw_ref, oc_ref, ol_ref, h_ref, lg_ref, *, nct_tiles):
    oh = ohf[0].astype(F32) + ohb[0].astype(F32)
    sq = oh * oh
    hi = sq.astype(BF16)
    lo = (sq - hi.astype(F32)).astype(BF16)
    r = lax.broadcasted_iota(jnp.int32, (512, 512), 0) // 64
    c = lax.broadcasted_iota(jnp.int32, (512, 512), 1) // 64
    avg = jnp.where(r == c, 1.0 / 64, 0.0).astype(BF16)
    ms = jnp.dot(hi, avg, preferred_element_type=F32) + jnp.dot(lo, avg, preferred_element_type=F32)
    oh = oh * lax.rsqrt(ms + EPS) * hn_ref[...] * jax.nn.sigmoid(g_ref[0])
    y = (omf[0].astype(F32) + omb[0].astype(F32)) * _silu(z_ref[0])
    y = y * lax.rsqrt(jnp.mean(y * y, axis=-1, keepdims=True) + EPS) * mn_ref[...]
    cat = jnp.concatenate([oh, y], axis=1).astype(BF16)
    j = pl.program_id(1)
    x = jnp.where(j < nct_tiles, xc_ref[0], xl_ref[0])
    out = x + gate_ref[0, 0] * jnp.dot(cat, w_ref[...], preferred_element_type=F32)
    _router_body(out, fg_ref[...], fsc_ref[0, 0], fsh_ref[0, 0], rw_ref, h_ref, lg_ref)

    @pl.when(j < nct_tiles)
    def _():
        oc_ref[0] = out

    @pl.when(j >= nct_tiles)
    def _():
        ol_ref[0] = out


def _merge_mixers(x_ctx, x_lat, gate2, ohf, ohb, omf, omb, p, hgrn_norm, mamba_norm, w_out,
                  ffn_gain, ffn_scale2, ffn_shift2, w_router, tm):
    b, tc, d = x_ctx.shape
    s = tc + x_lat.shape[1]
    nct_tiles = tc // tm
    e = w_router.shape[1]
    sel = lambda i, j: (i, jnp.where(j < nct_tiles, 0, 1), 0, 0)
    row = lambda i, j: (i, j, 0)
    const2 = lambda i, j: (0, 0)
    return pl.pallas_call(
        functools.partial(_merge_kernel, nct_tiles=nct_tiles),
        grid=(b, s // tm),
        in_specs=[pl.BlockSpec((1, tm, 512), row)] * 4 + [
            pl.BlockSpec((1, tm, 512), lambda i, j: (i, j, COL_G // 512)),
            pl.BlockSpec((1, tm, 512), lambda i, j: (i, j, COL_Z // 512))] + _ctx_lat_specs(
                tm, d, nct_tiles) + [
            pl.BlockSpec((1, 1, 1, d), sel),
            pl.BlockSpec((1, 512), const2), pl.BlockSpec((1, 512), const2),
            pl.BlockSpec((2 * 512, d), const2),
            pl.BlockSpec((1, d), const2),
            pl.BlockSpec((1, 1, 1, d), sel), pl.BlockSpec((1, 1, 1, d), sel),
            pl.BlockSpec((3 * d, e), const2)],
        out_specs=[pl.BlockSpec((1, tm, d), lambda i, j: (i, jnp.minimum(j, nct_tiles - 1), 0)),
                   pl.BlockSpec((1, tm, d), lambda i, j: (i, jnp.maximum(j - nct_tiles, 0), 0)),
                   pl.BlockSpec((1, tm, d), row), pl.BlockSpec((1, tm, e), row)],
        out_shape=[jax.ShapeDtypeStruct((b, tc, d), F32),
                   jax.ShapeDtypeStruct((b, s - tc, d), F32),
                   jax.ShapeDtypeStruct((b, s, d), BF16),
                   jax.ShapeDtypeStruct((b, s, e), F32)],
        compiler_params=_cparams(("parallel", "arbitrary")),
        name="merge_mixers",
    )(ohf, ohb, omf, omb, p, p, x_ctx, x_lat, gate2.reshape(b, 2, 1, d), hgrn_norm.reshape(1, 512),
      mamba_norm.reshape(1, 512), w_out, ffn_gain.reshape(1, d), ffn_scale2.reshape(b, 2, 1, d),
      ffn_shift2.reshape(b, 2, 1, d), _router_weights(w_router))


def _rope_tables(t):
    rows = t // GRID_W
    row = jnp.repeat(jnp.arange(rows, dtype=jnp.int32), GRID_W)
    col = jnp.tile(jnp.arange(GRID_W, dtype=jnp.int32), rows)
    nf = ROPE // 4
    inv_freq = ROPE_THETA ** (-jnp.arange(nf, dtype=F32) / nf)
    pos = jnp.stack([row, col], axis=-1).astype(F32)
    ang = pos[..., None] * inv_freq
    cos, sin = jnp.cos(ang), jnp.sin(ang)
    cos64 = jnp.broadcast_to(cos[:, :, None, :], (t, 2, 2, nf)).reshape(t, ROPE)
    sin64 = jnp.broadcast_to(sin[:, :, None, :], (t, 2, 2, nf)).reshape(t, ROPE)
    zero = jnp.zeros((t, ROPE), F32)
    return jnp.concatenate([cos64, zero], -1), jnp.concatenate([sin64, zero], -1)


def _swap_cols(w):
    nf = ROPE // 4
    wr = w.reshape(w.shape[:-1] + (2, 2, nf))
    return jnp.stack([-wr[..., 1, :], wr[..., 0, :]], axis=-2).reshape(w.shape)


def _rope_tile(tile, cos_t, sin_t):
    return tile * cos_t + pltpu.roll(tile, ROPE, axis=1) * sin_t


def _mla_down_kernel(x_ref, g_ref, sc_ref, sh_ref, w_ref, qn_ref, kvn_ref, cos_ref, sin_ref,
                     cq_ref, ckv_ref, kr_ref):
    a = _modnorm(x_ref[0], g_ref[...], sc_ref[0], sh_ref[0]).astype(BF16)
    c = jnp.dot(a, w_ref[...], preferred_element_type=F32)
    cq = c[:, :Q_LORA]
    ckv = c[:, Q_LORA:Q_LORA + KV_LORA]
    cq_ref[0] = _rms(cq, qn_ref[...]).astype(BF16)
    ckv_ref[0] = _rms(ckv, kvn_ref[...]).astype(BF16)
    kr_ref[0] = _rope_tile(c[:, Q_LORA + KV_LORA:], cos_ref[...], sin_ref[...]).astype(BF16)


def _mla_down(x, gain, scale, shift, w_cat, q_norm, kv_norm, cos_t, sin_t, tm):
    b, t, d = x.shape
    n = w_cat.shape[1]
    return pl.pallas_call(
        _mla_down_kernel,
        grid=(b, t // tm),
        in_specs=[pl.BlockSpec((1, tm, d), lambda i, j: (i, j, 0)),
                  pl.BlockSpec((1, d), lambda i, j: (0, 0)),
                  pl.BlockSpec((1, 1, d), lambda i, j: (i, 0, 0)),
                  pl.BlockSpec((1, 1, d), lambda i, j: (i, 0, 0)),
                  pl.BlockSpec((d, n), lambda i, j: (0, 0)),
                  pl.BlockSpec((1, Q_LORA), lambda i, j: (0, 0)),
                  pl.BlockSpec((1, KV_LORA), lambda i, j: (0, 0)),
                  pl.BlockSpec((tm, 128), lambda i, j: (j, 0)),
                  pl.BlockSpec((tm, 128), lambda i, j: (j, 0))],
        out_specs=[pl.BlockSpec((1, tm, Q_LORA), lambda i, j: (i, j, 0)),
                   pl.BlockSpec((1, tm, KV_LORA), lambda i, j: (i, j, 0)),
                   pl.BlockSpec((1, tm, 128), lambda i, j: (i, j, 0))],
        out_shape=[jax.ShapeDtypeStruct((b, t, Q_LORA), BF16),
                   jax.ShapeDtypeStruct((b, t, KV_LORA), BF16),
                   jax.ShapeDtypeStruct((b, t, 128), BF16)],
        compiler_params=_cparams(("parallel", "parallel")),
        name="mla_down",
    )(x, gain.reshape(1, d), scale, shift, w_cat, q_norm.reshape(1, -1), kv_norm.reshape(1, -1),
      cos_t, sin_t)


def _q_up_kernel(cq_ref, w_ref, cos_ref, sin_ref, q_ref):
    cq = cq_ref[0]
    for h in range(H_C):
        q = jnp.dot(cq, w_ref[h], preferred_element_type=F32)
        qn = q[:, :NOPE] * QK_SCALE
        qr = _rope_tile(q[:, NOPE:], cos_ref[...], sin_ref[...]) * QK_SCALE
        q_ref[0, h] = jnp.concatenate([qn, qr], axis=1).astype(BF16)


def _q_up(cq, w_uq_h, cos_t, sin_t, tm):
    b, t, r = cq.shape
    return pl.pallas_call(
        _q_up_kernel,
        grid=(b, t // tm),
        in_specs=[pl.BlockSpec((1, tm, r), lambda i, j: (i, j, 0)),
                  pl.BlockSpec((H_C, r, QK_DIM), lambda i, j: (0, 0, 0)),
                  pl.BlockSpec((tm, 128), lambda i, j: (j, 0)),
                  pl.BlockSpec((tm, 128), lambda i, j: (j, 0))],
        out_specs=pl.BlockSpec((1, H_C, tm, QK_DIM), lambda i, j: (i, 0, j, 0)),
        out_shape=jax.ShapeDtypeStruct((b, H_C, t, QK_DIM), BF16),
        compiler_params=_cparams(("parallel", "parallel")),
        name="q_up",
    )(cq, w_uq_h, cos_t, sin_t)


def _kv_up_kernel(ckv_ref, kr_ref, w_ref, k_ref, v_ref):
    ckv = ckv_ref[0]
    kr = kr_ref[0]
    for h in range(H_C):
        kv = jnp.dot(ckv, w_ref[h], preferred_element_type=F32)
        k_ref[0, h] = jnp.concatenate([kv[:, :NOPE].astype(BF16), kr], axis=1)
        v_ref[0, h] = kv[:, NOPE:].astype(BF16)


def _kv_up(ckv, kr, w_ukv_h, tm):
    b, s, r = ckv.shape
    return pl.pallas_call(
        _kv_up_kernel,
        grid=(b, s // tm),
        in_specs=[pl.BlockSpec((1, tm, r), lambda i, j: (i, j, 0)),
                  pl.BlockSpec((1, tm, 128), lambda i, j: (i, j, 0)),
                  pl.BlockSpec((H_C, r, NOPE + VH), lambda i, j: (0, 0, 0))],
        out_specs=[pl.BlockSpec((1, H_C, tm, QK_DIM), lambda i, j: (i, 0, j, 0)),
                   pl.BlockSpec((1, H_C, tm, VH), lambda i, j: (i, 0, j, 0))],
        out_shape=[jax.ShapeDtypeStruct((b, H_C, s, QK_DIM), BF16),
                   jax.ShapeDtypeStruct((b, H_C, s, VH), BF16)],
        compiler_params=_cparams(("parallel", "parallel")),
        name="kv_up",
    )(ckv, kr, w_ukv_h)


def _attn_kernel(qa_ref, qb_ref, q0_ref, k_ref, v_ref, o_ref, s0_scr, s1_scr, m_scr, *, tk, nk, tq):
    n = pl.program_id(2)

    def scores(q, j):
        ks = k_ref[0, 0, j * tk:(j + 1) * tk, :]
        return lax.dot_general(q, ks, (((1,), (1,)), ((), ())), preferred_element_type=F32)

    def fold_max(mrun, s):
        for c in range(tk // 128):
            mrun = jnp.maximum(mrun, s[:, c * 128:(c + 1) * 128])
        return mrun

    @pl.when(n == 0)
    def _():
        q0 = q0_ref[0, 0]
        mrun = jnp.full((tq, 128), -jnp.inf, F32)
        for j in range(nk):
            s = scores(q0, j)
            s0_scr[:, j * tk:(j + 1) * tk] = s
            mrun = fold_max(mrun, s)
        m_scr[...] = jnp.max(mrun, axis=-1, keepdims=True)

    def fused(qn, s_read, s_write, m):
        mrun = jnp.full((tq, 128), -jnp.inf, F32)
        lrun = jnp.zeros((tq, 128), F32)
        acc = jnp.zeros((tq, VH), F32)
        for j in range(nk):
            sn = scores(qn, j)
            s_write[:, j * tk:(j + 1) * tk] = sn
            mrun = fold_max(mrun, sn)
            p = jnp.exp2(s_read[:, j * tk:(j + 1) * tk] - m)
            for c in range(tk // 128):
                lrun = lrun + p[:, c * 128:(c + 1) * 128]
            acc = acc + jnp.dot(p.astype(BF16), v_ref[0, 0, j * tk:(j + 1) * tk, :],
                                preferred_element_type=F32)
        out = acc / jnp.sum(lrun, axis=-1, keepdims=True)
        return out, jnp.max(mrun, axis=-1, keepdims=True)

    out_a, m1 = fused(qa_ref[0, 0], s0_scr, s1_scr, m_scr[...])
    o_ref[0, 0:tq, :] = out_a.astype(o_ref.dtype)
    out_b, m2 = fused(qb_ref[0, 0], s1_scr, s0_scr, m1)
    o_ref[0, tq:2 * tq, :] = out_b.astype(o_ref.dtype)
    m_scr[...] = m2


def _attention(q, k, v, tq, tk):
    b, h, t, _ = q.shape
    s = k.shape[2]
    nq = t // tq
    assert t % (2 * tq) == 0 and s % tk == 0
    kern = functools.partial(_attn_kernel, tk=tk, nk=s // tk, tq=tq)
    return pl.pallas_call(
        kern,
        grid=(b, h, nq // 2),
        in_specs=[pl.BlockSpec((1, 1, tq, QK_DIM), lambda i, j, n: (i, j, 2 * n + 1, 0)),
                  pl.BlockSpec((1, 1, tq, QK_DIM),
                               lambda i, j, n: (i, j, jnp.minimum(2 * n + 2, nq - 1), 0)),
                  pl.BlockSpec((1, 1, tq, QK_DIM), lambda i, j, n: (i, j, 0, 0)),
                  pl.BlockSpec((1, 1, s, QK_DIM), lambda i, j, n: (i, j, 0, 0)),
                  pl.BlockSpec((1, 1, s, VH), lambda i, j, n: (i, j, 0, 0))],
        out_specs=pl.BlockSpec((1, 2 * tq, VH), lambda i, j, n: (i, n, j)),
        out_shape=jax.ShapeDtypeStruct((b, t, h * VH), BF16),
        scratch_shapes=[pltpu.VMEM((tq, s), F32), pltpu.VMEM((tq, s), F32),
                        pltpu.VMEM((tq, 1), F32)],
        compiler_params=_cparams(("parallel", "parallel", "arbitrary")),
        name="attention",
    )(q, q, q, k, v)


def _ffn_kernel(x_ref, wg_ref, wu_ref, wd_ref, g_ref, o_ref, acc_scr, wgb, wub, wdb, *, fc):
    @pl.when((pl.program_id(1) == 0) & (pl.program_id(2) == 0))
    def _():
        wgb[...] = wg_ref[0, 0].astype(BF16)
        wub[...] = wu_ref[0, 0].astype(BF16)
        wdb[...] = wd_ref[0, 0].astype(BF16)

    x = x_ref[0, 0]
    nf = wgb.shape[-1]
    for c in range(nf // fc):
        sl = slice(c * fc, (c + 1) * fc)
        hg = jnp.dot(x, wgb[:, sl], preferred_element_type=F32)
        hu = jnp.dot(x, wub[:, sl], preferred_element_type=F32)
        hid = (_silu(hg) * hu).astype(BF16)
        part = jnp.dot(hid, wdb[sl, :], preferred_element_type=F32)
        if c == 0:
            acc_scr[...] = part
        else:
            acc_scr[...] += part
    o_ref[0, 0] = (acc_scr[...] * g_ref[0, 0]).astype(o_ref.dtype)


def _expert_ffn(xg, gates, w_gate, w_up, w_down, layer, tm, fc):
    b, e, cap, d = xg.shape
    f = w_gate.shape[-1]
    tm = min(tm, cap)
    kern = functools.partial(_ffn_kernel, fc=fc)
    return pl.pallas_call(
        kern,
        grid=(e, b, cap // tm),
        in_specs=[pl.BlockSpec((1, 1, tm, d), lambda j, i, m: (i, j, m, 0)),
                  pl.BlockSpec((1, 1, d, f), lambda j, i, m: (layer, j, 0, 0)),
                  pl.BlockSpec((1, 1, d, f), lambda j, i, m: (layer, j, 0, 0)),
                  pl.BlockSpec((1, 1, f, d), lambda j, i, m: (layer, j, 0, 0)),
                  pl.BlockSpec((1, 1, tm, 1), lambda j, i, m: (i, j, m, 0))],
        out_specs=pl.BlockSpec((1, 1, tm, d), lambda j, i, m: (i, j, m, 0)),
        out_shape=jax.ShapeDtypeStruct((b, e, cap, d), BF16),
        scratch_shapes=[pltpu.VMEM((tm, d), F32), pltpu.VMEM((d, f), BF16),
                        pltpu.VMEM((d, f), BF16), pltpu.VMEM((f, d), BF16)],
        compiler_params=_cparams(("arbitrary", "arbitrary", "arbitrary")),
        name="expert_ffn",
    )(xg, w_gate, w_up, w_down, gates)


SLOT_WIN = 256
TOK_BLK = 128
COMBINE_EXPERTS = 4


def _combine_kernel(lo_ref, idx_ref, y_ref, x_ref, gate_ref, fg_ref, o_ref, *, nblk, nrow,
                    final_norm):
    b, j, eg = pl.program_id(0), pl.program_id(1), pl.program_id(2)

    @pl.when(eg == 0)
    def _():
        o_ref[...] = jnp.zeros(o_ref.shape, F32)

    tok = lax.broadcasted_iota(jnp.int32, (TOK_BLK, 1), 0)
    for blk in range(nblk):
        gblk = j * nblk + blk
        part = None
        for k in range(COMBINE_EXPERTS):
            w = jnp.minimum(lo_ref[b, eg * COMBINE_EXPERTS + k, gblk] // TOK_BLK, nrow - 2)
            ids = jnp.concatenate([idx_ref[0, k, pl.ds(w, 1), :],
                                   idx_ref[0, k, pl.ds(w + 1, 1), :]], axis=1)
            onehot = (ids == tok + gblk * TOK_BLK).astype(BF16)
            yw = y_ref[0, k, pl.ds(pl.multiple_of(w * TOK_BLK, TOK_BLK), SLOT_WIN), :]
            d = jnp.dot(onehot, yw, preferred_element_type=F32)
            part = d if part is None else part + d
        rows = slice(blk * TOK_BLK, (blk + 1) * TOK_BLK)
        o_ref[0, rows, :] += part

    @pl.when(eg == pl.num_programs(2) - 1)
    def _():
        out = x_ref[0] + gate_ref[0] * o_ref[0]
        o_ref[0] = _rms(out, fg_ref[...]) if final_norm else out


def _combine(x, gate_out, idx_s, y, final_gain=None):
    b, t, d = x.shape
    e, cap = idx_s.shape[1:]
    capp = max(cap, SLOT_WIN)
    if capp != cap:
        idx_s = jnp.pad(idx_s, ((0, 0), (0, 0), (0, capp - cap)), constant_values=-1)
        y = jnp.pad(y[:, :, :cap], ((0, 0), (0, 0), (0, capp - cap), (0, 0)))
    nrow = capp // TOK_BLK
    ts = min(2048, t)
    nblk = ts // TOK_BLK
    bounds = jnp.arange(t // TOK_BLK, dtype=jnp.int32) * TOK_BLK
    valid = idx_s >= 0
    lo = jnp.sum((valid[..., None] & (idx_s[..., None] < bounds)).astype(jnp.int32), axis=2)
    kern = functools.partial(_combine_kernel, nblk=nblk, nrow=nrow,
                             final_norm=final_gain is not None)
    fg = jnp.ones((1, d), F32) if final_gain is None else final_gain.reshape(1, d)
    grid_spec = pltpu.PrefetchScalarGridSpec(
        num_scalar_prefetch=1,
        grid=(b, t // ts, e // COMBINE_EXPERTS),
        in_specs=[pl.BlockSpec((1, COMBINE_EXPERTS, nrow, TOK_BLK), lambda i, j, k, lo_r: (i, k, 0, 0)),
                  pl.BlockSpec((1, COMBINE_EXPERTS, capp, d), lambda i, j, k, lo_r: (i, k, 0, 0)),
                  pl.BlockSpec((1, ts, d), lambda i, j, k, lo_r: (i, j, 0)),
                  pl.BlockSpec((1, 1, d), lambda i, j, k, lo_r: (i, 0, 0)),
                  pl.BlockSpec((1, d), lambda i, j, k, lo_r: (0, 0))],
        out_specs=pl.BlockSpec((1, ts, d), lambda i, j, k, lo_r: (i, j, 0)),
    )
    return pl.pallas_call(
        kern,
        grid_spec=grid_spec,
        out_shape=jax.ShapeDtypeStruct((b, t, d), F32),
        compiler_params=_cparams(("parallel", "parallel", "arbitrary")),
        name="moe_combine",
    )(lo, idx_s.reshape(b, e, nrow, TOK_BLK), y, x, gate_out, fg)


def _route(logits):
    t = logits.shape[1]
    cap = CAP_FACTOR * t // N_EXPERTS
    aff = jax.nn.softmax(logits, axis=-1)
    g, idx = lax.top_k(jnp.swapaxes(aff, 1, 2), cap)
    return lax.sort((idx, g), dimension=2, num_keys=1)


def _moe(x, hn, logits, gate_out, w_gate, w_up, w_down, layer, final_gain=None):
    b, t, d = x.shape
    idx_s, g_s = _route(logits)
    bidx = jnp.arange(b)[:, None, None]
    xg = hn[bidx, idx_s]
    y = _expert_ffn(xg, g_s[..., None], w_gate, w_up, w_down, layer, 512, 512)
    return _combine(x, gate_out, idx_s, y, final_gain)


def _moe_two_streams(x_ctx, x_lat, hn, logits, gate_ctx, gate_lat, w_gate, w_up, w_down, layer):
    b, tc, d = x_ctx.shape
    idx_c, g_c = _route(logits[:, :tc])
    idx_l, g_l = _route(logits[:, tc:])
    idx_cat = jnp.concatenate([idx_l + tc, idx_c], axis=2)
    g_cat = jnp.concatenate([g_l, g_c], axis=2)
    bidx = jnp.arange(b)[:, None, None]
    xg = hn[bidx, idx_cat]
    cap = idx_cat.shape[2]
    tm = next(m for m in (256, 352, 176, 96, 32, 16) if cap % m == 0)
    y = _expert_ffn(xg, g_cat[..., None], w_gate, w_up, w_down, layer, tm, 512)
    new_lat = _combine(x_lat, gate_lat, idx_l, y)
    new_ctx = _combine(x_ctx, gate_ctx, idx_c, y[:, :, idx_l.shape[2]:])
    return new_ctx, new_lat


def kernel(x, c, ctx, c_ctx, w_mod, b_mod, norm_mix, norm_ffn, norm_out, w_in, w_out_rec,
           conv_w, conv_b, lb_gamma, dt_bias, a_log, d_skip, hgrn_norm, mamba_norm,
           w_dq, q_norm, w_uq, w_dkv, kv_norm, w_ukv, w_kr, w_o,
           w_router, w_gate, w_up, w_down):
    B, T, D = x.shape
    TC = ctx.shape[1]
    lb = jnp.cumsum(jax.nn.softmax(lb_gamma.astype(F32), axis=0), axis=0)

    cvec = jnp.zeros((16, D), F32).at[:B].set(c).at[B].set(c_ctx)
    mods = _modulation(cvec, w_mod, b_mod)

    x_lat, x_ctx = x, ctx
    for l in range(DEPTH):
        need_ctx = l < DEPTH - 1
        m_lat = [mods[l, :B, i * D:(i + 1) * D][:, None, :] for i in range(6)]
        m_ctx = [jnp.broadcast_to(mods[l, B, i * D:(i + 1) * D][None, None, :], (B, 1, D))
                 for i in range(6)]
        if l % 2 == 0:
            e = l // 2
            w_in_p = jnp.pad(w_in[e], ((0, 0), (0, IN_COLS_PAD - IN_COLS))).astype(BF16)
            pair = lambda i: jnp.concatenate([m_ctx[i], m_lat[i]], axis=1)
            p = _modproj(x_ctx, x_lat, norm_mix[l], pair(1), pair(0), w_in_p, 256)
            ohf, omf, ohb, omb = _scan_mixers(p, TC // CHUNK, lb[e], conv_w[e], conv_b[e],
                                              dt_bias[e], a_log[e], d_skip[e])
            x_ctx, x_lat, hn, logits = _merge_mixers(
                x_ctx, x_lat, pair(2), ohf, ohb, omf, omb, p, hgrn_norm[e], mamba_norm[e],
                w_out_rec[e].astype(BF16), norm_ffn[l], pair(4), pair(3), w_router[l], 256)
            if not need_ctx:
                hn, logits = hn[:, TC:], logits[:, TC:]
        else:
            assert not need_ctx
            j = l // 2
            cos_l, sin_l = _rope_tables(T)
            cos_c = jnp.concatenate([jnp.ones((TC, ROPE), F32), jnp.zeros((TC, ROPE), F32)], -1)
            sin_c = jnp.zeros((TC, 128), F32)
            w_cat = jnp.concatenate([w_dq[j], w_dkv[j], w_kr[j], _swap_cols(w_kr[j])],
                                    axis=1).astype(BF16)
            wq = w_uq[j].reshape(Q_LORA, H_C, NOPE + ROPE)
            wq_h = jnp.concatenate([wq, _swap_cols(wq[..., NOPE:])], axis=-1)
            wq_h = jnp.transpose(wq_h, (1, 0, 2)).astype(BF16)
            wkv_h = jnp.transpose(w_ukv[j].reshape(KV_LORA, H_C, NOPE + VH), (1, 0, 2)).astype(BF16)
            cq_l, ckv_l, kr_l = _mla_down(x_lat, norm_mix[l], m_lat[1], m_lat[0], w_cat,
                                          q_norm[j], kv_norm[j], cos_l, sin_l, 512)
            _, ckv_c, kr_c = _mla_down(x_ctx, norm_mix[l], m_ctx[1], m_ctx[0], w_cat,
                                       q_norm[j], kv_norm[j], cos_c, sin_c, 256)
            ckv = jnp.concatenate([ckv_c, ckv_l], axis=1)
            kr = jnp.concatenate([kr_c, kr_l], axis=1)
            qh = _q_up(cq_l, wq_h, cos_l, sin_l, 512)
            kh, vh = _kv_up(ckv, kr, wkv_h, 768)
            o = _attention(qh, kh, vh, 256, 768)
            x_lat, hn, logits = _linear_residual(x_lat, m_lat[2], o, w_o[j].astype(BF16),
                                                 norm_ffn[l], m_lat[4], m_lat[3], w_router[l], 512)
        if need_ctx:
            x_ctx, x_lat = _moe_two_streams(x_ctx, x_lat, hn, logits, m_ctx[5], m_lat[5],
                                            w_gate, w_up, w_down, l)
        else:
            x_lat = _moe(x_lat, hn, logits, m_lat[5], w_gate, w_up, w_down, l, norm_out)
    return x_lat
```

```python
import functools
import math

import jax
import jax.numpy as jnp
from jax import lax
from jax.experimental import pallas as pl
from jax.experimental.pallas import tpu as pltpu

F32 = jnp.float32
BF16 = jnp.bfloat16

D_MODEL = 1024
DEPTH = 2
GRID_W = 64
EPS = 1e-6
CHUNK = 64
HA, DKA, DVA = 8, 64, 64
WA = HA * DVA
HB, PB = 8, 64
D_INNER = HB * PB
N_GROUPS, D_STATE = 2, 128
CONV_W = 5
REC_SPLITS = (HA * DKA, HA * DKA, HA * DKA, WA, WA, D_INNER, D_INNER,
              N_GROUPS * D_STATE, N_GROUPS * D_STATE, HB, HB)
IN_COLS = sum(REC_SPLITS)
IN_COLS_PAD = 4224
H_C, NOPE, ROPE, VH = 8, 128, 64, 128
Q_LORA, KV_LORA = 384, 256
ROPE_THETA = 10000.0
ATTN_SCALE = 1.0 / math.sqrt(NOPE + ROPE)
QK_SCALE = ATTN_SCALE * math.log2(math.e)
QK_DIM = 256
N_EXPERTS = 16
EXPERT_FF = 1024
CAP_FACTOR = 2

VMEM_LIMIT = 56 * 1024 * 1024


def _cparams(sem):
    return pltpu.CompilerParams(dimension_semantics=sem, vmem_limit_bytes=VMEM_LIMIT)


def _pick_tile(n, *candidates):
    return next(c for c in candidates if n % c == 0)


def _silu(v):
    return v * jax.nn.sigmoid(v)


def _modnorm(x, gain, scale, shift):
    ms = jnp.mean(x * x, axis=-1, keepdims=True)
    return (x * lax.rsqrt(ms + EPS) * gain) * (1.0 + scale) + shift


def _mod_kernel(s_ref, w_ref, b_ref, o_ref):
    s = _silu(s_ref[...])
    o_ref[0] = jnp.dot(s, w_ref[0], preferred_element_type=F32,
                       precision=lax.Precision.HIGHEST) + b_ref[0]


def _modulation(cvec, w_mod, b_mod):
    n = w_mod.shape[-1]
    tn = 1536
    return pl.pallas_call(
        _mod_kernel,
        grid=(DEPTH, n // tn),
        in_specs=[pl.BlockSpec((16, D_MODEL), lambda l, j: (0, 0)),
                  pl.BlockSpec((1, D_MODEL, tn), lambda l, j: (l, 0, j)),
                  pl.BlockSpec((1, 1, tn), lambda l, j: (l, 0, j))],
        out_specs=pl.BlockSpec((1, 16, tn), lambda l, j: (l, 0, j)),
        out_shape=jax.ShapeDtypeStruct((DEPTH, 16, n), F32),
        compiler_params=_cparams(("parallel", "parallel")),
        name="modulation",
    )(cvec, w_mod, b_mod.reshape(DEPTH, 1, n))


def _ctx_lat_specs(tm, d, nct_tiles):
    return [pl.BlockSpec((1, tm, d), lambda i, j: (i, jnp.minimum(j, nct_tiles - 1), 0)),
            pl.BlockSpec((1, tm, d), lambda i, j: (i, jnp.maximum(j - nct_tiles, 0), 0))]


def _modproj_kernel(xc_ref, xl_ref, g_ref, sc_ref, sh_ref, w_ref, o_ref, *, nct_tiles):
    x = jnp.where(pl.program_id(1) < nct_tiles, xc_ref[0], xl_ref[0])
    a = _modnorm(x, g_ref[...], sc_ref[0, 0], sh_ref[0, 0]).astype(BF16)
    o_ref[0] = jnp.dot(a, w_ref[...], preferred_element_type=F32).astype(o_ref.dtype)


def _modproj(x_ctx, x_lat, gain, scale2, shift2, w, tm, out_dtype=F32):
    b, tc, d = x_ctx.shape
    t = tc + x_lat.shape[1]
    nct_tiles = tc // tm
    n = w.shape[1]
    sel = lambda i, j: (i, jnp.where(j < nct_tiles, 0, 1), 0, 0)
    return pl.pallas_call(
        functools.partial(_modproj_kernel, nct_tiles=nct_tiles),
        grid=(b, t // tm),
        in_specs=_ctx_lat_specs(tm, d, nct_tiles) + [
                  pl.BlockSpec((1, d), lambda i, j: (0, 0)),
                  pl.BlockSpec((1, 1, 1, d), sel),
                  pl.BlockSpec((1, 1, 1, d), sel),
                  pl.BlockSpec((d, n), lambda i, j: (0, 0))],
        out_specs=pl.BlockSpec((1, tm, n), lambda i, j: (i, j, 0)),
        out_shape=jax.ShapeDtypeStruct((b, t, n), out_dtype),
        compiler_params=_cparams(("parallel", "parallel")),
        name="modproj",
    )(x_ctx, x_lat, gain.reshape(1, d), scale2.reshape(b, 2, 1, d), shift2.reshape(b, 2, 1, d), w)


def _router_body(x, gain, scale, shift, w_ref, h_ref, lg_ref):
    hn = _modnorm(x, gain, scale, shift)
    a = hn.astype(BF16)
    h_ref[0] = a
    b = (hn - a.astype(F32)).astype(BF16)
    lg_ref[0] = jnp.dot(jnp.concatenate([a, b, a], axis=1), w_ref[...], preferred_element_type=F32)


def _router_weights(w_router):
    w_hi = w_router.astype(BF16)
    w_lo = (w_router - w_hi.astype(F32)).astype(BF16)
    return jnp.concatenate([w_hi, w_hi, w_lo], axis=0)


def _linres_kernel(x_ref, gate_ref, y_ref, w_ref, fg_ref, fsc_ref, fsh_ref, rw_ref,
                   o_ref, h_ref, lg_ref):
    acc = jnp.dot(y_ref[0].astype(BF16), w_ref[...], preferred_element_type=F32)
    out = x_ref[0] + gate_ref[0] * acc
    o_ref[0] = out
    _router_body(out, fg_ref[...], fsc_ref[0], fsh_ref[0], rw_ref, h_ref, lg_ref)


def _linear_residual(x, gate, y, w, ffn_gain, ffn_scale, ffn_shift, w_router, tm):
    b, t, d = x.shape
    k = y.shape[-1]
    e = w_router.shape[1]
    row = lambda i, j: (i, j, 0)
    vec = lambda i, j: (i, 0, 0)
    const2 = lambda i, j: (0, 0)
    return pl.pallas_call(
        _linres_kernel,
        grid=(b, t // tm),
        in_specs=[pl.BlockSpec((1, tm, d), row),
                  pl.BlockSpec((1, 1, d), vec),
                  pl.BlockSpec((1, tm, k), row),
                  pl.BlockSpec((k, d), const2),
                  pl.BlockSpec((1, d), const2),
                  pl.BlockSpec((1, 1, d), vec),
                  pl.BlockSpec((1, 1, d), vec),
                  pl.BlockSpec((3 * d, e), const2)],
        out_specs=[pl.BlockSpec((1, tm, d), row), pl.BlockSpec((1, tm, d), row),
                   pl.BlockSpec((1, tm, e), row)],
        out_shape=[jax.ShapeDtypeStruct((b, t, d), F32), jax.ShapeDtypeStruct((b, t, d), BF16),
                   jax.ShapeDtypeStruct((b, t, e), F32)],
        compiler_params=_cparams(("parallel", "parallel")),
        name="linear_residual",
    )(x, gate, y, w, ffn_gain.reshape(1, d), ffn_scale, ffn_shift, _router_weights(w_router))


L = CHUNK
COL_Q, COL_FF, COL_FB, COL_I, COL_G, COL_Z, COL_X, COL_DT = 0, 512, 1024, 1536, 2048, 2560, 3072, 4096


def _rms(x, gain):
    return x * lax.rsqrt(jnp.mean(x * x, axis=-1, keepdims=True) + EPS) * gain


def _split3(x):
    a = x.astype(BF16)
    r = x - a.astype(F32)
    b = r.astype(BF16)
    c = (r - b.astype(F32)).astype(BF16)
    return a, b, c


def _cumsum_exact(tri3, x):
    a, b, c = _split3(x)
    return jnp.dot(tri3, jnp.concatenate([a, b, c], axis=0), preferred_element_type=F32)


def _expand_exact(x, e3):
    a, b, c = _split3(x)
    return jnp.dot(jnp.concatenate([a, b, c], axis=1), e3, preferred_element_type=F32)


def _dot_nt(a, b):
    return lax.dot_general(a, b, (((1,), (1,)), ((), ())), preferred_element_type=F32)


def _dot_tn(a, b):
    return lax.dot_general(a, b, (((0,), (0,)), ((), ())), preferred_element_type=F32)


def _scan_dir(rev, q_ref, f_ref, v_ref, x_ref, xp_ref, xn_ref, dt_ref, pv, nv,
              lb_row, cw_ref, cbias_ref, dtb_ref, aexp_ref, dskip_ref,
              oh_ref, om_ref, sg_ref, sm_ref, u_scr):
    d = 1 if rev else 0
    ti = lax.broadcasted_iota(jnp.int32, (L, L), 0)
    si = lax.broadcasted_iota(jnp.int32, (L, L), 1)
    keep = (si >= ti) if rev else (si <= ti)
    tri = keep.astype(BF16)
    tri3 = jnp.concatenate([tri, tri, tri], axis=1)
    t2 = lax.broadcasted_iota(jnp.int32, (L, 128), 0)
    s2 = lax.broadcasted_iota(jnp.int32, (L, 128), 1) % L
    keep2 = (s2 >= t2) if rev else (s2 <= t2)
    ref_row = (L - 1 - L // 2) if rev else L // 2
    last_row = 0 if rev else L - 1
    lane = lax.broadcasted_iota(jnp.int32, (1, 128), 1)
    lo = lane < 64
    r128 = lax.broadcasted_iota(jnp.int32, (128, 128), 0)
    c128 = lax.broadcasted_iota(jnp.int32, (128, 128), 1)
    blockdiag = (r128 < 64) == (c128 < 64)
    zero_b = jnp.zeros((), BF16)

    def pair_rows(a):
        return jnp.concatenate([jnp.where(lo, a, zero_b), jnp.where(lo, zero_b, a)], axis=0)

    q = q_ref[0]
    qs = _silu(q)
    f = lb_row + (1.0 - lb_row) * jax.nn.sigmoid(f_ref[0])
    k = 1.0 - f
    la = jnp.log(f)
    b = _cumsum_exact(tri3, la)
    bref = b[ref_row:ref_row + 1, :]
    blast = b[last_row:last_row + 1, :]
    qd = (qs * jnp.exp(b - bref)).astype(BF16)
    kd = (k * jnp.exp(bref - b)).astype(BF16)
    kl = (k * jnp.exp(blast - b)).astype(BF16)
    qb = (qs * jnp.exp(b)).astype(BF16)
    dec_col = jnp.broadcast_to(jnp.exp(blast), (8, 512)).T
    vb = v_ref[0].astype(BF16)
    for p in range(HA // 2):
        sl = slice(128 * p, 128 * (p + 1))
        qd_p, kd_p, kl_p, qb_p, v_p = qd[:, sl], kd[:, sl], kl[:, sl], qb[:, sl], vb[:, sl]
        st = sg_ref[d, p]
        att = _dot_nt(qd_p, pair_rows(kd_p))
        att = jnp.where(keep2, att, 0.0).astype(BF16)
        lhs = jnp.concatenate([att, qb_p], axis=1)
        rhs = jnp.concatenate([pair_rows(v_p), st.astype(BF16)], axis=0)
        oh_ref[0, :, sl] = jnp.dot(lhs, rhs, preferred_element_type=F32).astype(oh_ref.dtype)
        upd = _dot_tn(kl_p, v_p)
        sg_ref[d, p] = st * dec_col[sl, 0:1] + jnp.where(blockdiag, upd, 0.0)

    u_scr[0:8, :] = xp_ref[0] * pv
    u_scr[8:8 + L, :] = x_ref[0]
    u_scr[8 + L:16 + L, :] = xn_ref[0] * nv
    u = u_scr[...]
    y = cbias_ref[...] + cw_ref[2:3, :] * u[8:8 + L]
    for j in (0, 1, 3, 4):
        y = y + cw_ref[j:j + 1, :] * pltpu.roll(u, (2 - j) % (L + 16), axis=0)[8:8 + L]
    xbc = _silu(y)
    xm = xbc[:, :512]
    bm = xbc[:, 512:768].astype(BF16)
    cm = xbc[:, 768:1024].astype(BF16)
    draw = dt_ref[0] + dtb_ref[...]
    dt = jnp.maximum(draw, 0.0) + jnp.log(1.0 + jnp.exp(-jnp.abs(draw)))
    la_m = -dt * aexp_ref[...]
    cum = _cumsum_exact(tri3, la_m)
    er = lax.broadcasted_iota(jnp.int32, (128, 512), 0)
    ec = lax.broadcasted_iota(jnp.int32, (128, 512), 1)
    e1 = (er == (ec // 64) + 8 * d).astype(BF16)
    e1 = jnp.concatenate([e1, e1, e1], axis=0)
    both = _expand_exact(jnp.concatenate([dt, cum], axis=0), e1)
    dt_e, cum_e = both[:L], both[L:]
    cum_t = cum.T
    xdt = xm * dt_e
    clast = cum_e[last_row:last_row + 1, :]
    wx = (jnp.exp(clast - cum_e) * xdt).astype(BF16)
    dec_m = jnp.exp(clast)
    ecum = jnp.exp(cum_e)
    xdt_b = xdt.astype(BF16)
    for g in range(2):
        bm_g = bm[:, 128 * g:128 * (g + 1)]
        cm_g = cm[:, 128 * g:128 * (g + 1)]
        gl = slice(256 * g, 256 * (g + 1))
        cb2 = _dot_nt(cm_g, jnp.concatenate([bm_g, bm_g], axis=0))
        sm = sm_ref[d, g]
        y_int = jnp.dot(cm_g, sm.astype(BF16), preferred_element_type=F32) * ecum[:, gl]
        for pp in range(2):
            h0 = 4 * g + 2 * pp + 8 * d
            sl = slice(256 * g + 128 * pp, 256 * g + 128 * (pp + 1))
            crow = jnp.concatenate([cum_t[h0:h0 + 1, :], cum_t[h0 + 1:h0 + 2, :]], axis=1)
            lm = jnp.exp(jnp.where(keep2, cum_e[:, sl] - crow, -jnp.inf))
            yp = jnp.dot((cb2 * lm).astype(BF16), pair_rows(xdt_b[:, sl]), preferred_element_type=F32)
            yp = yp + y_int[:, 128 * pp:128 * (pp + 1)]
            if not rev:
                yp = yp + dskip_ref[:, sl] * xm[:, sl]
            om_ref[0, :, sl] = yp.astype(om_ref.dtype)
        sm_ref[d, g] = sm * dec_m[:, gl] + _dot_tn(bm_g, wx[:, gl])


def _scan_kernel(qf, ff, vf, xf, xpf, xnf, dtf, qb, fb, vb_, xb, xpb, xnb, dtb,
                 lb_ref, cw_ref, cbias_ref, dtbias_ref, aexp_ref, dskip_ref,
                 ohf, omf, ohb, omb, sg_ref, sm_ref, u_scr, *, nct, nc):
    c = pl.program_id(1)

    @pl.when(c == 0)
    def _():
        sg_ref[...] = jnp.zeros(sg_ref.shape, F32)
        sm_ref[...] = jnp.zeros(sm_ref.shape, F32)

    cb = jnp.where(c < nct, nct - 1 - c, nc + nct - 1 - c)

    def edge_flags(ch):
        pv = jnp.where((ch == 0) | (ch == nct), 0.0, 1.0)
        nv = jnp.where((ch == nct - 1) | (ch == nc - 1), 0.0, 1.0)
        return pv, nv

    pvf, nvf = edge_flags(c)
    pvb, nvb = edge_flags(cb)
    _scan_dir(False, qf, ff, vf, xf, xpf, xnf, dtf, pvf, nvf, lb_ref[0:1, :], cw_ref, cbias_ref,
              dtbias_ref, aexp_ref, dskip_ref, ohf, omf, sg_ref, sm_ref, u_scr)
    _scan_dir(True, qb, fb, vb_, xb, xpb, xnb, dtb, pvb, nvb, lb_ref[1:2, :], cw_ref, cbias_ref,
              dtbias_ref, aexp_ref, dskip_ref, ohb, omb, sg_ref, sm_ref, u_scr)


def _scan_mixers(p, nct, lb, conv_w, conv_b, dt_bias, a_log, d_skip):
    bsz, s, _ = p.shape
    nc = s // L
    nb8 = s // 8

    def bmap(c):
        return jnp.where(c < nct, nct - 1 - c, nc + nct - 1 - c)

    def specs(cmap, fcol):
        colblk = lambda col, w: col // w
        return [
            pl.BlockSpec((1, L, 512), lambda i, c: (i, cmap(c), colblk(COL_Q, 512))),
            pl.BlockSpec((1, L, 512), lambda i, c: (i, cmap(c), colblk(fcol, 512))),
            pl.BlockSpec((1, L, 512), lambda i, c: (i, cmap(c), colblk(COL_I, 512))),
            pl.BlockSpec((1, L, 1024), lambda i, c: (i, cmap(c), colblk(COL_X, 1024))),
            pl.BlockSpec((1, 8, 1024), lambda i, c: (i, jnp.maximum(cmap(c) * 8 - 1, 0), colblk(COL_X, 1024))),
            pl.BlockSpec((1, 8, 1024), lambda i, c: (i, jnp.minimum(cmap(c) * 8 + 8, nb8 - 1), colblk(COL_X, 1024))),
            pl.BlockSpec((1, L, 128), lambda i, c: (i, cmap(c), colblk(COL_DT, 128))),
        ]

    ident = lambda c: c
    const2 = lambda i, c: (0, 0)
    dtb = jnp.zeros((1, 128), F32).at[0, :16].set(dt_bias.reshape(16))
    aexp = jnp.zeros((1, 128), F32).at[0, :16].set(jnp.exp(a_log.reshape(16)))
    dsk = jnp.repeat(d_skip, 64).reshape(1, 512)
    kern = functools.partial(_scan_kernel, nct=nct, nc=nc)
    out_sd = jax.ShapeDtypeStruct((bsz, s, 512), BF16)
    return pl.pallas_call(
        kern,
        grid=(bsz, nc),
        in_specs=specs(ident, COL_FF) + specs(bmap, COL_FB) + [
            pl.BlockSpec((2, 512), const2), pl.BlockSpec((5, 1024), const2),
            pl.BlockSpec((1, 1024), const2), pl.BlockSpec((1, 128), const2),
            pl.BlockSpec((1, 128), const2), pl.BlockSpec((1, 512), const2)],
        out_specs=[pl.BlockSpec((1, L, 512), lambda i, c: (i, c, 0)),
                   pl.BlockSpec((1, L, 512), lambda i, c: (i, c, 0)),
                   pl.BlockSpec((1, L, 512), lambda i, c: (i, bmap(c), 0)),
                   pl.BlockSpec((1, L, 512), lambda i, c: (i, bmap(c), 0))],
        out_shape=[out_sd, out_sd, out_sd, out_sd],
        scratch_shapes=[pltpu.VMEM((2, HA // 2, 128, 128), F32),
                        pltpu.VMEM((2, 2, 128, 256), F32),
                        pltpu.VMEM((L + 16, 1024), F32)],
        compiler_params=_cparams(("parallel", "arbitrary")),
        name="scan_mixers",
    )(*([p] * 14), lb, conv_w, conv_b.reshape(1, 1024), dtb, aexp, dsk)


def _merge_kernel(ohf, ohb, omf, omb, g_ref, z_ref, xc_ref, xl_ref, gate_ref, hn_ref, mn_ref, w_ref,
                  fg_ref, fsc_ref, fsh_ref, rw_ref, oc_ref, ol_ref, h_ref, lg_ref, *, nct_tiles):
    oh = ohf[0].astype(F32) + ohb[0].astype(F32)
    sq = oh * oh
    hi = sq.astype(BF16)
    lo = (sq - hi.astype(F32)).astype(BF16)
    r = lax.broadcasted_iota(jnp.int32, (512, 512), 0) // 64
    c = lax.broadcasted_iota(jnp.int32, (512, 512), 1) // 64
    avg = jnp.where(r == c, 1.0 / 64, 0.0).astype(BF16)
    ms = jnp.dot(hi, avg, preferred_element_type=F32) + jnp.dot(lo, avg, preferred_element_type=F32)
    oh = oh * lax.rsqrt(ms + EPS) * hn_ref[...] * jax.nn.sigmoid(g_ref[0])
    y = (omf[0].astype(F32) + omb[0].astype(F32)) * _silu(z_ref[0])
    y = y * lax.rsqrt(jnp.mean(y * y, axis=-1, keepdims=True) + EPS) * mn_ref[...]
    cat = jnp.concatenate([oh, y], axis=1).astype(BF16)
    j = pl.program_id(1)
    x = jnp.where(j < nct_tiles, xc_ref[0], xl_ref[0])
    out = x + gate_ref[0, 0] * jnp.dot(cat, w_ref[...], preferred_element_type=F32)
    _router_body(out, fg_ref[...], fsc_ref[0, 0], fsh_ref[0, 0], rw_ref, h_ref, lg_ref)

    @pl.when(j < nct_tiles)
    def _():
        oc_ref[0] = out

    @pl.when(j >= nct_tiles)
    def _():
        ol_ref[0] = out


def _merge_mixers(x_ctx, x_lat, gate2, ohf, ohb, omf, omb, p, hgrn_norm, mamba_norm, w_out,
                  ffn_gain, ffn_scale2, ffn_shift2, w_router, tm):
    b, tc, d = x_ctx.shape
    s = tc + x_lat.shape[1]
    nct_tiles = tc // tm
    e = w_router.shape[1]
    sel = lambda i, j: (i, jnp.where(j < nct_tiles, 0, 1), 0, 0)
    row = lambda i, j: (i, j, 0)
    const2 = lambda i, j: (0, 0)
    return pl.pallas_call(
        functools.partial(_merge_kernel, nct_tiles=nct_tiles),
        grid=(b, s // tm),
        in_specs=[pl.BlockSpec((1, tm, 512), row)] * 4 + [
            pl.BlockSpec((1, tm, 512), lambda i, j: (i, j, COL_G // 512)),
            pl.BlockSpec((1, tm, 512), lambda i, j: (i, j, COL_Z // 512))] + _ctx_lat_specs(
                tm, d, nct_tiles) + [
            pl.BlockSpec((1, 1, 1, d), sel),
            pl.BlockSpec((1, 512), const2), pl.BlockSpec((1, 512), const2),
            pl.BlockSpec((2 * 512, d), const2),
            pl.BlockSpec((1, d), const2),
            pl.BlockSpec((1, 1, 1, d), sel), pl.BlockSpec((1, 1, 1, d), sel),
            pl.BlockSpec((3 * d, e), const2)],
        out_specs=[pl.BlockSpec((1, tm, d), lambda i, j: (i, jnp.minimum(j, nct_tiles - 1), 0)),
                   pl.BlockSpec((1, tm, d), lambda i, j: (i, jnp.maximum(j - nct_tiles, 0), 0)),
                   pl.BlockSpec((1, tm, d), row), pl.BlockSpec((1, tm, e), row)],
        out_shape=[jax.ShapeDtypeStruct((b, tc, d), F32),
                   jax.ShapeDtypeStruct((b, s - tc, d), F32),
                   jax.ShapeDtypeStruct((b, s, d), BF16),
                   jax.ShapeDtypeStruct((b, s, e), F32)],
        compiler_params=_cparams(("parallel", "arbitrary")),
        name="merge_mixers",
    )(ohf, ohb, omf, omb, p, p, x_ctx, x_lat, gate2.reshape(b, 2, 1, d), hgrn_norm.reshape(1, 512),
      mamba_norm.reshape(1, 512), w_out, ffn_gain.reshape(1, d), ffn_scale2.reshape(b, 2, 1, d),
      ffn_shift2.reshape(b, 2, 1, d), _router_weights(w_router))


def _rope_tables(t):
    rows = t // GRID_W
    row = jnp.repeat(jnp.arange(rows, dtype=jnp.int32), GRID_W)
    col = jnp.tile(jnp.arange(GRID_W, dtype=jnp.int32), rows)
    nf = ROPE // 4
    inv_freq = ROPE_THETA ** (-jnp.arange(nf, dtype=F32) / nf)
    pos = jnp.stack([row, col], axis=-1).astype(F32)
    ang = pos[..., None] * inv_freq
    cos, sin = jnp.cos(ang), jnp.sin(ang)
    cos64 = jnp.broadcast_to(cos[:, :, None, :], (t, 2, 2, nf)).reshape(t, ROPE)
    sin64 = jnp.broadcast_to(sin[:, :, None, :], (t, 2, 2, nf)).reshape(t, ROPE)
    zero = jnp.zeros((t, ROPE), F32)
    return jnp.concatenate([cos64, zero], -1), jnp.concatenate([sin64, zero], -1)


def _swap_cols(w):
    nf = ROPE // 4
    wr = w.reshape(w.shape[:-1] + (2, 2, nf))
    return jnp.stack([-wr[..., 1, :], wr[..., 0, :]], axis=-2).reshape(w.shape)


def _rope_tile(tile, cos_t, sin_t):
    return tile * cos_t + pltpu.roll(tile, ROPE, axis=1) * sin_t


def _mla_down_kernel(x_ref, g_ref, sc_ref, sh_ref, w_ref, qn_ref, kvn_ref, cos_ref, sin_ref,
                     cq_ref, ckv_ref, kr_ref):
    a = _modnorm(x_ref[0], g_ref[...], sc_ref[0], sh_ref[0]).astype(BF16)
    c = jnp.dot(a, w_ref[...], preferred_element_type=F32)
    cq = c[:, :Q_LORA]
    ckv = c[:, Q_LORA:Q_LORA + KV_LORA]
    cq_ref[0] = _rms(cq, qn_ref[...]).astype(BF16)
    ckv_ref[0] = _rms(ckv, kvn_ref[...]).astype(BF16)
    kr_ref[0] = _rope_tile(c[:, Q_LORA + KV_LORA:], cos_ref[...], sin_ref[...]).astype(BF16)


def _mla_down(x, gain, scale, shift, w_cat, q_norm, kv_norm, cos_t, sin_t, tm):
    b, t, d = x.shape
    n = w_cat.shape[1]
    return pl.pallas_call(
        _mla_down_kernel,
        grid=(b, t // tm),
        in_specs=[pl.BlockSpec((1, tm, d), lambda i, j: (i, j, 0)),
                  pl.BlockSpec((1, d), lambda i, j: (0, 0)),
                  pl.BlockSpec((1, 1, d), lambda i, j: (i, 0, 0)),
                  pl.BlockSpec((1, 1, d), lambda i, j: (i, 0, 0)),
                  pl.BlockSpec((d, n), lambda i, j: (0, 0)),
                  pl.BlockSpec((1, Q_LORA), lambda i, j: (0, 0)),
                  pl.BlockSpec((1, KV_LORA), lambda i, j: (0, 0)),
                  pl.BlockSpec((tm, 128), lambda i, j: (j, 0)),
                  pl.BlockSpec((tm, 128), lambda i, j: (j, 0))],
        out_specs=[pl.BlockSpec((1, tm, Q_LORA), lambda i, j: (i, j, 0)),
                   pl.BlockSpec((1, tm, KV_LORA), lambda i, j: (i, j, 0)),
                   pl.BlockSpec((1, tm, 128), lambda i, j: (i, j, 0))],
        out_shape=[jax.ShapeDtypeStruct((b, t, Q_LORA), BF16),
                   jax.ShapeDtypeStruct((b, t, KV_LORA), BF16),
                   jax.ShapeDtypeStruct((b, t, 128), BF16)],
        compiler_params=_cparams(("parallel", "parallel")),
        name="mla_down",
    )(x, gain.reshape(1, d), scale, shift, w_cat, q_norm.reshape(1, -1), kv_norm.reshape(1, -1),
      cos_t, sin_t)


def _q_up_kernel(cq_ref, w_ref, cos_ref, sin_ref, q_ref):
    cq = cq_ref[0]
    for h in range(H_C):
        q = jnp.dot(cq, w_ref[h], preferred_element_type=F32)
        qn = q[:, :NOPE] * QK_SCALE
        qr = _rope_tile(q[:, NOPE:], cos_ref[...], sin_ref[...]) * QK_SCALE
        q_ref[0, h] = jnp.concatenate([qn, qr], axis=1).astype(BF16)


def _q_up(cq, w_uq_h, cos_t, sin_t, tm):
    b, t, r = cq.shape
    return pl.pallas_call(
        _q_up_kernel,
        grid=(b, t // tm),
        in_specs=[pl.BlockSpec((1, tm, r), lambda i, j: (i, j, 0)),
                  pl.BlockSpec((H_C, r, QK_DIM), lambda i, j: (0, 0, 0)),
                  pl.BlockSpec((tm, 128), lambda i, j: (j, 0)),
                  pl.BlockSpec((tm, 128), lambda i, j: (j, 0))],
        out_specs=pl.BlockSpec((1, H_C, tm, QK_DIM), lambda i, j: (i, 0, j, 0)),
        out_shape=jax.ShapeDtypeStruct((b, H_C, t, QK_DIM), BF16),
        compiler_params=_cparams(("parallel", "parallel")),
        name="q_up",
    )(cq, w_uq_h, cos_t, sin_t)


def _kv_up_kernel(ckv_ref, kr_ref, w_ref, k_ref, v_ref):
    ckv = ckv_ref[0]
    kr = kr_ref[0]
    for h in range(H_C):
        kv = jnp.dot(ckv, w_ref[h], preferred_element_type=F32)
        k_ref[0, h] = jnp.concatenate([kv[:, :NOPE].astype(BF16), kr], axis=1)
        v_ref[0, h] = kv[:, NOPE:].astype(BF16)


def _kv_up(ckv, kr, w_ukv_h, tm):
    b, s, r = ckv.shape
    return pl.pallas_call(
        _kv_up_kernel,
        grid=(b, s // tm),
        in_specs=[pl.BlockSpec((1, tm, r), lambda i, j: (i, j, 0)),
                  pl.BlockSpec((1, tm, 128), lambda i, j: (i, j, 0)),
                  pl.BlockSpec((H_C, r, NOPE + VH), lambda i, j: (0, 0, 0))],
        out_specs=[pl.BlockSpec((1, H_C, tm, QK_DIM), lambda i, j: (i, 0, j, 0)),
                   pl.BlockSpec((1, H_C, tm, VH), lambda i, j: (i, 0, j, 0))],
        out_shape=[jax.ShapeDtypeStruct((b, H_C, s, QK_DIM), BF16),
                   jax.ShapeDtypeStruct((b, H_C, s, VH), BF16)],
        compiler_params=_cparams(("parallel", "parallel")),
        name="kv_up",
    )(ckv, kr, w_ukv_h)


def _attn_kernel(qa_ref, qb_ref, q0_ref, k_ref, v_ref, o_ref, s0_scr, s1_scr, m_scr, *, tk, nk, tq):
    n = pl.program_id(2)

    def scores(q, j):
        ks = k_ref[0, 0, j * tk:(j + 1) * tk, :]
        return lax.dot_general(q, ks, (((1,), (1,)), ((), ())), preferred_element_type=F32)

    def fold_max(mrun, s):
        for c in range(tk // 128):
            mrun = jnp.maximum(mrun, s[:, c * 128:(c + 1) * 128])
        return mrun

    @pl.when(n == 0)
    def _():
        q0 = q0_ref[0, 0]
        mrun = jnp.full((tq, 128), -jnp.inf, F32)
        for j in range(nk):
            s = scores(q0, j)
            s0_scr[:, j * tk:(j + 1) * tk] = s
            mrun = fold_max(mrun, s)
        m_scr[...] = jnp.max(mrun, axis=-1, keepdims=True)

    def fused(qn, s_read, s_write, m):
        mrun = jnp.full((tq, 128), -jnp.inf, F32)
        lrun = jnp.zeros((tq, 128), F32)
        acc = jnp.zeros((tq, VH), F32)
        for j in range(nk):
            sn = scores(qn, j)
            s_write[:, j * tk:(j + 1) * tk] = sn
            mrun = fold_max(mrun, sn)
            p = jnp.exp2(s_read[:, j * tk:(j + 1) * tk] - m)
            for c in range(tk // 128):
                lrun = lrun + p[:, c * 128:(c + 1) * 128]
            acc = acc + jnp.dot(p.astype(BF16), v_ref[0, 0, j * tk:(j + 1) * tk, :],
                                preferred_element_type=F32)
        out = acc / jnp.sum(lrun, axis=-1, keepdims=True)
        return out, jnp.max(mrun, axis=-1, keepdims=True)

    out_a, m1 = fused(qa_ref[0, 0], s0_scr, s1_scr, m_scr[...])
    o_ref[0, 0:tq, :] = out_a.astype(o_ref.dtype)
    out_b, m2 = fused(qb_ref[0, 0], s1_scr, s0_scr, m1)
    o_ref[0, tq:2 * tq, :] = out_b.astype(o_ref.dtype)
    m_scr[...] = m2


def _attention(q, k, v, tq, tk):
    b, h, t, _ = q.shape
    s = k.shape[2]
    nq = t // tq
    assert t % (2 * tq) == 0 and s % tk == 0
    kern = functools.partial(_attn_kernel, tk=tk, nk=s // tk, tq=tq)
    return pl.pallas_call(
        kern,
        grid=(b, h, nq // 2),
        in_specs=[pl.BlockSpec((1, 1, tq, QK_DIM), lambda i, j, n: (i, j, 2 * n + 1, 0)),
                  pl.BlockSpec((1, 1, tq, QK_DIM),
                               lambda i, j, n: (i, j, jnp.minimum(2 * n + 2, nq - 1), 0)),
                  pl.BlockSpec((1, 1, tq, QK_DIM), lambda i, j, n: (i, j, 0, 0)),
                  pl.BlockSpec((1, 1, s, QK_DIM), lambda i, j, n: (i, j, 0, 0)),
                  pl.BlockSpec((1, 1, s, VH), lambda i, j, n: (i, j, 0, 0))],
        out_specs=pl.BlockSpec((1, 2 * tq, VH), lambda i, j, n: (i, n, j)),
        out_shape=jax.ShapeDtypeStruct((b, t, h * VH), BF16),
        scratch_shapes=[pltpu.VMEM((tq, s), F32), pltpu.VMEM((tq, s), F32),
                        pltpu.VMEM((tq, 1), F32)],
        compiler_params=_cparams(("parallel", "parallel", "arbitrary")),
        name="attention",
    )(q, q, q, k, v)


def _ffn_kernel(x_ref, wg_ref, wu_ref, wd_ref, g_ref, o_ref, acc_scr, wgb, wub, wdb, *, fc):
    @pl.when((pl.program_id(1) == 0) & (pl.program_id(2) == 0))
    def _():
        wgb[...] = wg_ref[0, 0].astype(BF16)
        wub[...] = wu_ref[0, 0].astype(BF16)
        wdb[...] = wd_ref[0, 0].astype(BF16)

    x = x_ref[0, 0]
    nf = wgb.shape[-1]
    for c in range(nf // fc):
        sl = slice(c * fc, (c + 1) * fc)
        hg = jnp.dot(x, wgb[:, sl], preferred_element_type=F32)
        hu = jnp.dot(x, wub[:, sl], preferred_element_type=F32)
        hid = (_silu(hg) * hu).astype(BF16)
        part = jnp.dot(hid, wdb[sl, :], preferred_element_type=F32)
        if c == 0:
            acc_scr[...] = part
        else:
            acc_scr[...] += part
    o_ref[0, 0] = (acc_scr[...] * g_ref[0, 0]).astype(o_ref.dtype)


def _expert_ffn(xg, gates, w_gate, w_up, w_down, layer, tm, fc):
    b, e, cap, d = xg.shape
    f = w_gate.shape[-1]
    tm = min(tm, cap)
    kern = functools.partial(_ffn_kernel, fc=fc)
    return pl.pallas_call(
        kern,
        grid=(e, b, cap // tm),
        in_specs=[pl.BlockSpec((1, 1, tm, d), lambda j, i, m: (i, j, m, 0)),
                  pl.BlockSpec((1, 1, d, f), lambda j, i, m: (layer, j, 0, 0)),
                  pl.BlockSpec((1, 1, d, f), lambda j, i, m: (layer, j, 0, 0)),
                  pl.BlockSpec((1, 1, f, d), lambda j, i, m: (layer, j, 0, 0)),
                  pl.BlockSpec((1, 1, tm, 1), lambda j, i, m: (i, j, m, 0))],
        out_specs=pl.BlockSpec((1, 1, tm, d), lambda j, i, m: (i, j, m, 0)),
        out_shape=jax.ShapeDtypeStruct((b, e, cap, d), BF16),
        scratch_shapes=[pltpu.VMEM((tm, d), F32), pltpu.VMEM((d, f), BF16),
                        pltpu.VMEM((d, f), BF16), pltpu.VMEM((f, d), BF16)],
        compiler_params=_cparams(("arbitrary", "arbitrary", "arbitrary")),
        name="expert_ffn",
    )(xg, w_gate, w_up, w_down, gates)


SLOT_WIN = 256
TOK_BLK = 128
COMBINE_EXPERTS = 4


def _combine_kernel(lo_ref, idx_ref, y_ref, x_ref, gate_ref, fg_ref, o_ref, *, nblk, nrow,
                    final_norm):
    b, j, eg = pl.program_id(0), pl.program_id(1), pl.program_id(2)

    @pl.when(eg == 0)
    def _():
        o_ref[...] = jnp.zeros(o_ref.shape, F32)

    tok = lax.broadcasted_iota(jnp.int32, (TOK_BLK, 1), 0)
    for blk in range(nblk):
        gblk = j * nblk + blk
        part = None
        for k in range(COMBINE_EXPERTS):
            w = jnp.minimum(lo_ref[b, eg * COMBINE_EXPERTS + k, gblk] // TOK_BLK, nrow - 2)
            ids = jnp.concatenate([idx_ref[0, k, pl.ds(w, 1), :],
                                   idx_ref[0, k, pl.ds(w + 1, 1), :]], axis=1)
            onehot = (ids == tok + gblk * TOK_BLK).astype(BF16)
            yw = y_ref[0, k, pl.ds(pl.multiple_of(w * TOK_BLK, TOK_BLK), SLOT_WIN), :]
            d = jnp.dot(onehot, yw, preferred_element_type=F32)
            part = d if part is None else part + d
        rows = slice(blk * TOK_BLK, (blk + 1) * TOK_BLK)
        o_ref[0, rows, :] += part

    @pl.when(eg == pl.num_programs(2) - 1)
    def _():
        out = x_ref[0] + gate_ref[0] * o_ref[0]
        o_ref[0] = _rms(out, fg_ref[...]) if final_norm else out


def _combine(x, gate_out, idx_s, y, final_gain=None):
    b, t, d = x.shape
    e, cap = idx_s.shape[1:]
    capp = max(cap, SLOT_WIN)
    if capp != cap:
        idx_s = jnp.pad(idx_s, ((0, 0), (0, 0), (0, capp - cap)), constant_values=-1)
        y = jnp.pad(y[:, :, :cap], ((0, 0), (0, 0), (0, capp - cap), (0, 0)))
    nrow = capp // TOK_BLK
    ts = min(2048, t)
    nblk = ts // TOK_BLK
    bounds = jnp.arange(t // TOK_BLK, dtype=jnp.int32) * TOK_BLK
    valid = idx_s >= 0
    lo = jnp.sum((valid[..., None] & (idx_s[..., None] < bounds)).astype(jnp.int32), axis=2)
    kern = functools.partial(_combine_kernel, nblk=nblk, nrow=nrow,
                             final_norm=final_gain is not None)
    fg = jnp.ones((1, d), F32) if final_gain is None else final_gain.reshape(1, d)
    grid_spec = pltpu.PrefetchScalarGridSpec(
        num_scalar_prefetch=1,
        grid=(b, t // ts, e // COMBINE_EXPERTS),
        in_specs=[pl.BlockSpec((1, COMBINE_EXPERTS, nrow, TOK_BLK), lambda i, j, k, lo_r: (i, k, 0, 0)),
                  pl.BlockSpec((1, COMBINE_EXPERTS, capp, d), lambda i, j, k, lo_r: (i, k, 0, 0)),
                  pl.BlockSpec((1, ts, d), lambda i, j, k, lo_r: (i, j, 0)),
                  pl.BlockSpec((1, 1, d), lambda i, j, k, lo_r: (i, 0, 0)),
                  pl.BlockSpec((1, d), lambda i, j, k, lo_r: (0, 0))],
        out_specs=pl.BlockSpec((1, ts, d), lambda i, j, k, lo_r: (i, j, 0)),
    )
    return pl.pallas_call(
        kern,
        grid_spec=grid_spec,
        out_shape=jax.ShapeDtypeStruct((b, t, d), F32),
        compiler_params=_cparams(("parallel", "parallel", "arbitrary")),
        name="moe_combine",
    )(lo, idx_s.reshape(b, e, nrow, TOK_BLK), y, x, gate_out, fg)


def _route(logits):
    t = logits.shape[1]
    cap = CAP_FACTOR * t // N_EXPERTS
    aff = jax.nn.softmax(logits, axis=-1)
    g, idx = lax.top_k(jnp.swapaxes(aff, 1, 2), cap)
    return lax.sort((idx, g), dimension=2, num_keys=1)


def _moe(x, hn, logits, gate_out, w_gate, w_up, w_down, layer, final_gain=None):
    b, t, d = x.shape
    idx_s, g_s = _route(logits)
    bidx = jnp.arange(b)[:, None, None]
    xg = hn[bidx, idx_s]
    y = _expert_ffn(xg, g_s[..., None], w_gate, w_up, w_down, layer, 1024, 512)
    return _combine(x, gate_out, idx_s, y, final_gain)


def _moe_two_streams(x_ctx, x_lat, hn, logits, gate_ctx, gate_lat, w_gate, w_up, w_down, layer):
    b, tc, d = x_ctx.shape
    idx_c, g_c = _route(logits[:, :tc])
    idx_l, g_l = _route(logits[:, tc:])
    idx_cat = jnp.concatenate([idx_l + tc, idx_c], axis=2)
    g_cat = jnp.concatenate([g_l, g_c], axis=2)
    bidx = jnp.arange(b)[:, None, None]
    xg = hn[bidx, idx_cat]
    cap = idx_cat.shape[2]
    tm = _pick_tile(cap, 1024, 528, 512, 352, 256, 176, 96, 32, 16)
    y = _expert_ffn(xg, g_cat[..., None], w_gate, w_up, w_down, layer, tm, 512)
    new_lat = _combine(x_lat, gate_lat, idx_l, y)
    new_ctx = _combine(x_ctx, gate_ctx, idx_c, y[:, :, idx_l.shape[2]:])
    return new_ctx, new_lat


def kernel(x, c, ctx, c_ctx, w_mod, b_mod, norm_mix, norm_ffn, norm_out, w_in, w_out_rec,
           conv_w, conv_b, lb_gamma, dt_bias, a_log, d_skip, hgrn_norm, mamba_norm,
           w_dq, q_norm, w_uq, w_dkv, kv_norm, w_ukv, w_kr, w_o,
           w_router, w_gate, w_up, w_down):
    B, T, D = x.shape
    TC = ctx.shape[1]
    lb = jnp.cumsum(jax.nn.softmax(lb_gamma.astype(F32), axis=0), axis=0)

    cvec = jnp.zeros((16, D), F32).at[:B].set(c).at[B].set(c_ctx)
    mods = _modulation(cvec, w_mod, b_mod)

    x_lat, x_ctx = x, ctx
    for l in range(DEPTH):
        need_ctx = l < DEPTH - 1
        m_lat = [mods[l, :B, i * D:(i + 1) * D][:, None, :] for i in range(6)]
        m_ctx = [jnp.broadcast_to(mods[l, B, i * D:(i + 1) * D][None, None, :], (B, 1, D))
                 for i in range(6)]
        if l % 2 == 0:
            e = l // 2
            w_in_p = jnp.pad(w_in[e], ((0, 0), (0, IN_COLS_PAD - IN_COLS))).astype(BF16)
            pair = lambda i: jnp.concatenate([m_ctx[i], m_lat[i]], axis=1)
            p = _modproj(x_ctx, x_lat, norm_mix[l], pair(1), pair(0), w_in_p, 256)
            ohf, omf, ohb, omb = _scan_mixers(p, TC // CHUNK, lb[e], conv_w[e], conv_b[e],
                                              dt_bias[e], a_log[e], d_skip[e])
            x_ctx, x_lat, hn, logits = _merge_mixers(
                x_ctx, x_lat, pair(2), ohf, ohb, omf, omb, p, hgrn_norm[e], mamba_norm[e],
                w_out_rec[e].astype(BF16), norm_ffn[l], pair(4), pair(3), w_router[l], 256)
            if not need_ctx:
                hn, logits = hn[:, TC:], logits[:, TC:]
        else:
            assert not need_ctx
            j = l // 2
            cos_l, sin_l = _rope_tables(T)
            cos_c = jnp.concatenate([jnp.ones((TC, ROPE), F32), jnp.zeros((TC, ROPE), F32)], -1)
            sin_c = jnp.zeros((TC, 128), F32)
            w_cat = jnp.concatenate([w_dq[j], w_dkv[j], w_kr[j], _swap_cols(w_kr[j])],
                                    axis=1).astype(BF16)
            wq = w_uq[j].reshape(Q_LORA, H_C, NOPE + ROPE)
            wq_h = jnp.concatenate([wq, _swap_cols(wq[..., NOPE:])], axis=-1)
            wq_h = jnp.transpose(wq_h, (1, 0, 2)).astype(BF16)
            wkv_h = jnp.transpose(w_ukv[j].reshape(KV_LORA, H_C, NOPE + VH), (1, 0, 2)).astype(BF16)
            cq_l, ckv_l, kr_l = _mla_down(x_lat, norm_mix[l], m_lat[1], m_lat[0], w_cat,
                                          q_norm[j], kv_norm[j], cos_l, sin_l, 512)
            _, ckv_c, kr_c = _mla_down(x_ctx, norm_mix[l], m_ctx[1], m_ctx[0], w_cat,
                                       q_norm[j], kv_norm[j], cos_c, sin_c, 256)
            ckv = jnp.concatenate([ckv_c, ckv_l], axis=1)
            kr = jnp.concatenate([kr_c, kr_l], axis=1)
            qh = _q_up(cq_l, wq_h, cos_l, sin_l, _pick_tile(T, 1024, 512, 256))
            kh, vh = _kv_up(ckv, kr, wkv_h, _pick_tile(T + TC, 1408, 768, 256))
            o = _attention(qh, kh, vh, 256, 768)
            x_lat, hn, logits = _linear_residual(x_lat, m_lat[2], o, w_o[j].astype(BF16),
                                                 norm_ffn[l], m_lat[4], m_lat[3], w_router[l],
                                                 _pick_tile(T, 1024, 512))
        if need_ctx:
            x_ctx, x_lat = _moe_two_streams(x_ctx, x_lat, hn, logits, m_ctx[5], m_lat[5],
                                            w_gate, w_up, w_down, l)
        else:
            x_lat = _moe(x_lat, hn, logits, m_lat[5], w_gate, w_up, w_down, l, norm_out)
    return x_lat
```

```python
import functools
import math

import jax
import jax.numpy as jnp
from jax import lax
from jax.experimental import pallas as pl
from jax.experimental.pallas import tpu as pltpu

F32 = jnp.float32
BF16 = jnp.bfloat16

D_MODEL = 1024
DEPTH = 2
GRID_W = 64
EPS = 1e-6
CHUNK = 64
HA, DKA, DVA = 8, 64, 64
WA = HA * DVA
HB, PB = 8, 64
D_INNER = HB * PB
N_GROUPS, D_STATE = 2, 128
CONV_W = 5
REC_SPLITS = (HA * DKA, HA * DKA, HA * DKA, WA, WA, D_INNER, D_INNER,
              N_GROUPS * D_STATE, N_GROUPS * D_STATE, HB, HB)
IN_COLS = sum(REC_SPLITS)
IN_COLS_PAD = 4224
H_C, NOPE, ROPE, VH = 8, 128, 64, 128
Q_LORA, KV_LORA = 384, 256
ROPE_THETA = 10000.0
ATTN_SCALE = 1.0 / math.sqrt(NOPE + ROPE)
QK_SCALE = ATTN_SCALE * math.log2(math.e)
QK_DIM = 256
N_EXPERTS = 16
EXPERT_FF = 1024
CAP_FACTOR = 2

VMEM_LIMIT = 56 * 1024 * 1024


def _cparams(sem):
    return pltpu.CompilerParams(dimension_semantics=sem, vmem_limit_bytes=VMEM_LIMIT)


def _pick_tile(n, *candidates):
    return next(c for c in candidates if n % c == 0)


def _silu(v):
    return v * jax.nn.sigmoid(v)


def _modnorm(x, gain, scale, shift):
    ms = jnp.mean(x * x, axis=-1, keepdims=True)
    return (x * lax.rsqrt(ms + EPS) * gain) * (1.0 + scale) + shift


def _mod_kernel(s_ref, w_ref, b_ref, o_ref):
    s = _silu(s_ref[...])
    o_ref[0] = jnp.dot(s, w_ref[0], preferred_element_type=F32,
                       precision=lax.Precision.HIGHEST) + b_ref[0]


def _modulation(cvec, w_mod, b_mod):
    n = w_mod.shape[-1]
    tn = 1536
    return pl.pallas_call(
        _mod_kernel,
        grid=(DEPTH, n // tn),
        in_specs=[pl.BlockSpec((16, D_MODEL), lambda l, j: (0, 0)),
                  pl.BlockSpec((1, D_MODEL, tn), lambda l, j: (l, 0, j)),
                  pl.BlockSpec((1, 1, tn), lambda l, j: (l, 0, j))],
        out_specs=pl.BlockSpec((1, 16, tn), lambda l, j: (l, 0, j)),
        out_shape=jax.ShapeDtypeStruct((DEPTH, 16, n), F32),
        compiler_params=_cparams(("parallel", "parallel")),
        name="modulation",
    )(cvec, w_mod, b_mod.reshape(DEPTH, 1, n))


def _ctx_lat_specs(tm, d, nct_tiles):
    return [pl.BlockSpec((1, tm, d), lambda i, j: (i, jnp.minimum(j, nct_tiles - 1), 0)),
            pl.BlockSpec((1, tm, d), lambda i, j: (i, jnp.maximum(j - nct_tiles, 0), 0))]


def _modproj_kernel(xc_ref, xl_ref, g_ref, sc_ref, sh_ref, w_ref, o_ref, *, nct_tiles):
    x = jnp.where(pl.program_id(1) < nct_tiles, xc_ref[0], xl_ref[0])
    a = _modnorm(x, g_ref[...], sc_ref[0, 0], sh_ref[0, 0]).astype(BF16)
    o_ref[0] = jnp.dot(a, w_ref[...], preferred_element_type=F32).astype(o_ref.dtype)


def _modproj(x_ctx, x_lat, gain, scale2, shift2, w, tm, out_dtype=F32):
    b, tc, d = x_ctx.shape
    t = tc + x_lat.shape[1]
    nct_tiles = tc // tm
    n = w.shape[1]
    sel = lambda i, j: (i, jnp.where(j < nct_tiles, 0, 1), 0, 0)
    return pl.pallas_call(
        functools.partial(_modproj_kernel, nct_tiles=nct_tiles),
        grid=(b, t // tm),
        in_specs=_ctx_lat_specs(tm, d, nct_tiles) + [
                  pl.BlockSpec((1, d), lambda i, j: (0, 0)),
                  pl.BlockSpec((1, 1, 1, d), sel),
                  pl.BlockSpec((1, 1, 1, d), sel),
                  pl.BlockSpec((d, n), lambda i, j: (0, 0))],
        out_specs=pl.BlockSpec((1, tm, n), lambda i, j: (i, j, 0)),
        out_shape=jax.ShapeDtypeStruct((b, t, n), out_dtype),
        compiler_params=_cparams(("parallel", "parallel")),
        name="modproj",
    )(x_ctx, x_lat, gain.reshape(1, d), scale2.reshape(b, 2, 1, d), shift2.reshape(b, 2, 1, d), w)


def _router_body(x, gain, scale, shift, w_ref, h_ref, lg_ref):
    hn = _modnorm(x, gain, scale, shift)
    a = hn.astype(BF16)
    h_ref[0] = a
    b = (hn - a.astype(F32)).astype(BF16)
    lg_ref[0] = jnp.dot(jnp.concatenate([a, b, a], axis=1), w_ref[...], preferred_element_type=F32)


def _router_weights(w_router):
    w_hi = w_router.astype(BF16)
    w_lo = (w_router - w_hi.astype(F32)).astype(BF16)
    return jnp.concatenate([w_hi, w_hi, w_lo], axis=0)


def _linres_kernel(x_ref, gate_ref, y_ref, w_ref, fg_ref, fsc_ref, fsh_ref, rw_ref,
                   o_ref, h_ref, lg_ref):
    acc = jnp.dot(y_ref[0].astype(BF16), w_ref[...], preferred_element_type=F32)
    out = x_ref[0] + gate_ref[0] * acc
    o_ref[0] = out
    _router_body(out, fg_ref[...], fsc_ref[0], fsh_ref[0], rw_ref, h_ref, lg_ref)


def _linear_residual(x, gate, y, w, ffn_gain, ffn_scale, ffn_shift, w_router, tm):
    b, t, d = x.shape
    k = y.shape[-1]
    e = w_router.shape[1]
    row = lambda i, j: (i, j, 0)
    vec = lambda i, j: (i, 0, 0)
    const2 = lambda i, j: (0, 0)
    return pl.pallas_call(
        _linres_kernel,
        grid=(b, t // tm),
        in_specs=[pl.BlockSpec((1, tm, d), row),
                  pl.BlockSpec((1, 1, d), vec),
                  pl.BlockSpec((1, tm, k), row),
                  pl.BlockSpec((k, d), const2),
                  pl.BlockSpec((1, d), const2),
                  pl.BlockSpec((1, 1, d), vec),
                  pl.BlockSpec((1, 1, d), vec),
                  pl.BlockSpec((3 * d, e), const2)],
        out_specs=[pl.BlockSpec((1, tm, d), row), pl.BlockSpec((1, tm, d), row),
                   pl.BlockSpec((1, tm, e), row)],
        out_shape=[jax.ShapeDtypeStruct((b, t, d), F32), jax.ShapeDtypeStruct((b, t, d), BF16),
                   jax.ShapeDtypeStruct((b, t, e), F32)],
        compiler_params=_cparams(("parallel", "parallel")),
        name="linear_residual",
    )(x, gate, y, w, ffn_gain.reshape(1, d), ffn_scale, ffn_shift, _router_weights(w_router))


L = CHUNK
SCAN_BLOCK = 4 * L
COL_Q, COL_FF, COL_FB, COL_I, COL_G, COL_Z, COL_X, COL_DT = 0, 512, 1024, 1536, 2048, 2560, 3072, 4096


def _rms(x, gain):
    return x * lax.rsqrt(jnp.mean(x * x, axis=-1, keepdims=True) + EPS) * gain


def _split3(x):
    a = x.astype(BF16)
    r = x - a.astype(F32)
    b = r.astype(BF16)
    c = (r - b.astype(F32)).astype(BF16)
    return a, b, c


def _cumsum_exact(tri3, x):
    a, b, c = _split3(x)
    return jnp.dot(tri3, jnp.concatenate([a, b, c], axis=0), preferred_element_type=F32)


def _expand_exact(x, e3):
    a, b, c = _split3(x)
    return jnp.dot(jnp.concatenate([a, b, c], axis=1), e3, preferred_element_type=F32)


def _dot_nt(a, b):
    return lax.dot_general(a, b, (((1,), (1,)), ((), ())), preferred_element_type=F32)


def _dot_tn(a, b):
    return lax.dot_general(a, b, (((0,), (0,)), ((), ())), preferred_element_type=F32)


def _scan_dir(rev, r0, q_ref, f_ref, v_ref, dt_ref,
              lb_row, cw_ref, cbias_ref, dtb_ref, aexp_ref, dskip_ref,
              oh_ref, om_ref, sg_ref, sm_ref, u_scr):
    d = 1 if rev else 0
    rs = slice(r0, r0 + L)
    ti = lax.broadcasted_iota(jnp.int32, (L, L), 0)
    si = lax.broadcasted_iota(jnp.int32, (L, L), 1)
    keep = (si >= ti) if rev else (si <= ti)
    tri = keep.astype(BF16)
    tri3 = jnp.concatenate([tri, tri, tri], axis=1)
    t2 = lax.broadcasted_iota(jnp.int32, (L, 128), 0)
    s2 = lax.broadcasted_iota(jnp.int32, (L, 128), 1) % L
    keep2 = (s2 >= t2) if rev else (s2 <= t2)
    ref_row = (L - 1 - L // 2) if rev else L // 2
    last_row = 0 if rev else L - 1
    lane = lax.broadcasted_iota(jnp.int32, (1, 128), 1)
    lo = lane < 64
    r128 = lax.broadcasted_iota(jnp.int32, (128, 128), 0)
    c128 = lax.broadcasted_iota(jnp.int32, (128, 128), 1)
    blockdiag = (r128 < 64) == (c128 < 64)
    zero_b = jnp.zeros((), BF16)

    def pair_rows(a):
        return jnp.concatenate([jnp.where(lo, a, zero_b), jnp.where(lo, zero_b, a)], axis=0)

    q = q_ref[0, rs]
    qs = _silu(q)
    f = lb_row + (1.0 - lb_row) * jax.nn.sigmoid(f_ref[0, rs])
    k = 1.0 - f
    la = jnp.log(f)
    b = _cumsum_exact(tri3, la)
    bref = b[ref_row:ref_row + 1, :]
    blast = b[last_row:last_row + 1, :]
    qd = (qs * jnp.exp(b - bref)).astype(BF16)
    kd = (k * jnp.exp(bref - b)).astype(BF16)
    kl = (k * jnp.exp(blast - b)).astype(BF16)
    qb = (qs * jnp.exp(b)).astype(BF16)
    dec_col = jnp.broadcast_to(jnp.exp(blast), (8, 512)).T
    vb = v_ref[0, rs].astype(BF16)
    for p in range(HA // 2):
        sl = slice(128 * p, 128 * (p + 1))
        qd_p, kd_p, kl_p, qb_p, v_p = qd[:, sl], kd[:, sl], kl[:, sl], qb[:, sl], vb[:, sl]
        st = sg_ref[d, p]
        att = _dot_nt(qd_p, pair_rows(kd_p))
        att = jnp.where(keep2, att, 0.0).astype(BF16)
        lhs = jnp.concatenate([att, qb_p], axis=1)
        rhs = jnp.concatenate([pair_rows(v_p), st.astype(BF16)], axis=0)
        oh_ref[0, rs, sl] = jnp.dot(lhs, rhs, preferred_element_type=F32).astype(oh_ref.dtype)
        upd = _dot_tn(kl_p, v_p)
        sg_ref[d, p] = st * dec_col[sl, 0:1] + jnp.where(blockdiag, upd, 0.0)

    u = u_scr[r0:r0 + L + 16, :]
    y = cbias_ref[...] + cw_ref[2:3, :] * u[8:8 + L]
    for j in (0, 1, 3, 4):
        y = y + cw_ref[j:j + 1, :] * pltpu.roll(u, (2 - j) % (L + 16), axis=0)[8:8 + L]
    xbc = _silu(y)
    xm = xbc[:, :512]
    bm = xbc[:, 512:768].astype(BF16)
    cm = xbc[:, 768:1024].astype(BF16)
    draw = dt_ref[0, rs] + dtb_ref[...]
    dt = jnp.maximum(draw, 0.0) + jnp.log(1.0 + jnp.exp(-jnp.abs(draw)))
    la_m = -dt * aexp_ref[...]
    cum = _cumsum_exact(tri3, la_m)
    er = lax.broadcasted_iota(jnp.int32, (128, 512), 0)
    ec = lax.broadcasted_iota(jnp.int32, (128, 512), 1)
    e1 = (er == (ec // 64) + 8 * d).astype(BF16)
    e1 = jnp.concatenate([e1, e1, e1], axis=0)
    both = _expand_exact(jnp.concatenate([dt, cum], axis=0), e1)
    dt_e, cum_e = both[:L], both[L:]
    cum_t = cum.T
    xdt = xm * dt_e
    clast = cum_e[last_row:last_row + 1, :]
    wx = (jnp.exp(clast - cum_e) * xdt).astype(BF16)
    dec_m = jnp.exp(clast)
    ecum = jnp.exp(cum_e)
    xdt_b = xdt.astype(BF16)
    for g in range(2):
        bm_g = bm[:, 128 * g:128 * (g + 1)]
        cm_g = cm[:, 128 * g:128 * (g + 1)]
        gl = slice(256 * g, 256 * (g + 1))
        cb2 = _dot_nt(cm_g, jnp.concatenate([bm_g, bm_g], axis=0))
        sm = sm_ref[d, g]
        y_int = jnp.dot(cm_g, sm.astype(BF16), preferred_element_type=F32) * ecum[:, gl]
        for pp in range(2):
            h0 = 4 * g + 2 * pp + 8 * d
            sl = slice(256 * g + 128 * pp, 256 * g + 128 * (pp + 1))
            crow = jnp.concatenate([cum_t[h0:h0 + 1, :], cum_t[h0 + 1:h0 + 2, :]], axis=1)
            lm = jnp.exp(jnp.where(keep2, cum_e[:, sl] - crow, -jnp.inf))
            yp = jnp.dot((cb2 * lm).astype(BF16), pair_rows(xdt_b[:, sl]), preferred_element_type=F32)
            yp = yp + y_int[:, 128 * pp:128 * (pp + 1)]
            if not rev:
                yp = yp + dskip_ref[:, sl] * xm[:, sl]
            om_ref[0, rs, sl] = yp.astype(om_ref.dtype)
        sm_ref[d, g] = sm * dec_m[:, gl] + _dot_tn(bm_g, wx[:, gl])


def _scan_kernel(qf, ff, vf, xf, xpf, xnf, dtf, qb, fb, vb_, xb, xpb, xnb, dtb,
                 lb_ref, cw_ref, cbias_ref, dtbias_ref, aexp_ref, dskip_ref,
                 ohf, omf, ohb, omb, sg_ref, sm_ref, uf_scr, ub_scr, *, nct, nc):
    c = pl.program_id(1)

    @pl.when(c == 0)
    def _():
        sg_ref[...] = jnp.zeros(sg_ref.shape, F32)
        sm_ref[...] = jnp.zeros(sm_ref.shape, F32)

    cb = jnp.where(c < nct, nct - 1 - c, nc + nct - 1 - c)

    def edge_flags(ch):
        pv = jnp.where((ch == 0) | (ch == nct), 0.0, 1.0)
        nv = jnp.where((ch == nct - 1) | (ch == nc - 1), 0.0, 1.0)
        return pv, nv

    def fill(u_scr, x_ref, xp_ref, xn_ref, ch):
        pv, nv = edge_flags(ch)
        u_scr[0:8, :] = xp_ref[0] * pv
        u_scr[8:8 + SCAN_BLOCK, :] = x_ref[0]
        u_scr[8 + SCAN_BLOCK:16 + SCAN_BLOCK, :] = xn_ref[0] * nv

    fill(uf_scr, xf, xpf, xnf, c)
    fill(ub_scr, xb, xpb, xnb, cb)
    starts = range(0, SCAN_BLOCK, L)
    for r0 in starts:
        _scan_dir(False, r0, qf, ff, vf, dtf, lb_ref[0:1, :], cw_ref, cbias_ref,
                  dtbias_ref, aexp_ref, dskip_ref, ohf, omf, sg_ref, sm_ref, uf_scr)
    for r0 in reversed(starts):
        _scan_dir(True, r0, qb, fb, vb_, dtb, lb_ref[1:2, :], cw_ref, cbias_ref,
                  dtbias_ref, aexp_ref, dskip_ref, ohb, omb, sg_ref, sm_ref, ub_scr)


def _scan_mixers(p, nct, lb, conv_w, conv_b, dt_bias, a_log, d_skip):
    bsz, s, _ = p.shape
    sb = SCAN_BLOCK
    nc = s // sb
    nct = nct * L // sb
    nb8 = s // 8
    h8 = sb // 8

    def bmap(c):
        return jnp.where(c < nct, nct - 1 - c, nc + nct - 1 - c)

    def specs(cmap, fcol):
        colblk = lambda col, w: col // w
        return [
            pl.BlockSpec((1, sb, 512), lambda i, c: (i, cmap(c), colblk(COL_Q, 512))),
            pl.BlockSpec((1, sb, 512), lambda i, c: (i, cmap(c), colblk(fcol, 512))),
            pl.BlockSpec((1, sb, 512), lambda i, c: (i, cmap(c), colblk(COL_I, 512))),
            pl.BlockSpec((1, sb, 1024), lambda i, c: (i, cmap(c), colblk(COL_X, 1024))),
            pl.BlockSpec((1, 8, 1024), lambda i, c: (i, jnp.maximum(cmap(c) * h8 - 1, 0), colblk(COL_X, 1024))),
            pl.BlockSpec((1, 8, 1024), lambda i, c: (i, jnp.minimum(cmap(c) * h8 + h8, nb8 - 1), colblk(COL_X, 1024))),
            pl.BlockSpec((1, sb, 128), lambda i, c: (i, cmap(c), colblk(COL_DT, 128))),
        ]

    ident = lambda c: c
    const2 = lambda i, c: (0, 0)
    dtb = jnp.zeros((1, 128), F32).at[0, :16].set(dt_bias.reshape(16))
    aexp = jnp.zeros((1, 128), F32).at[0, :16].set(jnp.exp(a_log.reshape(16)))
    dsk = jnp.repeat(d_skip, 64).reshape(1, 512)
    kern = functools.partial(_scan_kernel, nct=nct, nc=nc)
    out_sd = jax.ShapeDtypeStruct((bsz, s, 512), BF16)
    return pl.pallas_call(
        kern,
        grid=(bsz, nc),
        in_specs=specs(ident, COL_FF) + specs(bmap, COL_FB) + [
            pl.BlockSpec((2, 512), const2), pl.BlockSpec((5, 1024), const2),
            pl.BlockSpec((1, 1024), const2), pl.BlockSpec((1, 128), const2),
            pl.BlockSpec((1, 128), const2), pl.BlockSpec((1, 512), const2)],
        out_specs=[pl.BlockSpec((1, sb, 512), lambda i, c: (i, c, 0)),
                   pl.BlockSpec((1, sb, 512), lambda i, c: (i, c, 0)),
                   pl.BlockSpec((1, sb, 512), lambda i, c: (i, bmap(c), 0)),
                   pl.BlockSpec((1, sb, 512), lambda i, c: (i, bmap(c), 0))],
        out_shape=[out_sd, out_sd, out_sd, out_sd],
        scratch_shapes=[pltpu.VMEM((2, HA // 2, 128, 128), F32),
                        pltpu.VMEM((2, 2, 128, 256), F32),
                        pltpu.VMEM((sb + 16, 1024), F32), pltpu.VMEM((sb + 16, 1024), F32)],
        compiler_params=_cparams(("parallel", "arbitrary")),
        name="scan_mixers",
    )(*([p] * 14), lb, conv_w, conv_b.reshape(1, 1024), dtb, aexp, dsk)


def _merge_kernel(ohf, ohb, omf, omb, g_ref, z_ref, xc_ref, xl_ref, gate_ref, hn_ref, mn_ref, w_ref,
                  fg_ref, fsc_ref, fsh_ref, rw_ref, oc_ref, ol_ref, h_ref, lg_ref, *, nct_tiles):
    oh = ohf[0].astype(F32) + ohb[0].astype(F32)
    sq = oh * oh
    hi = sq.astype(BF16)
    lo = (sq - hi.astype(F32)).astype(BF16)
    r = lax.broadcasted_iota(jnp.int32, (512, 512), 0) // 64
    c = lax.broadcasted_iota(jnp.int32, (512, 512), 1) // 64
    avg = jnp.where(r == c, 1.0 / 64, 0.0).astype(BF16)
    ms = jnp.dot(hi, avg, preferred_element_type=F32) + jnp.dot(lo, avg, preferred_element_type=F32)
    oh = oh * lax.rsqrt(ms + EPS) * hn_ref[...] * jax.nn.sigmoid(g_ref[0])
    y = (omf[0].astype(F32) + omb[0].astype(F32)) * _silu(z_ref[0])
    y = y * lax.rsqrt(jnp.mean(y * y, axis=-1, keepdims=True) + EPS) * mn_ref[...]
    cat = jnp.concatenate([oh, y], axis=1).astype(BF16)
    j = pl.program_id(1)
    x = jnp.where(j < nct_tiles, xc_ref[0], xl_ref[0])
    out = x + gate_ref[0, 0] * jnp.dot(cat, w_ref[...], preferred_element_type=F32)
    _router_body(out, fg_ref[...], fsc_ref[0, 0], fsh_ref[0, 0], rw_ref, h_ref, lg_ref)

    @pl.when(j < nct_tiles)
    def _():
        oc_ref[0] = out

    @pl.when(j >= nct_tiles)
    def _():
        ol_ref[0] = out


def _merge_mixers(x_ctx, x_lat, gate2, ohf, ohb, omf, omb, p, hgrn_norm, mamba_norm, w_out,
                  ffn_gain, ffn_scale2, ffn_shift2, w_router, tm):
    b, tc, d = x_ctx.shape
    s = tc + x_lat.shape[1]
    nct_tiles = tc // tm
    e = w_router.shape[1]
    sel = lambda i, j: (i, jnp.where(j < nct_tiles, 0, 1), 0, 0)
    row = lambda i, j: (i, j, 0)
    const2 = lambda i, j: (0, 0)
    return pl.pallas_call(
        functools.partial(_merge_kernel, nct_tiles=nct_tiles),
        grid=(b, s // tm),
        in_specs=[pl.BlockSpec((1, tm, 512), row)] * 4 + [
            pl.BlockSpec((1, tm, 512), lambda i, j: (i, j, COL_G // 512)),
            pl.BlockSpec((1, tm, 512), lambda i, j: (i, j, COL_Z // 512))] + _ctx_lat_specs(
                tm, d, nct_tiles) + [
            pl.BlockSpec((1, 1, 1, d), sel),
            pl.BlockSpec((1, 512), const2), pl.BlockSpec((1, 512), const2),
            pl.BlockSpec((2 * 512, d), const2),
            pl.BlockSpec((1, d), const2),
            pl.BlockSpec((1, 1, 1, d), sel), pl.BlockSpec((1, 1, 1, d), sel),
            pl.BlockSpec((3 * d, e), const2)],
        out_specs=[pl.BlockSpec((1, tm, d), lambda i, j: (i, jnp.minimum(j, nct_tiles - 1), 0)),
                   pl.BlockSpec((1, tm, d), lambda i, j: (i, jnp.maximum(j - nct_tiles, 0), 0)),
                   pl.BlockSpec((1, tm, d), row), pl.BlockSpec((1, tm, e), row)],
        out_shape=[jax.ShapeDtypeStruct((b, tc, d), F32),
                   jax.ShapeDtypeStruct((b, s - tc, d), F32),
                   jax.ShapeDtypeStruct((b, s, d), BF16),
                   jax.ShapeDtypeStruct((b, s, e), F32)],
        compiler_params=_cparams(("parallel", "arbitrary")),
        name="merge_mixers",
    )(ohf, ohb, omf, omb, p, p, x_ctx, x_lat, gate2.reshape(b, 2, 1, d), hgrn_norm.reshape(1, 512),
      mamba_norm.reshape(1, 512), w_out, ffn_gain.reshape(1, d), ffn_scale2.reshape(b, 2, 1, d),
      ffn_shift2.reshape(b, 2, 1, d), _router_weights(w_router))


def _rope_tables(t):
    rows = t // GRID_W
    row = jnp.repeat(jnp.arange(rows, dtype=jnp.int32), GRID_W)
    col = jnp.tile(jnp.arange(GRID_W, dtype=jnp.int32), rows)
    nf = ROPE // 4
    inv_freq = ROPE_THETA ** (-jnp.arange(nf, dtype=F32) / nf)
    pos = jnp.stack([row, col], axis=-1).astype(F32)
    ang = pos[..., None] * inv_freq
    cos, sin = jnp.cos(ang), jnp.sin(ang)
    cos64 = jnp.broadcast_to(cos[:, :, None, :], (t, 2, 2, nf)).reshape(t, ROPE)
    sin64 = jnp.broadcast_to(sin[:, :, None, :], (t, 2, 2, nf)).reshape(t, ROPE)
    zero = jnp.zeros((t, ROPE), F32)
    return jnp.concatenate([cos64, zero], -1), jnp.concatenate([sin64, zero], -1)


def _swap_cols(w):
    nf = ROPE // 4
    wr = w.reshape(w.shape[:-1] + (2, 2, nf))
    return jnp.stack([-wr[..., 1, :], wr[..., 0, :]], axis=-2).reshape(w.shape)


def _rope_tile(tile, cos_t, sin_t):
    return tile * cos_t + pltpu.roll(tile, ROPE, axis=1) * sin_t


def _mla_down_kernel(x_ref, g_ref, sc_ref, sh_ref, w_ref, qn_ref, kvn_ref, cos_ref, sin_ref,
                     cq_ref, ckv_ref, kr_ref):
    a = _modnorm(x_ref[0], g_ref[...], sc_ref[0], sh_ref[0]).astype(BF16)
    c = jnp.dot(a, w_ref[...], preferred_element_type=F32)
    cq = c[:, :Q_LORA]
    ckv = c[:, Q_LORA:Q_LORA + KV_LORA]
    cq_ref[0] = _rms(cq, qn_ref[...]).astype(BF16)
    ckv_ref[0] = _rms(ckv, kvn_ref[...]).astype(BF16)
    kr_ref[0] = _rope_tile(c[:, Q_LORA + KV_LORA:], cos_ref[...], sin_ref[...]).astype(BF16)


def _mla_down(x, gain, scale, shift, w_cat, q_norm, kv_norm, cos_t, sin_t, tm):
    b, t, d = x.shape
    n = w_cat.shape[1]
    return pl.pallas_call(
        _mla_down_kernel,
        grid=(b, t // tm),
        in_specs=[pl.BlockSpec((1, tm, d), lambda i, j: (i, j, 0)),
                  pl.BlockSpec((1, d), lambda i, j: (0, 0)),
                  pl.BlockSpec((1, 1, d), lambda i, j: (i, 0, 0)),
                  pl.BlockSpec((1, 1, d), lambda i, j: (i, 0, 0)),
                  pl.BlockSpec((d, n), lambda i, j: (0, 0)),
                  pl.BlockSpec((1, Q_LORA), lambda i, j: (0, 0)),
                  pl.BlockSpec((1, KV_LORA), lambda i, j: (0, 0)),
                  pl.BlockSpec((tm, 128), lambda i, j: (j, 0)),
                  pl.BlockSpec((tm, 128), lambda i, j: (j, 0))],
        out_specs=[pl.BlockSpec((1, tm, Q_LORA), lambda i, j: (i, j, 0)),
                   pl.BlockSpec((1, tm, KV_LORA), lambda i, j: (i, j, 0)),
                   pl.BlockSpec((1, tm, 128), lambda i, j: (i, j, 0))],
        out_shape=[jax.ShapeDtypeStruct((b, t, Q_LORA), BF16),
                   jax.ShapeDtypeStruct((b, t, KV_LORA), BF16),
                   jax.ShapeDtypeStruct((b, t, 128), BF16)],
        compiler_params=_cparams(("parallel", "parallel")),
        name="mla_down",
    )(x, gain.reshape(1, d), scale, shift, w_cat, q_norm.reshape(1, -1), kv_norm.reshape(1, -1),
      cos_t, sin_t)


def _q_up_kernel(cq_ref, w_ref, cos_ref, sin_ref, q_ref):
    cq = cq_ref[0]
    for h in range(H_C):
        q = jnp.dot(cq, w_ref[h], preferred_element_type=F32)
        qn = q[:, :NOPE] * QK_SCALE
        qr = _rope_tile(q[:, NOPE:], cos_ref[...], sin_ref[...]) * QK_SCALE
        q_ref[0, h] = jnp.concatenate([qn, qr], axis=1).astype(BF16)


def _q_up(cq, w_uq_h, cos_t, sin_t, tm):
    b, t, r = cq.shape
    return pl.pallas_call(
        _q_up_kernel,
        grid=(b, t // tm),
        in_specs=[pl.BlockSpec((1, tm, r), lambda i, j: (i, j, 0)),
                  pl.BlockSpec((H_C, r, QK_DIM), lambda i, j: (0, 0, 0)),
                  pl.BlockSpec((tm, 128), lambda i, j: (j, 0)),
                  pl.BlockSpec((tm, 128), lambda i, j: (j, 0))],
        out_specs=pl.BlockSpec((1, H_C, tm, QK_DIM), lambda i, j: (i, 0, j, 0)),
        out_shape=jax.ShapeDtypeStruct((b, H_C, t, QK_DIM), BF16),
        compiler_params=_cparams(("parallel", "parallel")),
        name="q_up",
    )(cq, w_uq_h, cos_t, sin_t)


def _kv_up_kernel(ckv_ref, kr_ref, w_ref, k_ref, v_ref):
    ckv = ckv_ref[0]
    kr = kr_ref[0]
    for h in range(H_C):
        kv = jnp.dot(ckv, w_ref[h], preferred_element_type=F32)
        k_ref[0, h] = jnp.concatenate([kv[:, :NOPE].astype(BF16), kr], axis=1)
        v_ref[0, h] = kv[:, NOPE:].astype(BF16)


def _kv_up(ckv, kr, w_ukv_h, tm):
    b, s, r = ckv.shape
    return pl.pallas_call(
        _kv_up_kernel,
        grid=(b, s // tm),
        in_specs=[pl.BlockSpec((1, tm, r), lambda i, j: (i, j, 0)),
                  pl.BlockSpec((1, tm, 128), lambda i, j: (i, j, 0)),
                  pl.BlockSpec((H_C, r, NOPE + VH), lambda i, j: (0, 0, 0))],
        out_specs=[pl.BlockSpec((1, H_C, tm, QK_DIM), lambda i, j: (i, 0, j, 0)),
                   pl.BlockSpec((1, H_C, tm, VH), lambda i, j: (i, 0, j, 0))],
        out_shape=[jax.ShapeDtypeStruct((b, H_C, s, QK_DIM), BF16),
                   jax.ShapeDtypeStruct((b, H_C, s, VH), BF16)],
        compiler_params=_cparams(("parallel", "parallel")),
        name="kv_up",
    )(ckv, kr, w_ukv_h)


def _attn_kernel(qa_ref, qb_ref, q0_ref, k_ref, v_ref, o_ref, s0_scr, s1_scr, m_scr, *, tk, nk, tq):
    n = pl.program_id(2)

    def scores(q, j):
        ks = k_ref[0, 0, j * tk:(j + 1) * tk, :]
        return lax.dot_general(q, ks, (((1,), (1,)), ((), ())), preferred_element_type=F32)

    def fold_max(mrun, s):
        for c in range(tk // 128):
            mrun = jnp.maximum(mrun, s[:, c * 128:(c + 1) * 128])
        return mrun

    @pl.when(n == 0)
    def _():
        q0 = q0_ref[0, 0]
        mrun = jnp.full((tq, 128), -jnp.inf, F32)
        for j in range(nk):
            s = scores(q0, j)
            s0_scr[:, j * tk:(j + 1) * tk] = s
            mrun = fold_max(mrun, s)
        m_scr[...] = jnp.max(mrun, axis=-1, keepdims=True)

    def fused(qn, s_read, s_write, m):
        mrun = jnp.full((tq, 128), -jnp.inf, F32)
        lrun = jnp.zeros((tq, 128), F32)
        acc = jnp.zeros((tq, VH), F32)
        for j in range(nk):
            sn = scores(qn, j)
            s_write[:, j * tk:(j + 1) * tk] = sn
            mrun = fold_max(mrun, sn)
            p = jnp.exp2(s_read[:, j * tk:(j + 1) * tk] - m)
            for c in range(tk // 128):
                lrun = lrun + p[:, c * 128:(c + 1) * 128]
            acc = acc + jnp.dot(p.astype(BF16), v_ref[0, 0, j * tk:(j + 1) * tk, :],
                                preferred_element_type=F32)
        out = acc / jnp.sum(lrun, axis=-1, keepdims=True)
        return out, jnp.max(mrun, axis=-1, keepdims=True)

    out_a, m1 = fused(qa_ref[0, 0], s0_scr, s1_scr, m_scr[...])
    o_ref[0, 0:tq, :] = out_a.astype(o_ref.dtype)
    out_b, m2 = fused(qb_ref[0, 0], s1_scr, s0_scr, m1)
    o_ref[0, tq:2 * tq, :] = out_b.astype(o_ref.dtype)
    m_scr[...] = m2


def _attention(q, k, v, tq, tk):
    b, h, t, _ = q.shape
    s = k.shape[2]
    nq = t // tq
    assert t % (2 * tq) == 0 and s % tk == 0
    kern = functools.partial(_attn_kernel, tk=tk, nk=s // tk, tq=tq)
    return pl.pallas_call(
        kern,
        grid=(b, h, nq // 2),
        in_specs=[pl.BlockSpec((1, 1, tq, QK_DIM), lambda i, j, n: (i, j, 2 * n + 1, 0)),
                  pl.BlockSpec((1, 1, tq, QK_DIM),
                               lambda i, j, n: (i, j, jnp.minimum(2 * n + 2, nq - 1), 0)),
                  pl.BlockSpec((1, 1, tq, QK_DIM), lambda i, j, n: (i, j, 0, 0)),
                  pl.BlockSpec((1, 1, s, QK_DIM), lambda i, j, n: (i, j, 0, 0)),
                  pl.BlockSpec((1, 1, s, VH), lambda i, j, n: (i, j, 0, 0))],
        out_specs=pl.BlockSpec((1, 2 * tq, VH), lambda i, j, n: (i, n, j)),
        out_shape=jax.ShapeDtypeStruct((b, t, h * VH), BF16),
        scratch_shapes=[pltpu.VMEM((tq, s), F32), pltpu.VMEM((tq, s), F32),
                        pltpu.VMEM((tq, 1), F32)],
        compiler_params=_cparams(("parallel", "parallel", "arbitrary")),
        name="attention",
    )(q, q, q, k, v)


def _ffn_kernel(x_ref, wg_ref, wu_ref, wd_ref, g_ref, o_ref, acc_scr, wgb, wub, wdb, *, fc):
    @pl.when((pl.program_id(1) == 0) & (pl.program_id(2) == 0))
    def _():
        wgb[...] = wg_ref[0, 0].astype(BF16)
        wub[...] = wu_ref[0, 0].astype(BF16)
        wdb[...] = wd_ref[0, 0].astype(BF16)

    x = x_ref[0, 0]
    nf = wgb.shape[-1]
    for c in range(nf // fc):
        sl = slice(c * fc, (c + 1) * fc)
        hg = jnp.dot(x, wgb[:, sl], preferred_element_type=F32)
        hu = jnp.dot(x, wub[:, sl], preferred_element_type=F32)
        hid = (_silu(hg) * hu).astype(BF16)
        part = jnp.dot(hid, wdb[sl, :], preferred_element_type=F32)
        if c == 0:
            acc_scr[...] = part
        else:
            acc_scr[...] += part
    o_ref[0, 0] = (acc_scr[...] * g_ref[0, 0]).astype(o_ref.dtype)


def _expert_ffn(xg, gates, w_gate, w_up, w_down, layer, tm, fc):
    b, e, cap, d = xg.shape
    f = w_gate.shape[-1]
    tm = min(tm, cap)
    kern = functools.partial(_ffn_kernel, fc=fc)
    return pl.pallas_call(
        kern,
        grid=(e, b, cap // tm),
        in_specs=[pl.BlockSpec((1, 1, tm, d), lambda j, i, m: (i, j, m, 0)),
                  pl.BlockSpec((1, 1, d, f), lambda j, i, m: (layer, j, 0, 0)),
                  pl.BlockSpec((1, 1, d, f), lambda j, i, m: (layer, j, 0, 0)),
                  pl.BlockSpec((1, 1, f, d), lambda j, i, m: (layer, j, 0, 0)),
                  pl.BlockSpec((1, 1, tm, 1), lambda j, i, m: (i, j, m, 0))],
        out_specs=pl.BlockSpec((1, 1, tm, d), lambda j, i, m: (i, j, m, 0)),
        out_shape=jax.ShapeDtypeStruct((b, e, cap, d), BF16),
        scratch_shapes=[pltpu.VMEM((tm, d), F32), pltpu.VMEM((d, f), BF16),
                        pltpu.VMEM((d, f), BF16), pltpu.VMEM((f, d), BF16)],
        compiler_params=_cparams(("arbitrary", "arbitrary", "arbitrary")),
        name="expert_ffn",
    )(xg, w_gate, w_up, w_down, gates)


SLOT_WIN = 256
TOK_BLK = 128
COMBINE_EXPERTS = 4


def _combine_kernel(lo_ref, idx_ref, y_ref, x_ref, gate_ref, fg_ref, o_ref, *, nblk, nrow,
                    final_norm):
    b, j, eg = pl.program_id(0), pl.program_id(1), pl.program_id(2)

    @pl.when(eg == 0)
    def _():
        o_ref[...] = jnp.zeros(o_ref.shape, F32)

    tok = lax.broadcasted_iota(jnp.int32, (TOK_BLK, 1), 0)
    for blk in range(nblk):
        gblk = j * nblk + blk
        part = None
        for k in range(COMBINE_EXPERTS):
            w = jnp.minimum(lo_ref[b, eg * COMBINE_EXPERTS + k, gblk] // TOK_BLK, nrow - 2)
            ids = jnp.concatenate([idx_ref[0, k, pl.ds(w, 1), :],
                                   idx_ref[0, k, pl.ds(w + 1, 1), :]], axis=1)
            onehot = (ids == tok + gblk * TOK_BLK).astype(BF16)
            yw = y_ref[0, k, pl.ds(pl.multiple_of(w * TOK_BLK, TOK_BLK), SLOT_WIN), :]
            d = jnp.dot(onehot, yw, preferred_element_type=F32)
            part = d if part is None else part + d
        rows = slice(blk * TOK_BLK, (blk + 1) * TOK_BLK)
        o_ref[0, rows, :] += part

    @pl.when(eg == pl.num_programs(2) - 1)
    def _():
        out = x_ref[0] + gate_ref[0] * o_ref[0]
        o_ref[0] = _rms(out, fg_ref[...]) if final_norm else out


def _combine(x, gate_out, idx_s, y, final_gain=None):
    b, t, d = x.shape
    e, cap = idx_s.shape[1:]
    capp = max(cap, SLOT_WIN)
    if capp != cap:
        idx_s = jnp.pad(idx_s, ((0, 0), (0, 0), (0, capp - cap)), constant_values=-1)
        y = jnp.pad(y[:, :, :cap], ((0, 0), (0, 0), (0, capp - cap), (0, 0)))
    nrow = capp // TOK_BLK
    ts = min(2048, t)
    nblk = ts // TOK_BLK
    bounds = jnp.arange(t // TOK_BLK, dtype=jnp.int32) * TOK_BLK
    valid = idx_s >= 0
    lo = jnp.sum((valid[..., None] & (idx_s[..., None] < bounds)).astype(jnp.int32), axis=2)
    kern = functools.partial(_combine_kernel, nblk=nblk, nrow=nrow,
                             final_norm=final_gain is not None)
    fg = jnp.ones((1, d), F32) if final_gain is None else final_gain.reshape(1, d)
    grid_spec = pltpu.PrefetchScalarGridSpec(
        num_scalar_prefetch=1,
        grid=(b, t // ts, e // COMBINE_EXPERTS),
        in_specs=[pl.BlockSpec((1, COMBINE_EXPERTS, nrow, TOK_BLK), lambda i, j, k, lo_r: (i, k, 0, 0)),
                  pl.BlockSpec((1, COMBINE_EXPERTS, capp, d), lambda i, j, k, lo_r: (i, k, 0, 0)),
                  pl.BlockSpec((1, ts, d), lambda i, j, k, lo_r: (i, j, 0)),
                  pl.BlockSpec((1, 1, d), lambda i, j, k, lo_r: (i, 0, 0)),
                  pl.BlockSpec((1, d), lambda i, j, k, lo_r: (0, 0))],
        out_specs=pl.BlockSpec((1, ts, d), lambda i, j, k, lo_r: (i, j, 0)),
    )
    return pl.pallas_call(
        kern,
        grid_spec=grid_spec,
        out_shape=jax.ShapeDtypeStruct((b, t, d), F32),
        compiler_params=_cparams(("parallel", "parallel", "arbitrary")),
        name="moe_combine",
    )(lo, idx_s.reshape(b, e, nrow, TOK_BLK), y, x, gate_out, fg)


def _route(logits):
    t = logits.shape[1]
    cap = CAP_FACTOR * t // N_EXPERTS
    aff = jax.nn.softmax(logits, axis=-1)
    g, idx = lax.top_k(jnp.swapaxes(aff, 1, 2), cap)
    return lax.sort((idx, g), dimension=2, num_keys=1)


def _moe(x, hn, logits, gate_out, w_gate, w_up, w_down, layer, final_gain=None):
    b, t, d = x.shape
    idx_s, g_s = _route(logits)
    bidx = jnp.arange(b)[:, None, None]
    xg = hn[bidx, idx_s]
    y = _expert_ffn(xg, g_s[..., None], w_gate, w_up, w_down, layer, 1024, 512)
    return _combine(x, gate_out, idx_s, y, final_gain)


def _moe_two_streams(x_ctx, x_lat, hn, logits, gate_ctx, gate_lat, w_gate, w_up, w_down, layer):
    b, tc, d = x_ctx.shape
    idx_c, g_c = _route(logits[:, :tc])
    idx_l, g_l = _route(logits[:, tc:])
    idx_cat = jnp.concatenate([idx_l + tc, idx_c], axis=2)
    g_cat = jnp.concatenate([g_l, g_c], axis=2)
    bidx = jnp.arange(b)[:, None, None]
    xg = hn[bidx, idx_cat]
    cap = idx_cat.shape[2]
    tm = _pick_tile(cap, 1024, 528, 512, 352, 256, 176, 96, 32, 16)
    y = _expert_ffn(xg, g_cat[..., None], w_gate, w_up, w_down, layer, tm, 512)
    new_lat = _combine(x_lat, gate_lat, idx_l, y)
    new_ctx = _combine(x_ctx, gate_ctx, idx_c, y[:, :, idx_l.shape[2]:])
    return new_ctx, new_lat


def kernel(x, c, ctx, c_ctx, w_mod, b_mod, norm_mix, norm_ffn, norm_out, w_in, w_out_rec,
           conv_w, conv_b, lb_gamma, dt_bias, a_log, d_skip, hgrn_norm, mamba_norm,
           w_dq, q_norm, w_uq, w_dkv, kv_norm, w_ukv, w_kr, w_o,
           w_router, w_gate, w_up, w_down):
    B, T, D = x.shape
    TC = ctx.shape[1]
    lb = jnp.cumsum(jax.nn.softmax(lb_gamma.astype(F32), axis=0), axis=0)

    cvec = jnp.zeros((16, D), F32).at[:B].set(c).at[B].set(c_ctx)
    mods = _modulation(cvec, w_mod, b_mod)

    x_lat, x_ctx = x, ctx
    for l in range(DEPTH):
        need_ctx = l < DEPTH - 1
        m_lat = [mods[l, :B, i * D:(i + 1) * D][:, None, :] for i in range(6)]
        m_ctx = [jnp.broadcast_to(mods[l, B, i * D:(i + 1) * D][None, None, :], (B, 1, D))
                 for i in range(6)]
        if l % 2 == 0:
            e = l // 2
            w_in_p = jnp.pad(w_in[e], ((0, 0), (0, IN_COLS_PAD - IN_COLS))).astype(BF16)
            pair = lambda i: jnp.concatenate([m_ctx[i], m_lat[i]], axis=1)
            p = _modproj(x_ctx, x_lat, norm_mix[l], pair(1), pair(0), w_in_p, 256)
            ohf, omf, ohb, omb = _scan_mixers(p, TC // CHUNK, lb[e], conv_w[e], conv_b[e],
                                              dt_bias[e], a_log[e], d_skip[e])
            x_ctx, x_lat, hn, logits = _merge_mixers(
                x_ctx, x_lat, pair(2), ohf, ohb, omf, omb, p, hgrn_norm[e], mamba_norm[e],
                w_out_rec[e].astype(BF16), norm_ffn[l], pair(4), pair(3), w_router[l], 256)
            if not need_ctx:
                hn, logits = hn[:, TC:], logits[:, TC:]
        else:
            assert not need_ctx
            j = l // 2
            cos_l, sin_l = _rope_tables(T)
            cos_c = jnp.concatenate([jnp.ones((TC, ROPE), F32), jnp.zeros((TC, ROPE), F32)], -1)
            sin_c = jnp.zeros((TC, 128), F32)
            w_cat = jnp.concatenate([w_dq[j], w_dkv[j], w_kr[j], _swap_cols(w_kr[j])],
                                    axis=1).astype(BF16)
            wq = w_uq[j].reshape(Q_LORA, H_C, NOPE + ROPE)
            wq_h = jnp.concatenate([wq, _swap_cols(wq[..., NOPE:])], axis=-1)
            wq_h = jnp.transpose(wq_h, (1, 0, 2)).astype(BF16)
            wkv_h = jnp.transpose(w_ukv[j].reshape(KV_LORA, H_C, NOPE + VH), (1, 0, 2)).astype(BF16)
            cq_l, ckv_l, kr_l = _mla_down(x_lat, norm_mix[l], m_lat[1], m_lat[0], w_cat,
                                          q_norm[j], kv_norm[j], cos_l, sin_l, 512)
            _, ckv_c, kr_c = _mla_down(x_ctx, norm_mix[l], m_ctx[1], m_ctx[0], w_cat,
                                       q_norm[j], kv_norm[j], cos_c, sin_c, 256)
            ckv = jnp.concatenate([ckv_c, ckv_l], axis=1)
            kr = jnp.concatenate([kr_c, kr_l], axis=1)
            qh = _q_up(cq_l, wq_h, cos_l, sin_l, _pick_tile(T, 1024, 512, 256))
            kh, vh = _kv_up(ckv, kr, wkv_h, _pick_tile(T + TC, 1408, 768, 256))
            o = _attention(qh, kh, vh, 256, 768)
            x_lat, hn, logits = _linear_residual(x_lat, m_lat[2], o, w_o[j].astype(BF16),
                                                 norm_ffn[l], m_lat[4], m_lat[3], w_router[l],
                                                 _pick_tile(T, 1024, 512))
        if need_ctx:
            x_ctx, x_lat = _moe_two_streams(x_ctx, x_lat, hn, logits, m_ctx[5], m_lat[5],
                                            w_gate, w_up, w_down, l)
        else:
            x_lat = _moe(x_lat, hn, logits, m_lat[5], w_gate, w_up, w_down, l, norm_out)
    return x_lat
```

```python
import functools
import math

import jax
import jax.numpy as jnp
from jax import lax
from jax.experimental import pallas as pl
from jax.experimental.pallas import tpu as pltpu

F32 = jnp.float32
BF16 = jnp.bfloat16

D_MODEL = 1024
DEPTH = 2
GRID_W = 64
EPS = 1e-6
CHUNK = 64
HA, DKA, DVA = 8, 64, 64
WA = HA * DVA
HB, PB = 8, 64
D_INNER = HB * PB
N_GROUPS, D_STATE = 2, 128
CONV_W = 5
REC_SPLITS = (HA * DKA, HA * DKA, HA * DKA, WA, WA, D_INNER, D_INNER,
              N_GROUPS * D_STATE, N_GROUPS * D_STATE, HB, HB)
IN_COLS = sum(REC_SPLITS)
IN_COLS_PAD = 4224
H_C, NOPE, ROPE, VH = 8, 128, 64, 128
Q_LORA, KV_LORA = 384, 256
ROPE_THETA = 10000.0
ATTN_SCALE = 1.0 / math.sqrt(NOPE + ROPE)
QK_SCALE = ATTN_SCALE * math.log2(math.e)
QK_DIM = 256
N_EXPERTS = 16
EXPERT_FF = 1024
CAP_FACTOR = 2

VMEM_LIMIT = 56 * 1024 * 1024


def _cparams(sem):
    return pltpu.CompilerParams(dimension_semantics=sem, vmem_limit_bytes=VMEM_LIMIT)


def _pick_tile(n, *candidates):
    return next(c for c in candidates if n % c == 0)


def _silu(v):
    return v * jax.nn.sigmoid(v)


def _modnorm(x, gain, scale, shift):
    ms = jnp.mean(x * x, axis=-1, keepdims=True)
    return (x * lax.rsqrt(ms + EPS) * gain) * (1.0 + scale) + shift


def _mod_kernel(s_ref, w_ref, b_ref, o_ref):
    s = _silu(s_ref[...])
    o_ref[0] = jnp.dot(s, w_ref[0], preferred_element_type=F32,
                       precision=lax.Precision.HIGHEST) + b_ref[0]


def _modulation(cvec, w_mod, b_mod):
    n = w_mod.shape[-1]
    tn = 1536
    return pl.pallas_call(
        _mod_kernel,
        grid=(DEPTH, n // tn),
        in_specs=[pl.BlockSpec((16, D_MODEL), lambda l, j: (0, 0)),
                  pl.BlockSpec((1, D_MODEL, tn), lambda l, j: (l, 0, j)),
                  pl.BlockSpec((1, 1, tn), lambda l, j: (l, 0, j))],
        out_specs=pl.BlockSpec((1, 16, tn), lambda l, j: (l, 0, j)),
        out_shape=jax.ShapeDtypeStruct((DEPTH, 16, n), F32),
        compiler_params=_cparams(("parallel", "parallel")),
        name="modulation",
    )(cvec, w_mod, b_mod.reshape(DEPTH, 1, n))


def _ctx_lat_specs(tm, d, nct_tiles):
    return [pl.BlockSpec((1, tm, d), lambda i, j: (i, jnp.minimum(j, nct_tiles - 1), 0)),
            pl.BlockSpec((1, tm, d), lambda i, j: (i, jnp.maximum(j - nct_tiles, 0), 0))]


def _modproj_kernel(xc_ref, xl_ref, g_ref, sc_ref, sh_ref, w_ref, o_ref, *, nct_tiles):
    x = jnp.where(pl.program_id(1) < nct_tiles, xc_ref[0], xl_ref[0])
    a = _modnorm(x, g_ref[...], sc_ref[0, 0], sh_ref[0, 0]).astype(BF16)
    o_ref[0] = jnp.dot(a, w_ref[...], preferred_element_type=F32).astype(o_ref.dtype)


def _modproj(x_ctx, x_lat, gain, scale2, shift2, w, tm, out_dtype=F32):
    b, tc, d = x_ctx.shape
    t = tc + x_lat.shape[1]
    nct_tiles = tc // tm
    n = w.shape[1]
    sel = lambda i, j: (i, jnp.where(j < nct_tiles, 0, 1), 0, 0)
    return pl.pallas_call(
        functools.partial(_modproj_kernel, nct_tiles=nct_tiles),
        grid=(b, t // tm),
        in_specs=_ctx_lat_specs(tm, d, nct_tiles) + [
                  pl.BlockSpec((1, d), lambda i, j: (0, 0)),
                  pl.BlockSpec((1, 1, 1, d), sel),
                  pl.BlockSpec((1, 1, 1, d), sel),
                  pl.BlockSpec((d, n), lambda i, j: (0, 0))],
        out_specs=pl.BlockSpec((1, tm, n), lambda i, j: (i, j, 0)),
        out_shape=jax.ShapeDtypeStruct((b, t, n), out_dtype),
        compiler_params=_cparams(("parallel", "parallel")),
        name="modproj",
    )(x_ctx, x_lat, gain.reshape(1, d), scale2.reshape(b, 2, 1, d), shift2.reshape(b, 2, 1, d), w)


def _router_body(x, gain, scale, shift, w_ref, h_ref, lg_ref):
    hn = _modnorm(x, gain, scale, shift)
    a = hn.astype(BF16)
    h_ref[0] = a
    b = (hn - a.astype(F32)).astype(BF16)
    lg_ref[0] = jnp.dot(jnp.concatenate([a, b, a], axis=1), w_ref[...], preferred_element_type=F32)


def _router_weights(w_router):
    w_hi = w_router.astype(BF16)
    w_lo = (w_router - w_hi.astype(F32)).astype(BF16)
    return jnp.concatenate([w_hi, w_hi, w_lo], axis=0)


def _linres_kernel(x_ref, gate_ref, y_ref, w_ref, fg_ref, fsc_ref, fsh_ref, rw_ref,
                   o_ref, h_ref, lg_ref):
    acc = jnp.dot(y_ref[0].astype(BF16), w_ref[...], preferred_element_type=F32)
    out = x_ref[0] + gate_ref[0] * acc
    o_ref[0] = out
    _router_body(out, fg_ref[...], fsc_ref[0], fsh_ref[0], rw_ref, h_ref, lg_ref)


def _linear_residual(x, gate, y, w, ffn_gain, ffn_scale, ffn_shift, w_router, tm):
    b, t, d = x.shape
    k = y.shape[-1]
    e = w_router.shape[1]
    row = lambda i, j: (i, j, 0)
    vec = lambda i, j: (i, 0, 0)
    const2 = lambda i, j: (0, 0)
    return pl.pallas_call(
        _linres_kernel,
        grid=(b, t // tm),
        in_specs=[pl.BlockSpec((1, tm, d), row),
                  pl.BlockSpec((1, 1, d), vec),
                  pl.BlockSpec((1, tm, k), row),
                  pl.BlockSpec((k, d), const2),
                  pl.BlockSpec((1, d), const2),
                  pl.BlockSpec((1, 1, d), vec),
                  pl.BlockSpec((1, 1, d), vec),
                  pl.BlockSpec((3 * d, e), const2)],
        out_specs=[pl.BlockSpec((1, tm, d), row), pl.BlockSpec((1, tm, d), row),
                   pl.BlockSpec((1, tm, e), row)],
        out_shape=[jax.ShapeDtypeStruct((b, t, d), F32), jax.ShapeDtypeStruct((b, t, d), BF16),
                   jax.ShapeDtypeStruct((b, t, e), F32)],
        compiler_params=_cparams(("parallel", "parallel")),
        name="linear_residual",
    )(x, gate, y, w, ffn_gain.reshape(1, d), ffn_scale, ffn_shift, _router_weights(w_router))


L = CHUNK
SCAN_BLOCK = 4 * L
COL_Q, COL_FF, COL_FB, COL_I, COL_G, COL_Z, COL_X, COL_DT = 0, 512, 1024, 1536, 2048, 2560, 3072, 4096


def _rms(x, gain):
    return x * lax.rsqrt(jnp.mean(x * x, axis=-1, keepdims=True) + EPS) * gain


def _split3(x):
    a = x.astype(BF16)
    r = x - a.astype(F32)
    b = r.astype(BF16)
    c = (r - b.astype(F32)).astype(BF16)
    return a, b, c


def _cumsum_exact(tri3, x):
    a, b, c = _split3(x)
    return jnp.dot(tri3, jnp.concatenate([a, b, c], axis=0), preferred_element_type=F32)


def _expand_exact(x, e3):
    a, b, c = _split3(x)
    return jnp.dot(jnp.concatenate([a, b, c], axis=1), e3, preferred_element_type=F32)


def _dot_nt(a, b):
    return lax.dot_general(a, b, (((1,), (1,)), ((), ())), preferred_element_type=F32)


def _dot_tn(a, b):
    return lax.dot_general(a, b, (((0,), (0,)), ((), ())), preferred_element_type=F32)


def _scan_dir(rev, r0, q_ref, f_ref, v_ref, dt_ref,
              lb_row, cw_ref, cbias_ref, dtb_ref, aexp_ref, dskip_ref,
              oh_ref, om_ref, sg_ref, sm_ref, u_scr):
    d = 1 if rev else 0
    rs = slice(r0, r0 + L)
    ti = lax.broadcasted_iota(jnp.int32, (L, L), 0)
    si = lax.broadcasted_iota(jnp.int32, (L, L), 1)
    keep = (si >= ti) if rev else (si <= ti)
    tri = keep.astype(BF16)
    tri3 = jnp.concatenate([tri, tri, tri], axis=1)
    t2 = lax.broadcasted_iota(jnp.int32, (L, 128), 0)
    s2 = lax.broadcasted_iota(jnp.int32, (L, 128), 1) % L
    keep2 = (s2 >= t2) if rev else (s2 <= t2)
    ref_row = (L - 1 - L // 2) if rev else L // 2
    last_row = 0 if rev else L - 1
    lane = lax.broadcasted_iota(jnp.int32, (1, 128), 1)
    lo = lane < 64
    r128 = lax.broadcasted_iota(jnp.int32, (128, 128), 0)
    c128 = lax.broadcasted_iota(jnp.int32, (128, 128), 1)
    blockdiag = (r128 < 64) == (c128 < 64)
    zero_b = jnp.zeros((), BF16)

    def pair_rows(a):
        return jnp.concatenate([jnp.where(lo, a, zero_b), jnp.where(lo, zero_b, a)], axis=0)

    q = q_ref[0, rs]
    qs = _silu(q)
    f = lb_row + (1.0 - lb_row) * jax.nn.sigmoid(f_ref[0, rs])
    k = 1.0 - f
    la = jnp.log(f)
    b = _cumsum_exact(tri3, la)
    bref = b[ref_row:ref_row + 1, :]
    blast = b[last_row:last_row + 1, :]
    qd = (qs * jnp.exp(b - bref)).astype(BF16)
    kd = (k * jnp.exp(bref - b)).astype(BF16)
    kl = (k * jnp.exp(blast - b)).astype(BF16)
    qb = (qs * jnp.exp(b)).astype(BF16)
    dec_col = jnp.broadcast_to(jnp.exp(blast), (8, 512)).T
    vb = v_ref[0, rs].astype(BF16)
    for p in range(HA // 2):
        sl = slice(128 * p, 128 * (p + 1))
        qd_p, kd_p, kl_p, qb_p, v_p = qd[:, sl], kd[:, sl], kl[:, sl], qb[:, sl], vb[:, sl]
        st = sg_ref[d, p]
        att = _dot_nt(qd_p, pair_rows(kd_p))
        att = jnp.where(keep2, att, 0.0).astype(BF16)
        lhs = jnp.concatenate([att, qb_p], axis=1)
        rhs = jnp.concatenate([pair_rows(v_p), st.astype(BF16)], axis=0)
        oh_ref[0, rs, sl] = jnp.dot(lhs, rhs, preferred_element_type=F32).astype(oh_ref.dtype)
        upd = _dot_tn(kl_p, v_p)
        sg_ref[d, p] = st * dec_col[sl, 0:1] + jnp.where(blockdiag, upd, 0.0)

    u = u_scr[r0:r0 + L + 16, :]
    y = cbias_ref[...] + cw_ref[2:3, :] * u[8:8 + L]
    for j in (0, 1, 3, 4):
        y = y + cw_ref[j:j + 1, :] * pltpu.roll(u, (2 - j) % (L + 16), axis=0)[8:8 + L]
    xbc = _silu(y)
    xm = xbc[:, :512]
    bm = xbc[:, 512:768].astype(BF16)
    cm = xbc[:, 768:1024].astype(BF16)
    draw = dt_ref[0, rs] + dtb_ref[...]
    dt = jnp.maximum(draw, 0.0) + jnp.log(1.0 + jnp.exp(-jnp.abs(draw)))
    la_m = -dt * aexp_ref[...]
    cum = _cumsum_exact(tri3, la_m)
    er = lax.broadcasted_iota(jnp.int32, (128, 512), 0)
    ec = lax.broadcasted_iota(jnp.int32, (128, 512), 1)
    e1 = (er == (ec // 64) + 8 * d).astype(BF16)
    e1 = jnp.concatenate([e1, e1, e1], axis=0)
    both = _expand_exact(jnp.concatenate([dt, cum], axis=0), e1)
    dt_e, cum_e = both[:L], both[L:]
    cum_t = cum.T
    xdt = xm * dt_e
    clast = cum_e[last_row:last_row + 1, :]
    wx = (jnp.exp(clast - cum_e) * xdt).astype(BF16)
    dec_m = jnp.exp(clast)
    ecum = jnp.exp(cum_e)
    xdt_b = xdt.astype(BF16)
    for g in range(2):
        bm_g = bm[:, 128 * g:128 * (g + 1)]
        cm_g = cm[:, 128 * g:128 * (g + 1)]
        gl = slice(256 * g, 256 * (g + 1))
        cb2 = _dot_nt(cm_g, jnp.concatenate([bm_g, bm_g], axis=0))
        sm = sm_ref[d, g]
        y_int = jnp.dot(cm_g, sm.astype(BF16), preferred_element_type=F32) * ecum[:, gl]
        for pp in range(2):
            h0 = 4 * g + 2 * pp + 8 * d
            sl = slice(256 * g + 128 * pp, 256 * g + 128 * (pp + 1))
            crow = jnp.concatenate([cum_t[h0:h0 + 1, :], cum_t[h0 + 1:h0 + 2, :]], axis=1)
            lm = jnp.exp(jnp.where(keep2, cum_e[:, sl] - crow, -jnp.inf))
            yp = jnp.dot((cb2 * lm).astype(BF16), pair_rows(xdt_b[:, sl]), preferred_element_type=F32)
            yp = yp + y_int[:, 128 * pp:128 * (pp + 1)]
            if not rev:
                yp = yp + dskip_ref[:, sl] * xm[:, sl]
            om_ref[0, rs, sl] = yp.astype(om_ref.dtype)
        sm_ref[d, g] = sm * dec_m[:, gl] + _dot_tn(bm_g, wx[:, gl])


def _scan_kernel(qf, ff, vf, xf, xpf, xnf, dtf, qb, fb, vb_, xb, xpb, xnb, dtb,
                 lb_ref, cw_ref, cbias_ref, dtbias_ref, aexp_ref, dskip_ref,
                 ohf, omf, ohb, omb, sg_ref, sm_ref, uf_scr, ub_scr, *, nct, nc):
    c = pl.program_id(1)

    @pl.when(c == 0)
    def _():
        sg_ref[...] = jnp.zeros(sg_ref.shape, F32)
        sm_ref[...] = jnp.zeros(sm_ref.shape, F32)

    cb = jnp.where(c < nct, nct - 1 - c, nc + nct - 1 - c)

    def edge_flags(ch):
        pv = jnp.where((ch == 0) | (ch == nct), 0.0, 1.0)
        nv = jnp.where((ch == nct - 1) | (ch == nc - 1), 0.0, 1.0)
        return pv, nv

    def fill(u_scr, x_ref, xp_ref, xn_ref, ch):
        pv, nv = edge_flags(ch)
        u_scr[0:8, :] = xp_ref[0] * pv
        u_scr[8:8 + SCAN_BLOCK, :] = x_ref[0]
        u_scr[8 + SCAN_BLOCK:16 + SCAN_BLOCK, :] = xn_ref[0] * nv

    fill(uf_scr, xf, xpf, xnf, c)
    fill(ub_scr, xb, xpb, xnb, cb)
    starts = range(0, SCAN_BLOCK, L)
    for r0 in starts:
        _scan_dir(False, r0, qf, ff, vf, dtf, lb_ref[0:1, :], cw_ref, cbias_ref,
                  dtbias_ref, aexp_ref, dskip_ref, ohf, omf, sg_ref, sm_ref, uf_scr)
    for r0 in reversed(starts):
        _scan_dir(True, r0, qb, fb, vb_, dtb, lb_ref[1:2, :], cw_ref, cbias_ref,
                  dtbias_ref, aexp_ref, dskip_ref, ohb, omb, sg_ref, sm_ref, ub_scr)


def _scan_mixers(p, nct, lb, conv_w, conv_b, dt_bias, a_log, d_skip):
    bsz, s, _ = p.shape
    sb = SCAN_BLOCK
    nc = s // sb
    nct = nct * L // sb
    nb8 = s // 8
    h8 = sb // 8

    def bmap(c):
        return jnp.where(c < nct, nct - 1 - c, nc + nct - 1 - c)

    def specs(cmap, fcol):
        colblk = lambda col, w: col // w
        return [
            pl.BlockSpec((1, sb, 512), lambda i, c: (i, cmap(c), colblk(COL_Q, 512))),
            pl.BlockSpec((1, sb, 512), lambda i, c: (i, cmap(c), colblk(fcol, 512))),
            pl.BlockSpec((1, sb, 512), lambda i, c: (i, cmap(c), colblk(COL_I, 512))),
            pl.BlockSpec((1, sb, 1024), lambda i, c: (i, cmap(c), colblk(COL_X, 1024))),
            pl.BlockSpec((1, 8, 1024), lambda i, c: (i, jnp.maximum(cmap(c) * h8 - 1, 0), colblk(COL_X, 1024))),
            pl.BlockSpec((1, 8, 1024), lambda i, c: (i, jnp.minimum(cmap(c) * h8 + h8, nb8 - 1), colblk(COL_X, 1024))),
            pl.BlockSpec((1, sb, 128), lambda i, c: (i, cmap(c), colblk(COL_DT, 128))),
        ]

    ident = lambda c: c
    const2 = lambda i, c: (0, 0)
    dtb = jnp.zeros((1, 128), F32).at[0, :16].set(dt_bias.reshape(16))
    aexp = jnp.zeros((1, 128), F32).at[0, :16].set(jnp.exp(a_log.reshape(16)))
    dsk = jnp.repeat(d_skip, 64).reshape(1, 512)
    kern = functools.partial(_scan_kernel, nct=nct, nc=nc)
    out_sd = jax.ShapeDtypeStruct((bsz, s, 512), BF16)
    return pl.pallas_call(
        kern,
        grid=(bsz, nc),
        in_specs=specs(ident, COL_FF) + specs(bmap, COL_FB) + [
            pl.BlockSpec((2, 512), const2), pl.BlockSpec((5, 1024), const2),
            pl.BlockSpec((1, 1024), const2), pl.BlockSpec((1, 128), const2),
            pl.BlockSpec((1, 128), const2), pl.BlockSpec((1, 512), const2)],
        out_specs=[pl.BlockSpec((1, sb, 512), lambda i, c: (i, c, 0)),
                   pl.BlockSpec((1, sb, 512), lambda i, c: (i, c, 0)),
                   pl.BlockSpec((1, sb, 512), lambda i, c: (i, bmap(c), 0)),
                   pl.BlockSpec((1, sb, 512), lambda i, c: (i, bmap(c), 0))],
        out_shape=[out_sd, out_sd, out_sd, out_sd],
        scratch_shapes=[pltpu.VMEM((2, HA // 2, 128, 128), F32),
                        pltpu.VMEM((2, 2, 128, 256), F32),
                        pltpu.VMEM((sb + 16, 1024), F32), pltpu.VMEM((sb + 16, 1024), F32)],
        compiler_params=_cparams(("parallel", "arbitrary")),
        name="scan_mixers",
    )(*([p] * 14), lb, conv_w, conv_b.reshape(1, 1024), dtb, aexp, dsk)


def _merge_kernel(ohf, ohb, omf, omb, g_ref, z_ref, xc_ref, xl_ref, gate_ref, hn_ref, mn_ref, w_ref,
                  fg_ref, fsc_ref, fsh_ref, rw_ref, oc_ref, ol_ref, h_ref, lg_ref, *, nct_tiles):
    oh = ohf[0].astype(F32) + ohb[0].astype(F32)
    sq = oh * oh
    hi = sq.astype(BF16)
    lo = (sq - hi.astype(F32)).astype(BF16)
    r = lax.broadcasted_iota(jnp.int32, (512, 512), 0) // 64
    c = lax.broadcasted_iota(jnp.int32, (512, 512), 1) // 64
    avg = jnp.where(r == c, 1.0 / 64, 0.0).astype(BF16)
    ms = jnp.dot(hi, avg, preferred_element_type=F32) + jnp.dot(lo, avg, preferred_element_type=F32)
    oh = oh * lax.rsqrt(ms + EPS) * hn_ref[...] * jax.nn.sigmoid(g_ref[0])
    y = (omf[0].astype(F32) + omb[0].astype(F32)) * _silu(z_ref[0])
    y = y * lax.rsqrt(jnp.mean(y * y, axis=-1, keepdims=True) + EPS) * mn_ref[...]
    cat = jnp.concatenate([oh, y], axis=1).astype(BF16)
    j = pl.program_id(1)
    x = jnp.where(j < nct_tiles, xc_ref[0], xl_ref[0])
    out = x + gate_ref[0, 0] * jnp.dot(cat, w_ref[...], preferred_element_type=F32)
    _router_body(out, fg_ref[...], fsc_ref[0, 0], fsh_ref[0, 0], rw_ref, h_ref, lg_ref)

    @pl.when(j < nct_tiles)
    def _():
        oc_ref[0] = out

    @pl.when(j >= nct_tiles)
    def _():
        ol_ref[0] = out


def _merge_mixers(x_ctx, x_lat, gate2, ohf, ohb, omf, omb, p, hgrn_norm, mamba_norm, w_out,
                  ffn_gain, ffn_scale2, ffn_shift2, w_router, tm):
    b, tc, d = x_ctx.shape
    s = tc + x_lat.shape[1]
    nct_tiles = tc // tm
    e = w_router.shape[1]
    sel = lambda i, j: (i, jnp.where(j < nct_tiles, 0, 1), 0, 0)
    row = lambda i, j: (i, j, 0)
    const2 = lambda i, j: (0, 0)
    return pl.pallas_call(
        functools.partial(_merge_kernel, nct_tiles=nct_tiles),
        grid=(b, s // tm),
        in_specs=[pl.BlockSpec((1, tm, 512), row)] * 4 + [
            pl.BlockSpec((1, tm, 512), lambda i, j: (i, j, COL_G // 512)),
            pl.BlockSpec((1, tm, 512), lambda i, j: (i, j, COL_Z // 512))] + _ctx_lat_specs(
                tm, d, nct_tiles) + [
            pl.BlockSpec((1, 1, 1, d), sel),
            pl.BlockSpec((1, 512), const2), pl.BlockSpec((1, 512), const2),
            pl.BlockSpec((2 * 512, d), const2),
            pl.BlockSpec((1, d), const2),
            pl.BlockSpec((1, 1, 1, d), sel), pl.BlockSpec((1, 1, 1, d), sel),
            pl.BlockSpec((3 * d, e), const2)],
        out_specs=[pl.BlockSpec((1, tm, d), lambda i, j: (i, jnp.minimum(j, nct_tiles - 1), 0)),
                   pl.BlockSpec((1, tm, d), lambda i, j: (i, jnp.maximum(j - nct_tiles, 0), 0)),
                   pl.BlockSpec((1, tm, d), row), pl.BlockSpec((1, tm, e), row)],
        out_shape=[jax.ShapeDtypeStruct((b, tc, d), F32),
                   jax.ShapeDtypeStruct((b, s - tc, d), F32),
                   jax.ShapeDtypeStruct((b, s, d), BF16),
                   jax.ShapeDtypeStruct((b, s, e), F32)],
        compiler_params=_cparams(("parallel", "arbitrary")),
        name="merge_mixers",
    )(ohf, ohb, omf, omb, p, p, x_ctx, x_lat, gate2.reshape(b, 2, 1, d), hgrn_norm.reshape(1, 512),
      mamba_norm.reshape(1, 512), w_out, ffn_gain.reshape(1, d), ffn_scale2.reshape(b, 2, 1, d),
      ffn_shift2.reshape(b, 2, 1, d), _router_weights(w_router))


def _rope_tables(t):
    rows = t // GRID_W
    row = jnp.repeat(jnp.arange(rows, dtype=jnp.int32), GRID_W)
    col = jnp.tile(jnp.arange(GRID_W, dtype=jnp.int32), rows)
    nf = ROPE // 4
    inv_freq = ROPE_THETA ** (-jnp.arange(nf, dtype=F32) / nf)
    pos = jnp.stack([row, col], axis=-1).astype(F32)
    ang = pos[..., None] * inv_freq
    cos, sin = jnp.cos(ang), jnp.sin(ang)
    cos64 = jnp.broadcast_to(cos[:, :, None, :], (t, 2, 2, nf)).reshape(t, ROPE)
    sin64 = jnp.broadcast_to(sin[:, :, None, :], (t, 2, 2, nf)).reshape(t, ROPE)
    zero = jnp.zeros((t, ROPE), F32)
    return jnp.concatenate([cos64, zero], -1), jnp.concatenate([sin64, zero], -1)


def _swap_cols(w):
    nf = ROPE // 4
    wr = w.reshape(w.shape[:-1] + (2, 2, nf))
    return jnp.stack([-wr[..., 1, :], wr[..., 0, :]], axis=-2).reshape(w.shape)


def _rope_tile(tile, cos_t, sin_t):
    return tile * cos_t + pltpu.roll(tile, ROPE, axis=1) * sin_t


def _mla_down_kernel(x_ref, g_ref, sc_ref, sh_ref, w_ref, qn_ref, kvn_ref, cos_ref, sin_ref,
                     cq_ref, ckv_ref, kr_ref):
    a = _modnorm(x_ref[0], g_ref[...], sc_ref[0], sh_ref[0]).astype(BF16)
    c = jnp.dot(a, w_ref[...], preferred_element_type=F32)
    cq = c[:, :Q_LORA]
    ckv = c[:, Q_LORA:Q_LORA + KV_LORA]
    cq_ref[0] = _rms(cq, qn_ref[...]).astype(BF16)
    ckv_ref[0] = _rms(ckv, kvn_ref[...]).astype(BF16)
    kr_ref[0] = _rope_tile(c[:, Q_LORA + KV_LORA:], cos_ref[...], sin_ref[...]).astype(BF16)


def _mla_down(x, gain, scale, shift, w_cat, q_norm, kv_norm, cos_t, sin_t, tm):
    b, t, d = x.shape
    n = w_cat.shape[1]
    return pl.pallas_call(
        _mla_down_kernel,
        grid=(b, t // tm),
        in_specs=[pl.BlockSpec((1, tm, d), lambda i, j: (i, j, 0)),
                  pl.BlockSpec((1, d), lambda i, j: (0, 0)),
                  pl.BlockSpec((1, 1, d), lambda i, j: (i, 0, 0)),
                  pl.BlockSpec((1, 1, d), lambda i, j: (i, 0, 0)),
                  pl.BlockSpec((d, n), lambda i, j: (0, 0)),
                  pl.BlockSpec((1, Q_LORA), lambda i, j: (0, 0)),
                  pl.BlockSpec((1, KV_LORA), lambda i, j: (0, 0)),
                  pl.BlockSpec((tm, 128), lambda i, j: (j, 0)),
                  pl.BlockSpec((tm, 128), lambda i, j: (j, 0))],
        out_specs=[pl.BlockSpec((1, tm, Q_LORA), lambda i, j: (i, j, 0)),
                   pl.BlockSpec((1, tm, KV_LORA), lambda i, j: (i, j, 0)),
                   pl.BlockSpec((1, tm, 128), lambda i, j: (i, j, 0))],
        out_shape=[jax.ShapeDtypeStruct((b, t, Q_LORA), BF16),
                   jax.ShapeDtypeStruct((b, t, KV_LORA), BF16),
                   jax.ShapeDtypeStruct((b, t, 128), BF16)],
        compiler_params=_cparams(("parallel", "parallel")),
        name="mla_down",
    )(x, gain.reshape(1, d), scale, shift, w_cat, q_norm.reshape(1, -1), kv_norm.reshape(1, -1),
      cos_t, sin_t)


def _q_up_kernel(cq_ref, w_ref, cos_ref, sin_ref, q_ref):
    cq = cq_ref[0]
    for h in range(H_C):
        q = jnp.dot(cq, w_ref[h], preferred_element_type=F32)
        qn = q[:, :NOPE] * QK_SCALE
        qr = _rope_tile(q[:, NOPE:], cos_ref[...], sin_ref[...]) * QK_SCALE
        q_ref[0, h] = jnp.concatenate([qn, qr], axis=1).astype(BF16)


def _q_up(cq, w_uq_h, cos_t, sin_t, tm):
    b, t, r = cq.shape
    return pl.pallas_call(
        _q_up_kernel,
        grid=(b, t // tm),
        in_specs=[pl.BlockSpec((1, tm, r), lambda i, j: (i, j, 0)),
                  pl.BlockSpec((H_C, r, QK_DIM), lambda i, j: (0, 0, 0)),
                  pl.BlockSpec((tm, 128), lambda i, j: (j, 0)),
                  pl.BlockSpec((tm, 128), lambda i, j: (j, 0))],
        out_specs=pl.BlockSpec((1, H_C, tm, QK_DIM), lambda i, j: (i, 0, j, 0)),
        out_shape=jax.ShapeDtypeStruct((b, H_C, t, QK_DIM), BF16),
        compiler_params=_cparams(("parallel", "parallel")),
        name="q_up",
    )(cq, w_uq_h, cos_t, sin_t)


def _kv_up_kernel(ckv_ref, kr_ref, w_ref, k_ref, v_ref):
    ckv = ckv_ref[0]
    kr = kr_ref[0]
    for h in range(H_C):
        kv = jnp.dot(ckv, w_ref[h], preferred_element_type=F32)
        k_ref[0, h] = jnp.concatenate([kv[:, :NOPE].astype(BF16), kr], axis=1)
        v_ref[0, h] = kv[:, NOPE:].astype(BF16)


def _kv_up(ckv, kr, w_ukv_h, tm):
    b, s, r = ckv.shape
    return pl.pallas_call(
        _kv_up_kernel,
        grid=(b, s // tm),
        in_specs=[pl.BlockSpec((1, tm, r), lambda i, j: (i, j, 0)),
                  pl.BlockSpec((1, tm, 128), lambda i, j: (i, j, 0)),
                  pl.BlockSpec((H_C, r, NOPE + VH), lambda i, j: (0, 0, 0))],
        out_specs=[pl.BlockSpec((1, H_C, tm, QK_DIM), lambda i, j: (i, 0, j, 0)),
                   pl.BlockSpec((1, H_C, tm, VH), lambda i, j: (i, 0, j, 0))],
        out_shape=[jax.ShapeDtypeStruct((b, H_C, s, QK_DIM), BF16),
                   jax.ShapeDtypeStruct((b, H_C, s, VH), BF16)],
        compiler_params=_cparams(("parallel", "parallel")),
        name="kv_up",
    )(ckv, kr, w_ukv_h)


def _attn_kernel(*refs, tk, nk, tq):
    *qn_refs, q0_ref, k_ref, v_ref, o_ref, s0_scr, s1_scr, m_scr = refs
    n = pl.program_id(2)

    def scores(q, j):
        ks = k_ref[0, 0, j * tk:(j + 1) * tk, :]
        return lax.dot_general(q, ks, (((1,), (1,)), ((), ())), preferred_element_type=F32)

    def fold_max(mrun, s):
        for c in range(tk // 128):
            mrun = jnp.maximum(mrun, s[:, c * 128:(c + 1) * 128])
        return mrun

    @pl.when(n == 0)
    def _():
        q0 = q0_ref[0, 0]
        mrun = jnp.full((tq, 128), -jnp.inf, F32)
        for j in range(nk):
            s = scores(q0, j)
            s0_scr[:, j * tk:(j + 1) * tk] = s
            mrun = fold_max(mrun, s)
        m_scr[...] = jnp.max(mrun, axis=-1, keepdims=True)

    def fused(qn, s_read, s_write, m):
        mrun = jnp.full((tq, 128), -jnp.inf, F32)
        lrun = jnp.zeros((tq, 128), F32)
        acc = jnp.zeros((tq, VH), F32)
        for j in range(nk):
            sn = scores(qn, j)
            s_write[:, j * tk:(j + 1) * tk] = sn
            mrun = fold_max(mrun, sn)
            p = jnp.exp2(s_read[:, j * tk:(j + 1) * tk] - m)
            for c in range(tk // 128):
                lrun = lrun + p[:, c * 128:(c + 1) * 128]
            acc = acc + jnp.dot(p.astype(BF16), v_ref[0, 0, j * tk:(j + 1) * tk, :],
                                preferred_element_type=F32)
        out = acc / jnp.sum(lrun, axis=-1, keepdims=True)
        return out, jnp.max(mrun, axis=-1, keepdims=True)

    m = m_scr[...]
    bufs = (s0_scr, s1_scr)
    for r, qn_ref in enumerate(qn_refs):
        out, m = fused(qn_ref[0, 0], bufs[r % 2], bufs[(r + 1) % 2], m)
        o_ref[0, r * tq:(r + 1) * tq, :] = out.astype(o_ref.dtype)
    m_scr[...] = m


ATTN_TILES = 4


def _attention(q, k, v, tq, tk):
    b, h, t, _ = q.shape
    s = k.shape[2]
    nq = t // tq
    nt = ATTN_TILES
    assert nt % 2 == 0 and t % (nt * tq) == 0 and s % tk == 0
    kern = functools.partial(_attn_kernel, tk=tk, nk=s // tk, tq=tq)

    def next_tile(r):
        return lambda i, j, n: (i, j, jnp.minimum(nt * n + r + 1, nq - 1), 0)

    return pl.pallas_call(
        kern,
        grid=(b, h, nq // nt),
        in_specs=[pl.BlockSpec((1, 1, tq, QK_DIM), next_tile(r)) for r in range(nt)] + [
                  pl.BlockSpec((1, 1, tq, QK_DIM), lambda i, j, n: (i, j, 0, 0)),
                  pl.BlockSpec((1, 1, s, QK_DIM), lambda i, j, n: (i, j, 0, 0)),
                  pl.BlockSpec((1, 1, s, VH), lambda i, j, n: (i, j, 0, 0))],
        out_specs=pl.BlockSpec((1, nt * tq, VH), lambda i, j, n: (i, n, j)),
        out_shape=jax.ShapeDtypeStruct((b, t, h * VH), BF16),
        scratch_shapes=[pltpu.VMEM((tq, s), F32), pltpu.VMEM((tq, s), F32),
                        pltpu.VMEM((tq, 1), F32)],
        compiler_params=_cparams(("parallel", "parallel", "arbitrary")),
        name="attention",
    )(*([q] * (nt + 1)), k, v)


def _ffn_kernel(x_ref, wg_ref, wu_ref, wd_ref, g_ref, o_ref, acc_scr, wgb, wub, wdb, *, fc):
    @pl.when((pl.program_id(1) == 0) & (pl.program_id(2) == 0))
    def _():
        wgb[...] = wg_ref[0, 0].astype(BF16)
        wub[...] = wu_ref[0, 0].astype(BF16)
        wdb[...] = wd_ref[0, 0].astype(BF16)

    x = x_ref[0, 0]
    nf = wgb.shape[-1]
    for c in range(nf // fc):
        sl = slice(c * fc, (c + 1) * fc)
        hg = jnp.dot(x, wgb[:, sl], preferred_element_type=F32)
        hu = jnp.dot(x, wub[:, sl], preferred_element_type=F32)
        hid = (_silu(hg) * hu).astype(BF16)
        part = jnp.dot(hid, wdb[sl, :], preferred_element_type=F32)
        if c == 0:
            acc_scr[...] = part
        else:
            acc_scr[...] += part
    o_ref[0, 0] = (acc_scr[...] * g_ref[0, 0]).astype(o_ref.dtype)


def _expert_ffn(xg, gates, w_gate, w_up, w_down, layer, tm, fc):
    b, e, cap, d = xg.shape
    f = w_gate.shape[-1]
    tm = min(tm, cap)
    kern = functools.partial(_ffn_kernel, fc=fc)
    return pl.pallas_call(
        kern,
        grid=(e, b, cap // tm),
        in_specs=[pl.BlockSpec((1, 1, tm, d), lambda j, i, m: (i, j, m, 0)),
                  pl.BlockSpec((1, 1, d, f), lambda j, i, m: (layer, j, 0, 0)),
                  pl.BlockSpec((1, 1, d, f), lambda j, i, m: (layer, j, 0, 0)),
                  pl.BlockSpec((1, 1, f, d), lambda j, i, m: (layer, j, 0, 0)),
                  pl.BlockSpec((1, 1, tm, 1), lambda j, i, m: (i, j, m, 0))],
        out_specs=pl.BlockSpec((1, 1, tm, d), lambda j, i, m: (i, j, m, 0)),
        out_shape=jax.ShapeDtypeStruct((b, e, cap, d), BF16),
        scratch_shapes=[pltpu.VMEM((tm, d), F32), pltpu.VMEM((d, f), BF16),
                        pltpu.VMEM((d, f), BF16), pltpu.VMEM((f, d), BF16)],
        compiler_params=_cparams(("arbitrary", "arbitrary", "arbitrary")),
        name="expert_ffn",
    )(xg, w_gate, w_up, w_down, gates)


SLOT_WIN = 256
TOK_BLK = 128
COMBINE_EXPERTS = 4


def _combine_kernel(lo_ref, idx_ref, y_ref, x_ref, gate_ref, fg_ref, o_ref, *, nblk, nrow,
                    final_norm):
    b, j, eg = pl.program_id(0), pl.program_id(1), pl.program_id(2)

    @pl.when(eg == 0)
    def _():
        o_ref[...] = jnp.zeros(o_ref.shape, F32)

    tok = lax.broadcasted_iota(jnp.int32, (TOK_BLK, 1), 0)
    for blk in range(nblk):
        gblk = j * nblk + blk
        part = None
        for k in range(COMBINE_EXPERTS):
            w = jnp.minimum(lo_ref[b, eg * COMBINE_EXPERTS + k, gblk] // TOK_BLK, nrow - 2)
            ids = jnp.concatenate([idx_ref[0, k, pl.ds(w, 1), :],
                                   idx_ref[0, k, pl.ds(w + 1, 1), :]], axis=1)
            onehot = (ids == tok + gblk * TOK_BLK).astype(BF16)
            yw = y_ref[0, k, pl.ds(pl.multiple_of(w * TOK_BLK, TOK_BLK), SLOT_WIN), :]
            d = jnp.dot(onehot, yw, preferred_element_type=F32)
            part = d if part is None else part + d
        rows = slice(blk * TOK_BLK, (blk + 1) * TOK_BLK)
        o_ref[0, rows, :] += part

    @pl.when(eg == pl.num_programs(2) - 1)
    def _():
        out = x_ref[0] + gate_ref[0] * o_ref[0]
        o_ref[0] = _rms(out, fg_ref[...]) if final_norm else out


def _combine(x, gate_out, idx_s, y, final_gain=None):
    b, t, d = x.shape
    e, cap = idx_s.shape[1:]
    capp = max(cap, SLOT_WIN)
    if capp != cap:
        idx_s = jnp.pad(idx_s, ((0, 0), (0, 0), (0, capp - cap)), constant_values=-1)
        y = jnp.pad(y[:, :, :cap], ((0, 0), (0, 0), (0, capp - cap), (0, 0)))
    nrow = capp // TOK_BLK
    ts = min(2048, t)
    nblk = ts // TOK_BLK
    bounds = jnp.arange(t // TOK_BLK, dtype=jnp.int32) * TOK_BLK
    valid = idx_s >= 0
    lo = jnp.sum((valid[..., None] & (idx_s[..., None] < bounds)).astype(jnp.int32), axis=2)
    kern = functools.partial(_combine_kernel, nblk=nblk, nrow=nrow,
                             final_norm=final_gain is not None)
    fg = jnp.ones((1, d), F32) if final_gain is None else final_gain.reshape(1, d)
    grid_spec = pltpu.PrefetchScalarGridSpec(
        num_scalar_prefetch=1,
        grid=(b, t // ts, e // COMBINE_EXPERTS),
        in_specs=[pl.BlockSpec((1, COMBINE_EXPERTS, nrow, TOK_BLK), lambda i, j, k, lo_r: (i, k, 0, 0)),
                  pl.BlockSpec((1, COMBINE_EXPERTS, capp, d), lambda i, j, k, lo_r: (i, k, 0, 0)),
                  pl.BlockSpec((1, ts, d), lambda i, j, k, lo_r: (i, j, 0)),
                  pl.BlockSpec((1, 1, d), lambda i, j, k, lo_r: (i, 0, 0)),
                  pl.BlockSpec((1, d), lambda i, j, k, lo_r: (0, 0))],
        out_specs=pl.BlockSpec((1, ts, d), lambda i, j, k, lo_r: (i, j, 0)),
    )
    return pl.pallas_call(
        kern,
        grid_spec=grid_spec,
        out_shape=jax.ShapeDtypeStruct((b, t, d), F32),
        compiler_params=_cparams(("parallel", "parallel", "arbitrary")),
        name="moe_combine",
    )(lo, idx_s.reshape(b, e, nrow, TOK_BLK), y, x, gate_out, fg)


def _route(logits):
    t = logits.shape[1]
    cap = CAP_FACTOR * t // N_EXPERTS
    aff = jax.nn.softmax(logits, axis=-1)
    g, idx = lax.top_k(jnp.swapaxes(aff, 1, 2), cap)
    return lax.sort((idx, g), dimension=2, num_keys=1)


def _moe(x, hn, logits, gate_out, w_gate, w_up, w_down, layer, final_gain=None):
    b, t, d = x.shape
    idx_s, g_s = _route(logits)
    bidx = jnp.arange(b)[:, None, None]
    xg = hn[bidx, idx_s]
    y = _expert_ffn(xg, g_s[..., None], w_gate, w_up, w_down, layer, 1024, 512)
    return _combine(x, gate_out, idx_s, y, final_gain)


def _moe_two_streams(x_ctx, x_lat, hn, logits, gate_ctx, gate_lat, w_gate, w_up, w_down, layer):
    b, tc, d = x_ctx.shape
    idx_c, g_c = _route(logits[:, :tc])
    idx_l, g_l = _route(logits[:, tc:])
    idx_cat = jnp.concatenate([idx_l + tc, idx_c], axis=2)
    g_cat = jnp.concatenate([g_l, g_c], axis=2)
    bidx = jnp.arange(b)[:, None, None]
    xg = hn[bidx, idx_cat]
    cap = idx_cat.shape[2]
    tm = _pick_tile(cap, 1024, 528, 512, 352, 256, 176, 96, 32, 16)
    y = _expert_ffn(xg, g_cat[..., None], w_gate, w_up, w_down, layer, tm, 512)
    new_lat = _combine(x_lat, gate_lat, idx_l, y)
    new_ctx = _combine(x_ctx, gate_ctx, idx_c, y[:, :, idx_l.shape[2]:])
    return new_ctx, new_lat


def kernel(x, c, ctx, c_ctx, w_mod, b_mod, norm_mix, norm_ffn, norm_out, w_in, w_out_rec,
           conv_w, conv_b, lb_gamma, dt_bias, a_log, d_skip, hgrn_norm, mamba_norm,
           w_dq, q_norm, w_uq, w_dkv, kv_norm, w_ukv, w_kr, w_o,
           w_router, w_gate, w_up, w_down):
    B, T, D = x.shape
    TC = ctx.shape[1]
    lb = jnp.cumsum(jax.nn.softmax(lb_gamma.astype(F32), axis=0), axis=0)

    cvec = jnp.zeros((16, D), F32).at[:B].set(c).at[B].set(c_ctx)
    mods = _modulation(cvec, w_mod, b_mod)

    x_lat, x_ctx = x, ctx
    for l in range(DEPTH):
        need_ctx = l < DEPTH - 1
        m_lat = [mods[l, :B, i * D:(i + 1) * D][:, None, :] for i in range(6)]
        m_ctx = [jnp.broadcast_to(mods[l, B, i * D:(i + 1) * D][None, None, :], (B, 1, D))
                 for i in range(6)]
        if l % 2 == 0:
            e = l // 2
            w_in_p = jnp.pad(w_in[e], ((0, 0), (0, IN_COLS_PAD - IN_COLS))).astype(BF16)
            pair = lambda i: jnp.concatenate([m_ctx[i], m_lat[i]], axis=1)
            p = _modproj(x_ctx, x_lat, norm_mix[l], pair(1), pair(0), w_in_p, 256)
            ohf, omf, ohb, omb = _scan_mixers(p, TC // CHUNK, lb[e], conv_w[e], conv_b[e],
                                              dt_bias[e], a_log[e], d_skip[e])
            x_ctx, x_lat, hn, logits = _merge_mixers(
                x_ctx, x_lat, pair(2), ohf, ohb, omf, omb, p, hgrn_norm[e], mamba_norm[e],
                w_out_rec[e].astype(BF16), norm_ffn[l], pair(4), pair(3), w_router[l], 256)
            if not need_ctx:
                hn, logits = hn[:, TC:], logits[:, TC:]
        else:
            assert not need_ctx
            j = l // 2
            cos_l, sin_l = _rope_tables(T)
            cos_c = jnp.concatenate([jnp.ones((TC, ROPE), F32), jnp.zeros((TC, ROPE), F32)], -1)
            sin_c = jnp.zeros((TC, 128), F32)
            w_cat = jnp.concatenate([w_dq[j], w_dkv[j], w_kr[j], _swap_cols(w_kr[j])],
                                    axis=1).astype(BF16)
            wq = w_uq[j].reshape(Q_LORA, H_C, NOPE + ROPE)
            wq_h = jnp.concatenate([wq, _swap_cols(wq[..., NOPE:])], axis=-1)
            wq_h = jnp.transpose(wq_h, (1, 0, 2)).astype(BF16)
            wkv_h = jnp.transpose(w_ukv[j].reshape(KV_LORA, H_C, NOPE + VH), (1, 0, 2)).astype(BF16)
            cq_l, ckv_l, kr_l = _mla_down(x_lat, norm_mix[l], m_lat[1], m_lat[0], w_cat,
                                          q_norm[j], kv_norm[j], cos_l, sin_l, 512)
            _, ckv_c, kr_c = _mla_down(x_ctx, norm_mix[l], m_ctx[1], m_ctx[0], w_cat,
                                       q_norm[j], kv_norm[j], cos_c, sin_c, 256)
            ckv = jnp.concatenate([ckv_c, ckv_l], axis=1)
            kr = jnp.concatenate([kr_c, kr_l], axis=1)
            qh = _q_up(cq_l, wq_h, cos_l, sin_l, _pick_tile(T, 1024, 512, 256))
            kh, vh = _kv_up(ckv, kr, wkv_h, _pick_tile(T + TC, 1408, 768, 256))
            o = _attention(qh, kh, vh, 256, 768)
            x_lat, hn, logits = _linear_residual(x_lat, m_lat[2], o, w_o[j].astype(BF16),
                                                 norm_ffn[l], m_lat[4], m_lat[3], w_router[l],
                                                 _pick_tile(T, 1024, 512))
        if need_ctx:
            x_ctx, x_lat = _moe_two_streams(x_ctx, x_lat, hn, logits, m_ctx[5], m_lat[5],
                                            w_gate, w_up, w_down, l)
        else:
            x_lat = _moe(x_lat, hn, logits, m_lat[5], w_gate, w_up, w_down, l, norm_out)
    return x_lat
```
